```python
import math
import jax, jax.numpy as jnp
from jax import lax
import numpy as np

D_MODEL = 2048
BATCH = 8
SEQ = 2048
DEPTH = 1

CHUNK = 64
Q_BLOCK = 128
SB_HEADS = 8
SB_HEAD_DIM = 128
D_SB = SB_HEADS * SB_HEAD_DIM
D_CONV = D_MODEL // 2
CONV_WIDTH = 31
D_FF = 4 * D_MODEL
EPS = 1e-6
IN_SPLITS = (D_SB, D_SB, D_SB, D_CONV, D_CONV, D_MODEL, D_MODEL)
D_IN = sum(IN_SPLITS)

kernel_name = "stickbreak_conformer_gated_hybrid"


def rmsnorm(x, g):
    xf = x.astype(jnp.float32)
    y = xf * lax.rsqrt(jnp.mean(xf * xf, axis=-1, keepdims=True) + EPS)
    return (y * g.astype(jnp.float32)).astype(x.dtype)


def layernorm(x, g, b):
    xf = x.astype(jnp.float32)
    mu = jnp.mean(xf, axis=-1, keepdims=True)
    var = jnp.mean(jnp.square(xf - mu), axis=-1, keepdims=True)
    y = (xf - mu) * lax.rsqrt(var + EPS)
    return (y * g.astype(jnp.float32) + b.astype(jnp.float32)).astype(x.dtype)


def stick_breaking_attention(q, k, v):
    b, s, h, dh = q.shape
    scale = 1.0 / math.sqrt(dh)
    outs = []
    for i in range(s // Q_BLOCK):
        t0, t1 = i * Q_BLOCK, (i + 1) * Q_BLOCK
        qb = q[:, t0:t1]
        kb = k[:, :t1]
        vb = v[:, :t1]
        z = jnp.einsum('bqhd,bkhd->bhqk', qb, kb).astype(jnp.float32) * scale
        tq = t0 + jnp.arange(Q_BLOCK)[:, None]
        sk = jnp.arange(t1)[None, :]
        mask = sk < tq
        log_1m = jnp.where(mask, jax.nn.log_sigmoid(-z), 0.0)
        suffix = lax.cumsum(log_1m, axis=3, reverse=True) - log_1m
        a = jnp.where(mask, jnp.exp(jax.nn.log_sigmoid(z) + suffix), 0.0)
        outs.append(jnp.einsum('bhqk,bkhd->bqhd', a.astype(vb.dtype), vb))
    return jnp.concatenate(outs, axis=1)


def causal_depthwise_conv(u, w, bias):
    c = u.shape[-1]
    y = lax.conv_general_dilated(
        u, w.reshape(CONV_WIDTH, 1, c).astype(u.dtype),
        window_strides=(1,), padding=[(CONV_WIDTH - 1, 0)],
        dimension_numbers=('NWC', 'WIO', 'NWC'), feature_group_count=c)
    return y + bias


def _fwd_setup_inputs(seed: int = 0) -> dict:
    key = jax.random.key(seed)
    ks = jax.random.split(key, 17)
    L = DEPTH
    nrm = lambda k, shape, fan_in: jax.random.normal(k, shape, jnp.float32) * (fan_in ** -0.5)
    gain = lambda k, shape: 1.0 + 0.02 * jax.random.normal(k, shape, jnp.float32)
    small = lambda k, shape: 0.02 * jax.random.normal(k, shape, jnp.float32)
    return {
        "x": jax.random.normal(ks[0], (BATCH, SEQ, D_MODEL), jnp.float32),
        "g_pre_mix": gain(ks[1], (L, D_MODEL)),
        "w_in": nrm(ks[2], (L, D_MODEL, D_IN), D_MODEL),
        "b_in": small(ks[3], (L, D_IN)),
        "w_dw": nrm(ks[4], (L, CONV_WIDTH, D_CONV), CONV_WIDTH),
        "b_dw": small(ks[5], (L, D_CONV)),
        "g_conv_ln": gain(ks[6], (L, D_CONV)),
        "b_conv_ln": small(ks[7], (L, D_CONV)),
        "w_sb_out": nrm(ks[8], (L, D_SB, D_MODEL), D_SB),
        "w_conv_out": nrm(ks[9], (L, D_CONV, D_MODEL), D_CONV),
        "w_o": nrm(ks[10], (L, D_MODEL, D_MODEL), D_MODEL),
        "g_post_mix": gain(ks[11], (L, D_MODEL)),
        "g_pre_mlp": gain(ks[12], (L, D_MODEL)),
        "w_up": nrm(ks[13], (L, D_MODEL, D_FF), D_MODEL),
        "w_down": nrm(ks[14], (L, D_FF, D_MODEL), D_FF),
        "g_post_mlp": gain(ks[15], (L, D_MODEL)),
    }


def _fwd_reference(x, g_pre_mix, w_in, b_in, w_dw, b_dw, g_conv_ln, b_conv_ln,
              w_sb_out, w_conv_out, w_o, g_post_mix, g_pre_mlp, w_up, w_down,
              g_post_mlp):
    b, s, _ = x.shape
    offs = np.cumsum(IN_SPLITS)[:-1].tolist()
    for l in range(DEPTH):
        h = rmsnorm(x, g_pre_mix[l])
        proj = jnp.einsum('bsd,de->bse', h, w_in[l]) + b_in[l]
        q, k, v, glu_a, glu_b, gate_sb, gate_cv = jnp.split(proj, offs, axis=-1)
        hd = (b, s, SB_HEADS, SB_HEAD_DIM)
        o_sb = stick_breaking_attention(q.reshape(hd), k.reshape(hd), v.reshape(hd))
        o_sb = jnp.einsum('bse,ed->bsd', o_sb.reshape(b, s, D_SB), w_sb_out[l])
        u = glu_a * jax.nn.sigmoid(glu_b)
        u = causal_depthwise_conv(u, w_dw[l], b_dw[l])
        u = jax.nn.silu(layernorm(u, g_conv_ln[l], b_conv_ln[l]))
        o_cv = jnp.einsum('bsc,cd->bsd', u, w_conv_out[l])
        merged = jax.nn.sigmoid(gate_sb) * o_sb + jax.nn.sigmoid(gate_cv) * o_cv
        y = jnp.einsum('bsd,de->bse', merged, w_o[l])
        x = x + rmsnorm(y, g_post_mix[l])
        h = rmsnorm(x, g_pre_mlp[l])
        f = jnp.square(jax.nn.relu(jnp.einsum('bsd,df->bsf', h, w_up[l])))
        f = jnp.einsum('bsf,fd->bsd', f, w_down[l])
        x = x + rmsnorm(f, g_post_mlp[l])
    return x


import jax as _jax
import jax.numpy as _jnp

TWIN_FORMAT = 'train_step'
FWD_PARAMS = ['x', 'g_pre_mix', 'w_in', 'b_in', 'w_dw', 'b_dw', 'g_conv_ln', 'b_conv_ln', 'w_sb_out', 'w_conv_out', 'w_o', 'g_post_mix', 'g_pre_mlp', 'w_up', 'w_down', 'g_post_mlp']
TWIN_WEIGHTS = ['g_pre_mix', 'w_in', 'b_in', 'w_dw', 'b_dw', 'g_conv_ln', 'b_conv_ln', 'w_sb_out', 'w_conv_out', 'w_o', 'g_post_mix', 'g_pre_mlp', 'w_up', 'w_down', 'g_post_mlp']
TWIN_DIFF_INPUT = 'x'
TWIN_INPUTS = ['x', 'g_pre_mix', 'w_in', 'b_in', 'w_dw', 'b_dw', 'g_conv_ln', 'b_conv_ln', 'w_sb_out', 'w_conv_out', 'w_o', 'g_post_mix', 'g_pre_mlp', 'w_up', 'w_down', 'g_post_mlp', 'loss_target', 'm_g_pre_mix', 'm_w_in', 'm_b_in', 'm_w_dw', 'm_b_dw', 'm_g_conv_ln', 'm_b_conv_ln', 'm_w_sb_out', 'm_w_conv_out', 'm_w_o', 'm_g_post_mix', 'm_g_pre_mlp', 'm_w_up', 'm_w_down', 'm_g_post_mlp', 'v_g_pre_mix', 'v_w_in', 'v_b_in', 'v_w_dw', 'v_b_dw', 'v_g_conv_ln', 'v_b_conv_ln', 'v_w_sb_out', 'v_w_conv_out', 'v_w_o', 'v_g_post_mix', 'v_g_pre_mlp', 'v_w_up', 'v_w_down', 'v_g_post_mlp']
TWIN_OUTPUTS = ['loss', 'grad_x', 'grad_g_pre_mix', 'grad_w_in', 'grad_b_in', 'grad_w_dw', 'grad_b_dw', 'grad_g_conv_ln', 'grad_b_conv_ln', 'grad_w_sb_out', 'grad_w_conv_out', 'grad_w_o', 'grad_g_post_mix', 'grad_g_pre_mlp', 'grad_w_up', 'grad_w_down', 'grad_g_post_mlp', 'delta_g_pre_mix', 'delta_w_in', 'delta_b_in', 'delta_w_dw', 'delta_b_dw', 'delta_g_conv_ln', 'delta_b_conv_ln', 'delta_w_sb_out', 'delta_w_conv_out', 'delta_w_o', 'delta_g_post_mix', 'delta_g_pre_mlp', 'delta_w_up', 'delta_w_down', 'delta_g_post_mlp', 'new_m_g_pre_mix', 'new_m_w_in', 'new_m_b_in', 'new_m_w_dw', 'new_m_b_dw', 'new_m_g_conv_ln', 'new_m_b_conv_ln', 'new_m_w_sb_out', 'new_m_w_conv_out', 'new_m_w_o', 'new_m_g_post_mix', 'new_m_g_pre_mlp', 'new_m_w_up', 'new_m_w_down', 'new_m_g_post_mlp', 'new_v_g_pre_mix', 'new_v_w_in', 'new_v_b_in', 'new_v_w_dw', 'new_v_b_dw', 'new_v_g_conv_ln', 'new_v_b_conv_ln', 'new_v_w_sb_out', 'new_v_w_conv_out', 'new_v_w_o', 'new_v_g_post_mix', 'new_v_g_pre_mlp', 'new_v_w_up', 'new_v_w_down', 'new_v_g_post_mlp']
TWIN_LEAF_KINDS = {'loss': 'loss', 'grad_x': 'grad_x', 'grad_g_pre_mix': 'grad_w', 'grad_w_in': 'grad_w', 'grad_b_in': 'grad_w', 'grad_w_dw': 'grad_w', 'grad_b_dw': 'grad_w', 'grad_g_conv_ln': 'grad_w', 'grad_b_conv_ln': 'grad_w', 'grad_w_sb_out': 'grad_w', 'grad_w_conv_out': 'grad_w', 'grad_w_o': 'grad_w', 'grad_g_post_mix': 'grad_w', 'grad_g_pre_mlp': 'grad_w', 'grad_w_up': 'grad_w', 'grad_w_down': 'grad_w', 'grad_g_post_mlp': 'grad_w', 'delta_g_pre_mix': 'delta_w', 'delta_w_in': 'delta_w', 'delta_b_in': 'delta_w', 'delta_w_dw': 'delta_w', 'delta_b_dw': 'delta_w', 'delta_g_conv_ln': 'delta_w', 'delta_b_conv_ln': 'delta_w', 'delta_w_sb_out': 'delta_w', 'delta_w_conv_out': 'delta_w', 'delta_w_o': 'delta_w', 'delta_g_post_mix': 'delta_w', 'delta_g_pre_mlp': 'delta_w', 'delta_w_up': 'delta_w', 'delta_w_down': 'delta_w', 'delta_g_post_mlp': 'delta_w', 'new_m_g_pre_mix': 'new_m', 'new_m_w_in': 'new_m', 'new_m_b_in': 'new_m', 'new_m_w_dw': 'new_m', 'new_m_b_dw': 'new_m', 'new_m_g_conv_ln': 'new_m', 'new_m_b_conv_ln': 'new_m', 'new_m_w_sb_out': 'new_m', 'new_m_w_conv_out': 'new_m', 'new_m_w_o': 'new_m', 'new_m_g_post_mix': 'new_m', 'new_m_g_pre_mlp': 'new_m', 'new_m_w_up': 'new_m', 'new_m_w_down': 'new_m', 'new_m_g_post_mlp': 'new_m', 'new_v_g_pre_mix': 'new_v', 'new_v_w_in': 'new_v', 'new_v_b_in': 'new_v', 'new_v_w_dw': 'new_v', 'new_v_b_dw': 'new_v', 'new_v_g_conv_ln': 'new_v', 'new_v_b_conv_ln': 'new_v', 'new_v_w_sb_out': 'new_v', 'new_v_w_conv_out': 'new_v', 'new_v_w_o': 'new_v', 'new_v_g_post_mix': 'new_v', 'new_v_g_pre_mlp': 'new_v', 'new_v_w_up': 'new_v', 'new_v_w_down': 'new_v', 'new_v_g_post_mlp': 'new_v'}


def _forward(args):
    return _fwd_reference(*[args[k] for k in FWD_PARAMS])


def _output_shape():
    out = _jax.eval_shape(lambda: _forward(_fwd_setup_inputs(0)))
    return out.shape, out.dtype

N_MICROBATCH = 1
ADAM_LR = 0.001
ADAM_B1 = 0.9
ADAM_B2 = 0.999
ADAM_EPS = 1e-08
ADAM_WD = 0.01
ADAM_STEP = 10
PER_EXAMPLE_BATCH_AXIS = {'x': 0, 'loss_target': 0}
SHARED_INPUTS = []
_WEIGHT_DTYPES = {'g_pre_mix': _jnp.float32, 'w_in': _jnp.float32, 'b_in': _jnp.float32, 'w_dw': _jnp.float32, 'b_dw': _jnp.float32, 'g_conv_ln': _jnp.float32, 'b_conv_ln': _jnp.float32, 'w_sb_out': _jnp.float32, 'w_conv_out': _jnp.float32, 'w_o': _jnp.float32, 'g_post_mix': _jnp.float32, 'g_pre_mlp': _jnp.float32, 'w_up': _jnp.float32, 'w_down': _jnp.float32, 'g_post_mlp': _jnp.float32}
MOMENT_SCALE = {'g_pre_mix': 1.974421e-01, 'w_in': 8.456589e-02, 'b_in': 1.980685e+00, 'w_dw': 2.862622e-01, 'b_dw': 4.583041e+00, 'g_conv_ln': 1.789674e+00, 'b_conv_ln': 2.628329e+00, 'w_sb_out': 1.321230e-01, 'w_conv_out': 7.285287e-01, 'w_o': 7.046593e-01, 'g_post_mix': 8.016961e+00, 'g_pre_mlp': 2.338783e-01, 'w_up': 1.187546e-01, 'w_down': 6.921773e-01, 'g_post_mlp': 8.243181e+00}


def _to_microbatches(a, axis):
    t = _jnp.moveaxis(a, axis, 0)
    t = t.reshape((N_MICROBATCH, t.shape[0] // N_MICROBATCH) + t.shape[1:])
    return _jnp.moveaxis(t, 1, axis + 1)


def setup_inputs(seed: int = 0) -> dict:
    inp = _fwd_setup_inputs(seed)
    key = _jax.random.fold_in(_jax.random.key(seed), 7919)
    shape, _ = _output_shape()
    out = dict(inp)
    out["loss_target"] = _jax.random.normal(_jax.random.fold_in(key, 0), shape, _jnp.float32)
    for i, name in enumerate(TWIN_WEIGHTS):
        w = inp[name].astype(_jnp.float32)
        if MOMENT_SCALE is None:
            s = _jnp.sqrt(_jnp.mean(_jnp.square(w)) + 1e-30)
        else:
            s = MOMENT_SCALE[name]
        km, kv = _jax.random.split(_jax.random.fold_in(key, i + 1))
        out[name] = w
        out["m_" + name] = s * _jax.random.normal(km, w.shape, _jnp.float32)
        out["v_" + name] = (s * s) * _jax.random.uniform(kv, w.shape, _jnp.float32, 0.5, 1.5)
    if N_MICROBATCH > 1:
        for name, axis in PER_EXAMPLE_BATCH_AXIS.items():
            out[name] = _to_microbatches(out[name], axis)
    return {'x': out['x'], 'g_pre_mix': out['g_pre_mix'], 'w_in': out['w_in'], 'b_in': out['b_in'], 'w_dw': out['w_dw'], 'b_dw': out['b_dw'], 'g_conv_ln': out['g_conv_ln'], 'b_conv_ln': out['b_conv_ln'], 'w_sb_out': out['w_sb_out'], 'w_conv_out': out['w_conv_out'], 'w_o': out['w_o'], 'g_post_mix': out['g_post_mix'], 'g_pre_mlp': out['g_pre_mlp'], 'w_up': out['w_up'], 'w_down': out['w_down'], 'g_post_mlp': out['g_post_mlp'], 'loss_target': out['loss_target'], 'm_g_pre_mix': out['m_g_pre_mix'], 'm_w_in': out['m_w_in'], 'm_b_in': out['m_b_in'], 'm_w_dw': out['m_w_dw'], 'm_b_dw': out['m_b_dw'], 'm_g_conv_ln': out['m_g_conv_ln'], 'm_b_conv_ln': out['m_b_conv_ln'], 'm_w_sb_out': out['m_w_sb_out'], 'm_w_conv_out': out['m_w_conv_out'], 'm_w_o': out['m_w_o'], 'm_g_post_mix': out['m_g_post_mix'], 'm_g_pre_mlp': out['m_g_pre_mlp'], 'm_w_up': out['m_w_up'], 'm_w_down': out['m_w_down'], 'm_g_post_mlp': out['m_g_post_mlp'], 'v_g_pre_mix': out['v_g_pre_mix'], 'v_w_in': out['v_w_in'], 'v_b_in': out['v_b_in'], 'v_w_dw': out['v_w_dw'], 'v_b_dw': out['v_b_dw'], 'v_g_conv_ln': out['v_g_conv_ln'], 'v_b_conv_ln': out['v_b_conv_ln'], 'v_w_sb_out': out['v_w_sb_out'], 'v_w_conv_out': out['v_w_conv_out'], 'v_w_o': out['v_w_o'], 'v_g_post_mix': out['v_g_post_mix'], 'v_g_pre_mlp': out['v_g_pre_mlp'], 'v_w_up': out['v_w_up'], 'v_w_down': out['v_w_down'], 'v_g_post_mlp': out['v_g_post_mlp']}


def _loss(weights, diff, rest, loss_target):
    with _jax.named_scope("forward"):
        args = {**rest, TWIN_DIFF_INPUT: diff, **{k: w.astype(_WEIGHT_DTYPES[k]) for k, w in weights.items()}}
        y = _forward(args)
    with _jax.named_scope("loss_head"):
        err = _jnp.square(y.astype(_jnp.float32) - loss_target)
        return 0.5 * _jnp.sum(_jnp.mean(err, axis=-1)) if err.ndim else 0.5 * err


def _adamw(w, g, m, v):
    m = ADAM_B1 * m + (1.0 - ADAM_B1) * g
    v = ADAM_B2 * v + (1.0 - ADAM_B2) * _jnp.square(g)
    m_hat = m / (1.0 - ADAM_B1 ** ADAM_STEP)
    v_hat = v / (1.0 - ADAM_B2 ** ADAM_STEP)
    delta = -ADAM_LR * (m_hat / (_jnp.sqrt(v_hat) + ADAM_EPS) + ADAM_WD * w)
    return delta, m, v


def reference(x, g_pre_mix, w_in, b_in, w_dw, b_dw, g_conv_ln, b_conv_ln, w_sb_out, w_conv_out, w_o, g_post_mix, g_pre_mlp, w_up, w_down, g_post_mlp, loss_target, m_g_pre_mix, m_w_in, m_b_in, m_w_dw, m_b_dw, m_g_conv_ln, m_b_conv_ln, m_w_sb_out, m_w_conv_out, m_w_o, m_g_post_mix, m_g_pre_mlp, m_w_up, m_w_down, m_g_post_mlp, v_g_pre_mix, v_w_in, v_b_in, v_w_dw, v_b_dw, v_g_conv_ln, v_b_conv_ln, v_w_sb_out, v_w_conv_out, v_w_o, v_g_post_mix, v_g_pre_mlp, v_w_up, v_w_down, v_g_post_mlp):
    given = dict(x=x, g_pre_mix=g_pre_mix, w_in=w_in, b_in=b_in, w_dw=w_dw, b_dw=b_dw, g_conv_ln=g_conv_ln, b_conv_ln=b_conv_ln, w_sb_out=w_sb_out, w_conv_out=w_conv_out, w_o=w_o, g_post_mix=g_post_mix, g_pre_mlp=g_pre_mlp, w_up=w_up, w_down=w_down, g_post_mlp=g_post_mlp, loss_target=loss_target, m_g_pre_mix=m_g_pre_mix, m_w_in=m_w_in, m_b_in=m_b_in, m_w_dw=m_w_dw, m_b_dw=m_b_dw, m_g_conv_ln=m_g_conv_ln, m_b_conv_ln=m_b_conv_ln, m_w_sb_out=m_w_sb_out, m_w_conv_out=m_w_conv_out, m_w_o=m_w_o, m_g_post_mix=m_g_post_mix, m_g_pre_mlp=m_g_pre_mlp, m_w_up=m_w_up, m_w_down=m_w_down, m_g_post_mlp=m_g_post_mlp, v_g_pre_mix=v_g_pre_mix, v_w_in=v_w_in, v_b_in=v_b_in, v_w_dw=v_w_dw, v_b_dw=v_b_dw, v_g_conv_ln=v_g_conv_ln, v_b_conv_ln=v_b_conv_ln, v_w_sb_out=v_w_sb_out, v_w_conv_out=v_w_conv_out, v_w_o=v_w_o, v_g_post_mix=v_g_post_mix, v_g_pre_mlp=v_g_pre_mlp, v_w_up=v_w_up, v_w_down=v_w_down, v_g_post_mlp=v_g_post_mlp)
    weights = {n: given[n] for n in TWIN_WEIGHTS}
    shared = {n: given[n] for n in SHARED_INPUTS}
    per_example = {n: given[n] for n in ['x']}
    grad_fn = _jax.value_and_grad(_loss, argnums=(0, 1))

    def one_microbatch(ex, loss_target):
        ex = dict(ex)
        diff = ex.pop(TWIN_DIFF_INPUT)
        return grad_fn(weights, diff, {**shared, **ex}, loss_target)

    if N_MICROBATCH == 1:
        loss, (grad_w, grad_x) = one_microbatch(per_example, given["loss_target"])
    else:
        def body(carry, xs):
            loss_sum, grad_sum = carry
            l_k, (gw_k, gx_k) = one_microbatch(xs[0], xs[1])
            with _jax.named_scope("update"):
                return (loss_sum + l_k, _jax.tree.map(_jnp.add, grad_sum, gw_k)), gx_k

        init = (_jnp.zeros((), _jnp.float32), _jax.tree.map(_jnp.zeros_like, weights))
        (loss, grad_w), grad_x = _jax.lax.scan(body, init, (per_example, given["loss_target"]))
    with _jax.named_scope("update"):
        delta_w, new_m, new_v = {}, {}, {}
        for n in TWIN_WEIGHTS:
            delta_w[n], new_m[n], new_v[n] = _adamw(weights[n], grad_w[n], given["m_" + n], given["v_" + n])
    return (loss, grad_x, *[grad_w[n] for n in TWIN_WEIGHTS], *[delta_w[n] for n in TWIN_WEIGHTS],
            *[new_m[n] for n in TWIN_WEIGHTS], *[new_v[n] for n in TWIN_WEIGHTS])
```

```python
import functools
import math

import jax
import jax.numpy as jnp
from jax import lax
from jax.experimental import pallas as pl
from jax.experimental.pallas import tpu as pltpu

F32 = jnp.float32
BF16 = jnp.bfloat16
NDEV = 8
LANES = 128
EPS = 1e-6
CONV_WIDTH = 31
CONV_PAD = 32
HEAD_DIM = 128
ADAM_LR = 0.001
ADAM_B1 = 0.9
ADAM_B2 = 0.999
ADAM_EPS = 1e-08
ADAM_WD = 0.01
ADAM_STEP = 10
VMEM_LIMIT = 56 * 1024 * 1024
MESH = pl.DeviceIdType.MESH
ANY = pl.BlockSpec(memory_space=pl.ANY)


def _tile(n, pref):
    t = min(n, pref)
    assert n % t == 0, (n, t)
    return t


def _params(sem):
    return pltpu.CompilerParams(dimension_semantics=sem, vmem_limit_bytes=VMEM_LIMIT)


def _sigmoid(v):
    return 1.0 / (1.0 + jnp.exp(-v))


def _matmul(name, a, b, outs, grid, a_spec, b_spec, out_specs, n_red, nt, acc_shape,
            extra=(), extra_specs=(), epilogue=None):
    n_extra = len(extra)
    n_out = len(outs)
    red_axes = tuple(range(len(grid) - n_red, len(grid)))
    red_sizes = tuple(grid[ax] for ax in red_axes)
    dims = (((1,), (1,)), ((), ())) if nt else (((1,), (0,)), ((), ()))

    def body(*refs):
        a_ref, b_ref = refs[0], refs[1]
        ex_refs = refs[2:2 + n_extra]
        o_refs = refs[2 + n_extra:2 + n_extra + n_out]
        acc_ref = refs[-1]
        ks = [pl.program_id(ax) for ax in red_axes]
        first = functools.reduce(jnp.logical_and, [k == 0 for k in ks])
        last = functools.reduce(jnp.logical_and, [k == s - 1 for k, s in zip(ks, red_sizes)])

        @pl.when(first)
        def _():
            acc_ref[...] = jnp.zeros_like(acc_ref)

        acc_ref[...] += lax.dot_general(a_ref[...].astype(BF16), b_ref[...].astype(BF16), dims,
                                        preferred_element_type=F32)

        @pl.when(last)
        def _():
            acc = acc_ref[...]
            if epilogue is None:
                vals = (acc,)
            else:
                vals = epilogue(acc, *[r[...] for r in ex_refs])
            for o_ref, val in zip(o_refs, vals):
                o_ref[...] = val.astype(o_ref.dtype)

    sem = ("parallel",) * (len(grid) - n_red) + ("arbitrary",) * n_red
    res = pl.pallas_call(
        body, name=name, grid=grid,
        in_specs=[a_spec, b_spec, *extra_specs], out_specs=list(out_specs), out_shape=list(outs),
        scratch_shapes=[pltpu.VMEM(acc_shape, F32)], compiler_params=_params(sem),
    )(a, b, *extra)
    return res


def _mm_cols(name, a, wg, bias=None, out_dtype=F32):
    m, k = a.shape
    _, _, ns = wg.shape
    tm, tk = _tile(m, 1024), _tile(k, 1024)
    grid = (m // tm, NDEV, k // tk)
    extra, extra_specs, epi = (), (), None
    if bias is not None:
        extra, extra_specs = (bias,), (pl.BlockSpec((1, ns), lambda i, d, kk: (0, d)),)
        epi = lambda acc, bv: (acc + bv,)
    return _matmul(name, a, wg, [jax.ShapeDtypeStruct((m, NDEV * ns), out_dtype)], grid,
                   pl.BlockSpec((tm, tk), lambda i, d, kk: (i, kk)),
                   pl.BlockSpec((None, tk, ns), lambda i, d, kk: (d, kk, 0)),
                   [pl.BlockSpec((tm, ns), lambda i, d, kk: (i, d))], 1, False, (tm, ns),
                   extra, extra_specs, epi)[0]


def _mm_cols_t(name, a, wg):
    m, _ = a.shape
    _, n, ns = wg.shape
    tm, tn = _tile(m, 1024), _tile(n, 1024)
    grid = (m // tm, n // tn, NDEV)
    return _matmul(name, a, wg, [jax.ShapeDtypeStruct((m, n), F32)], grid,
                   pl.BlockSpec((tm, ns), lambda i, j, d: (i, d)),
                   pl.BlockSpec((None, tn, ns), lambda i, j, d: (d, j, 0)),
                   [pl.BlockSpec((tm, tn), lambda i, j, d: (i, j))], 1, True, (tm, tn))[0]


def _mm_dw_cols(name, at, g):
    m, t = at.shape
    ns = g.shape[1] // NDEV
    tm, tk = _tile(m, 1024), _tile(t, 1024)
    grid = (m // tm, NDEV, t // tk)
    return _matmul(name, at, g, [jax.ShapeDtypeStruct((NDEV, m, ns), BF16)], grid,
                   pl.BlockSpec((tm, tk), lambda i, d, kk: (i, kk)),
                   pl.BlockSpec((tk, ns), lambda i, d, kk: (kk, d)),
                   [pl.BlockSpec((None, tm, ns), lambda i, d, kk: (d, i, 0))], 1, False, (tm, ns))[0]


def _mm_plain(name, a, b, nt, out_dtype, extra=(), extra_specs=(), epilogue=None, outs=None, out_specs=None):
    m, k = a.shape
    n = b.shape[0] if nt else b.shape[1]
    tm, tn, tk = _tile(m, 1024), _tile(n, 1024), _tile(k, 1024)
    grid = (m // tm, n // tn, k // tk)
    b_spec = (pl.BlockSpec((tn, tk), lambda i, j, kk: (j, kk)) if nt
              else pl.BlockSpec((tk, tn), lambda i, j, kk: (kk, j)))
    if outs is None:
        outs = [jax.ShapeDtypeStruct((m, n), out_dtype)]
        out_specs = [pl.BlockSpec((tm, tn), lambda i, j, kk: (i, j))]
    return _matmul(name, a, b, outs, grid, pl.BlockSpec((tm, tk), lambda i, j, kk: (i, kk)), b_spec,
                   out_specs, 1, nt, (tm, tn), extra, extra_specs, epilogue)


def _rms_rows(v):
    return lax.rsqrt(jnp.mean(v * v, axis=-1, keepdims=True) + EPS)


def _prenorm(x, g):
    s, d = x.shape
    ts = _tile(s, 256)

    def body(x_ref, g_ref, h_ref, ht_ref):
        xv = x_ref[...]
        h = xv * _rms_rows(xv) * g_ref[...]
        h_ref[...] = h.astype(BF16)
        ht_ref[...] = h.T.astype(BF16)

    return pl.pallas_call(
        body, name="prenorm", grid=(s // ts,),
        in_specs=[pl.BlockSpec((ts, d), lambda i: (i, 0)), pl.BlockSpec((1, d), lambda i: (0, 0))],
        out_specs=[pl.BlockSpec((ts, d), lambda i: (i, 0)), pl.BlockSpec((d, ts), lambda i: (0, i))],
        out_shape=[jax.ShapeDtypeStruct((s, d), BF16), jax.ShapeDtypeStruct((d, s), BF16)],
        compiler_params=_params(("parallel",)),
    )(x, g)


def _ln_silu(u1, g, b):
    s, c = u1.shape
    ts = _tile(s, 256)

    def body(u_ref, g_ref, b_ref, o_ref, ot_ref):
        u = u_ref[...]
        mu = jnp.mean(u, axis=-1, keepdims=True)
        var = jnp.mean(jnp.square(u - mu), axis=-1, keepdims=True)
        u2 = (u - mu) * lax.rsqrt(var + EPS) * g_ref[...] + b_ref[...]
        u3 = u2 * _sigmoid(u2)
        o_ref[...] = u3.astype(BF16)
        ot_ref[...] = u3.T.astype(BF16)

    return pl.pallas_call(
        body, name="ln_silu", grid=(s // ts,),
        in_specs=[pl.BlockSpec((ts, c), lambda i: (i, 0)), pl.BlockSpec((1, c), lambda i: (0, 0)),
                  pl.BlockSpec((1, c), lambda i: (0, 0))],
        out_specs=[pl.BlockSpec((ts, c), lambda i: (i, 0)), pl.BlockSpec((c, ts), lambda i: (0, i))],
        out_shape=[jax.ShapeDtypeStruct((s, c), BF16), jax.ShapeDtypeStruct((c, s), BF16)],
        compiler_params=_params(("parallel",)),
    )(u1, g, b)


def _ln_silu_bwd(du3, u1, g, b):
    s, c = u1.shape
    ts = _tile(s, 256)

    def body(d_ref, u_ref, g_ref, b_ref, du1_ref, dg_ref, db_ref):
        @pl.when(pl.program_id(0) == 0)
        def _():
            dg_ref[...] = jnp.zeros_like(dg_ref)
            db_ref[...] = jnp.zeros_like(db_ref)

        u = u_ref[...]
        mu = jnp.mean(u, axis=-1, keepdims=True)
        var = jnp.mean(jnp.square(u - mu), axis=-1, keepdims=True)
        rstd = lax.rsqrt(var + EPS)
        uhat = (u - mu) * rstd
        u2 = uhat * g_ref[...] + b_ref[...]
        sg = _sigmoid(u2)
        du2 = d_ref[...] * (sg * (1.0 + u2 * (1.0 - sg)))
        dg_ref[...] += jnp.sum(du2 * uhat, axis=0, keepdims=True)
        db_ref[...] += jnp.sum(du2, axis=0, keepdims=True)
        duh = du2 * g_ref[...]
        du1_ref[...] = rstd * (duh - jnp.mean(duh, axis=-1, keepdims=True)
                               - uhat * jnp.mean(duh * uhat, axis=-1, keepdims=True))

    row = pl.BlockSpec((ts, c), lambda i: (i, 0))
    vec = pl.BlockSpec((1, c), lambda i: (0, 0))
    return pl.pallas_call(
        body, name="ln_silu_bwd", grid=(s // ts,),
        in_specs=[row, row, vec, vec], out_specs=[row, vec, vec],
        out_shape=[jax.ShapeDtypeStruct((s, c), F32), jax.ShapeDtypeStruct((1, c), F32),
                   jax.ShapeDtypeStruct((1, c), F32)],
        compiler_params=_params(("arbitrary",)),
    )(du3, u1, g, b)


def _merge(proj, o_sbp, o_cv, d):
    s = proj.shape[0]
    w = d // 2
    ts = _tile(s, 256)

    def body(gs_ref, gc_ref, a_ref, b_ref, m_ref, mt_ref):
        mg = _sigmoid(gs_ref[...]) * a_ref[...] + _sigmoid(gc_ref[...]) * b_ref[...]
        m_ref[...] = mg.astype(BF16)
        mt_ref[...] = mg.T.astype(BF16)

    blk = pl.BlockSpec((ts, w), lambda i, j: (i, j))
    return pl.pallas_call(
        body, name="merge", grid=(s // ts, 2),
        in_specs=[pl.BlockSpec((ts, w), lambda i, j: (i, 5 + j)), pl.BlockSpec((ts, w), lambda i, j: (i, 7 + j)),
                  blk, blk],
        out_specs=[blk, pl.BlockSpec((w, ts), lambda i, j: (j, i))],
        out_shape=[jax.ShapeDtypeStruct((s, d), BF16), jax.ShapeDtypeStruct((d, s), BF16)],
        compiler_params=_params(("parallel", "parallel")),
    )(proj, proj, o_sbp, o_cv)


def _merge_bwd(dmerged, proj, o_sbp, o_cv, d):
    s = proj.shape[0]
    w = d // 2
    ts = _tile(s, 256)

    def body(dm_ref, gs_ref, gc_ref, a_ref, b_ref, da_ref, db_ref, dgs_ref, dgc_ref):
        dm = dm_ref[...]
        ss = _sigmoid(gs_ref[...])
        sc = _sigmoid(gc_ref[...])
        da_ref[...] = (dm * ss).astype(BF16)
        db_ref[...] = (dm * sc).astype(BF16)
        dgs_ref[...] = (dm * a_ref[...] * ss * (1.0 - ss)).astype(BF16)
        dgc_ref[...] = (dm * b_ref[...] * sc * (1.0 - sc)).astype(BF16)

    blk = pl.BlockSpec((ts, w), lambda i, j: (i, j))
    sds = jax.ShapeDtypeStruct((s, d), BF16)
    return pl.pallas_call(
        body, name="merge_bwd", grid=(s // ts, 2),
        in_specs=[blk, pl.BlockSpec((ts, w), lambda i, j: (i, 5 + j)), pl.BlockSpec((ts, w), lambda i, j: (i, 7 + j)),
                  blk, blk],
        out_specs=[blk, blk, blk, blk], out_shape=[sds, sds, sds, sds],
        compiler_params=_params(("parallel", "parallel")),
    )(dmerged, proj, proj, o_sbp, o_cv)


def _postnorm_mix(x, y, g_post, g_pre):
    s, d = x.shape
    ts = _tile(s, 256)

    def body(x_ref, y_ref, gp_ref, gn_ref, x1_ref, h_ref, ht_ref):
        yv = y_ref[...]
        x1 = x_ref[...] + yv * _rms_rows(yv) * gp_ref[...]
        x1_ref[...] = x1
        h = x1 * _rms_rows(x1) * gn_ref[...]
        h_ref[...] = h.astype(BF16)
        ht_ref[...] = h.T.astype(BF16)

    row = pl.BlockSpec((ts, d), lambda i: (i, 0))
    vec = pl.BlockSpec((1, d), lambda i: (0, 0))
    return pl.pallas_call(
        body, name="postnorm_mix", grid=(s // ts,),
        in_specs=[row, row, vec, vec], out_specs=[row, row, pl.BlockSpec((d, ts), lambda i: (0, i))],
        out_shape=[jax.ShapeDtypeStruct((s, d), F32), jax.ShapeDtypeStruct((s, d), BF16),
                   jax.ShapeDtypeStruct((d, s), BF16)],
        compiler_params=_params(("parallel",)),
    )(x, y, g_post, g_pre)


def _rms_bwd(dout, vin, g):
    r = _rms_rows(vin)
    vhat = vin * r
    dyh = dout * g
    dvin = r * (dyh - vhat * jnp.mean(dyh * vhat, axis=-1, keepdims=True))
    return dvin, jnp.sum(dout * vhat, axis=0, keepdims=True)


def _loss_head(x1, f2, tgt, g):
    s, d = x1.shape
    ts = _tile(s, 256)

    def body(x1_ref, f_ref, t_ref, g_ref, dx2_ref, df2_ref, dg_ref, loss_ref):
        @pl.when(pl.program_id(0) == 0)
        def _():
            dg_ref[...] = jnp.zeros_like(dg_ref)
            loss_ref[...] = jnp.zeros_like(loss_ref)

        fv = f_ref[...]
        x2 = x1_ref[...] + fv * _rms_rows(fv) * g_ref[...]
        err = x2 - t_ref[...]
        loss_ref[...] += 0.5 * jnp.sum(jnp.mean(err * err, axis=-1, keepdims=True), axis=0, keepdims=True)
        dx2 = err * (1.0 / d)
        dx2_ref[...] = dx2
        df2, dg = _rms_bwd(dx2, fv, g_ref[...])
        df2_ref[...] = df2.astype(BF16)
        dg_ref[...] += dg

    row = pl.BlockSpec((ts, d), lambda i: (i, 0))
    vec = pl.BlockSpec((1, d), lambda i: (0, 0))
    return pl.pallas_call(
        body, name="loss_head", grid=(s // ts,),
        in_specs=[row, row, row, vec],
        out_specs=[row, row, vec, pl.BlockSpec((1, LANES), lambda i: (0, 0))],
        out_shape=[jax.ShapeDtypeStruct((s, d), F32), jax.ShapeDtypeStruct((s, d), BF16),
                   jax.ShapeDtypeStruct((1, d), F32), jax.ShapeDtypeStruct((1, LANES), F32)],
        compiler_params=_params(("arbitrary",)),
    )(x1, f2, tgt, g)


def _midnorm_bwd(dx2, dh2, x1, y, g_post, g_pre):
    s, d = x1.shape
    ts = _tile(s, 256)

    def body(dx2_ref, dh_ref, x1_ref, y_ref, gp_ref, gn_ref, dx1_ref, dy_ref, dgn_ref, dgp_ref):
        @pl.when(pl.program_id(0) == 0)
        def _():
            dgn_ref[...] = jnp.zeros_like(dgn_ref)
            dgp_ref[...] = jnp.zeros_like(dgp_ref)

        dxa, dgn = _rms_bwd(dh_ref[...], x1_ref[...], gn_ref[...])
        dx1 = dx2_ref[...] + dxa
        dx1_ref[...] = dx1
        dy, dgp = _rms_bwd(dx1, y_ref[...], gp_ref[...])
        dy_ref[...] = dy.astype(BF16)
        dgn_ref[...] += dgn
        dgp_ref[...] += dgp

    row = pl.BlockSpec((ts, d), lambda i: (i, 0))
    vec = pl.BlockSpec((1, d), lambda i: (0, 0))
    return pl.pallas_call(
        body, name="midnorm_bwd", grid=(s // ts,),
        in_specs=[row, row, row, row, vec, vec], out_specs=[row, row, vec, vec],
        out_shape=[jax.ShapeDtypeStruct((s, d), F32), jax.ShapeDtypeStruct((s, d), BF16),
                   jax.ShapeDtypeStruct((1, d), F32), jax.ShapeDtypeStruct((1, d), F32)],
        compiler_params=_params(("arbitrary",)),
    )(dx2, dh2, x1, y, g_post, g_pre)


def _prenorm_bwd(dx1, dh, x, g):
    s, d = x.shape
    ts = _tile(s, 256)

    def body(dx1_ref, dh_ref, x_ref, g_ref, dx_ref, dg_ref):
        @pl.when(pl.program_id(0) == 0)
        def _():
            dg_ref[...] = jnp.zeros_like(dg_ref)

        dxa, dg = _rms_bwd(dh_ref[...], x_ref[...], g_ref[...])
        dx_ref[...] = dx1_ref[...] + dxa
        dg_ref[...] += dg

    row = pl.BlockSpec((ts, d), lambda i: (i, 0))
    vec = pl.BlockSpec((1, d), lambda i: (0, 0))
    return pl.pallas_call(
        body, name="prenorm_bwd", grid=(s // ts,),
        in_specs=[row, row, row, vec], out_specs=[row, vec],
        out_shape=[jax.ShapeDtypeStruct((s, d), F32), jax.ShapeDtypeStruct((1, d), F32)],
        compiler_params=_params(("arbitrary",)),
    )(dx1, dh, x, g)


def _colsum(a):
    s, n = a.shape
    ts = _tile(s, 256)

    def body(a_ref, o_ref):
        @pl.when(pl.program_id(0) == 0)
        def _():
            o_ref[...] = jnp.zeros_like(o_ref)

        o_ref[...] += jnp.sum(a_ref[...].astype(F32), axis=0, keepdims=True)

    return pl.pallas_call(
        body, name="colsum", grid=(s // ts,),
        in_specs=[pl.BlockSpec((ts, n), lambda i: (i, 0))], out_specs=pl.BlockSpec((1, n), lambda i: (0, 0)),
        out_shape=jax.ShapeDtypeStruct((1, n), F32), compiler_params=_params(("arbitrary",)),
    )(a)


def _shift_rows(win, off, t):
    n = win.shape[0]
    if off == 0:
        return win[:t]
    return pltpu.roll(win, n - off, axis=0)[:t]


def _conv_fwd(proj, w_pad, b_dw, c_total):
    s = proj.shape[0]
    nct = c_total // LANES
    t = _tile(s, 256)

    def body(ga_ref, gb_ref, w_ref, b_ref, o_ref, u0_ref):
        u0_ref[pl.ds(0, CONV_PAD), :] = jnp.zeros((CONV_PAD, LANES), F32)
        u0_ref[pl.ds(CONV_PAD, s), :] = ga_ref[...] * _sigmoid(gb_ref[...])
        wv = w_ref[...]

        def chunk(r, carry):
            r0 = pl.multiple_of(r * t, t)
            win = u0_ref[pl.ds(r0, t + CONV_PAD), :]
            acc = jnp.broadcast_to(b_ref[...], (t, LANES))
            for j in range(CONV_WIDTH):
                acc = acc + wv[j:j + 1, :] * _shift_rows(win, j + CONV_PAD - (CONV_WIDTH - 1), t)
            o_ref[pl.ds(r0, t), :] = acc
            return carry

        lax.fori_loop(0, s // t, chunk, 0)

    return pl.pallas_call(
        body, name="conv_fwd", grid=(nct,),
        in_specs=[pl.BlockSpec((s, LANES), lambda c: (0, 3 * nct + c)), pl.BlockSpec((s, LANES), lambda c: (0, 4 * nct + c)),
                  pl.BlockSpec((CONV_PAD, LANES), lambda c: (0, c)), pl.BlockSpec((1, LANES), lambda c: (0, c))],
        out_specs=pl.BlockSpec((s, LANES), lambda c: (0, c)),
        out_shape=jax.ShapeDtypeStruct((s, c_total), F32),
        scratch_shapes=[pltpu.VMEM((s + CONV_PAD, LANES), F32)],
        compiler_params=_params(("parallel",)),
    )(proj, proj, w_pad, b_dw)


def _conv_bwd(du1, proj, w_pad, c_total):
    s = proj.shape[0]
    nct = c_total // LANES
    t = _tile(s, 256)

    def body(d_ref, ga_ref, gb_ref, w_ref, dga_ref, dgb_ref, dw_ref, db_ref, u0_ref, dp_ref):
        sg = _sigmoid(gb_ref[...])
        u0_ref[pl.ds(0, CONV_PAD), :] = jnp.zeros((CONV_PAD, LANES), F32)
        u0_ref[pl.ds(CONV_PAD, s), :] = ga_ref[...] * sg
        dp_ref[pl.ds(0, s), :] = d_ref[...]
        dp_ref[pl.ds(s, CONV_PAD), :] = jnp.zeros((CONV_PAD, LANES), F32)
        dw_ref[...] = jnp.zeros_like(dw_ref)
        db_ref[...] = jnp.sum(d_ref[...], axis=0, keepdims=True)
        wv = w_ref[...]

        def chunk(r, carry):
            r0 = pl.multiple_of(r * t, t)
            win = u0_ref[pl.ds(r0, t + CONV_PAD), :]
            dwin = dp_ref[pl.ds(r0, t + CONV_PAD), :]
            dcur = dwin[:t]
            du0 = jnp.zeros((t, LANES), F32)
            for j in range(CONV_WIDTH):
                du0 = du0 + wv[j:j + 1, :] * _shift_rows(dwin, CONV_WIDTH - 1 - j, t)
                sh = _shift_rows(win, j + CONV_PAD - (CONV_WIDTH - 1), t)
                dw_ref[j:j + 1, :] += jnp.sum(dcur * sh, axis=0, keepdims=True)
            gav = ga_ref[pl.ds(r0, t), :]
            sgv = _sigmoid(gb_ref[pl.ds(r0, t), :])
            dga_ref[pl.ds(r0, t), :] = (du0 * sgv).astype(BF16)
            dgb_ref[pl.ds(r0, t), :] = (du0 * gav * sgv * (1.0 - sgv)).astype(BF16)
            return carry

        lax.fori_loop(0, s // t, chunk, 0)

    col = pl.BlockSpec((s, LANES), lambda c: (0, c))
    return pl.pallas_call(
        body, name="conv_bwd", grid=(nct,),
        in_specs=[col, pl.BlockSpec((s, LANES), lambda c: (0, 3 * nct + c)),
                  pl.BlockSpec((s, LANES), lambda c: (0, 4 * nct + c)), pl.BlockSpec((CONV_PAD, LANES), lambda c: (0, c))],
        out_specs=[col, col, pl.BlockSpec((CONV_PAD, LANES), lambda c: (0, c)), pl.BlockSpec((1, LANES), lambda c: (0, c))],
        out_shape=[jax.ShapeDtypeStruct((s, c_total), BF16), jax.ShapeDtypeStruct((s, c_total), BF16),
                   jax.ShapeDtypeStruct((CONV_PAD, c_total), F32), jax.ShapeDtypeStruct((1, c_total), F32)],
        scratch_shapes=[pltpu.VMEM((s + CONV_PAD, LANES), F32), pltpu.VMEM((s + CONV_PAD, LANES), F32)],
        compiler_params=_params(("parallel",)),
    )(du1, proj, proj, w_pad)


TQ_PREF = 256
TK = 128


def _split_dot(v, tri):
    hi = v.astype(BF16)
    lo = (v - hi.astype(F32)).astype(BF16)
    return (jnp.dot(hi, tri, preferred_element_type=F32) + jnp.dot(lo, tri, preferred_element_type=F32))


def _sb_block(q, k_ref, j, i, tq, scale, c_lm, tri_after):
    kb = k_ref[pl.ds(pl.multiple_of(j * TK, TK), TK), :].astype(BF16)
    z = lax.dot_general(q, kb, (((1,), (1,)), ((), ())), preferred_element_type=F32) * scale
    tpos = i * tq + lax.broadcasted_iota(jnp.int32, (tq, TK), 0)
    spos = j * TK + lax.broadcasted_iota(jnp.int32, (tq, TK), 1)
    mask = spos < tpos
    sp = jnp.log(1.0 + jnp.exp(-jnp.abs(z)))
    lsz = jnp.minimum(z, 0.0) - sp
    lm = jnp.where(mask, -jnp.maximum(z, 0.0) - sp, 0.0)
    suf = c_lm + _split_dot(lm, tri_after)
    a = jnp.where(mask, jnp.exp(lsz + suf), 0.0)
    return lsz, lm, a, mask


def _tri(after):
    r = lax.broadcasted_iota(jnp.int32, (TK, TK), 0)
    c = lax.broadcasted_iota(jnp.int32, (TK, TK), 1)
    return (r > c).astype(BF16) if after else (r < c).astype(BF16)


def _attn_fwd(proj, n_heads):
    s = proj.shape[0]
    tq = _tile(s, TQ_PREF)
    scale = 1.0 / math.sqrt(HEAD_DIM)
    ratio = tq // TK

    def body(q_ref, k_ref, v_ref, o_ref, ot_ref, acc_ref, clm_ref):
        i = pl.program_id(1)
        q = q_ref[...].astype(BF16)
        tri_after = _tri(True)
        acc_ref[...] = jnp.zeros_like(acc_ref)
        clm_ref[...] = jnp.zeros_like(clm_ref)
        nkb = (i + 1) * ratio

        def step(jj, carry):
            j = nkb - 1 - jj
            _, lm, a, _ = _sb_block(q, k_ref, j, i, tq, scale, clm_ref[...], tri_after)
            vb = v_ref[pl.ds(pl.multiple_of(j * TK, TK), TK), :].astype(BF16)
            acc_ref[...] += jnp.dot(a.astype(BF16), vb, preferred_element_type=F32)
            clm_ref[...] += jnp.sum(lm, axis=1, keepdims=True)
            return carry

        lax.fori_loop(0, nkb, step, 0)
        o = acc_ref[...]
        o_ref[...] = o
        ot_ref[...] = o.T.astype(BF16)

    return pl.pallas_call(
        body, name="attn_fwd", grid=(n_heads, s // tq),
        in_specs=[pl.BlockSpec((tq, HEAD_DIM), lambda h, i: (i, h)),
                  pl.BlockSpec((s, HEAD_DIM), lambda h, i: (0, n_heads + h)),
                  pl.BlockSpec((s, HEAD_DIM), lambda h, i: (0, 2 * n_heads + h))],
        out_specs=[pl.BlockSpec((tq, HEAD_DIM), lambda h, i: (i, h)), pl.BlockSpec((HEAD_DIM, tq), lambda h, i: (h, i))],
        out_shape=[jax.ShapeDtypeStruct((s, n_heads * HEAD_DIM), F32), jax.ShapeDtypeStruct((n_heads * HEAD_DIM, s), BF16)],
        scratch_shapes=[pltpu.VMEM((tq, HEAD_DIM), F32), pltpu.VMEM((tq, 1), F32)],
        compiler_params=_params(("parallel", "arbitrary")),
    )(proj, proj, proj)


def _attn_bwd(proj, do_sb, n_heads):
    s = proj.shape[0]
    tq = _tile(s, TQ_PREF)
    scale = 1.0 / math.sqrt(HEAD_DIM)
    ratio = tq // TK
    n_kb = s // TK
    n_qb = s // tq

    def body(q_ref, k_ref, v_ref, do_ref, dq_ref, dk_ref, dv_ref, dka_ref, dva_ref, dl_ref, be_ref, dqa_ref, c_ref):
        i = pl.program_id(1)

        @pl.when(i == 0)
        def _():
            dka_ref[...] = jnp.zeros_like(dka_ref)
            dva_ref[...] = jnp.zeros_like(dva_ref)

        q = q_ref[...].astype(BF16)
        dob = do_ref[...].astype(BF16)
        tri_after = _tri(True)
        tri_before = _tri(False)
        nkb = (i + 1) * ratio

        c_ref[...] = jnp.zeros_like(c_ref)

        def sweep_a(jj, carry):
            j = nkb - 1 - jj
            lsz, lm, a, _ = _sb_block(q, k_ref, j, i, tq, scale, c_ref[...], tri_after)
            rows = pl.ds(pl.multiple_of(j * TK, TK), TK)
            vb = v_ref[rows, :].astype(BF16)
            da = lax.dot_general(dob, vb, (((1,), (1,)), ((), ())), preferred_element_type=F32)
            dl_ref[j] = da * a
            be_ref[j] = jnp.exp(lsz)
            dva_ref[rows, :] += jnp.dot(a.T.astype(BF16), dob, preferred_element_type=F32)
            c_ref[...] += jnp.sum(lm, axis=1, keepdims=True)
            return carry

        lax.fori_loop(0, nkb, sweep_a, 0)

        c_ref[...] = jnp.zeros_like(c_ref)
        dqa_ref[...] = jnp.zeros_like(dqa_ref)

        def sweep_b(j, carry):
            dl = dl_ref[j]
            beta = be_ref[j]
            tpos = i * tq + lax.broadcasted_iota(jnp.int32, (tq, TK), 0)
            spos = j * TK + lax.broadcasted_iota(jnp.int32, (tq, TK), 1)
            p = c_ref[...] + _split_dot(dl, tri_before)
            dz = jnp.where(spos < tpos, (dl * (1.0 - beta) - beta * p) * scale, 0.0)
            rows = pl.ds(pl.multiple_of(j * TK, TK), TK)
            kb = k_ref[rows, :].astype(BF16)
            dqa_ref[...] += jnp.dot(dz.astype(BF16), kb, preferred_element_type=F32)
            dka_ref[rows, :] += jnp.dot(dz.T.astype(BF16), q, preferred_element_type=F32)
            c_ref[...] += jnp.sum(dl, axis=1, keepdims=True)
            return carry

        lax.fori_loop(0, nkb, sweep_b, 0)
        dq_ref[...] = dqa_ref[...].astype(BF16)

        @pl.when(i == n_qb - 1)
        def _():
            dk_ref[...] = dka_ref[...].astype(BF16)
            dv_ref[...] = dva_ref[...].astype(BF16)

    qblk = pl.BlockSpec((tq, HEAD_DIM), lambda h, i: (i, h))
    full = pl.BlockSpec((s, HEAD_DIM), lambda h, i: (0, h))
    sds = jax.ShapeDtypeStruct((s, n_heads * HEAD_DIM), BF16)
    return pl.pallas_call(
        body, name="attn_bwd", grid=(n_heads, n_qb),
        in_specs=[qblk, pl.BlockSpec((s, HEAD_DIM), lambda h, i: (0, n_heads + h)),
                  pl.BlockSpec((s, HEAD_DIM), lambda h, i: (0, 2 * n_heads + h)), qblk],
        out_specs=[qblk, full, full], out_shape=[sds, sds, sds],
        scratch_shapes=[pltpu.VMEM((s, HEAD_DIM), F32), pltpu.VMEM((s, HEAD_DIM), F32),
                        pltpu.VMEM((n_kb, tq, TK), F32), pltpu.VMEM((n_kb, tq, TK), F32),
                        pltpu.VMEM((tq, HEAD_DIM), F32), pltpu.VMEM((tq, 1), F32)],
        compiler_params=_params(("parallel", "arbitrary")),
    )(proj, proj, proj, do_sb)


def _position():
    return lax.axis_index("x"), lax.axis_index("y"), lax.axis_index("c")


def _all_gather_weights(shards):
    n = len(shards)

    def body(*refs):
        ins, outs = refs[:n], refs[n:2 * n]
        send_sems, recv_sems, local_sems = refs[2 * n:]
        x, y, c = _position()
        me, sibling = (x, y, c), (x, y, 1 - c)
        chips = [(1 - x, y), (x, 1 - y), (1 - x, 1 - y)]

        def copy(a, k, block, to, src=None):
            px, py, pc = block
            dst = outs[a].at[4 * px + 2 * py + pc]
            return pltpu.make_async_remote_copy(src_ref=dst if src is None else src, dst_ref=dst,
                                                send_sem=send_sems.at[a, k], recv_sem=recv_sems.at[a, k],
                                                device_id=to, device_id_type=MESH)

        mine = [pltpu.make_async_copy(ins[a], outs[a].at[4 * x + 2 * y + c], local_sems.at[a]) for a in range(n)]
        for cp in mine:
            cp.start()
        started = []
        for a in range(n):
            first = [copy(a, 0, me, sibling, src=ins[a])]
            first += [copy(a, 1 + j, me, (*chip, c), src=ins[a]) for j, chip in enumerate(chips)]
            for cp in first:
                cp.start()
            started += first
        for a in range(n):
            for j, chip in enumerate(chips):
                copy(a, 1 + j, (*chip, c), me).wait_recv()
                cp = copy(a, 4 + j, (*chip, c), sibling)
                cp.start()
                started.append(cp)
        for a in range(n):
            copy(a, 0, sibling, me).wait_recv()
            for j, chip in enumerate(chips):
                copy(a, 4 + j, (*chip, 1 - c), me).wait_recv()
        for cp in started:
            cp.wait_send()
        for cp in mine:
            cp.wait()

    return pl.pallas_call(
        body, name="all_gather_weights",
        in_specs=[ANY] * n, out_specs=[ANY] * n,
        out_shape=[jax.ShapeDtypeStruct((NDEV, *sh.shape), sh.dtype) for sh in shards],
        scratch_shapes=[pltpu.SemaphoreType.DMA((n, 7)), pltpu.SemaphoreType.DMA((n, 7)), pltpu.SemaphoreType.DMA((n,))],
    )(*shards)


def _all_gather_small(part):
    def body(in_ref, out_ref, send_sems, recv_sems, local_sem):
        x, y, c = _position()
        me = 4 * x + 2 * y + c
        mine = pltpu.make_async_copy(in_ref, out_ref.at[me], local_sem)
        mine.start()
        flips = [(fx, fy, fc) for fx in (0, 1) for fy in (0, 1) for fc in (0, 1)][1:]
        copies = []
        for k, (fx, fy, fc) in enumerate(flips):
            cp = pltpu.make_async_remote_copy(src_ref=in_ref, dst_ref=out_ref.at[me], send_sem=send_sems.at[k],
                                              recv_sem=recv_sems.at[k],
                                              device_id=(x ^ fx, y ^ fy, c ^ fc), device_id_type=MESH)
            cp.start()
            copies.append(cp)
        for k, (fx, fy, fc) in enumerate(flips):
            peer = 4 * (x ^ fx) + 2 * (y ^ fy) + (c ^ fc)
            pltpu.make_async_remote_copy(src_ref=in_ref, dst_ref=out_ref.at[peer], send_sem=send_sems.at[k],
                                         recv_sem=recv_sems.at[k], device_id=(x, y, c), device_id_type=MESH).wait_recv()
        for cp in copies:
            cp.wait_send()
        mine.wait()

    return pl.pallas_call(
        body, name="all_gather_small", in_specs=[ANY], out_specs=ANY,
        out_shape=jax.ShapeDtypeStruct((NDEV, *part.shape), part.dtype),
        scratch_shapes=[pltpu.SemaphoreType.DMA((7,)), pltpu.SemaphoreType.DMA((7,)), pltpu.SemaphoreType.DMA],
    )(part)


def _exchange_sibling(grads):
    n = len(grads)

    def body(*refs):
        ins, outs = refs[:n], refs[n:2 * n]
        send_sems, recv_sems = refs[2 * n:]
        x, y, c = _position()
        copies = []
        for a in range(n):
            for k in range(4):
                cp = pltpu.make_async_remote_copy(src_ref=ins[a].at[k, 1 - c], dst_ref=outs[a].at[k],
                                                  send_sem=send_sems.at[a, k], recv_sem=recv_sems.at[a, k],
                                                  device_id=(x, y, 1 - c), device_id_type=MESH)
                cp.start()
                copies.append(cp)
        for cp in copies:
            cp.wait()

    return pl.pallas_call(
        body, name="exchange_sibling", in_specs=[ANY] * n, out_specs=[ANY] * n,
        out_shape=[jax.ShapeDtypeStruct((4, *g.shape[2:]), g.dtype) for g in grads],
        scratch_shapes=[pltpu.SemaphoreType.DMA((n, 4)), pltpu.SemaphoreType.DMA((n, 4))],
    )(*grads)


def _pair_sum(g, recv, core):
    _, _, r, c = g.shape
    tr = _tile(r, 256)

    def body(core_ref, g_ref, r_ref, o_ref):
        o_ref[...] = (g_ref[...].astype(F32) + r_ref[...].astype(F32)).astype(o_ref.dtype)

    return pl.pallas_call(
        body, name="pair_sum",
        grid_spec=pltpu.PrefetchScalarGridSpec(
            num_scalar_prefetch=1, grid=(4, r // tr),
            in_specs=[pl.BlockSpec((None, None, tr, c), lambda k, i, core_ref: (k, core_ref[0], i, 0)),
                      pl.BlockSpec((None, tr, c), lambda k, i, core_ref: (k, i, 0))],
            out_specs=pl.BlockSpec((None, tr, c), lambda k, i, core_ref: (k, i, 0))),
        out_shape=jax.ShapeDtypeStruct((4, r, c), g.dtype),
        compiler_params=_params(("parallel", "parallel")),
    )(core, g, recv)


def _exchange_chips(sums):
    n = len(sums)

    def body(*refs):
        ins, outs = refs[:n], refs[n:2 * n]
        send_sems, recv_sems, local_sems = refs[2 * n:]
        x, y, c = _position()
        chips = [(1 - x, y), (x, 1 - y), (1 - x, 1 - y)]
        own = [pltpu.make_async_copy(ins[a].at[2 * x + y], outs[a].at[3], local_sems.at[a]) for a in range(n)]
        for cp in own:
            cp.start()
        copies = []
        for a in range(n):
            for j, (px, py) in enumerate(chips):
                cp = pltpu.make_async_remote_copy(src_ref=ins[a].at[2 * px + py], dst_ref=outs[a].at[j],
                                                  send_sem=send_sems.at[a, j], recv_sem=recv_sems.at[a, j],
                                                  device_id=(px, py, c), device_id_type=MESH)
                cp.start()
                copies.append(cp)
        for cp in copies:
            cp.wait()
        for cp in own:
            cp.wait()

    return pl.pallas_call(
        body, name="exchange_chips", in_specs=[ANY] * n, out_specs=[ANY] * n,
        out_shape=[jax.ShapeDtypeStruct(sm.shape, sm.dtype) for sm in sums],
        scratch_shapes=[pltpu.SemaphoreType.DMA((n, 3)), pltpu.SemaphoreType.DMA((n, 3)), pltpu.SemaphoreType.DMA((n,))],
    )(*sums)


def _sum_small(gathered):
    _, r, l = gathered.shape

    def body(g_ref, o_ref):
        acc = g_ref[0]
        for d in range(1, NDEV):
            acc = acc + g_ref[d]
        o_ref[...] = acc

    return pl.pallas_call(
        body, name="sum_small", in_specs=[pl.BlockSpec((NDEV, r, l), lambda: (0, 0, 0))],
        out_specs=pl.BlockSpec((r, l), lambda: (0, 0)), out_shape=jax.ShapeDtypeStruct((r, l), F32),
    )(gathered)


def _adamw(name, w, m, v, parts, part_specs, tr):
    r, c = w.shape
    n_parts = len(parts)

    def body(*refs):
        w_ref, m_ref, v_ref = refs[:3]
        p_refs = refs[3:3 + n_parts]
        g_ref, d_ref, nm_ref, nv_ref = refs[3 + n_parts:]
        g = p_refs[0][...].astype(F32)
        for p in p_refs[1:]:
            g = g + p[...].astype(F32)
        nm = ADAM_B1 * m_ref[...] + (1.0 - ADAM_B1) * g
        nv = ADAM_B2 * v_ref[...] + (1.0 - ADAM_B2) * jnp.square(g)
        m_hat = nm / (1.0 - ADAM_B1 ** ADAM_STEP)
        v_hat = nv / (1.0 - ADAM_B2 ** ADAM_STEP)
        g_ref[...] = g
        d_ref[...] = -ADAM_LR * (m_hat / (jnp.sqrt(v_hat) + ADAM_EPS) + ADAM_WD * w_ref[...])
        nm_ref[...] = nm
        nv_ref[...] = nv

    blk = pl.BlockSpec((tr, c), lambda i: (i, 0))
    sds = jax.ShapeDtypeStruct((r, c), F32)
    return pl.pallas_call(
        body, name=name, grid=(r // tr,),
        in_specs=[blk, blk, blk, *part_specs], out_specs=[blk] * 4, out_shape=[sds] * 4,
        compiler_params=_params(("parallel",)),
    )(w, m, v, *parts)


def _adamw_big(name, w, m, v, recv):
    r, c = w.shape
    tr = _tile(r, 128)
    order = (3, 0, 1, 2)
    specs = [pl.BlockSpec((None, tr, c), functools.partial(lambda i, slot: (slot, i, 0), slot=sl)) for sl in order]
    return _adamw(name, w, m, v, [recv] * 4, specs, tr)


def _adamw_small(name, w, m, v, g):
    r, c = w.shape
    return _adamw(name, w, m, v, [g], [pl.BlockSpec((r, c), lambda i: (0, 0))], r)


def kernel(x, g_pre_mix, w_in, b_in, w_dw, b_dw, g_conv_ln, b_conv_ln, w_sb_out, w_conv_out, w_o, g_post_mix, g_pre_mlp, w_up, w_down, g_post_mlp, loss_target, m_g_pre_mix, m_w_in, m_b_in, m_w_dw, m_b_dw, m_g_conv_ln, m_b_conv_ln, m_w_sb_out, m_w_conv_out, m_w_o, m_g_post_mix, m_g_pre_mlp, m_w_up, m_w_down, m_g_post_mlp, v_g_pre_mix, v_w_in, v_b_in, v_w_dw, v_b_dw, v_g_conv_ln, v_b_conv_ln, v_w_sb_out, v_w_conv_out, v_w_o, v_g_post_mix, v_g_pre_mlp, v_w_up, v_w_down, v_g_post_mlp):
    xs, tgt = x[0], loss_target[0]
    s, d = xs.shape
    d_half = d // 2
    n_heads = d_half // HEAD_DIM
    d_ff = NDEV * w_up.shape[2]
    core = lax.axis_index("c").astype(jnp.int32).reshape(1)
    dev = 4 * lax.axis_index("x") + 2 * lax.axis_index("y") + lax.axis_index("c")

    w_dw_pad = jnp.pad(w_dw[0], ((0, CONV_PAD - CONV_WIDTH), (0, 0)))
    shards = [w_in[0].astype(BF16), w_sb_out[0].astype(BF16), w_conv_out[0].astype(BF16), w_o[0].astype(BF16),
              w_up[0].astype(BF16), w_down[0].astype(BF16), w_dw_pad]
    wg_in, wg_sb, wg_cv, wg_o, wg_up, wg_down, wg_dw = _all_gather_weights(shards)
    wf_o = wg_o.reshape(d, d)
    wf_down = wg_down.reshape(d_ff, d)
    wf_dw = wg_dw.transpose(1, 0, 2).reshape(CONV_PAD, d_half)

    h, h_t = _prenorm(xs, g_pre_mix)
    proj = _mm_cols("proj", h, wg_in, bias=b_in)
    o_sb, o_sb_t = _attn_fwd(proj, n_heads)
    u1 = _conv_fwd(proj, wf_dw, b_dw, d_half)
    u3, u3_t = _ln_silu(u1, g_conv_ln, b_conv_ln)
    o_sbp = _mm_cols("sb_out", o_sb, wg_sb)
    o_cv = _mm_cols("conv_out", u3, wg_cv)
    merged, merged_t = _merge(proj, o_sbp, o_cv, d)
    y = _mm_plain("w_o", merged, wf_o, False, F32)[0]
    x1, h2, h2_t = _postnorm_mix(xs, y, g_post_mix, g_pre_mlp)

    tm_up = _tile(s, 1024)
    ns_up = wg_up.shape[2]

    def up_epilogue(acc):
        f = jnp.square(jnp.maximum(acc, 0.0))
        return acc, f, f.T

    a_act, f, f_t = _matmul(
        "w_up", h2, wg_up,
        [jax.ShapeDtypeStruct((s, d_ff), BF16), jax.ShapeDtypeStruct((s, d_ff), BF16), jax.ShapeDtypeStruct((d_ff, s), BF16)],
        (s // tm_up, NDEV, d // _tile(d, 1024)),
        pl.BlockSpec((tm_up, _tile(d, 1024)), lambda i, dd, kk: (i, kk)),
        pl.BlockSpec((None, _tile(d, 1024), ns_up), lambda i, dd, kk: (dd, kk, 0)),
        [pl.BlockSpec((tm_up, ns_up), lambda i, dd, kk: (i, dd)), pl.BlockSpec((tm_up, ns_up), lambda i, dd, kk: (i, dd)),
         pl.BlockSpec((ns_up, tm_up), lambda i, dd, kk: (dd, i))],
        1, False, (tm_up, ns_up), epilogue=up_epilogue)
    f2 = _mm_plain("w_down", f, wf_down, False, F32)[0]
    dx2, df2, dg_post_mlp, loss_part = _loss_head(x1, f2, tgt, g_post_mlp)

    tm_b, tn_b = _tile(s, 1024), _tile(d_ff, 1024)
    da = _mm_plain("w_down_bwd", df2, wf_down, True, BF16,
                   extra=(a_act,), extra_specs=(pl.BlockSpec((tm_b, tn_b), lambda i, j, kk: (i, j)),),
                   epilogue=lambda acc, av: (acc * (2.0 * jnp.maximum(av.astype(F32), 0.0)),),
                   outs=[jax.ShapeDtypeStruct((s, d_ff), BF16)],
                   out_specs=[pl.BlockSpec((tm_b, tn_b), lambda i, j, kk: (i, j))])[0]
    gw_down = _mm_plain("w_down_grad", f_t, df2, False, BF16)[0]
    gw_up = _mm_dw_cols("w_up_grad", h2_t, da)
    dh2 = _mm_cols_t("w_up_bwd", da, wg_up)
    dx1, dy, dg_pre_mlp, dg_post_mix = _midnorm_bwd(dx2, dh2, x1, y, g_post_mix, g_pre_mlp)
    gw_o = _mm_plain("w_o_grad", merged_t, dy, False, BF16)[0]
    dmerged = _mm_plain("w_o_bwd", dy, wf_o, True, F32)[0]
    do_sbp, do_cv, dgate_sb, dgate_cv = _merge_bwd(dmerged, proj, o_sbp, o_cv, d)
    gw_cv = _mm_dw_cols("conv_out_grad", u3_t, do_cv)
    gw_sb = _mm_dw_cols("sb_out_grad", o_sb_t, do_sbp)
    du3 = _mm_cols_t("conv_out_bwd", do_cv, wg_cv)
    do_sb = _mm_cols_t("sb_out_bwd", do_sbp, wg_sb)
    du1, dg_ln, db_ln = _ln_silu_bwd(du3, u1, g_conv_ln, b_conv_ln)
    dglu_a, dglu_b, dw_dw, db_dw = _conv_bwd(du1, proj, wf_dw, d_half)
    dq, dk, dv = _attn_bwd(proj, do_sb, n_heads)
    dproj = jnp.concatenate([dq, dk, dv, dglu_a, dglu_b, dgate_sb, dgate_cv], axis=1)
    db_in = _colsum(dproj)
    gw_in = _mm_dw_cols("w_in_grad", h_t, dproj)
    dh = _mm_cols_t("w_in_bwd", dproj, wg_in)
    grad_x, dg_pre_mix = _prenorm_bwd(dx1, dh, xs, g_pre_mix)

    big = [gw_in, gw_sb, gw_cv, gw_o.reshape(NDEV, d // NDEV, d), gw_up, gw_down.reshape(NDEV, d_ff // NDEV, d)]
    big = [g.reshape(4, 2, *g.shape[1:]) for g in big]
    from_sibling = _exchange_sibling(big)
    sums = [_pair_sum(g, r, core) for g, r in zip(big, from_sibling)]
    recv = _exchange_chips(sums)

    small = [dg_pre_mix, db_in, dw_dw.reshape(1, -1), db_dw, dg_ln, db_ln, dg_post_mix, dg_pre_mlp, dg_post_mlp]
    sizes = [a.shape[1] for a in small]
    packed = jnp.concatenate(small, axis=1).reshape(-1, LANES)
    total = _sum_small(_all_gather_small(packed)).reshape(1, -1)
    offs = [0]
    for n in sizes:
        offs.append(offs[-1] + n)
    (g_g_pre_mix, g_b_in, g_w_dw_flat, g_b_dw, g_g_conv_ln, g_b_conv_ln, g_g_post_mix, g_g_pre_mlp,
     g_g_post_mlp) = [total[:, offs[k]:offs[k + 1]] for k in range(len(sizes))]
    ch = w_dw.shape[2]
    g_w_dw = lax.dynamic_slice_in_dim(g_w_dw_flat.reshape(CONV_PAD, d_half), dev * ch, ch, axis=1)[:CONV_WIDTH]

    loss = lax.psum(loss_part[0, 0], ("x", "y", "c"))

    res = {}
    res["g_pre_mix"] = _adamw_small("adamw_g_pre_mix", g_pre_mix, m_g_pre_mix, v_g_pre_mix, g_g_pre_mix)
    res["w_in"] = _adamw_big("adamw_w_in", w_in[0], m_w_in[0], v_w_in[0], recv[0])
    res["b_in"] = _adamw_small("adamw_b_in", b_in, m_b_in, v_b_in, g_b_in)
    res["w_dw"] = _adamw_small("adamw_w_dw", w_dw[0], m_w_dw[0], v_w_dw[0], g_w_dw)
    res["b_dw"] = _adamw_small("adamw_b_dw", b_dw, m_b_dw, v_b_dw, g_b_dw)
    res["g_conv_ln"] = _adamw_small("adamw_g_conv_ln", g_conv_ln, m_g_conv_ln, v_g_conv_ln, g_g_conv_ln)
    res["b_conv_ln"] = _adamw_small("adamw_b_conv_ln", b_conv_ln, m_b_conv_ln, v_b_conv_ln, g_b_conv_ln)
    res["w_sb_out"] = _adamw_big("adamw_w_sb_out", w_sb_out[0], m_w_sb_out[0], v_w_sb_out[0], recv[1])
    res["w_conv_out"] = _adamw_big("adamw_w_conv_out", w_conv_out[0], m_w_conv_out[0], v_w_conv_out[0], recv[2])
    res["w_o"] = _adamw_big("adamw_w_o", w_o[0], m_w_o[0], v_w_o[0], recv[3])
    res["g_post_mix"] = _adamw_small("adamw_g_post_mix", g_post_mix, m_g_post_mix, v_g_post_mix, g_g_post_mix)
    res["g_pre_mlp"] = _adamw_small("adamw_g_pre_mlp", g_pre_mlp, m_g_pre_mlp, v_g_pre_mlp, g_g_pre_mlp)
    res["w_up"] = _adamw_big("adamw_w_up", w_up[0], m_w_up[0], v_w_up[0], recv[4])
    res["w_down"] = _adamw_big("adamw_w_down", w_down[0], m_w_down[0], v_w_down[0], recv[5])
    res["g_post_mlp"] = _adamw_small("adamw_g_post_mlp", g_post_mlp, m_g_post_mlp, v_g_post_mlp, g_g_post_mlp)

    names = ["g_pre_mix", "w_in", "b_in", "w_dw", "b_dw", "g_conv_ln", "b_conv_ln", "w_sb_out", "w_conv_out", "w_o",
             "g_post_mix", "g_pre_mlp", "w_up", "w_down", "g_post_mlp"]
    three_d = {"w_in", "w_dw", "w_sb_out", "w_conv_out", "w_o", "w_up", "w_down"}

    def shaped(nm, arr):
        return arr[None] if nm in three_d else arr

    out = [loss, grad_x[None]]
    for k in range(4):
        out += [shaped(nm, res[nm][k]) for nm in names]
    return tuple(out)
```

```python
import functools
import math

import jax
import jax.numpy as jnp
from jax import lax
from jax.experimental import pallas as pl
from jax.experimental.pallas import tpu as pltpu

F32 = jnp.float32
BF16 = jnp.bfloat16
NDEV = 8
LANES = 128
EPS = 1e-6
CONV_WIDTH = 31
CONV_PAD = 32
HEAD_DIM = 128
ADAM_LR = 0.001
ADAM_B1 = 0.9
ADAM_B2 = 0.999
ADAM_EPS = 1e-08
ADAM_WD = 0.01
ADAM_STEP = 10
VMEM_LIMIT = 56 * 1024 * 1024
MESH = pl.DeviceIdType.MESH
ANY = pl.BlockSpec(memory_space=pl.ANY)


def _tile(n, pref):
    t = min(n, pref)
    assert n % t == 0, (n, t)
    return t


def _params(sem):
    return pltpu.CompilerParams(dimension_semantics=sem, vmem_limit_bytes=VMEM_LIMIT)


def _sigmoid(v):
    return 1.0 / (1.0 + jnp.exp(-v))


def _comm_steps(comm, grid, c_ins, c_outs, c_scr, at_start):
    pids = [pl.program_id(ax) for ax in range(len(grid))]
    edge = [p == (0 if at_start else g - 1) for p, g in zip(pids, grid)]

    @pl.when(functools.reduce(jnp.logical_and, edge))
    def _():
        (comm.start if at_start else comm.finish)(c_ins, c_outs, c_scr)


def _matmul(name, a, b, outs, grid, a_spec, b_spec, out_specs, n_red, nt, acc_shape,
            extra=(), extra_specs=(), epilogue=None, comm=None):
    n_extra = len(extra)
    n_out = len(outs)
    comm = comm or _Comm([], [], [], None, None)
    n_cin, n_cout = len(comm.ins), len(comm.outs)
    red_axes = tuple(range(len(grid) - n_red, len(grid)))
    red_sizes = tuple(grid[ax] for ax in red_axes)
    dims = (((1,), (1,)), ((), ())) if nt else (((1,), (0,)), ((), ()))

    def body(*refs):
        a_ref, b_ref = refs[0], refs[1]
        ex_refs = refs[2:2 + n_extra]
        pos = 2 + n_extra
        c_ins = refs[pos:pos + n_cin]
        o_refs = refs[pos + n_cin:pos + n_cin + n_out]
        pos += n_cin + n_out
        c_outs = refs[pos:pos + n_cout]
        acc_ref = refs[pos + n_cout]
        c_scr = refs[pos + n_cout + 1:]
        if n_cin:
            _comm_steps(comm, grid, c_ins, c_outs, c_scr, True)
        ks = [pl.program_id(ax) for ax in red_axes]
        first = functools.reduce(jnp.logical_and, [k == 0 for k in ks])
        last = functools.reduce(jnp.logical_and, [k == s - 1 for k, s in zip(ks, red_sizes)])

        @pl.when(first)
        def _():
            acc_ref[...] = jnp.zeros_like(acc_ref)

        acc_ref[...] += lax.dot_general(a_ref[...].astype(BF16), b_ref[...].astype(BF16), dims,
                                        preferred_element_type=F32)

        @pl.when(last)
        def _():
            acc = acc_ref[...]
            if epilogue is None:
                vals = (acc,)
            else:
                vals = epilogue(acc, *[r[...] for r in ex_refs])
            for o_ref, val in zip(o_refs, vals):
                o_ref[...] = val.astype(o_ref.dtype)

        if n_cin:
            _comm_steps(comm, grid, c_ins, c_outs, c_scr, False)

    sem = (("arbitrary",) * len(grid) if n_cin else
           ("parallel",) * (len(grid) - n_red) + ("arbitrary",) * n_red)
    res = pl.pallas_call(
        body, name=name, grid=grid,
        in_specs=[a_spec, b_spec, *extra_specs, *[ANY] * n_cin],
        out_specs=[*out_specs, *[ANY] * n_cout], out_shape=[*outs, *comm.outs],
        scratch_shapes=[pltpu.VMEM(acc_shape, F32), *comm.scratch], compiler_params=_params(sem),
    )(a, b, *extra, *comm.ins)
    return res


def _mm_cols(name, a, wg, bias=None, out_dtype=F32, comm=None):
    m, k = a.shape
    _, _, ns = wg.shape
    tm, tk = _tile(m, 1024), _tile(k, 1024)
    grid = (m // tm, NDEV, k // tk)
    extra, extra_specs, epi = (), (), None
    if bias is not None:
        extra, extra_specs = (bias,), (pl.BlockSpec((1, ns), lambda i, d, kk: (0, d)),)
        epi = lambda acc, bv: (acc + bv,)
    return _matmul(name, a, wg, [jax.ShapeDtypeStruct((m, NDEV * ns), out_dtype)], grid,
                   pl.BlockSpec((tm, tk), lambda i, d, kk: (i, kk)),
                   pl.BlockSpec((None, tk, ns), lambda i, d, kk: (d, kk, 0)),
                   [pl.BlockSpec((tm, ns), lambda i, d, kk: (i, d))], 1, False, (tm, ns),
                   extra, extra_specs, epi, comm)


def _mm_cols_t(name, a, wg, comm=None):
    m, _ = a.shape
    _, n, ns = wg.shape
    tm, tn = _tile(m, 1024), _tile(n, 1024)
    grid = (m // tm, n // tn, NDEV)
    return _matmul(name, a, wg, [jax.ShapeDtypeStruct((m, n), F32)], grid,
                   pl.BlockSpec((tm, ns), lambda i, j, d: (i, d)),
                   pl.BlockSpec((None, tn, ns), lambda i, j, d: (d, j, 0)),
                   [pl.BlockSpec((tm, tn), lambda i, j, d: (i, j))], 1, True, (tm, tn), comm=comm)


def _mm_dw_cols(name, at, g):
    m, t = at.shape
    ns = g.shape[1] // NDEV
    tm, tk = _tile(m, 1024), _tile(t, 1024)
    grid = (m // tm, NDEV, t // tk)
    return _matmul(name, at, g, [jax.ShapeDtypeStruct((NDEV, m, ns), BF16)], grid,
                   pl.BlockSpec((tm, tk), lambda i, d, kk: (i, kk)),
                   pl.BlockSpec((tk, ns), lambda i, d, kk: (kk, d)),
                   [pl.BlockSpec((None, tm, ns), lambda i, d, kk: (d, i, 0))], 1, False, (tm, ns))[0]


def _mm_plain(name, a, b, nt, out_dtype, extra=(), extra_specs=(), epilogue=None, outs=None, out_specs=None):
    m, k = a.shape
    n = b.shape[0] if nt else b.shape[1]
    tm, tn, tk = _tile(m, 1024), _tile(n, 1024), _tile(k, 1024)
    grid = (m // tm, n // tn, k // tk)
    b_spec = (pl.BlockSpec((tn, tk), lambda i, j, kk: (j, kk)) if nt
              else pl.BlockSpec((tk, tn), lambda i, j, kk: (kk, j)))
    if outs is None:
        outs = [jax.ShapeDtypeStruct((m, n), out_dtype)]
        out_specs = [pl.BlockSpec((tm, tn), lambda i, j, kk: (i, j))]
    return _matmul(name, a, b, outs, grid, pl.BlockSpec((tm, tk), lambda i, j, kk: (i, kk)), b_spec,
                   out_specs, 1, nt, (tm, tn), extra, extra_specs, epilogue)


def _rms_rows(v):
    return lax.rsqrt(jnp.mean(v * v, axis=-1, keepdims=True) + EPS)


def _prenorm(x, g):
    s, d = x.shape
    ts = _tile(s, 256)

    def body(x_ref, g_ref, h_ref, ht_ref):
        xv = x_ref[...]
        h = xv * _rms_rows(xv) * g_ref[...]
        h_ref[...] = h.astype(BF16)
        ht_ref[...] = h.T.astype(BF16)

    return pl.pallas_call(
        body, name="prenorm", grid=(s // ts,),
        in_specs=[pl.BlockSpec((ts, d), lambda i: (i, 0)), pl.BlockSpec((1, d), lambda i: (0, 0))],
        out_specs=[pl.BlockSpec((ts, d), lambda i: (i, 0)), pl.BlockSpec((d, ts), lambda i: (0, i))],
        out_shape=[jax.ShapeDtypeStruct((s, d), BF16), jax.ShapeDtypeStruct((d, s), BF16)],
        compiler_params=_params(("parallel",)),
    )(x, g)


def _ln_silu(u1, g, b):
    s, c = u1.shape
    ts = _tile(s, 256)

    def body(u_ref, g_ref, b_ref, o_ref, ot_ref):
        u = u_ref[...]
        mu = jnp.mean(u, axis=-1, keepdims=True)
        var = jnp.mean(jnp.square(u - mu), axis=-1, keepdims=True)
        u2 = (u - mu) * lax.rsqrt(var + EPS) * g_ref[...] + b_ref[...]
        u3 = u2 * _sigmoid(u2)
        o_ref[...] = u3.astype(BF16)
        ot_ref[...] = u3.T.astype(BF16)

    return pl.pallas_call(
        body, name="ln_silu", grid=(s // ts,),
        in_specs=[pl.BlockSpec((ts, c), lambda i: (i, 0)), pl.BlockSpec((1, c), lambda i: (0, 0)),
                  pl.BlockSpec((1, c), lambda i: (0, 0))],
        out_specs=[pl.BlockSpec((ts, c), lambda i: (i, 0)), pl.BlockSpec((c, ts), lambda i: (0, i))],
        out_shape=[jax.ShapeDtypeStruct((s, c), BF16), jax.ShapeDtypeStruct((c, s), BF16)],
        compiler_params=_params(("parallel",)),
    )(u1, g, b)


def _ln_silu_bwd(du3, u1, g, b):
    s, c = u1.shape
    ts = _tile(s, 256)

    def body(d_ref, u_ref, g_ref, b_ref, du1_ref, dg_ref, db_ref):
        @pl.when(pl.program_id(0) == 0)
        def _():
            dg_ref[...] = jnp.zeros_like(dg_ref)
            db_ref[...] = jnp.zeros_like(db_ref)

        u = u_ref[...]
        mu = jnp.mean(u, axis=-1, keepdims=True)
        var = jnp.mean(jnp.square(u - mu), axis=-1, keepdims=True)
        rstd = lax.rsqrt(var + EPS)
        uhat = (u - mu) * rstd
        u2 = uhat * g_ref[...] + b_ref[...]
        sg = _sigmoid(u2)
        du2 = d_ref[...] * (sg * (1.0 + u2 * (1.0 - sg)))
        dg_ref[...] += jnp.sum(du2 * uhat, axis=0, keepdims=True)
        db_ref[...] += jnp.sum(du2, axis=0, keepdims=True)
        duh = du2 * g_ref[...]
        du1_ref[...] = rstd * (duh - jnp.mean(duh, axis=-1, keepdims=True)
                               - uhat * jnp.mean(duh * uhat, axis=-1, keepdims=True))

    row = pl.BlockSpec((ts, c), lambda i: (i, 0))
    vec = pl.BlockSpec((1, c), lambda i: (0, 0))
    return pl.pallas_call(
        body, name="ln_silu_bwd", grid=(s // ts,),
        in_specs=[row, row, vec, vec], out_specs=[row, vec, vec],
        out_shape=[jax.ShapeDtypeStruct((s, c), F32), jax.ShapeDtypeStruct((1, c), F32),
                   jax.ShapeDtypeStruct((1, c), F32)],
        compiler_params=_params(("arbitrary",)),
    )(du3, u1, g, b)


def _merge(proj, o_sbp, o_cv, d):
    s = proj.shape[0]
    w = d // 2
    ts = _tile(s, 256)

    def body(gs_ref, gc_ref, a_ref, b_ref, m_ref, mt_ref):
        mg = _sigmoid(gs_ref[...]) * a_ref[...] + _sigmoid(gc_ref[...]) * b_ref[...]
        m_ref[...] = mg.astype(BF16)
        mt_ref[...] = mg.T.astype(BF16)

    blk = pl.BlockSpec((ts, w), lambda i, j: (i, j))
    return pl.pallas_call(
        body, name="merge", grid=(s // ts, 2),
        in_specs=[pl.BlockSpec((ts, w), lambda i, j: (i, 5 + j)), pl.BlockSpec((ts, w), lambda i, j: (i, 7 + j)),
                  blk, blk],
        out_specs=[blk, pl.BlockSpec((w, ts), lambda i, j: (j, i))],
        out_shape=[jax.ShapeDtypeStruct((s, d), BF16), jax.ShapeDtypeStruct((d, s), BF16)],
        compiler_params=_params(("parallel", "parallel")),
    )(proj, proj, o_sbp, o_cv)


def _merge_bwd(dmerged, proj, o_sbp, o_cv, d):
    s = proj.shape[0]
    w = d // 2
    ts = _tile(s, 256)

    def body(dm_ref, gs_ref, gc_ref, a_ref, b_ref, da_ref, db_ref, dgs_ref, dgc_ref):
        dm = dm_ref[...]
        ss = _sigmoid(gs_ref[...])
        sc = _sigmoid(gc_ref[...])
        da_ref[...] = (dm * ss).astype(BF16)
        db_ref[...] = (dm * sc).astype(BF16)
        dgs_ref[...] = (dm * a_ref[...] * ss * (1.0 - ss)).astype(BF16)
        dgc_ref[...] = (dm * b_ref[...] * sc * (1.0 - sc)).astype(BF16)

    blk = pl.BlockSpec((ts, w), lambda i, j: (i, j))
    sds = jax.ShapeDtypeStruct((s, d), BF16)
    return pl.pallas_call(
        body, name="merge_bwd", grid=(s // ts, 2),
        in_specs=[blk, pl.BlockSpec((ts, w), lambda i, j: (i, 5 + j)), pl.BlockSpec((ts, w), lambda i, j: (i, 7 + j)),
                  blk, blk],
        out_specs=[blk, blk, blk, blk], out_shape=[sds, sds, sds, sds],
        compiler_params=_params(("parallel", "parallel")),
    )(dmerged, proj, proj, o_sbp, o_cv)


def _postnorm_mix(x, y, g_post, g_pre):
    s, d = x.shape
    ts = _tile(s, 256)

    def body(x_ref, y_ref, gp_ref, gn_ref, x1_ref, h_ref, ht_ref):
        yv = y_ref[...]
        x1 = x_ref[...] + yv * _rms_rows(yv) * gp_ref[...]
        x1_ref[...] = x1
        h = x1 * _rms_rows(x1) * gn_ref[...]
        h_ref[...] = h.astype(BF16)
        ht_ref[...] = h.T.astype(BF16)

    row = pl.BlockSpec((ts, d), lambda i: (i, 0))
    vec = pl.BlockSpec((1, d), lambda i: (0, 0))
    return pl.pallas_call(
        body, name="postnorm_mix", grid=(s // ts,),
        in_specs=[row, row, vec, vec], out_specs=[row, row, pl.BlockSpec((d, ts), lambda i: (0, i))],
        out_shape=[jax.ShapeDtypeStruct((s, d), F32), jax.ShapeDtypeStruct((s, d), BF16),
                   jax.ShapeDtypeStruct((d, s), BF16)],
        compiler_params=_params(("parallel",)),
    )(x, y, g_post, g_pre)


def _rms_bwd(dout, vin, g):
    r = _rms_rows(vin)
    vhat = vin * r
    dyh = dout * g
    dvin = r * (dyh - vhat * jnp.mean(dyh * vhat, axis=-1, keepdims=True))
    return dvin, jnp.sum(dout * vhat, axis=0, keepdims=True)


def _loss_head(x1, f2, tgt, g):
    s, d = x1.shape
    ts = _tile(s, 256)

    def body(x1_ref, f_ref, t_ref, g_ref, dx2_ref, df2_ref, dg_ref, loss_ref):
        @pl.when(pl.program_id(0) == 0)
        def _():
            dg_ref[...] = jnp.zeros_like(dg_ref)
            loss_ref[...] = jnp.zeros_like(loss_ref)

        fv = f_ref[...]
        x2 = x1_ref[...] + fv * _rms_rows(fv) * g_ref[...]
        err = x2 - t_ref[...]
        loss_ref[...] += 0.5 * jnp.sum(jnp.mean(err * err, axis=-1, keepdims=True), axis=0, keepdims=True)
        dx2 = err * (1.0 / d)
        dx2_ref[...] = dx2
        df2, dg = _rms_bwd(dx2, fv, g_ref[...])
        df2_ref[...] = df2.astype(BF16)
        dg_ref[...] += dg

    row = pl.BlockSpec((ts, d), lambda i: (i, 0))
    vec = pl.BlockSpec((1, d), lambda i: (0, 0))
    return pl.pallas_call(
        body, name="loss_head", grid=(s // ts,),
        in_specs=[row, row, row, vec],
        out_specs=[row, row, vec, pl.BlockSpec((1, LANES), lambda i: (0, 0))],
        out_shape=[jax.ShapeDtypeStruct((s, d), F32), jax.ShapeDtypeStruct((s, d), BF16),
                   jax.ShapeDtypeStruct((1, d), F32), jax.ShapeDtypeStruct((1, LANES), F32)],
        compiler_params=_params(("arbitrary",)),
    )(x1, f2, tgt, g)


def _midnorm_bwd(dx2, dh2, x1, y, g_post, g_pre):
    s, d = x1.shape
    ts = _tile(s, 256)

    def body(dx2_ref, dh_ref, x1_ref, y_ref, gp_ref, gn_ref, dx1_ref, dy_ref, dgn_ref, dgp_ref):
        @pl.when(pl.program_id(0) == 0)
        def _():
            dgn_ref[...] = jnp.zeros_like(dgn_ref)
            dgp_ref[...] = jnp.zeros_like(dgp_ref)

        dxa, dgn = _rms_bwd(dh_ref[...], x1_ref[...], gn_ref[...])
        dx1 = dx2_ref[...] + dxa
        dx1_ref[...] = dx1
        dy, dgp = _rms_bwd(dx1, y_ref[...], gp_ref[...])
        dy_ref[...] = dy.astype(BF16)
        dgn_ref[...] += dgn
        dgp_ref[...] += dgp

    row = pl.BlockSpec((ts, d), lambda i: (i, 0))
    vec = pl.BlockSpec((1, d), lambda i: (0, 0))
    return pl.pallas_call(
        body, name="midnorm_bwd", grid=(s // ts,),
        in_specs=[row, row, row, row, vec, vec], out_specs=[row, row, vec, vec],
        out_shape=[jax.ShapeDtypeStruct((s, d), F32), jax.ShapeDtypeStruct((s, d), BF16),
                   jax.ShapeDtypeStruct((1, d), F32), jax.ShapeDtypeStruct((1, d), F32)],
        compiler_params=_params(("arbitrary",)),
    )(dx2, dh2, x1, y, g_post, g_pre)


def _prenorm_bwd(dx1, dh, x, g):
    s, d = x.shape
    ts = _tile(s, 256)

    def body(dx1_ref, dh_ref, x_ref, g_ref, dx_ref, dg_ref):
        @pl.when(pl.program_id(0) == 0)
        def _():
            dg_ref[...] = jnp.zeros_like(dg_ref)

        dxa, dg = _rms_bwd(dh_ref[...], x_ref[...], g_ref[...])
        dx_ref[...] = dx1_ref[...] + dxa
        dg_ref[...] += dg

    row = pl.BlockSpec((ts, d), lambda i: (i, 0))
    vec = pl.BlockSpec((1, d), lambda i: (0, 0))
    return pl.pallas_call(
        body, name="prenorm_bwd", grid=(s // ts,),
        in_specs=[row, row, row, vec], out_specs=[row, vec],
        out_shape=[jax.ShapeDtypeStruct((s, d), F32), jax.ShapeDtypeStruct((1, d), F32)],
        compiler_params=_params(("arbitrary",)),
    )(dx1, dh, x, g)


def _colsum(a):
    s, n = a.shape
    ts = _tile(s, 256)

    def body(a_ref, o_ref):
        @pl.when(pl.program_id(0) == 0)
        def _():
            o_ref[...] = jnp.zeros_like(o_ref)

        o_ref[...] += jnp.sum(a_ref[...].astype(F32), axis=0, keepdims=True)

    return pl.pallas_call(
        body, name="colsum", grid=(s // ts,),
        in_specs=[pl.BlockSpec((ts, n), lambda i: (i, 0))], out_specs=pl.BlockSpec((1, n), lambda i: (0, 0)),
        out_shape=jax.ShapeDtypeStruct((1, n), F32), compiler_params=_params(("arbitrary",)),
    )(a)


def _shift_rows(win, off, t):
    n = win.shape[0]
    if off == 0:
        return win[:t]
    return pltpu.roll(win, n - off, axis=0)[:t]


def _conv_fwd(proj, w_pad, b_dw, c_total):
    s = proj.shape[0]
    nct = c_total // LANES
    t = _tile(s, 256)

    def body(ga_ref, gb_ref, w_ref, b_ref, o_ref, u0_ref):
        u0_ref[pl.ds(0, CONV_PAD), :] = jnp.zeros((CONV_PAD, LANES), F32)
        u0_ref[pl.ds(CONV_PAD, s), :] = ga_ref[...] * _sigmoid(gb_ref[...])
        wv = w_ref[...]

        def chunk(r, carry):
            r0 = pl.multiple_of(r * t, t)
            win = u0_ref[pl.ds(r0, t + CONV_PAD), :]
            acc = jnp.broadcast_to(b_ref[...], (t, LANES))
            for j in range(CONV_WIDTH):
                acc = acc + wv[j:j + 1, :] * _shift_rows(win, j + CONV_PAD - (CONV_WIDTH - 1), t)
            o_ref[pl.ds(r0, t), :] = acc
            return carry

        lax.fori_loop(0, s // t, chunk, 0)

    return pl.pallas_call(
        body, name="conv_fwd", grid=(nct,),
        in_specs=[pl.BlockSpec((s, LANES), lambda c: (0, 3 * nct + c)), pl.BlockSpec((s, LANES), lambda c: (0, 4 * nct + c)),
                  pl.BlockSpec((CONV_PAD, LANES), lambda c: (0, c)), pl.BlockSpec((1, LANES), lambda c: (0, c))],
        out_specs=pl.BlockSpec((s, LANES), lambda c: (0, c)),
        out_shape=jax.ShapeDtypeStruct((s, c_total), F32),
        scratch_shapes=[pltpu.VMEM((s + CONV_PAD, LANES), F32)],
        compiler_params=_params(("parallel",)),
    )(proj, proj, w_pad, b_dw)


def _conv_bwd(du1, proj, w_pad, c_total):
    s = proj.shape[0]
    nct = c_total // LANES
    t = _tile(s, 256)

    def body(d_ref, ga_ref, gb_ref, w_ref, dga_ref, dgb_ref, dw_ref, db_ref, u0_ref, dp_ref):
        sg = _sigmoid(gb_ref[...])
        u0_ref[pl.ds(0, CONV_PAD), :] = jnp.zeros((CONV_PAD, LANES), F32)
        u0_ref[pl.ds(CONV_PAD, s), :] = ga_ref[...] * sg
        dp_ref[pl.ds(0, s), :] = d_ref[...]
        dp_ref[pl.ds(s, CONV_PAD), :] = jnp.zeros((CONV_PAD, LANES), F32)
        dw_ref[...] = jnp.zeros_like(dw_ref)
        db_ref[...] = jnp.sum(d_ref[...], axis=0, keepdims=True)
        wv = w_ref[...]

        def chunk(r, carry):
            r0 = pl.multiple_of(r * t, t)
            win = u0_ref[pl.ds(r0, t + CONV_PAD), :]
            dwin = dp_ref[pl.ds(r0, t + CONV_PAD), :]
            dcur = dwin[:t]
            du0 = jnp.zeros((t, LANES), F32)
            for j in range(CONV_WIDTH):
                du0 = du0 + wv[j:j + 1, :] * _shift_rows(dwin, CONV_WIDTH - 1 - j, t)
                sh = _shift_rows(win, j + CONV_PAD - (CONV_WIDTH - 1), t)
                dw_ref[j:j + 1, :] += jnp.sum(dcur * sh, axis=0, keepdims=True)
            gav = ga_ref[pl.ds(r0, t), :]
            sgv = _sigmoid(gb_ref[pl.ds(r0, t), :])
            dga_ref[pl.ds(r0, t), :] = (du0 * sgv).astype(BF16)
            dgb_ref[pl.ds(r0, t), :] = (du0 * gav * sgv * (1.0 - sgv)).astype(BF16)
            return carry

        lax.fori_loop(0, s // t, chunk, 0)

    col = pl.BlockSpec((s, LANES), lambda c: (0, c))
    return pl.pallas_call(
        body, name="conv_bwd", grid=(nct,),
        in_specs=[col, pl.BlockSpec((s, LANES), lambda c: (0, 3 * nct + c)),
                  pl.BlockSpec((s, LANES), lambda c: (0, 4 * nct + c)), pl.BlockSpec((CONV_PAD, LANES), lambda c: (0, c))],
        out_specs=[col, col, pl.BlockSpec((CONV_PAD, LANES), lambda c: (0, c)), pl.BlockSpec((1, LANES), lambda c: (0, c))],
        out_shape=[jax.ShapeDtypeStruct((s, c_total), BF16), jax.ShapeDtypeStruct((s, c_total), BF16),
                   jax.ShapeDtypeStruct((CONV_PAD, c_total), F32), jax.ShapeDtypeStruct((1, c_total), F32)],
        scratch_shapes=[pltpu.VMEM((s + CONV_PAD, LANES), F32), pltpu.VMEM((s + CONV_PAD, LANES), F32)],
        compiler_params=_params(("parallel",)),
    )(du1, proj, proj, w_pad)


TQ_PREF = 256
TK = 128


def _split_dot(v, tri):
    hi = v.astype(BF16)
    lo = (v - hi.astype(F32)).astype(BF16)
    return (jnp.dot(hi, tri, preferred_element_type=F32) + jnp.dot(lo, tri, preferred_element_type=F32))


def _sb_block(q, k_ref, j, i, tq, scale, c_lm, tri_after):
    kb = k_ref[pl.ds(pl.multiple_of(j * TK, TK), TK), :].astype(BF16)
    z = lax.dot_general(q, kb, (((1,), (1,)), ((), ())), preferred_element_type=F32) * scale
    tpos = i * tq + lax.broadcasted_iota(jnp.int32, (tq, TK), 0)
    spos = j * TK + lax.broadcasted_iota(jnp.int32, (tq, TK), 1)
    mask = spos < tpos
    sp = jnp.log(1.0 + jnp.exp(-jnp.abs(z)))
    lsz = jnp.minimum(z, 0.0) - sp
    lm = jnp.where(mask, -jnp.maximum(z, 0.0) - sp, 0.0)
    suf = c_lm + _split_dot(lm, tri_after)
    a = jnp.where(mask, jnp.exp(lsz + suf), 0.0)
    return lsz, lm, a, mask


def _tri(after):
    r = lax.broadcasted_iota(jnp.int32, (TK, TK), 0)
    c = lax.broadcasted_iota(jnp.int32, (TK, TK), 1)
    return (r > c).astype(BF16) if after else (r < c).astype(BF16)


def _attn_fwd(proj, n_heads, comm):
    s = proj.shape[0]
    tq = _tile(s, TQ_PREF)
    scale = 1.0 / math.sqrt(HEAD_DIM)
    ratio = tq // TK
    grid = (n_heads, s // tq)
    n_cin, n_cout = len(comm.ins), len(comm.outs)

    def body(*refs):
        q_ref, k_ref, v_ref = refs[:3]
        c_ins = refs[3:3 + n_cin]
        o_ref, ot_ref = refs[3 + n_cin:5 + n_cin]
        c_outs = refs[5 + n_cin:5 + n_cin + n_cout]
        acc_ref, clm_ref = refs[5 + n_cin + n_cout:7 + n_cin + n_cout]
        c_scr = refs[7 + n_cin + n_cout:]
        _comm_steps(comm, grid, c_ins, c_outs, c_scr, True)
        i = pl.program_id(1)
        q = q_ref[...].astype(BF16)
        tri_after = _tri(True)
        acc_ref[...] = jnp.zeros_like(acc_ref)
        clm_ref[...] = jnp.zeros_like(clm_ref)
        nkb = (i + 1) * ratio

        def step(jj, carry):
            j = nkb - 1 - jj
            _, lm, a, _ = _sb_block(q, k_ref, j, i, tq, scale, clm_ref[...], tri_after)
            vb = v_ref[pl.ds(pl.multiple_of(j * TK, TK), TK), :].astype(BF16)
            acc_ref[...] += jnp.dot(a.astype(BF16), vb, preferred_element_type=F32)
            clm_ref[...] += jnp.sum(lm, axis=1, keepdims=True)
            return carry

        lax.fori_loop(0, nkb, step, 0)
        o = acc_ref[...]
        o_ref[...] = o
        ot_ref[...] = o.T.astype(BF16)
        _comm_steps(comm, grid, c_ins, c_outs, c_scr, False)

    return pl.pallas_call(
        body, name="attn_fwd", grid=grid,
        in_specs=[pl.BlockSpec((tq, HEAD_DIM), lambda h, i: (i, h)),
                  pl.BlockSpec((s, HEAD_DIM), lambda h, i: (0, n_heads + h)),
                  pl.BlockSpec((s, HEAD_DIM), lambda h, i: (0, 2 * n_heads + h)), *[ANY] * n_cin],
        out_specs=[pl.BlockSpec((tq, HEAD_DIM), lambda h, i: (i, h)), pl.BlockSpec((HEAD_DIM, tq), lambda h, i: (h, i)),
                   *[ANY] * n_cout],
        out_shape=[jax.ShapeDtypeStruct((s, n_heads * HEAD_DIM), F32), jax.ShapeDtypeStruct((n_heads * HEAD_DIM, s), BF16),
                   *comm.outs],
        scratch_shapes=[pltpu.VMEM((tq, HEAD_DIM), F32), pltpu.VMEM((tq, 1), F32), *comm.scratch],
        compiler_params=_params(("arbitrary", "arbitrary")),
    )(proj, proj, proj, *comm.ins)


def _attn_bwd(proj, do_sb, n_heads, comm):
    s = proj.shape[0]
    tq = _tile(s, TQ_PREF)
    scale = 1.0 / math.sqrt(HEAD_DIM)
    ratio = tq // TK
    n_kb = s // TK
    n_qb = s // tq
    grid = (n_heads, n_qb)
    n_cin, n_cout = len(comm.ins), len(comm.outs)

    def body(*refs):
        q_ref, k_ref, v_ref, do_ref = refs[:4]
        c_ins = refs[4:4 + n_cin]
        dq_ref, dk_ref, dv_ref = refs[4 + n_cin:7 + n_cin]
        c_outs = refs[7 + n_cin:7 + n_cin + n_cout]
        dka_ref, dva_ref, dl_ref, be_ref, dqa_ref, c_ref = refs[7 + n_cin + n_cout:13 + n_cin + n_cout]
        c_scr = refs[13 + n_cin + n_cout:]
        _comm_steps(comm, grid, c_ins, c_outs, c_scr, True)
        i = pl.program_id(1)

        @pl.when(i == 0)
        def _():
            dka_ref[...] = jnp.zeros_like(dka_ref)
            dva_ref[...] = jnp.zeros_like(dva_ref)

        q = q_ref[...].astype(BF16)
        dob = do_ref[...].astype(BF16)
        tri_after = _tri(True)
        tri_before = _tri(False)
        nkb = (i + 1) * ratio

        c_ref[...] = jnp.zeros_like(c_ref)

        def sweep_a(jj, carry):
            j = nkb - 1 - jj
            lsz, lm, a, _ = _sb_block(q, k_ref, j, i, tq, scale, c_ref[...], tri_after)
            rows = pl.ds(pl.multiple_of(j * TK, TK), TK)
            vb = v_ref[rows, :].astype(BF16)
            da = lax.dot_general(dob, vb, (((1,), (1,)), ((), ())), preferred_element_type=F32)
            dl_ref[j] = da * a
            be_ref[j] = jnp.exp(lsz)
            dva_ref[rows, :] += jnp.dot(a.T.astype(BF16), dob, preferred_element_type=F32)
            c_ref[...] += jnp.sum(lm, axis=1, keepdims=True)
            return carry

        lax.fori_loop(0, nkb, sweep_a, 0)

        c_ref[...] = jnp.zeros_like(c_ref)
        dqa_ref[...] = jnp.zeros_like(dqa_ref)

        def sweep_b(j, carry):
            dl = dl_ref[j]
            beta = be_ref[j]
            tpos = i * tq + lax.broadcasted_iota(jnp.int32, (tq, TK), 0)
            spos = j * TK + lax.broadcasted_iota(jnp.int32, (tq, TK), 1)
            p = c_ref[...] + _split_dot(dl, tri_before)
            dz = jnp.where(spos < tpos, (dl * (1.0 - beta) - beta * p) * scale, 0.0)
            rows = pl.ds(pl.multiple_of(j * TK, TK), TK)
            kb = k_ref[rows, :].astype(BF16)
            dqa_ref[...] += jnp.dot(dz.astype(BF16), kb, preferred_element_type=F32)
            dka_ref[rows, :] += jnp.dot(dz.T.astype(BF16), q, preferred_element_type=F32)
            c_ref[...] += jnp.sum(dl, axis=1, keepdims=True)
            return carry

        lax.fori_loop(0, nkb, sweep_b, 0)
        dq_ref[...] = dqa_ref[...].astype(BF16)

        @pl.when(i == n_qb - 1)
        def _():
            dk_ref[...] = dka_ref[...].astype(BF16)
            dv_ref[...] = dva_ref[...].astype(BF16)

        _comm_steps(comm, grid, c_ins, c_outs, c_scr, False)

    qblk = pl.BlockSpec((tq, HEAD_DIM), lambda h, i: (i, h))
    full = pl.BlockSpec((s, HEAD_DIM), lambda h, i: (0, h))
    sds = jax.ShapeDtypeStruct((s, n_heads * HEAD_DIM), BF16)
    return pl.pallas_call(
        body, name="attn_bwd", grid=grid,
        in_specs=[qblk, pl.BlockSpec((s, HEAD_DIM), lambda h, i: (0, n_heads + h)),
                  pl.BlockSpec((s, HEAD_DIM), lambda h, i: (0, 2 * n_heads + h)), qblk, *[ANY] * n_cin],
        out_specs=[qblk, full, full, *[ANY] * n_cout], out_shape=[sds, sds, sds, *comm.outs],
        scratch_shapes=[pltpu.VMEM((s, HEAD_DIM), F32), pltpu.VMEM((s, HEAD_DIM), F32),
                        pltpu.VMEM((n_kb, tq, TK), F32), pltpu.VMEM((n_kb, tq, TK), F32),
                        pltpu.VMEM((tq, HEAD_DIM), F32), pltpu.VMEM((tq, 1), F32), *comm.scratch],
        compiler_params=_params(("arbitrary", "arbitrary")),
    )(proj, proj, proj, do_sb, *comm.ins)


def _position():
    return lax.axis_index("x"), lax.axis_index("y"), lax.axis_index("c")


class _Comm:
    def __init__(self, ins, outs, scratch, start, finish):
        self.ins, self.outs, self.scratch, self.start, self.finish = list(ins), list(outs), list(scratch), start, finish


def _ag_comm(shards):
    n = len(shards)

    def parts(ins, outs, scr):
        send_sems, recv_sems, local_sems = scr
        x, y, c = _position()
        chips = [(1 - x, y), (x, 1 - y), (1 - x, 1 - y)]

        def copy(a, k, block, to, src=None):
            px, py, pc = block
            dst = outs[a].at[4 * px + 2 * py + pc]
            return pltpu.make_async_remote_copy(src_ref=dst if src is None else src, dst_ref=dst,
                                                send_sem=send_sems.at[a, k], recv_sem=recv_sems.at[a, k],
                                                device_id=to, device_id_type=MESH)

        mine = [pltpu.make_async_copy(ins[a], outs[a].at[4 * x + 2 * y + c], local_sems.at[a]) for a in range(n)]
        first = []
        for a in range(n):
            first.append(copy(a, 0, (x, y, c), (x, y, 1 - c), src=ins[a]))
            first += [copy(a, 1 + j, (x, y, c), (*chip, c), src=ins[a]) for j, chip in enumerate(chips)]
        return copy, mine, first, chips, (x, y, c), (x, y, 1 - c)

    def start(ins, outs, scr):
        _, mine, first, _, _, _ = parts(ins, outs, scr)
        for cp in mine + first:
            cp.start()

    def finish(ins, outs, scr):
        copy, mine, first, chips, me, sibling = parts(ins, outs, scr)
        c = me[2]
        passed = []
        for a in range(n):
            for j, chip in enumerate(chips):
                copy(a, 1 + j, (*chip, c), me).wait_recv()
                cp = copy(a, 4 + j, (*chip, c), sibling)
                cp.start()
                passed.append(cp)
        for a in range(n):
            copy(a, 0, sibling, me).wait_recv()
            for j, chip in enumerate(chips):
                copy(a, 4 + j, (*chip, 1 - c), me).wait_recv()
        for cp in first + passed:
            cp.wait_send()
        for cp in mine:
            cp.wait()

    return _Comm(shards, [jax.ShapeDtypeStruct((NDEV, *sh.shape), sh.dtype) for sh in shards],
                 [pltpu.SemaphoreType.DMA((n, 7)), pltpu.SemaphoreType.DMA((n, 7)), pltpu.SemaphoreType.DMA((n,))],
                 start, finish)


def _run_comm(name, comm):
    n_in, n_out = len(comm.ins), len(comm.outs)

    def body(*refs):
        ins, outs, scr = refs[:n_in], refs[n_in:n_in + n_out], refs[n_in + n_out:]
        comm.start(ins, outs, scr)
        comm.finish(ins, outs, scr)

    return pl.pallas_call(body, name=name, in_specs=[ANY] * n_in, out_specs=[ANY] * n_out, out_shape=comm.outs,
                          scratch_shapes=comm.scratch)(*comm.ins)


def _all_gather_small(part):
    def body(in_ref, out_ref, send_sems, recv_sems, local_sem):
        x, y, c = _position()
        me = 4 * x + 2 * y + c
        mine = pltpu.make_async_copy(in_ref, out_ref.at[me], local_sem)
        mine.start()
        flips = [(fx, fy, fc) for fx in (0, 1) for fy in (0, 1) for fc in (0, 1)][1:]
        copies = []
        for k, (fx, fy, fc) in enumerate(flips):
            cp = pltpu.make_async_remote_copy(src_ref=in_ref, dst_ref=out_ref.at[me], send_sem=send_sems.at[k],
                                              recv_sem=recv_sems.at[k],
                                              device_id=(x ^ fx, y ^ fy, c ^ fc), device_id_type=MESH)
            cp.start()
            copies.append(cp)
        for k, (fx, fy, fc) in enumerate(flips):
            peer = 4 * (x ^ fx) + 2 * (y ^ fy) + (c ^ fc)
            pltpu.make_async_remote_copy(src_ref=in_ref, dst_ref=out_ref.at[peer], send_sem=send_sems.at[k],
                                         recv_sem=recv_sems.at[k], device_id=(x, y, c), device_id_type=MESH).wait_recv()
        for cp in copies:
            cp.wait_send()
        mine.wait()

    return pl.pallas_call(
        body, name="all_gather_small", in_specs=[ANY], out_specs=ANY,
        out_shape=jax.ShapeDtypeStruct((NDEV, *part.shape), part.dtype),
        scratch_shapes=[pltpu.SemaphoreType.DMA((7,)), pltpu.SemaphoreType.DMA((7,)), pltpu.SemaphoreType.DMA],
    )(part)


def _sibling_comm(grads):
    n = len(grads)

    def copies(ins, outs, scr):
        send_sems, recv_sems = scr
        x, y, c = _position()
        return [pltpu.make_async_remote_copy(src_ref=ins[a].at[k, 1 - c], dst_ref=outs[a].at[k],
                                             send_sem=send_sems.at[a, k], recv_sem=recv_sems.at[a, k],
                                             device_id=(x, y, 1 - c), device_id_type=MESH)
                for a in range(n) for k in range(4)]

    def start(ins, outs, scr):
        for cp in copies(ins, outs, scr):
            cp.start()

    def finish(ins, outs, scr):
        for cp in copies(ins, outs, scr):
            cp.wait()

    return _Comm(grads, [jax.ShapeDtypeStruct((4, *g.shape[2:]), g.dtype) for g in grads],
                 [pltpu.SemaphoreType.DMA((n, 4)), pltpu.SemaphoreType.DMA((n, 4))], start, finish)


def _pair_sum(g, recv, core):
    _, _, r, c = g.shape
    tr = _tile(r, 256)

    def body(core_ref, g_ref, r_ref, o_ref):
        o_ref[...] = (g_ref[...].astype(F32) + r_ref[...].astype(F32)).astype(o_ref.dtype)

    return pl.pallas_call(
        body, name="pair_sum",
        grid_spec=pltpu.PrefetchScalarGridSpec(
            num_scalar_prefetch=1, grid=(4, r // tr),
            in_specs=[pl.BlockSpec((None, None, tr, c), lambda k, i, core_ref: (k, core_ref[0], i, 0)),
                      pl.BlockSpec((None, tr, c), lambda k, i, core_ref: (k, i, 0))],
            out_specs=pl.BlockSpec((None, tr, c), lambda k, i, core_ref: (k, i, 0))),
        out_shape=jax.ShapeDtypeStruct((4, r, c), g.dtype),
        compiler_params=_params(("parallel", "parallel")),
    )(core, g, recv)


def _chips_comm(sums):
    n = len(sums)

    def copies(ins, outs, scr):
        send_sems, recv_sems, local_sems = scr
        x, y, c = _position()
        chips = [(1 - x, y), (x, 1 - y), (1 - x, 1 - y)]
        own = [pltpu.make_async_copy(ins[a].at[2 * x + y], outs[a].at[3], local_sems.at[a]) for a in range(n)]
        remote = [pltpu.make_async_remote_copy(src_ref=ins[a].at[2 * px + py], dst_ref=outs[a].at[j],
                                               send_sem=send_sems.at[a, j], recv_sem=recv_sems.at[a, j],
                                               device_id=(px, py, c), device_id_type=MESH)
                  for a in range(n) for j, (px, py) in enumerate(chips)]
        return own + remote

    def start(ins, outs, scr):
        for cp in copies(ins, outs, scr):
            cp.start()

    def finish(ins, outs, scr):
        for cp in copies(ins, outs, scr):
            cp.wait()

    return _Comm(sums, [jax.ShapeDtypeStruct(sm.shape, sm.dtype) for sm in sums],
                 [pltpu.SemaphoreType.DMA((n, 3)), pltpu.SemaphoreType.DMA((n, 3)), pltpu.SemaphoreType.DMA((n,))],
                 start, finish)


def _sum_small(gathered):
    _, r, l = gathered.shape

    def body(g_ref, o_ref):
        acc = g_ref[0]
        for d in range(1, NDEV):
            acc = acc + g_ref[d]
        o_ref[...] = acc

    return pl.pallas_call(
        body, name="sum_small", in_specs=[pl.BlockSpec((NDEV, r, l), lambda: (0, 0, 0))],
        out_specs=pl.BlockSpec((r, l), lambda: (0, 0)), out_shape=jax.ShapeDtypeStruct((r, l), F32),
    )(gathered)


def _adamw(name, w, m, v, parts, part_specs, tr):
    r, c = w.shape
    n_parts = len(parts)

    def body(*refs):
        w_ref, m_ref, v_ref = refs[:3]
        p_refs = refs[3:3 + n_parts]
        g_ref, d_ref, nm_ref, nv_ref = refs[3 + n_parts:]
        g = p_refs[0][...].astype(F32)
        for p in p_refs[1:]:
            g = g + p[...].astype(F32)
        nm = ADAM_B1 * m_ref[...] + (1.0 - ADAM_B1) * g
        nv = ADAM_B2 * v_ref[...] + (1.0 - ADAM_B2) * jnp.square(g)
        m_hat = nm / (1.0 - ADAM_B1 ** ADAM_STEP)
        v_hat = nv / (1.0 - ADAM_B2 ** ADAM_STEP)
        g_ref[...] = g
        d_ref[...] = -ADAM_LR * (m_hat / (jnp.sqrt(v_hat) + ADAM_EPS) + ADAM_WD * w_ref[...])
        nm_ref[...] = nm
        nv_ref[...] = nv

    blk = pl.BlockSpec((tr, c), lambda i: (i, 0))
    sds = jax.ShapeDtypeStruct((r, c), F32)
    return pl.pallas_call(
        body, name=name, grid=(r // tr,),
        in_specs=[blk, blk, blk, *part_specs], out_specs=[blk] * 4, out_shape=[sds] * 4,
        compiler_params=_params(("parallel",)),
    )(w, m, v, *parts)


def _adamw_big(name, w, m, v, recv):
    r, c = w.shape
    tr = _tile(r, 128)
    order = (3, 0, 1, 2)
    specs = [pl.BlockSpec((None, tr, c), functools.partial(lambda i, slot: (slot, i, 0), slot=sl)) for sl in order]
    return _adamw(name, w, m, v, [recv] * 4, specs, tr)


def _adamw_small(name, w, m, v, g):
    r, c = w.shape
    return _adamw(name, w, m, v, [g], [pl.BlockSpec((r, c), lambda i: (0, 0))], r)


def kernel(x, g_pre_mix, w_in, b_in, w_dw, b_dw, g_conv_ln, b_conv_ln, w_sb_out, w_conv_out, w_o, g_post_mix, g_pre_mlp, w_up, w_down, g_post_mlp, loss_target, m_g_pre_mix, m_w_in, m_b_in, m_w_dw, m_b_dw, m_g_conv_ln, m_b_conv_ln, m_w_sb_out, m_w_conv_out, m_w_o, m_g_post_mix, m_g_pre_mlp, m_w_up, m_w_down, m_g_post_mlp, v_g_pre_mix, v_w_in, v_b_in, v_w_dw, v_b_dw, v_g_conv_ln, v_b_conv_ln, v_w_sb_out, v_w_conv_out, v_w_o, v_g_post_mix, v_g_pre_mlp, v_w_up, v_w_down, v_g_post_mlp):
    xs, tgt = x[0], loss_target[0]
    s, d = xs.shape
    d_half = d // 2
    n_heads = d_half // HEAD_DIM
    d_ff = NDEV * w_up.shape[2]
    core = lax.axis_index("c").astype(jnp.int32).reshape(1)
    dev = 4 * lax.axis_index("x") + 2 * lax.axis_index("y") + lax.axis_index("c")

    w_dw_pad = jnp.pad(w_dw[0], ((0, CONV_PAD - CONV_WIDTH), (0, 0)))
    shards = [w_in[0].astype(BF16), w_sb_out[0].astype(BF16), w_conv_out[0].astype(BF16), w_o[0].astype(BF16),
              w_up[0].astype(BF16), w_down[0].astype(BF16), w_dw_pad]
    sh_in, sh_sb, sh_cv, sh_o, sh_up, sh_down, sh_dw = shards
    wg_in, wg_dw = _run_comm("all_gather_first", _ag_comm([sh_in, sh_dw]))
    wf_dw = wg_dw.transpose(1, 0, 2).reshape(CONV_PAD, d_half)

    h, h_t = _prenorm(xs, g_pre_mix)
    proj, wg_sb, wg_cv, wg_o = _mm_cols("proj", h, wg_in, bias=b_in, comm=_ag_comm([sh_sb, sh_cv, sh_o]))
    o_sb, o_sb_t, wg_up, wg_down = _attn_fwd(proj, n_heads, _ag_comm([sh_up, sh_down]))
    wf_o = wg_o.reshape(d, d)
    wf_down = wg_down.reshape(d_ff, d)
    u1 = _conv_fwd(proj, wf_dw, b_dw, d_half)
    u3, u3_t = _ln_silu(u1, g_conv_ln, b_conv_ln)
    o_sbp = _mm_cols("sb_out", o_sb, wg_sb)[0]
    o_cv = _mm_cols("conv_out", u3, wg_cv)[0]
    merged, merged_t = _merge(proj, o_sbp, o_cv, d)
    y = _mm_plain("w_o", merged, wf_o, False, F32)[0]
    x1, h2, h2_t = _postnorm_mix(xs, y, g_post_mix, g_pre_mlp)

    tm_up = _tile(s, 1024)
    ns_up = wg_up.shape[2]

    def up_epilogue(acc):
        f = jnp.square(jnp.maximum(acc, 0.0))
        return acc, f, f.T

    a_act, f, f_t = _matmul(
        "w_up", h2, wg_up,
        [jax.ShapeDtypeStruct((s, d_ff), BF16), jax.ShapeDtypeStruct((s, d_ff), BF16), jax.ShapeDtypeStruct((d_ff, s), BF16)],
        (s // tm_up, NDEV, d // _tile(d, 1024)),
        pl.BlockSpec((tm_up, _tile(d, 1024)), lambda i, dd, kk: (i, kk)),
        pl.BlockSpec((None, _tile(d, 1024), ns_up), lambda i, dd, kk: (dd, kk, 0)),
        [pl.BlockSpec((tm_up, ns_up), lambda i, dd, kk: (i, dd)), pl.BlockSpec((tm_up, ns_up), lambda i, dd, kk: (i, dd)),
         pl.BlockSpec((ns_up, tm_up), lambda i, dd, kk: (dd, i))],
        1, False, (tm_up, ns_up), epilogue=up_epilogue)
    f2 = _mm_plain("w_down", f, wf_down, False, F32)[0]
    dx2, df2, dg_post_mlp, loss_part = _loss_head(x1, f2, tgt, g_post_mlp)

    tm_b, tn_b = _tile(s, 1024), _tile(d_ff, 1024)
    da = _mm_plain("w_down_bwd", df2, wf_down, True, BF16,
                   extra=(a_act,), extra_specs=(pl.BlockSpec((tm_b, tn_b), lambda i, j, kk: (i, j)),),
                   epilogue=lambda acc, av: (acc * (2.0 * jnp.maximum(av.astype(F32), 0.0)),),
                   outs=[jax.ShapeDtypeStruct((s, d_ff), BF16)],
                   out_specs=[pl.BlockSpec((tm_b, tn_b), lambda i, j, kk: (i, j))])[0]
    gw_down = _mm_plain("w_down_grad", f_t, df2, False, BF16)[0]
    gw_up = _mm_dw_cols("w_up_grad", h2_t, da)
    big_down = gw_down.reshape(4, 2, d_ff // NDEV, d)
    big_up = gw_up.reshape(4, 2, d, d_ff // NDEV)
    dh2, sib_down, sib_up = _mm_cols_t("w_up_bwd", da, wg_up, comm=_sibling_comm([big_down, big_up]))
    sum_down = _pair_sum(big_down, sib_down, core)
    sum_up = _pair_sum(big_up, sib_up, core)
    dx1, dy, dg_pre_mlp, dg_post_mix = _midnorm_bwd(dx2, dh2, x1, y, g_post_mix, g_pre_mlp)
    gw_o = _mm_plain("w_o_grad", merged_t, dy, False, BF16)[0]
    dmerged = _mm_plain("w_o_bwd", dy, wf_o, True, F32)[0]
    do_sbp, do_cv, dgate_sb, dgate_cv = _merge_bwd(dmerged, proj, o_sbp, o_cv, d)
    gw_cv = _mm_dw_cols("conv_out_grad", u3_t, do_cv)
    gw_sb = _mm_dw_cols("sb_out_grad", o_sb_t, do_sbp)
    du3 = _mm_cols_t("conv_out_bwd", do_cv, wg_cv)[0]
    do_sb = _mm_cols_t("sb_out_bwd", do_sbp, wg_sb)[0]
    du1, dg_ln, db_ln = _ln_silu_bwd(du3, u1, g_conv_ln, b_conv_ln)
    dglu_a, dglu_b, dw_dw, db_dw = _conv_bwd(du1, proj, wf_dw, d_half)
    big_mid = [gw_sb.reshape(4, 2, d_half, d // NDEV), gw_cv.reshape(4, 2, d_half, d // NDEV),
               gw_o.reshape(4, 2, d // NDEV, d)]
    sib_mid = _run_comm("exchange_sibling_mid", _sibling_comm(big_mid))
    sum_sb, sum_cv, sum_o = [_pair_sum(g, r, core) for g, r in zip(big_mid, sib_mid)]
    dq, dk, dv, r_sb, r_cv, r_o, r_up, r_down = _attn_bwd(
        proj, do_sb, n_heads, _chips_comm([sum_sb, sum_cv, sum_o, sum_up, sum_down]))
    dproj = jnp.concatenate([dq, dk, dv, dglu_a, dglu_b, dgate_sb, dgate_cv], axis=1)
    db_in = _colsum(dproj)
    gw_in = _mm_dw_cols("w_in_grad", h_t, dproj)
    big_in = gw_in.reshape(4, 2, d, gw_in.shape[2])
    dh, sib_in = _mm_cols_t("w_in_bwd", dproj, wg_in, comm=_sibling_comm([big_in]))
    sum_in = _pair_sum(big_in, sib_in, core)
    grad_x, dg_pre_mix = _prenorm_bwd(dx1, dh, xs, g_pre_mix)
    r_in, = _run_comm("exchange_chips_in", _chips_comm([sum_in]))
    recv = [r_in, r_sb, r_cv, r_o, r_up, r_down]

    small = [dg_pre_mix, db_in, dw_dw.reshape(1, -1), db_dw, dg_ln, db_ln, dg_post_mix, dg_pre_mlp, dg_post_mlp]
    sizes = [a.shape[1] for a in small]
    packed = jnp.concatenate(small, axis=1).reshape(-1, LANES)
    total = _sum_small(_all_gather_small(packed)).reshape(1, -1)
    offs = [0]
    for n in sizes:
        offs.append(offs[-1] + n)
    (g_g_pre_mix, g_b_in, g_w_dw_flat, g_b_dw, g_g_conv_ln, g_b_conv_ln, g_g_post_mix, g_g_pre_mlp,
     g_g_post_mlp) = [total[:, offs[k]:offs[k + 1]] for k in range(len(sizes))]
    ch = w_dw.shape[2]
    g_w_dw = lax.dynamic_slice_in_dim(g_w_dw_flat.reshape(CONV_PAD, d_half), dev * ch, ch, axis=1)[:CONV_WIDTH]

    loss = lax.psum(loss_part[0, 0], ("x", "y", "c"))

    res = {}
    res["g_pre_mix"] = _adamw_small("adamw_g_pre_mix", g_pre_mix, m_g_pre_mix, v_g_pre_mix, g_g_pre_mix)
    res["w_in"] = _adamw_big("adamw_w_in", w_in[0], m_w_in[0], v_w_in[0], recv[0])
    res["b_in"] = _adamw_small("adamw_b_in", b_in, m_b_in, v_b_in, g_b_in)
    res["w_dw"] = _adamw_small("adamw_w_dw", w_dw[0], m_w_dw[0], v_w_dw[0], g_w_dw)
    res["b_dw"] = _adamw_small("adamw_b_dw", b_dw, m_b_dw, v_b_dw, g_b_dw)
    res["g_conv_ln"] = _adamw_small("adamw_g_conv_ln", g_conv_ln, m_g_conv_ln, v_g_conv_ln, g_g_conv_ln)
    res["b_conv_ln"] = _adamw_small("adamw_b_conv_ln", b_conv_ln, m_b_conv_ln, v_b_conv_ln, g_b_conv_ln)
    res["w_sb_out"] = _adamw_big("adamw_w_sb_out", w_sb_out[0], m_w_sb_out[0], v_w_sb_out[0], recv[1])
    res["w_conv_out"] = _adamw_big("adamw_w_conv_out", w_conv_out[0], m_w_conv_out[0], v_w_conv_out[0], recv[2])
    res["w_o"] = _adamw_big("adamw_w_o", w_o[0], m_w_o[0], v_w_o[0], recv[3])
    res["g_post_mix"] = _adamw_small("adamw_g_post_mix", g_post_mix, m_g_post_mix, v_g_post_mix, g_g_post_mix)
    res["g_pre_mlp"] = _adamw_small("adamw_g_pre_mlp", g_pre_mlp, m_g_pre_mlp, v_g_pre_mlp, g_g_pre_mlp)
    res["w_up"] = _adamw_big("adamw_w_up", w_up[0], m_w_up[0], v_w_up[0], recv[4])
    res["w_down"] = _adamw_big("adamw_w_down", w_down[0], m_w_down[0], v_w_down[0], recv[5])
    res["g_post_mlp"] = _adamw_small("adamw_g_post_mlp", g_post_mlp, m_g_post_mlp, v_g_post_mlp, g_g_post_mlp)

    names = ["g_pre_mix", "w_in", "b_in", "w_dw", "b_dw", "g_conv_ln", "b_conv_ln", "w_sb_out", "w_conv_out", "w_o",
             "g_post_mix", "g_pre_mlp", "w_up", "w_down", "g_post_mlp"]
    three_d = {"w_in", "w_dw", "w_sb_out", "w_conv_out", "w_o", "w_up", "w_down"}

    def shaped(nm, arr):
        return arr[None] if nm in three_d else arr

    out = [loss, grad_x[None]]
    for k in range(4):
        out += [shaped(nm, res[nm][k]) for nm in names]
    return tuple(out)
```

```python
import functools
import math

import jax
import jax.numpy as jnp
from jax import lax
from jax.experimental import pallas as pl
from jax.experimental.pallas import tpu as pltpu

F32 = jnp.float32
BF16 = jnp.bfloat16
NDEV = 8
LANES = 128
EPS = 1e-6
CONV_WIDTH = 31
CONV_PAD = 32
HEAD_DIM = 128
ADAM_LR = 0.001
ADAM_B1 = 0.9
ADAM_B2 = 0.999
ADAM_EPS = 1e-08
ADAM_WD = 0.01
ADAM_STEP = 10
VMEM_LIMIT = 56 * 1024 * 1024
MESH = pl.DeviceIdType.MESH
ANY = pl.BlockSpec(memory_space=pl.ANY)


def _tile(n, pref):
    t = min(n, pref)
    assert n % t == 0, (n, t)
    return t


def _params(sem):
    return pltpu.CompilerParams(dimension_semantics=sem, vmem_limit_bytes=VMEM_LIMIT)


def _sigmoid(v):
    return 1.0 / (1.0 + jnp.exp(-v))


def _comm_steps(comm, grid, c_ins, c_outs, c_scr, at_start):
    pids = [pl.program_id(ax) for ax in range(len(grid))]
    edge = [p == (0 if at_start else g - 1) for p, g in zip(pids, grid)]

    @pl.when(functools.reduce(jnp.logical_and, edge))
    def _():
        (comm.start if at_start else comm.finish)(c_ins, c_outs, c_scr)


def _matmul(name, a, b, outs, grid, a_spec, b_spec, out_specs, n_red, nt, acc_shape,
            extra=(), extra_specs=(), epilogue=None, comm=None):
    n_extra = len(extra)
    n_out = len(outs)
    comm = comm or _Comm([], [], [], None, None)
    n_cin, n_cout = len(comm.ins), len(comm.outs)
    red_axes = tuple(range(len(grid) - n_red, len(grid)))
    red_sizes = tuple(grid[ax] for ax in red_axes)
    dims = (((1,), (1,)), ((), ())) if nt else (((1,), (0,)), ((), ()))

    def body(*refs):
        a_ref, b_ref = refs[0], refs[1]
        ex_refs = refs[2:2 + n_extra]
        pos = 2 + n_extra
        c_ins = refs[pos:pos + n_cin]
        o_refs = refs[pos + n_cin:pos + n_cin + n_out]
        pos += n_cin + n_out
        c_outs = refs[pos:pos + n_cout]
        acc_ref = refs[pos + n_cout]
        c_scr = refs[pos + n_cout + 1:]
        if n_cin:
            _comm_steps(comm, grid, c_ins, c_outs, c_scr, True)
        def write(acc):
            vals = (acc,) if epilogue is None else epilogue(acc, *[r[...] for r in ex_refs])
            for o_ref, val in zip(o_refs, vals):
                o_ref[...] = val.astype(o_ref.dtype)

        part = lax.dot_general(a_ref[...].astype(BF16), b_ref[...].astype(BF16), dims, preferred_element_type=F32)
        if all(sz == 1 for sz in red_sizes):
            write(part)
        else:
            ks = [pl.program_id(ax) for ax in red_axes]
            first = functools.reduce(jnp.logical_and, [k == 0 for k in ks])
            last = functools.reduce(jnp.logical_and, [k == sz - 1 for k, sz in zip(ks, red_sizes)])

            @pl.when(first)
            def _():
                acc_ref[...] = part

            @pl.when(jnp.logical_not(first))
            def _():
                acc_ref[...] += part

            @pl.when(last)
            def _():
                write(acc_ref[...])

        if n_cin:
            _comm_steps(comm, grid, c_ins, c_outs, c_scr, False)

    sem = (("arbitrary",) * len(grid) if n_cin else
           ("parallel",) * (len(grid) - n_red) + ("arbitrary",) * n_red)
    res = pl.pallas_call(
        body, name=name, grid=grid,
        in_specs=[a_spec, b_spec, *extra_specs, *[ANY] * n_cin],
        out_specs=[*out_specs, *[ANY] * n_cout], out_shape=[*outs, *comm.outs],
        scratch_shapes=[pltpu.VMEM(acc_shape if any(sz > 1 for sz in red_sizes) else (8, LANES), F32), *comm.scratch], compiler_params=_params(sem),
    )(a, b, *extra, *comm.ins)
    return res


def _mm_cols(name, a, wg, bias=None, out_dtype=F32, comm=None):
    m, k = a.shape
    _, _, ns = wg.shape
    tm, tk = _tile(m, 1024), _tile(k, 2048)
    grid = (m // tm, NDEV, k // tk)
    extra, extra_specs, epi = (), (), None
    if bias is not None:
        extra, extra_specs = (bias,), (pl.BlockSpec((1, ns), lambda i, d, kk: (0, d)),)
        epi = lambda acc, bv: (acc + bv,)
    return _matmul(name, a, wg, [jax.ShapeDtypeStruct((m, NDEV * ns), out_dtype)], grid,
                   pl.BlockSpec((tm, tk), lambda i, d, kk: (i, kk)),
                   pl.BlockSpec((None, tk, ns), lambda i, d, kk: (d, kk, 0)),
                   [pl.BlockSpec((tm, ns), lambda i, d, kk: (i, d))], 1, False, (tm, ns),
                   extra, extra_specs, epi, comm)


def _mm_cols_t(name, a, wg, comm=None):
    m, _ = a.shape
    _, n, ns = wg.shape
    tm, tn = _tile(m, 1024), _tile(n, 1024)
    grid = (m // tm, n // tn, NDEV)
    return _matmul(name, a, wg, [jax.ShapeDtypeStruct((m, n), F32)], grid,
                   pl.BlockSpec((tm, ns), lambda i, j, d: (i, d)),
                   pl.BlockSpec((None, tn, ns), lambda i, j, d: (d, j, 0)),
                   [pl.BlockSpec((tm, tn), lambda i, j, d: (i, j))], 1, True, (tm, tn), comm=comm)


def _mm_dw_cols(name, at, g):
    m, t = at.shape
    ns = g.shape[1] // NDEV
    tm, tk = _tile(m, 1024), _tile(t, 2048)
    grid = (m // tm, NDEV, t // tk)
    return _matmul(name, at, g, [jax.ShapeDtypeStruct((NDEV, m, ns), BF16)], grid,
                   pl.BlockSpec((tm, tk), lambda i, d, kk: (i, kk)),
                   pl.BlockSpec((tk, ns), lambda i, d, kk: (kk, d)),
                   [pl.BlockSpec((None, tm, ns), lambda i, d, kk: (d, i, 0))], 1, False, (tm, ns))[0]


def _mm_plain(name, a, b, nt, out_dtype, extra=(), extra_specs=(), epilogue=None, outs=None, out_specs=None):
    m, k = a.shape
    n = b.shape[0] if nt else b.shape[1]
    tm, tn, tk = _tile(m, 1024), _tile(n, 1024), _tile(k, 2048)
    grid = (m // tm, n // tn, k // tk)
    b_spec = (pl.BlockSpec((tn, tk), lambda i, j, kk: (j, kk)) if nt
              else pl.BlockSpec((tk, tn), lambda i, j, kk: (kk, j)))
    if outs is None:
        outs = [jax.ShapeDtypeStruct((m, n), out_dtype)]
        out_specs = [pl.BlockSpec((tm, tn), lambda i, j, kk: (i, j))]
    return _matmul(name, a, b, outs, grid, pl.BlockSpec((tm, tk), lambda i, j, kk: (i, kk)), b_spec,
                   out_specs, 1, nt, (tm, tn), extra, extra_specs, epilogue)


def _rms_rows(v):
    return lax.rsqrt(jnp.mean(v * v, axis=-1, keepdims=True) + EPS)


def _prenorm(x, g):
    s, d = x.shape
    ts = _tile(s, 256)

    def body(x_ref, g_ref, h_ref, ht_ref):
        xv = x_ref[...]
        h = xv * _rms_rows(xv) * g_ref[...]
        h_ref[...] = h.astype(BF16)
        ht_ref[...] = h.T.astype(BF16)

    return pl.pallas_call(
        body, name="prenorm", grid=(s // ts,),
        in_specs=[pl.BlockSpec((ts, d), lambda i: (i, 0)), pl.BlockSpec((1, d), lambda i: (0, 0))],
        out_specs=[pl.BlockSpec((ts, d), lambda i: (i, 0)), pl.BlockSpec((d, ts), lambda i: (0, i))],
        out_shape=[jax.ShapeDtypeStruct((s, d), BF16), jax.ShapeDtypeStruct((d, s), BF16)],
        compiler_params=_params(("parallel",)),
    )(x, g)


def _ln_silu(u1, g, b):
    s, c = u1.shape
    ts = _tile(s, 256)

    def body(u_ref, g_ref, b_ref, o_ref, ot_ref):
        u = u_ref[...]
        mu = jnp.mean(u, axis=-1, keepdims=True)
        var = jnp.mean(jnp.square(u - mu), axis=-1, keepdims=True)
        u2 = (u - mu) * lax.rsqrt(var + EPS) * g_ref[...] + b_ref[...]
        u3 = u2 * _sigmoid(u2)
        o_ref[...] = u3.astype(BF16)
        ot_ref[...] = u3.T.astype(BF16)

    return pl.pallas_call(
        body, name="ln_silu", grid=(s // ts,),
        in_specs=[pl.BlockSpec((ts, c), lambda i: (i, 0)), pl.BlockSpec((1, c), lambda i: (0, 0)),
                  pl.BlockSpec((1, c), lambda i: (0, 0))],
        out_specs=[pl.BlockSpec((ts, c), lambda i: (i, 0)), pl.BlockSpec((c, ts), lambda i: (0, i))],
        out_shape=[jax.ShapeDtypeStruct((s, c), BF16), jax.ShapeDtypeStruct((c, s), BF16)],
        compiler_params=_params(("parallel",)),
    )(u1, g, b)


def _ln_silu_bwd(du3, u1, g, b):
    s, c = u1.shape
    ts = _tile(s, 256)

    def body(d_ref, u_ref, g_ref, b_ref, du1_ref, dg_ref, db_ref):
        @pl.when(pl.program_id(0) == 0)
        def _():
            dg_ref[...] = jnp.zeros_like(dg_ref)
            db_ref[...] = jnp.zeros_like(db_ref)

        u = u_ref[...]
        mu = jnp.mean(u, axis=-1, keepdims=True)
        var = jnp.mean(jnp.square(u - mu), axis=-1, keepdims=True)
        rstd = lax.rsqrt(var + EPS)
        uhat = (u - mu) * rstd
        u2 = uhat * g_ref[...] + b_ref[...]
        sg = _sigmoid(u2)
        du2 = d_ref[...] * (sg * (1.0 + u2 * (1.0 - sg)))
        dg_ref[...] += jnp.sum(du2 * uhat, axis=0, keepdims=True)
        db_ref[...] += jnp.sum(du2, axis=0, keepdims=True)
        duh = du2 * g_ref[...]
        du1_ref[...] = rstd * (duh - jnp.mean(duh, axis=-1, keepdims=True)
                               - uhat * jnp.mean(duh * uhat, axis=-1, keepdims=True))

    row = pl.BlockSpec((ts, c), lambda i: (i, 0))
    vec = pl.BlockSpec((1, c), lambda i: (0, 0))
    return pl.pallas_call(
        body, name="ln_silu_bwd", grid=(s // ts,),
        in_specs=[row, row, vec, vec], out_specs=[row, vec, vec],
        out_shape=[jax.ShapeDtypeStruct((s, c), F32), jax.ShapeDtypeStruct((1, c), F32),
                   jax.ShapeDtypeStruct((1, c), F32)],
        compiler_params=_params(("arbitrary",)),
    )(du3, u1, g, b)


def _merge(proj, o_sbp, o_cv, d):
    s = proj.shape[0]
    w = d // 2
    ts = _tile(s, 256)

    def body(gs_ref, gc_ref, a_ref, b_ref, m_ref, mt_ref):
        mg = _sigmoid(gs_ref[...]) * a_ref[...] + _sigmoid(gc_ref[...]) * b_ref[...]
        m_ref[...] = mg.astype(BF16)
        mt_ref[...] = mg.T.astype(BF16)

    blk = pl.BlockSpec((ts, w), lambda i, j: (i, j))
    return pl.pallas_call(
        body, name="merge", grid=(s // ts, 2),
        in_specs=[pl.BlockSpec((ts, w), lambda i, j: (i, 5 + j)), pl.BlockSpec((ts, w), lambda i, j: (i, 7 + j)),
                  blk, blk],
        out_specs=[blk, pl.BlockSpec((w, ts), lambda i, j: (j, i))],
        out_shape=[jax.ShapeDtypeStruct((s, d), BF16), jax.ShapeDtypeStruct((d, s), BF16)],
        compiler_params=_params(("parallel", "parallel")),
    )(proj, proj, o_sbp, o_cv)


def _merge_bwd(dmerged, proj, o_sbp, o_cv, d):
    s = proj.shape[0]
    w = d // 2
    ts = _tile(s, 256)

    def body(dm_ref, gs_ref, gc_ref, a_ref, b_ref, da_ref, db_ref, dgs_ref, dgc_ref):
        dm = dm_ref[...]
        ss = _sigmoid(gs_ref[...])
        sc = _sigmoid(gc_ref[...])
        da_ref[...] = (dm * ss).astype(BF16)
        db_ref[...] = (dm * sc).astype(BF16)
        dgs_ref[...] = (dm * a_ref[...] * ss * (1.0 - ss)).astype(BF16)
        dgc_ref[...] = (dm * b_ref[...] * sc * (1.0 - sc)).astype(BF16)

    blk = pl.BlockSpec((ts, w), lambda i, j: (i, j))
    sds = jax.ShapeDtypeStruct((s, d), BF16)
    return pl.pallas_call(
        body, name="merge_bwd", grid=(s // ts, 2),
        in_specs=[blk, pl.BlockSpec((ts, w), lambda i, j: (i, 5 + j)), pl.BlockSpec((ts, w), lambda i, j: (i, 7 + j)),
                  blk, blk],
        out_specs=[blk, blk, blk, blk], out_shape=[sds, sds, sds, sds],
        compiler_params=_params(("parallel", "parallel")),
    )(dmerged, proj, proj, o_sbp, o_cv)


def _postnorm_mix(x, y, g_post, g_pre):
    s, d = x.shape
    ts = _tile(s, 256)

    def body(x_ref, y_ref, gp_ref, gn_ref, x1_ref, h_ref, ht_ref):
        yv = y_ref[...]
        x1 = x_ref[...] + yv * _rms_rows(yv) * gp_ref[...]
        x1_ref[...] = x1
        h = x1 * _rms_rows(x1) * gn_ref[...]
        h_ref[...] = h.astype(BF16)
        ht_ref[...] = h.T.astype(BF16)

    row = pl.BlockSpec((ts, d), lambda i: (i, 0))
    vec = pl.BlockSpec((1, d), lambda i: (0, 0))
    return pl.pallas_call(
        body, name="postnorm_mix", grid=(s // ts,),
        in_specs=[row, row, vec, vec], out_specs=[row, row, pl.BlockSpec((d, ts), lambda i: (0, i))],
        out_shape=[jax.ShapeDtypeStruct((s, d), F32), jax.ShapeDtypeStruct((s, d), BF16),
                   jax.ShapeDtypeStruct((d, s), BF16)],
        compiler_params=_params(("parallel",)),
    )(x, y, g_post, g_pre)


def _rms_bwd(dout, vin, g):
    r = _rms_rows(vin)
    vhat = vin * r
    dyh = dout * g
    dvin = r * (dyh - vhat * jnp.mean(dyh * vhat, axis=-1, keepdims=True))
    return dvin, jnp.sum(dout * vhat, axis=0, keepdims=True)


def _loss_head(x1, f2, tgt, g):
    s, d = x1.shape
    ts = _tile(s, 256)

    def body(x1_ref, f_ref, t_ref, g_ref, dx2_ref, df2_ref, dg_ref, loss_ref):
        @pl.when(pl.program_id(0) == 0)
        def _():
            dg_ref[...] = jnp.zeros_like(dg_ref)
            loss_ref[...] = jnp.zeros_like(loss_ref)

        fv = f_ref[...]
        x2 = x1_ref[...] + fv * _rms_rows(fv) * g_ref[...]
        err = x2 - t_ref[...]
        loss_ref[...] += 0.5 * jnp.sum(jnp.mean(err * err, axis=-1, keepdims=True), axis=0, keepdims=True)
        dx2 = err * (1.0 / d)
        dx2_ref[...] = dx2
        df2, dg = _rms_bwd(dx2, fv, g_ref[...])
        df2_ref[...] = df2.astype(BF16)
        dg_ref[...] += dg

    row = pl.BlockSpec((ts, d), lambda i: (i, 0))
    vec = pl.BlockSpec((1, d), lambda i: (0, 0))
    return pl.pallas_call(
        body, name="loss_head", grid=(s // ts,),
        in_specs=[row, row, row, vec],
        out_specs=[row, row, vec, pl.BlockSpec((1, LANES), lambda i: (0, 0))],
        out_shape=[jax.ShapeDtypeStruct((s, d), F32), jax.ShapeDtypeStruct((s, d), BF16),
                   jax.ShapeDtypeStruct((1, d), F32), jax.ShapeDtypeStruct((1, LANES), F32)],
        compiler_params=_params(("arbitrary",)),
    )(x1, f2, tgt, g)


def _midnorm_bwd(dx2, dh2, x1, y, g_post, g_pre):
    s, d = x1.shape
    ts = _tile(s, 256)

    def body(dx2_ref, dh_ref, x1_ref, y_ref, gp_ref, gn_ref, dx1_ref, dy_ref, dgn_ref, dgp_ref):
        @pl.when(pl.program_id(0) == 0)
        def _():
            dgn_ref[...] = jnp.zeros_like(dgn_ref)
            dgp_ref[...] = jnp.zeros_like(dgp_ref)

        dxa, dgn = _rms_bwd(dh_ref[...], x1_ref[...], gn_ref[...])
        dx1 = dx2_ref[...] + dxa
        dx1_ref[...] = dx1
        dy, dgp = _rms_bwd(dx1, y_ref[...], gp_ref[...])
        dy_ref[...] = dy.astype(BF16)
        dgn_ref[...] += dgn
        dgp_ref[...] += dgp

    row = pl.BlockSpec((ts, d), lambda i: (i, 0))
    vec = pl.BlockSpec((1, d), lambda i: (0, 0))
    return pl.pallas_call(
        body, name="midnorm_bwd", grid=(s // ts,),
        in_specs=[row, row, row, row, vec, vec], out_specs=[row, row, vec, vec],
        out_shape=[jax.ShapeDtypeStruct((s, d), F32), jax.ShapeDtypeStruct((s, d), BF16),
                   jax.ShapeDtypeStruct((1, d), F32), jax.ShapeDtypeStruct((1, d), F32)],
        compiler_params=_params(("arbitrary",)),
    )(dx2, dh2, x1, y, g_post, g_pre)


def _prenorm_bwd(dx1, dh, x, g):
    s, d = x.shape
    ts = _tile(s, 256)

    def body(dx1_ref, dh_ref, x_ref, g_ref, dx_ref, dg_ref):
        @pl.when(pl.program_id(0) == 0)
        def _():
            dg_ref[...] = jnp.zeros_like(dg_ref)

        dxa, dg = _rms_bwd(dh_ref[...], x_ref[...], g_ref[...])
        dx_ref[...] = dx1_ref[...] + dxa
        dg_ref[...] += dg

    row = pl.BlockSpec((ts, d), lambda i: (i, 0))
    vec = pl.BlockSpec((1, d), lambda i: (0, 0))
    return pl.pallas_call(
        body, name="prenorm_bwd", grid=(s // ts,),
        in_specs=[row, row, row, vec], out_specs=[row, vec],
        out_shape=[jax.ShapeDtypeStruct((s, d), F32), jax.ShapeDtypeStruct((1, d), F32)],
        compiler_params=_params(("arbitrary",)),
    )(dx1, dh, x, g)


def _colsum(a):
    s, n = a.shape
    ts = _tile(s, 256)

    def body(a_ref, o_ref):
        @pl.when(pl.program_id(0) == 0)
        def _():
            o_ref[...] = jnp.zeros_like(o_ref)

        o_ref[...] += jnp.sum(a_ref[...].astype(F32), axis=0, keepdims=True)

    return pl.pallas_call(
        body, name="colsum", grid=(s // ts,),
        in_specs=[pl.BlockSpec((ts, n), lambda i: (i, 0))], out_specs=pl.BlockSpec((1, n), lambda i: (0, 0)),
        out_shape=jax.ShapeDtypeStruct((1, n), F32), compiler_params=_params(("arbitrary",)),
    )(a)


def _shift_rows(win, off, t):
    n = win.shape[0]
    if off == 0:
        return win[:t]
    return pltpu.roll(win, n - off, axis=0)[:t]


def _conv_fwd(proj, w_pad, b_dw, c_total):
    s = proj.shape[0]
    nct = c_total // LANES
    t = _tile(s, 256)

    def body(ga_ref, gb_ref, w_ref, b_ref, o_ref, u0_ref):
        u0_ref[pl.ds(0, CONV_PAD), :] = jnp.zeros((CONV_PAD, LANES), F32)
        u0_ref[pl.ds(CONV_PAD, s), :] = ga_ref[...] * _sigmoid(gb_ref[...])
        wv = w_ref[...]

        def chunk(r, carry):
            r0 = pl.multiple_of(r * t, t)
            win = u0_ref[pl.ds(r0, t + CONV_PAD), :]
            acc = jnp.broadcast_to(b_ref[...], (t, LANES))
            for j in range(CONV_WIDTH):
                acc = acc + wv[j:j + 1, :] * _shift_rows(win, j + CONV_PAD - (CONV_WIDTH - 1), t)
            o_ref[pl.ds(r0, t), :] = acc
            return carry

        lax.fori_loop(0, s // t, chunk, 0)

    return pl.pallas_call(
        body, name="conv_fwd", grid=(nct,),
        in_specs=[pl.BlockSpec((s, LANES), lambda c: (0, 3 * nct + c)), pl.BlockSpec((s, LANES), lambda c: (0, 4 * nct + c)),
                  pl.BlockSpec((CONV_PAD, LANES), lambda c: (0, c)), pl.BlockSpec((1, LANES), lambda c: (0, c))],
        out_specs=pl.BlockSpec((s, LANES), lambda c: (0, c)),
        out_shape=jax.ShapeDtypeStruct((s, c_total), F32),
        scratch_shapes=[pltpu.VMEM((s + CONV_PAD, LANES), F32)],
        compiler_params=_params(("parallel",)),
    )(proj, proj, w_pad, b_dw)


def _conv_bwd(du1, proj, w_pad, c_total):
    s = proj.shape[0]
    nct = c_total // LANES
    t = _tile(s, 256)

    def body(d_ref, ga_ref, gb_ref, w_ref, dga_ref, dgb_ref, dw_ref, db_ref, u0_ref, dp_ref):
        sg = _sigmoid(gb_ref[...])
        u0_ref[pl.ds(0, CONV_PAD), :] = jnp.zeros((CONV_PAD, LANES), F32)
        u0_ref[pl.ds(CONV_PAD, s), :] = ga_ref[...] * sg
        dp_ref[pl.ds(0, s), :] = d_ref[...]
        dp_ref[pl.ds(s, CONV_PAD), :] = jnp.zeros((CONV_PAD, LANES), F32)
        dw_ref[...] = jnp.zeros_like(dw_ref)
        db_ref[...] = jnp.sum(d_ref[...], axis=0, keepdims=True)
        wv = w_ref[...]

        def chunk(r, carry):
            r0 = pl.multiple_of(r * t, t)
            win = u0_ref[pl.ds(r0, t + CONV_PAD), :]
            dwin = dp_ref[pl.ds(r0, t + CONV_PAD), :]
            dcur = dwin[:t]
            du0 = jnp.zeros((t, LANES), F32)
            for j in range(CONV_WIDTH):
                du0 = du0 + wv[j:j + 1, :] * _shift_rows(dwin, CONV_WIDTH - 1 - j, t)
                sh = _shift_rows(win, j + CONV_PAD - (CONV_WIDTH - 1), t)
                dw_ref[j:j + 1, :] += jnp.sum(dcur * sh, axis=0, keepdims=True)
            gav = ga_ref[pl.ds(r0, t), :]
            sgv = _sigmoid(gb_ref[pl.ds(r0, t), :])
            dga_ref[pl.ds(r0, t), :] = (du0 * sgv).astype(BF16)
            dgb_ref[pl.ds(r0, t), :] = (du0 * gav * sgv * (1.0 - sgv)).astype(BF16)
            return carry

        lax.fori_loop(0, s // t, chunk, 0)

    col = pl.BlockSpec((s, LANES), lambda c: (0, c))
    return pl.pallas_call(
        body, name="conv_bwd", grid=(nct,),
        in_specs=[col, pl.BlockSpec((s, LANES), lambda c: (0, 3 * nct + c)),
                  pl.BlockSpec((s, LANES), lambda c: (0, 4 * nct + c)), pl.BlockSpec((CONV_PAD, LANES), lambda c: (0, c))],
        out_specs=[col, col, pl.BlockSpec((CONV_PAD, LANES), lambda c: (0, c)), pl.BlockSpec((1, LANES), lambda c: (0, c))],
        out_shape=[jax.ShapeDtypeStruct((s, c_total), BF16), jax.ShapeDtypeStruct((s, c_total), BF16),
                   jax.ShapeDtypeStruct((CONV_PAD, c_total), F32), jax.ShapeDtypeStruct((1, c_total), F32)],
        scratch_shapes=[pltpu.VMEM((s + CONV_PAD, LANES), F32), pltpu.VMEM((s + CONV_PAD, LANES), F32)],
        compiler_params=_params(("parallel",)),
    )(du1, proj, proj, w_pad)


TQ_PREF = 256
TK = 256


def _split_dot(v, tri):
    hi = v.astype(BF16)
    lo = (v - hi.astype(F32)).astype(BF16)
    return (jnp.dot(hi, tri, preferred_element_type=F32) + jnp.dot(lo, tri, preferred_element_type=F32))


def _sb_block(q, k_ref, j, i, tq, scale, c_lm, tri_after):
    kb = k_ref[pl.ds(pl.multiple_of(j * TK, TK), TK), :].astype(BF16)
    z = lax.dot_general(q, kb, (((1,), (1,)), ((), ())), preferred_element_type=F32) * scale
    tpos = i * tq + lax.broadcasted_iota(jnp.int32, (tq, TK), 0)
    spos = j * TK + lax.broadcasted_iota(jnp.int32, (tq, TK), 1)
    mask = spos < tpos
    sp = jnp.log(1.0 + jnp.exp(-jnp.abs(z)))
    lsz = jnp.minimum(z, 0.0) - sp
    lm = jnp.where(mask, -jnp.maximum(z, 0.0) - sp, 0.0)
    suf = c_lm + _split_dot(lm, tri_after)
    a = jnp.where(mask, jnp.exp(lsz + suf), 0.0)
    return lsz, lm, a, mask


def _tri(after):
    r = lax.broadcasted_iota(jnp.int32, (TK, TK), 0)
    c = lax.broadcasted_iota(jnp.int32, (TK, TK), 1)
    return (r > c).astype(BF16) if after else (r < c).astype(BF16)


def _attn_fwd(proj, n_heads, comm):
    s = proj.shape[0]
    tq = _tile(s, TQ_PREF)
    scale = 1.0 / math.sqrt(HEAD_DIM)
    ratio = tq // TK
    grid = (n_heads, s // tq)
    n_cin, n_cout = len(comm.ins), len(comm.outs)

    def body(*refs):
        q_ref, k_ref, v_ref = refs[:3]
        c_ins = refs[3:3 + n_cin]
        o_ref, ot_ref = refs[3 + n_cin:5 + n_cin]
        c_outs = refs[5 + n_cin:5 + n_cin + n_cout]
        acc_ref, clm_ref = refs[5 + n_cin + n_cout:7 + n_cin + n_cout]
        c_scr = refs[7 + n_cin + n_cout:]
        _comm_steps(comm, grid, c_ins, c_outs, c_scr, True)
        i = pl.program_id(1)
        q = q_ref[...].astype(BF16)
        tri_after = _tri(True)
        acc_ref[...] = jnp.zeros_like(acc_ref)
        clm_ref[...] = jnp.zeros_like(clm_ref)
        nkb = (i + 1) * ratio

        def step(jj, carry):
            for u in range(ratio):
                j = nkb - 1 - (jj * ratio + u)
                _, lm, a, _ = _sb_block(q, k_ref, j, i, tq, scale, clm_ref[...], tri_after)
                vb = v_ref[pl.ds(pl.multiple_of(j * TK, TK), TK), :].astype(BF16)
                acc_ref[...] += jnp.dot(a.astype(BF16), vb, preferred_element_type=F32)
                clm_ref[...] += jnp.sum(lm, axis=1, keepdims=True)
            return carry

        lax.fori_loop(0, i + 1, step, 0)
        o = acc_ref[...]
        o_ref[...] = o
        ot_ref[...] = o.T.astype(BF16)
        _comm_steps(comm, grid, c_ins, c_outs, c_scr, False)

    return pl.pallas_call(
        body, name="attn_fwd", grid=grid,
        in_specs=[pl.BlockSpec((tq, HEAD_DIM), lambda h, i: (i, h)),
                  pl.BlockSpec((s, HEAD_DIM), lambda h, i: (0, n_heads + h)),
                  pl.BlockSpec((s, HEAD_DIM), lambda h, i: (0, 2 * n_heads + h)), *[ANY] * n_cin],
        out_specs=[pl.BlockSpec((tq, HEAD_DIM), lambda h, i: (i, h)), pl.BlockSpec((HEAD_DIM, tq), lambda h, i: (h, i)),
                   *[ANY] * n_cout],
        out_shape=[jax.ShapeDtypeStruct((s, n_heads * HEAD_DIM), F32), jax.ShapeDtypeStruct((n_heads * HEAD_DIM, s), BF16),
                   *comm.outs],
        scratch_shapes=[pltpu.VMEM((tq, HEAD_DIM), F32), pltpu.VMEM((tq, 1), F32), *comm.scratch],
        compiler_params=_params(("arbitrary", "arbitrary")),
    )(proj, proj, proj, *comm.ins)


def _attn_bwd(proj, do_sb, n_heads, comm):
    s = proj.shape[0]
    tq = _tile(s, TQ_PREF)
    scale = 1.0 / math.sqrt(HEAD_DIM)
    ratio = tq // TK
    n_kb = s // TK
    n_qb = s // tq
    grid = (n_heads, n_qb)
    n_cin, n_cout = len(comm.ins), len(comm.outs)

    def body(*refs):
        q_ref, k_ref, v_ref, do_ref = refs[:4]
        c_ins = refs[4:4 + n_cin]
        dq_ref, dk_ref, dv_ref = refs[4 + n_cin:7 + n_cin]
        c_outs = refs[7 + n_cin:7 + n_cin + n_cout]
        dka_ref, dva_ref, dl_ref, be_ref, dqa_ref, c_ref = refs[7 + n_cin + n_cout:13 + n_cin + n_cout]
        c_scr = refs[13 + n_cin + n_cout:]
        _comm_steps(comm, grid, c_ins, c_outs, c_scr, True)
        i = pl.program_id(1)

        @pl.when(i == 0)
        def _():
            dka_ref[...] = jnp.zeros_like(dka_ref)
            dva_ref[...] = jnp.zeros_like(dva_ref)

        q = q_ref[...].astype(BF16)
        dob = do_ref[...].astype(BF16)
        tri_after = _tri(True)
        tri_before = _tri(False)
        nkb = (i + 1) * ratio

        c_ref[...] = jnp.zeros_like(c_ref)

        def sweep_a(jj, carry):
            j = nkb - 1 - jj
            lsz, lm, a, _ = _sb_block(q, k_ref, j, i, tq, scale, c_ref[...], tri_after)
            rows = pl.ds(pl.multiple_of(j * TK, TK), TK)
            vb = v_ref[rows, :].astype(BF16)
            da = lax.dot_general(dob, vb, (((1,), (1,)), ((), ())), preferred_element_type=F32)
            dl_ref[j] = da * a
            be_ref[j] = jnp.exp(lsz)
            dva_ref[rows, :] += jnp.dot(a.T.astype(BF16), dob, preferred_element_type=F32)
            c_ref[...] += jnp.sum(lm, axis=1, keepdims=True)
            return carry

        lax.fori_loop(0, nkb, sweep_a, 0)

        c_ref[...] = jnp.zeros_like(c_ref)
        dqa_ref[...] = jnp.zeros_like(dqa_ref)

        def sweep_b(j, carry):
            dl = dl_ref[j]
            beta = be_ref[j]
            tpos = i * tq + lax.broadcasted_iota(jnp.int32, (tq, TK), 0)
            spos = j * TK + lax.broadcasted_iota(jnp.int32, (tq, TK), 1)
            p = c_ref[...] + _split_dot(dl, tri_before)
            dz = jnp.where(spos < tpos, (dl * (1.0 - beta) - beta * p) * scale, 0.0)
            rows = pl.ds(pl.multiple_of(j * TK, TK), TK)
            kb = k_ref[rows, :].astype(BF16)
            dqa_ref[...] += jnp.dot(dz.astype(BF16), kb, preferred_element_type=F32)
            dka_ref[rows, :] += jnp.dot(dz.T.astype(BF16), q, preferred_element_type=F32)
            c_ref[...] += jnp.sum(dl, axis=1, keepdims=True)
            return carry

        lax.fori_loop(0, nkb, sweep_b, 0)
        dq_ref[...] = dqa_ref[...].astype(BF16)

        @pl.when(i == n_qb - 1)
        def _():
            dk_ref[...] = dka_ref[...].astype(BF16)
            dv_ref[...] = dva_ref[...].astype(BF16)

        _comm_steps(comm, grid, c_ins, c_outs, c_scr, False)

    qblk = pl.BlockSpec((tq, HEAD_DIM), lambda h, i: (i, h))
    full = pl.BlockSpec((s, HEAD_DIM), lambda h, i: (0, h))
    sds = jax.ShapeDtypeStruct((s, n_heads * HEAD_DIM), BF16)
    return pl.pallas_call(
        body, name="attn_bwd", grid=grid,
        in_specs=[qblk, pl.BlockSpec((s, HEAD_DIM), lambda h, i: (0, n_heads + h)),
                  pl.BlockSpec((s, HEAD_DIM), lambda h, i: (0, 2 * n_heads + h)), qblk, *[ANY] * n_cin],
        out_specs=[qblk, full, full, *[ANY] * n_cout], out_shape=[sds, sds, sds, *comm.outs],
        scratch_shapes=[pltpu.VMEM((s, HEAD_DIM), F32), pltpu.VMEM((s, HEAD_DIM), F32),
                        pltpu.VMEM((n_kb, tq, TK), F32), pltpu.VMEM((n_kb, tq, TK), F32),
                        pltpu.VMEM((tq, HEAD_DIM), F32), pltpu.VMEM((tq, 1), F32), *comm.scratch],
        compiler_params=_params(("arbitrary", "arbitrary")),
    )(proj, proj, proj, do_sb, *comm.ins)


def _position():
    return lax.axis_index("x"), lax.axis_index("y"), lax.axis_index("c")


class _Comm:
    def __init__(self, ins, outs, scratch, start, finish):
        self.ins, self.outs, self.scratch, self.start, self.finish = list(ins), list(outs), list(scratch), start, finish


def _ag_comm(shards):
    n = len(shards)

    def parts(ins, outs, scr):
        send_sems, recv_sems, local_sems = scr
        x, y, c = _position()
        chips = [(1 - x, y), (x, 1 - y), (1 - x, 1 - y)]

        def copy(a, k, block, to, src=None):
            px, py, pc = block
            dst = outs[a].at[4 * px + 2 * py + pc]
            return pltpu.make_async_remote_copy(src_ref=dst if src is None else src, dst_ref=dst,
                                                send_sem=send_sems.at[a, k], recv_sem=recv_sems.at[a, k],
                                                device_id=to, device_id_type=MESH)

        mine = [pltpu.make_async_copy(ins[a], outs[a].at[4 * x + 2 * y + c], local_sems.at[a]) for a in range(n)]
        first = []
        for a in range(n):
            first.append(copy(a, 0, (x, y, c), (x, y, 1 - c), src=ins[a]))
            first += [copy(a, 1 + j, (x, y, c), (*chip, c), src=ins[a]) for j, chip in enumerate(chips)]
        return copy, mine, first, chips, (x, y, c), (x, y, 1 - c)

    def start(ins, outs, scr):
        _, mine, first, _, _, _ = parts(ins, outs, scr)
        for cp in mine + first:
            cp.start()

    def finish(ins, outs, scr):
        copy, mine, first, chips, me, sibling = parts(ins, outs, scr)
        c = me[2]
        passed = []
        for a in range(n):
            for j, chip in enumerate(chips):
                copy(a, 1 + j, (*chip, c), me).wait_recv()
                cp = copy(a, 4 + j, (*chip, c), sibling)
                cp.start()
                passed.append(cp)
        for a in range(n):
            copy(a, 0, sibling, me).wait_recv()
            for j, chip in enumerate(chips):
                copy(a, 4 + j, (*chip, 1 - c), me).wait_recv()
        for cp in first + passed:
            cp.wait_send()
        for cp in mine:
            cp.wait()

    return _Comm(shards, [jax.ShapeDtypeStruct((NDEV, *sh.shape), sh.dtype) for sh in shards],
                 [pltpu.SemaphoreType.DMA((n, 7)), pltpu.SemaphoreType.DMA((n, 7)), pltpu.SemaphoreType.DMA((n,))],
                 start, finish)


def _run_comm(name, comm):
    n_in, n_out = len(comm.ins), len(comm.outs)

    def body(*refs):
        ins, outs, scr = refs[:n_in], refs[n_in:n_in + n_out], refs[n_in + n_out:]
        comm.start(ins, outs, scr)
        comm.finish(ins, outs, scr)

    return pl.pallas_call(body, name=name, in_specs=[ANY] * n_in, out_specs=[ANY] * n_out, out_shape=comm.outs,
                          scratch_shapes=comm.scratch)(*comm.ins)


def _all_gather_small(part):
    def body(in_ref, out_ref, send_sems, recv_sems, local_sem):
        x, y, c = _position()
        me = 4 * x + 2 * y + c
        mine = pltpu.make_async_copy(in_ref, out_ref.at[me], local_sem)
        mine.start()
        flips = [(fx, fy, fc) for fx in (0, 1) for fy in (0, 1) for fc in (0, 1)][1:]
        copies = []
        for k, (fx, fy, fc) in enumerate(flips):
            cp = pltpu.make_async_remote_copy(src_ref=in_ref, dst_ref=out_ref.at[me], send_sem=send_sems.at[k],
                                              recv_sem=recv_sems.at[k],
                                              device_id=(x ^ fx, y ^ fy, c ^ fc), device_id_type=MESH)
            cp.start()
            copies.append(cp)
        for k, (fx, fy, fc) in enumerate(flips):
            peer = 4 * (x ^ fx) + 2 * (y ^ fy) + (c ^ fc)
            pltpu.make_async_remote_copy(src_ref=in_ref, dst_ref=out_ref.at[peer], send_sem=send_sems.at[k],
                                         recv_sem=recv_sems.at[k], device_id=(x, y, c), device_id_type=MESH).wait_recv()
        for cp in copies:
            cp.wait_send()
        mine.wait()

    return pl.pallas_call(
        body, name="all_gather_small", in_specs=[ANY], out_specs=ANY,
        out_shape=jax.ShapeDtypeStruct((NDEV, *part.shape), part.dtype),
        scratch_shapes=[pltpu.SemaphoreType.DMA((7,)), pltpu.SemaphoreType.DMA((7,)), pltpu.SemaphoreType.DMA],
    )(part)


def _sibling_comm(grads):
    n = len(grads)

    def copies(ins, outs, scr):
        send_sems, recv_sems = scr
        x, y, c = _position()
        return [pltpu.make_async_remote_copy(src_ref=ins[a].at[k, 1 - c], dst_ref=outs[a].at[k],
                                             send_sem=send_sems.at[a, k], recv_sem=recv_sems.at[a, k],
                                             device_id=(x, y, 1 - c), device_id_type=MESH)
                for a in range(n) for k in range(4)]

    def start(ins, outs, scr):
        for cp in copies(ins, outs, scr):
            cp.start()

    def finish(ins, outs, scr):
        for cp in copies(ins, outs, scr):
            cp.wait()

    return _Comm(grads, [jax.ShapeDtypeStruct((4, *g.shape[2:]), g.dtype) for g in grads],
                 [pltpu.SemaphoreType.DMA((n, 4)), pltpu.SemaphoreType.DMA((n, 4))], start, finish)


def _pair_sum(g, recv, core):
    _, _, r, c = g.shape
    tr = _tile(r, 256)

    def body(core_ref, g_ref, r_ref, o_ref):
        o_ref[...] = (g_ref[...].astype(F32) + r_ref[...].astype(F32)).astype(o_ref.dtype)

    return pl.pallas_call(
        body, name="pair_sum",
        grid_spec=pltpu.PrefetchScalarGridSpec(
            num_scalar_prefetch=1, grid=(4, r // tr),
            in_specs=[pl.BlockSpec((None, None, tr, c), lambda k, i, core_ref: (k, core_ref[0], i, 0)),
                      pl.BlockSpec((None, tr, c), lambda k, i, core_ref: (k, i, 0))],
            out_specs=pl.BlockSpec((None, tr, c), lambda k, i, core_ref: (k, i, 0))),
        out_shape=jax.ShapeDtypeStruct((4, r, c), g.dtype),
        compiler_params=_params(("parallel", "parallel")),
    )(core, g, recv)


def _chips_comm(sums):
    n = len(sums)

    def copies(ins, outs, scr):
        send_sems, recv_sems, local_sems = scr
        x, y, c = _position()
        chips = [(1 - x, y), (x, 1 - y), (1 - x, 1 - y)]
        own = [pltpu.make_async_copy(ins[a].at[2 * x + y], outs[a].at[3], local_sems.at[a]) for a in range(n)]
        remote = [pltpu.make_async_remote_copy(src_ref=ins[a].at[2 * px + py], dst_ref=outs[a].at[j],
                                               send_sem=send_sems.at[a, j], recv_sem=recv_sems.at[a, j],
                                               device_id=(px, py, c), device_id_type=MESH)
                  for a in range(n) for j, (px, py) in enumerate(chips)]
        return own + remote

    def start(ins, outs, scr):
        for cp in copies(ins, outs, scr):
            cp.start()

    def finish(ins, outs, scr):
        for cp in copies(ins, outs, scr):
            cp.wait()

    return _Comm(sums, [jax.ShapeDtypeStruct(sm.shape, sm.dtype) for sm in sums],
                 [pltpu.SemaphoreType.DMA((n, 3)), pltpu.SemaphoreType.DMA((n, 3)), pltpu.SemaphoreType.DMA((n,))],
                 start, finish)


def _sum_small(gathered):
    _, r, l = gathered.shape

    def body(g_ref, o_ref):
        acc = g_ref[0]
        for d in range(1, NDEV):
            acc = acc + g_ref[d]
        o_ref[...] = acc

    return pl.pallas_call(
        body, name="sum_small", in_specs=[pl.BlockSpec((NDEV, r, l), lambda: (0, 0, 0))],
        out_specs=pl.BlockSpec((r, l), lambda: (0, 0)), out_shape=jax.ShapeDtypeStruct((r, l), F32),
    )(gathered)


def _adamw(name, w, m, v, parts, part_specs, tr):
    r, c = w.shape
    n_parts = len(parts)

    def body(*refs):
        w_ref, m_ref, v_ref = refs[:3]
        p_refs = refs[3:3 + n_parts]
        g_ref, d_ref, nm_ref, nv_ref = refs[3 + n_parts:]
        g = p_refs[0][...].astype(F32)
        for p in p_refs[1:]:
            g = g + p[...].astype(F32)
        nm = ADAM_B1 * m_ref[...] + (1.0 - ADAM_B1) * g
        nv = ADAM_B2 * v_ref[...] + (1.0 - ADAM_B2) * jnp.square(g)
        m_hat = nm / (1.0 - ADAM_B1 ** ADAM_STEP)
        v_hat = nv / (1.0 - ADAM_B2 ** ADAM_STEP)
        g_ref[...] = g
        d_ref[...] = -ADAM_LR * (m_hat / (jnp.sqrt(v_hat) + ADAM_EPS) + ADAM_WD * w_ref[...])
        nm_ref[...] = nm
        nv_ref[...] = nv

    blk = pl.BlockSpec((tr, c), lambda i: (i, 0))
    sds = jax.ShapeDtypeStruct((r, c), F32)
    return pl.pallas_call(
        body, name=name, grid=(r // tr,),
        in_specs=[blk, blk, blk, *part_specs], out_specs=[blk] * 4, out_shape=[sds] * 4,
        compiler_params=_params(("parallel",)),
    )(w, m, v, *parts)


def _adamw_big(name, w, m, v, recv):
    r, c = w.shape
    tr = _tile(r, 128)
    order = (3, 0, 1, 2)
    specs = [pl.BlockSpec((None, tr, c), functools.partial(lambda i, slot: (slot, i, 0), slot=sl)) for sl in order]
    return _adamw(name, w, m, v, [recv] * 4, specs, tr)


def _adamw_small(name, w, m, v, g):
    r, c = w.shape
    return _adamw(name, w, m, v, [g], [pl.BlockSpec((r, c), lambda i: (0, 0))], r)


def kernel(x, g_pre_mix, w_in, b_in, w_dw, b_dw, g_conv_ln, b_conv_ln, w_sb_out, w_conv_out, w_o, g_post_mix, g_pre_mlp, w_up, w_down, g_post_mlp, loss_target, m_g_pre_mix, m_w_in, m_b_in, m_w_dw, m_b_dw, m_g_conv_ln, m_b_conv_ln, m_w_sb_out, m_w_conv_out, m_w_o, m_g_post_mix, m_g_pre_mlp, m_w_up, m_w_down, m_g_post_mlp, v_g_pre_mix, v_w_in, v_b_in, v_w_dw, v_b_dw, v_g_conv_ln, v_b_conv_ln, v_w_sb_out, v_w_conv_out, v_w_o, v_g_post_mix, v_g_pre_mlp, v_w_up, v_w_down, v_g_post_mlp):
    xs, tgt = x[0], loss_target[0]
    s, d = xs.shape
    d_half = d // 2
    n_heads = d_half // HEAD_DIM
    d_ff = NDEV * w_up.shape[2]
    core = lax.axis_index("c").astype(jnp.int32).reshape(1)
    dev = 4 * lax.axis_index("x") + 2 * lax.axis_index("y") + lax.axis_index("c")

    w_dw_pad = jnp.pad(w_dw[0], ((0, CONV_PAD - CONV_WIDTH), (0, 0)))
    shards = [w_in[0].astype(BF16), w_sb_out[0].astype(BF16), w_conv_out[0].astype(BF16), w_o[0].astype(BF16),
              w_up[0].astype(BF16), w_down[0].astype(BF16), w_dw_pad]
    sh_in, sh_sb, sh_cv, sh_o, sh_up, sh_down, sh_dw = shards
    wg_in, wg_dw = _run_comm("all_gather_first", _ag_comm([sh_in, sh_dw]))
    wf_dw = wg_dw.transpose(1, 0, 2).reshape(CONV_PAD, d_half)

    h, h_t = _prenorm(xs, g_pre_mix)
    proj, wg_sb, wg_cv, wg_o = _mm_cols("proj", h, wg_in, bias=b_in, comm=_ag_comm([sh_sb, sh_cv, sh_o]))
    o_sb, o_sb_t, wg_up = _attn_fwd(proj, n_heads, _ag_comm([sh_up]))
    wf_o = wg_o.reshape(d, d)
    u1 = _conv_fwd(proj, wf_dw, b_dw, d_half)
    u3, u3_t = _ln_silu(u1, g_conv_ln, b_conv_ln)
    o_sbp = _mm_cols("sb_out", o_sb, wg_sb)[0]
    o_cv = _mm_cols("conv_out", u3, wg_cv)[0]
    merged, merged_t = _merge(proj, o_sbp, o_cv, d)
    y = _mm_plain("w_o", merged, wf_o, False, F32)[0]
    x1, h2, h2_t = _postnorm_mix(xs, y, g_post_mix, g_pre_mlp)

    tm_up = _tile(s, 1024)
    ns_up = wg_up.shape[2]

    def up_epilogue(acc):
        f = jnp.square(jnp.maximum(acc, 0.0))
        return acc, f, f.T

    tk_up = _tile(d, 2048)
    a_act, f, f_t, wg_down = _matmul(
        "w_up", h2, wg_up,
        [jax.ShapeDtypeStruct((s, d_ff), BF16), jax.ShapeDtypeStruct((s, d_ff), BF16), jax.ShapeDtypeStruct((d_ff, s), BF16)],
        (s // tm_up, NDEV, d // tk_up),
        pl.BlockSpec((tm_up, tk_up), lambda i, dd, kk: (i, kk)),
        pl.BlockSpec((None, tk_up, ns_up), lambda i, dd, kk: (dd, kk, 0)),
        [pl.BlockSpec((tm_up, ns_up), lambda i, dd, kk: (i, dd)), pl.BlockSpec((tm_up, ns_up), lambda i, dd, kk: (i, dd)),
         pl.BlockSpec((ns_up, tm_up), lambda i, dd, kk: (dd, i))],
        1, False, (tm_up, ns_up), epilogue=up_epilogue, comm=_ag_comm([sh_down]))
    wf_down = wg_down.reshape(d_ff, d)
    f2 = _mm_plain("w_down", f, wf_down, False, F32)[0]
    dx2, df2, dg_post_mlp, loss_part = _loss_head(x1, f2, tgt, g_post_mlp)

    tm_b, tn_b = _tile(s, 1024), _tile(d_ff, 1024)
    da = _mm_plain("w_down_bwd", df2, wf_down, True, BF16,
                   extra=(a_act,), extra_specs=(pl.BlockSpec((tm_b, tn_b), lambda i, j, kk: (i, j)),),
                   epilogue=lambda acc, av: (acc * (2.0 * jnp.maximum(av.astype(F32), 0.0)),),
                   outs=[jax.ShapeDtypeStruct((s, d_ff), BF16)],
                   out_specs=[pl.BlockSpec((tm_b, tn_b), lambda i, j, kk: (i, j))])[0]
    gw_down = _mm_plain("w_down_grad", f_t, df2, False, BF16)[0]
    gw_up = _mm_dw_cols("w_up_grad", h2_t, da)
    big_down = gw_down.reshape(4, 2, d_ff // NDEV, d)
    big_up = gw_up.reshape(4, 2, d, d_ff // NDEV)
    dh2, sib_down, sib_up = _mm_cols_t("w_up_bwd", da, wg_up, comm=_sibling_comm([big_down, big_up]))
    sum_down = _pair_sum(big_down, sib_down, core)
    sum_up = _pair_sum(big_up, sib_up, core)
    dx1, dy, dg_pre_mlp, dg_post_mix = _midnorm_bwd(dx2, dh2, x1, y, g_post_mix, g_pre_mlp)
    gw_o = _mm_plain("w_o_grad", merged_t, dy, False, BF16)[0]
    dmerged = _mm_plain("w_o_bwd", dy, wf_o, True, F32)[0]
    do_sbp, do_cv, dgate_sb, dgate_cv = _merge_bwd(dmerged, proj, o_sbp, o_cv, d)
    gw_cv = _mm_dw_cols("conv_out_grad", u3_t, do_cv)
    gw_sb = _mm_dw_cols("sb_out_grad", o_sb_t, do_sbp)
    du3 = _mm_cols_t("conv_out_bwd", do_cv, wg_cv)[0]
    do_sb = _mm_cols_t("sb_out_bwd", do_sbp, wg_sb)[0]
    du1, dg_ln, db_ln = _ln_silu_bwd(du3, u1, g_conv_ln, b_conv_ln)
    dglu_a, dglu_b, dw_dw, db_dw = _conv_bwd(du1, proj, wf_dw, d_half)
    big_mid = [gw_sb.reshape(4, 2, d_half, d // NDEV), gw_cv.reshape(4, 2, d_half, d // NDEV),
               gw_o.reshape(4, 2, d // NDEV, d)]
    sib_mid = _run_comm("exchange_sibling_mid", _sibling_comm(big_mid))
    sum_sb, sum_cv, sum_o = [_pair_sum(g, r, core) for g, r in zip(big_mid, sib_mid)]
    dq, dk, dv, r_sb, r_cv, r_o, r_up, r_down = _attn_bwd(
        proj, do_sb, n_heads, _chips_comm([sum_sb, sum_cv, sum_o, sum_up, sum_down]))
    dproj = jnp.concatenate([dq, dk, dv, dglu_a, dglu_b, dgate_sb, dgate_cv], axis=1)
    db_in = _colsum(dproj)
    gw_in = _mm_dw_cols("w_in_grad", h_t, dproj)
    big_in = gw_in.reshape(4, 2, d, gw_in.shape[2])
    sib_in, = _run_comm("exchange_sibling_in", _sibling_comm([big_in]))
    sum_in = _pair_sum(big_in, sib_in, core)
    dh, r_in = _mm_cols_t("w_in_bwd", dproj, wg_in, comm=_chips_comm([sum_in]))
    grad_x, dg_pre_mix = _prenorm_bwd(dx1, dh, xs, g_pre_mix)
    recv = [r_in, r_sb, r_cv, r_o, r_up, r_down]

    small = [dg_pre_mix, db_in, dw_dw.reshape(1, -1), db_dw, dg_ln, db_ln, dg_post_mix, dg_pre_mlp, dg_post_mlp]
    sizes = [a.shape[1] for a in small]
    packed = jnp.concatenate(small, axis=1).reshape(-1, LANES)
    total = _sum_small(_all_gather_small(packed)).reshape(1, -1)
    offs = [0]
    for n in sizes:
        offs.append(offs[-1] + n)
    (g_g_pre_mix, g_b_in, g_w_dw_flat, g_b_dw, g_g_conv_ln, g_b_conv_ln, g_g_post_mix, g_g_pre_mlp,
     g_g_post_mlp) = [total[:, offs[k]:offs[k + 1]] for k in range(len(sizes))]
    ch = w_dw.shape[2]
    g_w_dw = lax.dynamic_slice_in_dim(g_w_dw_flat.reshape(CONV_PAD, d_half), dev * ch, ch, axis=1)[:CONV_WIDTH]

    loss = lax.psum(loss_part[0, 0], ("x", "y", "c"))

    res = {}
    res["g_pre_mix"] = _adamw_small("adamw_g_pre_mix", g_pre_mix, m_g_pre_mix, v_g_pre_mix, g_g_pre_mix)
    res["w_in"] = _adamw_big("adamw_w_in", w_in[0], m_w_in[0], v_w_in[0], recv[0])
    res["b_in"] = _adamw_small("adamw_b_in", b_in, m_b_in, v_b_in, g_b_in)
    res["w_dw"] = _adamw_small("adamw_w_dw", w_dw[0], m_w_dw[0], v_w_dw[0], g_w_dw)
    res["b_dw"] = _adamw_small("adamw_b_dw", b_dw, m_b_dw, v_b_dw, g_b_dw)
    res["g_conv_ln"] = _adamw_small("adamw_g_conv_ln", g_conv_ln, m_g_conv_ln, v_g_conv_ln, g_g_conv_ln)
    res["b_conv_ln"] = _adamw_small("adamw_b_conv_ln", b_conv_ln, m_b_conv_ln, v_b_conv_ln, g_b_conv_ln)
    res["w_sb_out"] = _adamw_big("adamw_w_sb_out", w_sb_out[0], m_w_sb_out[0], v_w_sb_out[0], recv[1])
    res["w_conv_out"] = _adamw_big("adamw_w_conv_out", w_conv_out[0], m_w_conv_out[0], v_w_conv_out[0], recv[2])
    res["w_o"] = _adamw_big("adamw_w_o", w_o[0], m_w_o[0], v_w_o[0], recv[3])
    res["g_post_mix"] = _adamw_small("adamw_g_post_mix", g_post_mix, m_g_post_mix, v_g_post_mix, g_g_post_mix)
    res["g_pre_mlp"] = _adamw_small("adamw_g_pre_mlp", g_pre_mlp, m_g_pre_mlp, v_g_pre_mlp, g_g_pre_mlp)
    res["w_up"] = _adamw_big("adamw_w_up", w_up[0], m_w_up[0], v_w_up[0], recv[4])
    res["w_down"] = _adamw_big("adamw_w_down", w_down[0], m_w_down[0], v_w_down[0], recv[5])
    res["g_post_mlp"] = _adamw_small("adamw_g_post_mlp", g_post_mlp, m_g_post_mlp, v_g_post_mlp, g_g_post_mlp)

    names = ["g_pre_mix", "w_in", "b_in", "w_dw", "b_dw", "g_conv_ln", "b_conv_ln", "w_sb_out", "w_conv_out", "w_o",
             "g_post_mix", "g_pre_mlp", "w_up", "w_down", "g_post_mlp"]
    three_d = {"w_in", "w_dw", "w_sb_out", "w_conv_out", "w_o", "w_up", "w_down"}

    def shaped(nm, arr):
        return arr[None] if nm in three_d else arr

    out = [loss, grad_x[None]]
    for k in range(4):
        out += [shaped(nm, res[nm][k]) for nm in names]
    return tuple(out)
```

```python
import functools
import math

import jax
import jax.numpy as jnp
from jax import lax
from jax.experimental import pallas as pl
from jax.experimental.pallas import tpu as pltpu

F32 = jnp.float32
BF16 = jnp.bfloat16
NDEV = 8
LANES = 128
EPS = 1e-6
CONV_WIDTH = 31
CONV_PAD = 32
HEAD_DIM = 128
ADAM_LR = 0.001
ADAM_B1 = 0.9
ADAM_B2 = 0.999
ADAM_EPS = 1e-08
ADAM_WD = 0.01
ADAM_STEP = 10
VMEM_LIMIT = 56 * 1024 * 1024
MESH = pl.DeviceIdType.MESH
ANY = pl.BlockSpec(memory_space=pl.ANY)


def _tile(n, pref):
    t = min(n, pref)
    assert n % t == 0, (n, t)
    return t


def _sigmoid(v):
    return 1.0 / (1.0 + jnp.exp(-v))


def _position():
    return lax.axis_index("x"), lax.axis_index("y"), lax.axis_index("c")


class _Comm:
    def __init__(self, ins, outs, scratch, start, finish, aliases=None):
        self.ins, self.outs, self.scratch = list(ins), list(outs), list(scratch)
        self.start, self.finish, self.aliases = start, finish, dict(aliases or {})


_NO_COMM = _Comm([], [], [], None, None)


def _call(name, body, grid, in_specs, out_specs, out_shape, scratch_shapes, sem, args, comm=None):
    comm = comm or _NO_COMM
    n_in, n_out, n_scr = len(in_specs), len(out_specs), len(scratch_shapes)
    n_cin, n_cout = len(comm.ins), len(comm.outs)

    def edge(c_ins, c_outs, c_scr, at_start):
        pids = [pl.program_id(ax) for ax in range(len(grid))]
        conds = [p == (0 if at_start else g - 1) for p, g in zip(pids, grid)]

        @pl.when(functools.reduce(jnp.logical_and, conds))
        def _():
            (comm.start if at_start else comm.finish)(c_ins, c_outs, c_scr)

    def wrapped(*refs):
        ins, c_ins = refs[:n_in], refs[n_in:n_in + n_cin]
        pos = n_in + n_cin
        outs, c_outs = refs[pos:pos + n_out], refs[pos + n_out:pos + n_out + n_cout]
        pos += n_out + n_cout
        scr, c_scr = refs[pos:pos + n_scr], refs[pos + n_scr:]
        if n_cin:
            edge(c_ins, c_outs, c_scr, True)
        body(*ins, *outs, *scr)
        if n_cin:
            edge(c_ins, c_outs, c_scr, False)

    if n_cin:
        sem = ("arbitrary",) * len(grid)
    return pl.pallas_call(
        wrapped, name=name, grid=grid,
        in_specs=[*in_specs, *[ANY] * n_cin], out_specs=[*out_specs, *[ANY] * n_cout],
        out_shape=[*out_shape, *comm.outs], scratch_shapes=[*scratch_shapes, *comm.scratch],
        input_output_aliases={n_in + ci: n_out + co for ci, co in comm.aliases.items()},
        compiler_params=pltpu.CompilerParams(dimension_semantics=sem, vmem_limit_bytes=VMEM_LIMIT),
    )(*args, *comm.ins)


def _run_comm(name, comm):
    n_in, n_out = len(comm.ins), len(comm.outs)

    def body(*refs):
        ins, outs, scr = refs[:n_in], refs[n_in:n_in + n_out], refs[n_in + n_out:]
        comm.start(ins, outs, scr)
        comm.finish(ins, outs, scr)

    return pl.pallas_call(body, name=name, in_specs=[ANY] * n_in, out_specs=[ANY] * n_out, out_shape=comm.outs,
                          scratch_shapes=comm.scratch,
                          input_output_aliases=comm.aliases)(*comm.ins)


def _rows_of(ref, rows):
    return ref if rows is None else ref.at[pl.ds(rows[0], rows[1])]


def _ag_comm(shards, rows=None, into=None):
    n = len(shards)

    def parts(ins, outs, scr):
        send_sems, recv_sems, local_sems = scr
        x, y, c = _position()
        chips = [(1 - x, y), (x, 1 - y), (1 - x, 1 - y)]

        def copy(a, k, block, to, own=False):
            px, py, pc = block
            dst = _rows_of(outs[a].at[4 * px + 2 * py + pc], rows)
            return pltpu.make_async_remote_copy(src_ref=_rows_of(ins[a], rows) if own else dst, dst_ref=dst,
                                                send_sem=send_sems.at[a, k], recv_sem=recv_sems.at[a, k],
                                                device_id=to, device_id_type=MESH)

        mine = [pltpu.make_async_copy(_rows_of(ins[a], rows), _rows_of(outs[a].at[4 * x + 2 * y + c], rows),
                                      local_sems.at[a]) for a in range(n)]
        first = []
        for a in range(n):
            first.append(copy(a, 0, (x, y, c), (x, y, 1 - c), own=True))
            first += [copy(a, 1 + j, (x, y, c), (*chip, c), own=True) for j, chip in enumerate(chips)]
        return copy, mine, first, chips, (x, y, c), (x, y, 1 - c)

    def start(ins, outs, scr):
        _, mine, first, _, _, _ = parts(ins, outs, scr)
        for cp in mine + first:
            cp.start()

    def finish(ins, outs, scr):
        copy, mine, first, chips, me, sibling = parts(ins, outs, scr)
        c = me[2]
        passed = []
        for a in range(n):
            for j, chip in enumerate(chips):
                copy(a, 1 + j, (*chip, c), me).wait_recv()
                cp = copy(a, 4 + j, (*chip, c), sibling)
                cp.start()
                passed.append(cp)
        for a in range(n):
            copy(a, 0, sibling, me).wait_recv()
            for j, chip in enumerate(chips):
                copy(a, 4 + j, (*chip, 1 - c), me).wait_recv()
        for cp in first + passed:
            cp.wait_send()
        for cp in mine:
            cp.wait()

    return _Comm([*shards, *(into or [])], [jax.ShapeDtypeStruct((NDEV, *sh.shape), sh.dtype) for sh in shards],
                 [pltpu.SemaphoreType.DMA((n, 7)), pltpu.SemaphoreType.DMA((n, 7)), pltpu.SemaphoreType.DMA((n,))],
                 start, finish, {n + a: a for a in range(n)} if into else None)


def _sibling_comm(grads):
    n = len(grads)

    def copies(ins, outs, scr):
        send_sems, recv_sems = scr
        x, y, c = _position()
        return [pltpu.make_async_remote_copy(src_ref=ins[a].at[k, 1 - c], dst_ref=outs[a].at[k],
                                             send_sem=send_sems.at[a, k], recv_sem=recv_sems.at[a, k],
                                             device_id=(x, y, 1 - c), device_id_type=MESH)
                for a in range(n) for k in range(4)]

    def start(ins, outs, scr):
        for cp in copies(ins, outs, scr):
            cp.start()

    def finish(ins, outs, scr):
        for cp in copies(ins, outs, scr):
            cp.wait()

    return _Comm(grads, [jax.ShapeDtypeStruct((4, *g.shape[2:]), g.dtype) for g in grads],
                 [pltpu.SemaphoreType.DMA((n, 4)), pltpu.SemaphoreType.DMA((n, 4))], start, finish)


def _chips_comm(sums, rows=None, into=None):
    n = len(sums)

    def copies(ins, outs, scr):
        send_sems, recv_sems, local_sems = scr
        x, y, c = _position()
        chips = [(1 - x, y), (x, 1 - y), (1 - x, 1 - y)]
        own = [pltpu.make_async_copy(_rows_of(ins[a].at[2 * x + y], rows), _rows_of(outs[a].at[3], rows), local_sems.at[a])
               for a in range(n)]
        remote = [pltpu.make_async_remote_copy(src_ref=_rows_of(ins[a].at[2 * px + py], rows),
                                               dst_ref=_rows_of(outs[a].at[j], rows),
                                               send_sem=send_sems.at[a, j], recv_sem=recv_sems.at[a, j],
                                               device_id=(px, py, c), device_id_type=MESH)
                  for a in range(n) for j, (px, py) in enumerate(chips)]
        return own + remote

    def start(ins, outs, scr):
        for cp in copies(ins, outs, scr):
            cp.start()

    def finish(ins, outs, scr):
        for cp in copies(ins, outs, scr):
            cp.wait()

    return _Comm([*sums, *(into or [])], [jax.ShapeDtypeStruct(sm.shape, sm.dtype) for sm in sums],
                 [pltpu.SemaphoreType.DMA((n, 3)), pltpu.SemaphoreType.DMA((n, 3)), pltpu.SemaphoreType.DMA((n,))],
                 start, finish, {n + a: a for a in range(n)} if into else None)


class _Chunked:
    def __init__(self, make, arrays, n_rows, n_chunks):
        self.make, self.arrays, self.into = make, arrays, None
        step = n_rows // n_chunks
        assert step * n_chunks == n_rows
        self.todo = [(k * step, step) for k in range(n_chunks)]

    def take(self, count=1):
        r0, nr = self.todo[0][0], sum(t[1] for t in self.todo[:count])
        self.todo = self.todo[count:]
        return self.make(self.arrays, (r0, nr), self.into)

    def done(self, outs):
        self.into = list(outs)
        return self.into


def _all_gather_small(part):
    def body(in_ref, out_ref, send_sems, recv_sems, local_sem):
        x, y, c = _position()
        me = 4 * x + 2 * y + c
        mine = pltpu.make_async_copy(in_ref, out_ref.at[me], local_sem)
        mine.start()
        flips = [(fx, fy, fc) for fx in (0, 1) for fy in (0, 1) for fc in (0, 1)][1:]
        copies = []
        for k, (fx, fy, fc) in enumerate(flips):
            cp = pltpu.make_async_remote_copy(src_ref=in_ref, dst_ref=out_ref.at[me], send_sem=send_sems.at[k],
                                              recv_sem=recv_sems.at[k],
                                              device_id=(x ^ fx, y ^ fy, c ^ fc), device_id_type=MESH)
            cp.start()
            copies.append(cp)
        for k, (fx, fy, fc) in enumerate(flips):
            peer = 4 * (x ^ fx) + 2 * (y ^ fy) + (c ^ fc)
            pltpu.make_async_remote_copy(src_ref=in_ref, dst_ref=out_ref.at[peer], send_sem=send_sems.at[k],
                                         recv_sem=recv_sems.at[k], device_id=(x, y, c), device_id_type=MESH).wait_recv()
        for cp in copies:
            cp.wait_send()
        mine.wait()

    return pl.pallas_call(
        body, name="all_gather_small", in_specs=[ANY], out_specs=ANY,
        out_shape=jax.ShapeDtypeStruct((NDEV, *part.shape), part.dtype),
        scratch_shapes=[pltpu.SemaphoreType.DMA((7,)), pltpu.SemaphoreType.DMA((7,)), pltpu.SemaphoreType.DMA],
    )(part)


def _pair_sum(g, recv, core):
    _, _, r, c = g.shape
    tr = _tile(r, 512)

    def body(core_ref, g_ref, r_ref, o_ref):
        o_ref[...] = (g_ref[...].astype(F32) + r_ref[...].astype(F32)).astype(o_ref.dtype)

    return pl.pallas_call(
        body, name="pair_sum",
        grid_spec=pltpu.PrefetchScalarGridSpec(
            num_scalar_prefetch=1, grid=(4, r // tr),
            in_specs=[pl.BlockSpec((None, None, tr, c), lambda k, i, core_ref: (k, core_ref[0], i, 0)),
                      pl.BlockSpec((None, tr, c), lambda k, i, core_ref: (k, i, 0))],
            out_specs=pl.BlockSpec((None, tr, c), lambda k, i, core_ref: (k, i, 0))),
        out_shape=jax.ShapeDtypeStruct((4, r, c), g.dtype),
        compiler_params=pltpu.CompilerParams(dimension_semantics=("parallel", "parallel"), vmem_limit_bytes=VMEM_LIMIT),
    )(core, g, recv)


def _sum_small(gathered):
    _, r, l = gathered.shape

    def body(g_ref, o_ref):
        acc = g_ref[0]
        for d in range(1, NDEV):
            acc = acc + g_ref[d]
        o_ref[...] = acc

    return pl.pallas_call(
        body, name="sum_small", in_specs=[pl.BlockSpec((NDEV, r, l), lambda: (0, 0, 0))],
        out_specs=pl.BlockSpec((r, l), lambda: (0, 0)), out_shape=jax.ShapeDtypeStruct((r, l), F32),
    )(gathered)


def _matmul(name, a, b, outs, grid, a_spec, b_spec, out_specs, n_red, nt, acc_shape,
            extra=(), extra_specs=(), epilogue=None, comm=None):
    n_extra, n_out = len(extra), len(outs)
    red_axes = tuple(range(len(grid) - n_red, len(grid)))
    red_sizes = tuple(grid[ax] for ax in red_axes)
    single = all(sz == 1 for sz in red_sizes)
    dims = (((1,), (1,)), ((), ())) if nt else (((1,), (0,)), ((), ()))

    def body(*refs):
        a_ref, b_ref = refs[0], refs[1]
        ex_refs = refs[2:2 + n_extra]
        o_refs = refs[2 + n_extra:2 + n_extra + n_out]
        acc_ref = refs[-1]

        def write(acc):
            vals = (acc,) if epilogue is None else epilogue(acc, *[r[...] for r in ex_refs])
            for o_ref, val in zip(o_refs, vals):
                o_ref[...] = val.astype(o_ref.dtype)

        part = lax.dot_general(a_ref[...].astype(BF16), b_ref[...].astype(BF16), dims, preferred_element_type=F32)
        if single:
            write(part)
        else:
            ks = [pl.program_id(ax) for ax in red_axes]
            first = functools.reduce(jnp.logical_and, [k == 0 for k in ks])
            last = functools.reduce(jnp.logical_and, [k == sz - 1 for k, sz in zip(ks, red_sizes)])

            @pl.when(first)
            def _():
                acc_ref[...] = part

            @pl.when(jnp.logical_not(first))
            def _():
                acc_ref[...] += part

            @pl.when(last)
            def _():
                write(acc_ref[...])

    sem = ("parallel",) * (len(grid) - n_red) + ("arbitrary",) * n_red
    return _call(name, body, grid, [a_spec, b_spec, *extra_specs], list(out_specs), list(outs),
                 [pltpu.VMEM((8, LANES) if single else acc_shape, F32)], sem, (a, b, *extra), comm)


def _mm_cols(name, a, wg, bias=None, out_dtype=F32, comm=None):
    m, k = a.shape
    _, _, ns = wg.shape
    tm, tk = _tile(m, 1024), _tile(k, 2048)
    grid = (m // tm, NDEV, k // tk)
    extra, extra_specs, epi = (), (), None
    if bias is not None:
        extra, extra_specs = (bias,), (pl.BlockSpec((1, ns), lambda i, d, kk: (0, d)),)
        epi = lambda acc, bv: (acc + bv,)
    return _matmul(name, a, wg, [jax.ShapeDtypeStruct((m, NDEV * ns), out_dtype)], grid,
                   pl.BlockSpec((tm, tk), lambda i, d, kk: (i, kk)),
                   pl.BlockSpec((None, tk, ns), lambda i, d, kk: (d, kk, 0)),
                   [pl.BlockSpec((tm, ns), lambda i, d, kk: (i, d))], 1, False, (tm, ns),
                   extra, extra_specs, epi, comm)


def _mm_cols_t(name, a, wg, comm=None):
    m, _ = a.shape
    _, n, ns = wg.shape
    tm, tn = _tile(m, 1024), _tile(n, 1024)
    grid = (m // tm, n // tn, NDEV)
    return _matmul(name, a, wg, [jax.ShapeDtypeStruct((m, n), F32)], grid,
                   pl.BlockSpec((tm, ns), lambda i, j, d: (i, d)),
                   pl.BlockSpec((None, tn, ns), lambda i, j, d: (d, j, 0)),
                   [pl.BlockSpec((tm, tn), lambda i, j, d: (i, j))], 1, True, (tm, tn), comm=comm)


def _mm_dw_cols(name, at, g, comm=None):
    m, t = at.shape
    ns = g.shape[1] // NDEV
    tm, tk = _tile(m, 1024), _tile(t, 2048)
    grid = (m // tm, NDEV, t // tk)
    return _matmul(name, at, g, [jax.ShapeDtypeStruct((NDEV, m, ns), BF16)], grid,
                   pl.BlockSpec((tm, tk), lambda i, d, kk: (i, kk)),
                   pl.BlockSpec((tk, ns), lambda i, d, kk: (kk, d)),
                   [pl.BlockSpec((None, tm, ns), lambda i, d, kk: (d, i, 0))], 1, False, (tm, ns), comm=comm)


def _mm_plain(name, a, b, nt, out_dtype, extra=(), extra_specs=(), epilogue=None, outs=None, out_specs=None, comm=None):
    m, k = a.shape
    n = b.shape[0] if nt else b.shape[1]
    tm, tn, tk = _tile(m, 1024), _tile(n, 1024), _tile(k, 2048)
    grid = (m // tm, n // tn, k // tk)
    b_spec = (pl.BlockSpec((tn, tk), lambda i, j, kk: (j, kk)) if nt
              else pl.BlockSpec((tk, tn), lambda i, j, kk: (kk, j)))
    if outs is None:
        outs = [jax.ShapeDtypeStruct((m, n), out_dtype)]
        out_specs = [pl.BlockSpec((tm, tn), lambda i, j, kk: (i, j))]
    return _matmul(name, a, b, outs, grid, pl.BlockSpec((tm, tk), lambda i, j, kk: (i, kk)), b_spec,
                   out_specs, 1, nt, (tm, tn), extra, extra_specs, epilogue, comm)


def _rms_rows(v):
    return lax.rsqrt(jnp.mean(v * v, axis=-1, keepdims=True) + EPS)


def _prenorm(x, g):
    s, d = x.shape
    ts = _tile(s, 256)

    def body(x_ref, g_ref, h_ref, ht_ref):
        xv = x_ref[...]
        h = xv * _rms_rows(xv) * g_ref[...]
        h_ref[...] = h.astype(BF16)
        ht_ref[...] = h.T.astype(BF16)

    return _call("prenorm", body, (s // ts,),
                 [pl.BlockSpec((ts, d), lambda i: (i, 0)), pl.BlockSpec((1, d), lambda i: (0, 0))],
                 [pl.BlockSpec((ts, d), lambda i: (i, 0)), pl.BlockSpec((d, ts), lambda i: (0, i))],
                 [jax.ShapeDtypeStruct((s, d), BF16), jax.ShapeDtypeStruct((d, s), BF16)], [], ("parallel",), (x, g))


def _ln_silu(u1, g, b):
    s, c = u1.shape
    ts = _tile(s, 256)

    def body(u_ref, g_ref, b_ref, o_ref, ot_ref):
        u = u_ref[...]
        mu = jnp.mean(u, axis=-1, keepdims=True)
        var = jnp.mean(jnp.square(u - mu), axis=-1, keepdims=True)
        u2 = (u - mu) * lax.rsqrt(var + EPS) * g_ref[...] + b_ref[...]
        u3 = u2 * _sigmoid(u2)
        o_ref[...] = u3.astype(BF16)
        ot_ref[...] = u3.T.astype(BF16)

    vec = pl.BlockSpec((1, c), lambda i: (0, 0))
    return _call("ln_silu", body, (s // ts,), [pl.BlockSpec((ts, c), lambda i: (i, 0)), vec, vec],
                 [pl.BlockSpec((ts, c), lambda i: (i, 0)), pl.BlockSpec((c, ts), lambda i: (0, i))],
                 [jax.ShapeDtypeStruct((s, c), BF16), jax.ShapeDtypeStruct((c, s), BF16)], [], ("parallel",), (u1, g, b))


def _ln_silu_bwd(du3, u1, g, b, comm=None):
    s, c = u1.shape
    ts = _tile(s, 256)

    def body(d_ref, u_ref, g_ref, b_ref, du1_ref, dg_ref, db_ref):
        @pl.when(pl.program_id(0) == 0)
        def _():
            dg_ref[...] = jnp.zeros_like(dg_ref)
            db_ref[...] = jnp.zeros_like(db_ref)

        u = u_ref[...]
        mu = jnp.mean(u, axis=-1, keepdims=True)
        var = jnp.mean(jnp.square(u - mu), axis=-1, keepdims=True)
        rstd = lax.rsqrt(var + EPS)
        uhat = (u - mu) * rstd
        u2 = uhat * g_ref[...] + b_ref[...]
        sg = _sigmoid(u2)
        du2 = d_ref[...] * (sg * (1.0 + u2 * (1.0 - sg)))
        dg_ref[...] += jnp.sum(du2 * uhat, axis=0, keepdims=True)
        db_ref[...] += jnp.sum(du2, axis=0, keepdims=True)
        duh = du2 * g_ref[...]
        du1_ref[...] = rstd * (duh - jnp.mean(duh, axis=-1, keepdims=True)
                               - uhat * jnp.mean(duh * uhat, axis=-1, keepdims=True))

    row = pl.BlockSpec((ts, c), lambda i: (i, 0))
    vec = pl.BlockSpec((1, c), lambda i: (0, 0))
    return _call("ln_silu_bwd", body, (s // ts,), [row, row, vec, vec], [row, vec, vec],
                 [jax.ShapeDtypeStruct((s, c), F32), jax.ShapeDtypeStruct((1, c), F32), jax.ShapeDtypeStruct((1, c), F32)],
                 [], ("arbitrary",), (du3, u1, g, b), comm)


def _merge(proj, o_sbp, o_cv, d, comm=None):
    s = proj.shape[0]
    w = d // 2
    ts = _tile(s, 256)

    def body(gs_ref, gc_ref, a_ref, b_ref, m_ref, mt_ref):
        mg = _sigmoid(gs_ref[...]) * a_ref[...] + _sigmoid(gc_ref[...]) * b_ref[...]
        m_ref[...] = mg.astype(BF16)
        mt_ref[...] = mg.T.astype(BF16)

    blk = pl.BlockSpec((ts, w), lambda i, j: (i, j))
    return _call("merge", body, (s // ts, 2),
                 [pl.BlockSpec((ts, w), lambda i, j: (i, 5 + j)), pl.BlockSpec((ts, w), lambda i, j: (i, 7 + j)), blk, blk],
                 [blk, pl.BlockSpec((w, ts), lambda i, j: (j, i))],
                 [jax.ShapeDtypeStruct((s, d), BF16), jax.ShapeDtypeStruct((d, s), BF16)], [],
                 ("parallel", "parallel"), (proj, proj, o_sbp, o_cv), comm)


def _merge_bwd(dmerged, proj, o_sbp, o_cv, d, comm=None):
    s = proj.shape[0]
    w = d // 2
    ts = _tile(s, 256)

    def body(dm_ref, gs_ref, gc_ref, a_ref, b_ref, da_ref, db_ref, dgs_ref, dgc_ref):
        dm = dm_ref[...]
        ss = _sigmoid(gs_ref[...])
        sc = _sigmoid(gc_ref[...])
        da_ref[...] = (dm * ss).astype(BF16)
        db_ref[...] = (dm * sc).astype(BF16)
        dgs_ref[...] = (dm * a_ref[...] * ss * (1.0 - ss)).astype(BF16)
        dgc_ref[...] = (dm * b_ref[...] * sc * (1.0 - sc)).astype(BF16)

    blk = pl.BlockSpec((ts, w), lambda i, j: (i, j))
    sds = jax.ShapeDtypeStruct((s, d), BF16)
    return _call("merge_bwd", body, (s // ts, 2),
                 [blk, pl.BlockSpec((ts, w), lambda i, j: (i, 5 + j)), pl.BlockSpec((ts, w), lambda i, j: (i, 7 + j)),
                  blk, blk],
                 [blk, blk, blk, blk], [sds, sds, sds, sds], [], ("parallel", "parallel"),
                 (dmerged, proj, proj, o_sbp, o_cv), comm)


def _postnorm_mix(x, y, g_post, g_pre, comm=None):
    s, d = x.shape
    ts = _tile(s, 256)

    def body(x_ref, y_ref, gp_ref, gn_ref, x1_ref, h_ref, ht_ref):
        yv = y_ref[...]
        x1 = x_ref[...] + yv * _rms_rows(yv) * gp_ref[...]
        x1_ref[...] = x1
        h = x1 * _rms_rows(x1) * gn_ref[...]
        h_ref[...] = h.astype(BF16)
        ht_ref[...] = h.T.astype(BF16)

    row = pl.BlockSpec((ts, d), lambda i: (i, 0))
    vec = pl.BlockSpec((1, d), lambda i: (0, 0))
    return _call("postnorm_mix", body, (s // ts,), [row, row, vec, vec],
                 [row, row, pl.BlockSpec((d, ts), lambda i: (0, i))],
                 [jax.ShapeDtypeStruct((s, d), F32), jax.ShapeDtypeStruct((s, d), BF16), jax.ShapeDtypeStruct((d, s), BF16)],
                 [], ("parallel",), (x, y, g_post, g_pre), comm)


def _rms_bwd(dout, vin, g):
    r = _rms_rows(vin)
    vhat = vin * r
    dyh = dout * g
    dvin = r * (dyh - vhat * jnp.mean(dyh * vhat, axis=-1, keepdims=True))
    return dvin, jnp.sum(dout * vhat, axis=0, keepdims=True)


def _loss_head(x1, f2, tgt, g):
    s, d = x1.shape
    ts = _tile(s, 256)

    def body(x1_ref, f_ref, t_ref, g_ref, dx2_ref, df2_ref, dg_ref, loss_ref):
        @pl.when(pl.program_id(0) == 0)
        def _():
            dg_ref[...] = jnp.zeros_like(dg_ref)
            loss_ref[...] = jnp.zeros_like(loss_ref)

        fv = f_ref[...]
        x2 = x1_ref[...] + fv * _rms_rows(fv) * g_ref[...]
        err = x2 - t_ref[...]
        loss_ref[...] += 0.5 * jnp.sum(jnp.mean(err * err, axis=-1, keepdims=True), axis=0, keepdims=True)
        dx2 = err * (1.0 / d)
        dx2_ref[...] = dx2
        df2, dg = _rms_bwd(dx2, fv, g_ref[...])
        df2_ref[...] = df2.astype(BF16)
        dg_ref[...] += dg

    row = pl.BlockSpec((ts, d), lambda i: (i, 0))
    vec = pl.BlockSpec((1, d), lambda i: (0, 0))
    return _call("loss_head", body, (s // ts,), [row, row, row, vec],
                 [row, row, vec, pl.BlockSpec((1, LANES), lambda i: (0, 0))],
                 [jax.ShapeDtypeStruct((s, d), F32), jax.ShapeDtypeStruct((s, d), BF16),
                  jax.ShapeDtypeStruct((1, d), F32), jax.ShapeDtypeStruct((1, LANES), F32)],
                 [], ("arbitrary",), (x1, f2, tgt, g))


def _midnorm_bwd(dx2, dh2, x1, y, g_post, g_pre, comm=None):
    s, d = x1.shape
    ts = _tile(s, 256)

    def body(dx2_ref, dh_ref, x1_ref, y_ref, gp_ref, gn_ref, dx1_ref, dy_ref, dgn_ref, dgp_ref):
        @pl.when(pl.program_id(0) == 0)
        def _():
            dgn_ref[...] = jnp.zeros_like(dgn_ref)
            dgp_ref[...] = jnp.zeros_like(dgp_ref)

        dxa, dgn = _rms_bwd(dh_ref[...], x1_ref[...], gn_ref[...])
        dx1 = dx2_ref[...] + dxa
        dx1_ref[...] = dx1
        dy, dgp = _rms_bwd(dx1, y_ref[...], gp_ref[...])
        dy_ref[...] = dy.astype(BF16)
        dgn_ref[...] += dgn
        dgp_ref[...] += dgp

    row = pl.BlockSpec((ts, d), lambda i: (i, 0))
    vec = pl.BlockSpec((1, d), lambda i: (0, 0))
    return _call("midnorm_bwd", body, (s // ts,), [row, row, row, row, vec, vec], [row, row, vec, vec],
                 [jax.ShapeDtypeStruct((s, d), F32), jax.ShapeDtypeStruct((s, d), BF16),
                  jax.ShapeDtypeStruct((1, d), F32), jax.ShapeDtypeStruct((1, d), F32)],
                 [], ("arbitrary",), (dx2, dh2, x1, y, g_post, g_pre), comm)


def _prenorm_bwd(dx1, dh, x, g, comm=None):
    s, d = x.shape
    ts = _tile(s, 256)

    def body(dx1_ref, dh_ref, x_ref, g_ref, dx_ref, dg_ref):
        @pl.when(pl.program_id(0) == 0)
        def _():
            dg_ref[...] = jnp.zeros_like(dg_ref)

        dxa, dg = _rms_bwd(dh_ref[...], x_ref[...], g_ref[...])
        dx_ref[...] = dx1_ref[...] + dxa
        dg_ref[...] += dg

    row = pl.BlockSpec((ts, d), lambda i: (i, 0))
    vec = pl.BlockSpec((1, d), lambda i: (0, 0))
    return _call("prenorm_bwd", body, (s // ts,), [row, row, row, vec], [row, vec],
                 [jax.ShapeDtypeStruct((s, d), F32), jax.ShapeDtypeStruct((1, d), F32)],
                 [], ("arbitrary",), (dx1, dh, x, g), comm)


def _colsum(a):
    s, n = a.shape
    ts = _tile(s, 256)

    def body(a_ref, o_ref):
        @pl.when(pl.program_id(0) == 0)
        def _():
            o_ref[...] = jnp.zeros_like(o_ref)

        o_ref[...] += jnp.sum(a_ref[...].astype(F32), axis=0, keepdims=True)

    return _call("colsum", body, (s // ts,), [pl.BlockSpec((ts, n), lambda i: (i, 0))],
                 [pl.BlockSpec((1, n), lambda i: (0, 0))], [jax.ShapeDtypeStruct((1, n), F32)], [], ("arbitrary",), (a,))[0]


def _shift_rows(win, off, t):
    n = win.shape[0]
    if off == 0:
        return win[:t]
    return pltpu.roll(win, n - off, axis=0)[:t]


def _conv_fwd(proj, w_pad, b_dw, c_total, comm=None):
    s = proj.shape[0]
    nct = c_total // LANES
    t = _tile(s, 256)

    def body(ga_ref, gb_ref, w_ref, b_ref, o_ref, u0_ref):
        u0_ref[pl.ds(0, CONV_PAD), :] = jnp.zeros((CONV_PAD, LANES), F32)
        u0_ref[pl.ds(CONV_PAD, s), :] = ga_ref[...] * _sigmoid(gb_ref[...])
        wv = w_ref[...]

        def chunk(r, carry):
            r0 = pl.multiple_of(r * t, t)
            win = u0_ref[pl.ds(r0, t + CONV_PAD), :]
            acc = jnp.broadcast_to(b_ref[...], (t, LANES))
            for j in range(CONV_WIDTH):
                acc = acc + wv[j:j + 1, :] * _shift_rows(win, j + CONV_PAD - (CONV_WIDTH - 1), t)
            o_ref[pl.ds(r0, t), :] = acc
            return carry

        lax.fori_loop(0, s // t, chunk, 0)

    return _call("conv_fwd", body, (nct,),
                 [pl.BlockSpec((s, LANES), lambda c: (0, 3 * nct + c)), pl.BlockSpec((s, LANES), lambda c: (0, 4 * nct + c)),
                  pl.BlockSpec((CONV_PAD, LANES), lambda c: (0, c)), pl.BlockSpec((1, LANES), lambda c: (0, c))],
                 [pl.BlockSpec((s, LANES), lambda c: (0, c))], [jax.ShapeDtypeStruct((s, c_total), F32)],
                 [pltpu.VMEM((s + CONV_PAD, LANES), F32)], ("parallel",), (proj, proj, w_pad, b_dw), comm)


def _conv_bwd(du1, proj, w_pad, c_total, comm=None):
    s = proj.shape[0]
    nct = c_total // LANES
    t = _tile(s, 256)

    def body(d_ref, ga_ref, gb_ref, w_ref, dga_ref, dgb_ref, dw_ref, db_ref, u0_ref, dp_ref):
        sg = _sigmoid(gb_ref[...])
        u0_ref[pl.ds(0, CONV_PAD), :] = jnp.zeros((CONV_PAD, LANES), F32)
        u0_ref[pl.ds(CONV_PAD, s), :] = ga_ref[...] * sg
        dp_ref[pl.ds(0, s), :] = d_ref[...]
        dp_ref[pl.ds(s, CONV_PAD), :] = jnp.zeros((CONV_PAD, LANES), F32)
        dw_ref[...] = jnp.zeros_like(dw_ref)
        db_ref[...] = jnp.sum(d_ref[...], axis=0, keepdims=True)
        wv = w_ref[...]

        def chunk(r, carry):
            r0 = pl.multiple_of(r * t, t)
            win = u0_ref[pl.ds(r0, t + CONV_PAD), :]
            dwin = dp_ref[pl.ds(r0, t + CONV_PAD), :]
            dcur = dwin[:t]
            du0 = jnp.zeros((t, LANES), F32)
            for j in range(CONV_WIDTH):
                du0 = du0 + wv[j:j + 1, :] * _shift_rows(dwin, CONV_WIDTH - 1 - j, t)
                sh = _shift_rows(win, j + CONV_PAD - (CONV_WIDTH - 1), t)
                dw_ref[j:j + 1, :] += jnp.sum(dcur * sh, axis=0, keepdims=True)
            gav = ga_ref[pl.ds(r0, t), :]
            sgv = _sigmoid(gb_ref[pl.ds(r0, t), :])
            dga_ref[pl.ds(r0, t), :] = (du0 * sgv).astype(BF16)
            dgb_ref[pl.ds(r0, t), :] = (du0 * gav * sgv * (1.0 - sgv)).astype(BF16)
            return carry

        lax.fori_loop(0, s // t, chunk, 0)

    col = pl.BlockSpec((s, LANES), lambda c: (0, c))
    return _call("conv_bwd", body, (nct,),
                 [col, pl.BlockSpec((s, LANES), lambda c: (0, 3 * nct + c)),
                  pl.BlockSpec((s, LANES), lambda c: (0, 4 * nct + c)), pl.BlockSpec((CONV_PAD, LANES), lambda c: (0, c))],
                 [col, col, pl.BlockSpec((CONV_PAD, LANES), lambda c: (0, c)), pl.BlockSpec((1, LANES), lambda c: (0, c))],
                 [jax.ShapeDtypeStruct((s, c_total), BF16), jax.ShapeDtypeStruct((s, c_total), BF16),
                  jax.ShapeDtypeStruct((CONV_PAD, c_total), F32), jax.ShapeDtypeStruct((1, c_total), F32)],
                 [pltpu.VMEM((s + CONV_PAD, LANES), F32), pltpu.VMEM((s + CONV_PAD, LANES), F32)],
                 ("parallel",), (du1, proj, proj, w_pad), comm)


TQ_PREF = 256
TK = 256


def _split_dot(v, tri):
    hi = v.astype(BF16)
    lo = (v - hi.astype(F32)).astype(BF16)
    return (jnp.dot(hi, tri, preferred_element_type=F32) + jnp.dot(lo, tri, preferred_element_type=F32))


def _sb_block(q, k_ref, j, i, tq, scale, c_lm, tri_after):
    kb = k_ref[pl.ds(pl.multiple_of(j * TK, TK), TK), :].astype(BF16)
    z = lax.dot_general(q, kb, (((1,), (1,)), ((), ())), preferred_element_type=F32) * scale
    tpos = i * tq + lax.broadcasted_iota(jnp.int32, (tq, TK), 0)
    spos = j * TK + lax.broadcasted_iota(jnp.int32, (tq, TK), 1)
    mask = spos < tpos
    sp = jnp.log(1.0 + jnp.exp(-jnp.abs(z)))
    lsz = jnp.minimum(z, 0.0) - sp
    lm = jnp.where(mask, -jnp.maximum(z, 0.0) - sp, 0.0)
    suf = c_lm + _split_dot(lm, tri_after)
    a = jnp.where(mask, jnp.exp(lsz + suf), 0.0)
    return lsz, lm, a, mask


def _tri(after):
    r = lax.broadcasted_iota(jnp.int32, (TK, TK), 0)
    c = lax.broadcasted_iota(jnp.int32, (TK, TK), 1)
    return (r > c).astype(BF16) if after else (r < c).astype(BF16)


def _attn_fwd(proj, n_heads, comm=None):
    s = proj.shape[0]
    tq = _tile(s, TQ_PREF)
    scale = 1.0 / math.sqrt(HEAD_DIM)
    ratio = tq // TK

    def body(q_ref, k_ref, v_ref, o_ref, ot_ref, acc_ref, clm_ref):
        i = pl.program_id(1)
        q = q_ref[...].astype(BF16)
        tri_after = _tri(True)
        acc_ref[...] = jnp.zeros_like(acc_ref)
        clm_ref[...] = jnp.zeros_like(clm_ref)
        nkb = (i + 1) * ratio

        def step(jj, carry):
            j = nkb - 1 - jj
            _, lm, a, _ = _sb_block(q, k_ref, j, i, tq, scale, clm_ref[...], tri_after)
            vb = v_ref[pl.ds(pl.multiple_of(j * TK, TK), TK), :].astype(BF16)
            acc_ref[...] += jnp.dot(a.astype(BF16), vb, preferred_element_type=F32)
            clm_ref[...] += jnp.sum(lm, axis=1, keepdims=True)
            return carry

        lax.fori_loop(0, nkb, step, 0)
        o = acc_ref[...]
        o_ref[...] = o
        ot_ref[...] = o.T.astype(BF16)

    return _call("attn_fwd", body, (n_heads, s // tq),
                 [pl.BlockSpec((tq, HEAD_DIM), lambda h, i: (i, h)),
                  pl.BlockSpec((s, HEAD_DIM), lambda h, i: (0, n_heads + h)),
                  pl.BlockSpec((s, HEAD_DIM), lambda h, i: (0, 2 * n_heads + h))],
                 [pl.BlockSpec((tq, HEAD_DIM), lambda h, i: (i, h)), pl.BlockSpec((HEAD_DIM, tq), lambda h, i: (h, i))],
                 [jax.ShapeDtypeStruct((s, n_heads * HEAD_DIM), F32), jax.ShapeDtypeStruct((n_heads * HEAD_DIM, s), BF16)],
                 [pltpu.VMEM((tq, HEAD_DIM), F32), pltpu.VMEM((tq, 1), F32)],
                 ("parallel", "arbitrary"), (proj, proj, proj), comm)


def _attn_bwd(proj, do_sb, n_heads, comm=None):
    s = proj.shape[0]
    tq = _tile(s, TQ_PREF)
    scale = 1.0 / math.sqrt(HEAD_DIM)
    ratio = tq // TK
    n_kb = s // TK
    n_qb = s // tq

    def body(q_ref, k_ref, v_ref, do_ref, dq_ref, dk_ref, dv_ref, dka_ref, dva_ref, dl_ref, be_ref, dqa_ref, c_ref):
        i = pl.program_id(1)

        @pl.when(i == 0)
        def _():
            dka_ref[...] = jnp.zeros_like(dka_ref)
            dva_ref[...] = jnp.zeros_like(dva_ref)

        q = q_ref[...].astype(BF16)
        dob = do_ref[...].astype(BF16)
        tri_after = _tri(True)
        tri_before = _tri(False)
        nkb = (i + 1) * ratio

        c_ref[...] = jnp.zeros_like(c_ref)

        def sweep_a(jj, carry):
            j = nkb - 1 - jj
            lsz, lm, a, _ = _sb_block(q, k_ref, j, i, tq, scale, c_ref[...], tri_after)
            rows = pl.ds(pl.multiple_of(j * TK, TK), TK)
            vb = v_ref[rows, :].astype(BF16)
            da = lax.dot_general(dob, vb, (((1,), (1,)), ((), ())), preferred_element_type=F32)
            dl_ref[j] = da * a
            be_ref[j] = jnp.exp(lsz)
            dva_ref[rows, :] += jnp.dot(a.T.astype(BF16), dob, preferred_element_type=F32)
            c_ref[...] += jnp.sum(lm, axis=1, keepdims=True)
            return carry

        lax.fori_loop(0, nkb, sweep_a, 0)

        c_ref[...] = jnp.zeros_like(c_ref)
        dqa_ref[...] = jnp.zeros_like(dqa_ref)

        def sweep_b(j, carry):
            dl = dl_ref[j]
            beta = be_ref[j]
            tpos = i * tq + lax.broadcasted_iota(jnp.int32, (tq, TK), 0)
            spos = j * TK + lax.broadcasted_iota(jnp.int32, (tq, TK), 1)
            p = c_ref[...] + _split_dot(dl, tri_before)
            dz = jnp.where(spos < tpos, (dl * (1.0 - beta) - beta * p) * scale, 0.0)
            rows = pl.ds(pl.multiple_of(j * TK, TK), TK)
            kb = k_ref[rows, :].astype(BF16)
            dqa_ref[...] += jnp.dot(dz.astype(BF16), kb, preferred_element_type=F32)
            dka_ref[rows, :] += jnp.dot(dz.T.astype(BF16), q, preferred_element_type=F32)
            c_ref[...] += jnp.sum(dl, axis=1, keepdims=True)
            return carry

        lax.fori_loop(0, nkb, sweep_b, 0)
        dq_ref[...] = dqa_ref[...].astype(BF16)

        @pl.when(i == n_qb - 1)
        def _():
            dk_ref[...] = dka_ref[...].astype(BF16)
            dv_ref[...] = dva_ref[...].astype(BF16)

    qblk = pl.BlockSpec((tq, HEAD_DIM), lambda h, i: (i, h))
    full = pl.BlockSpec((s, HEAD_DIM), lambda h, i: (0, h))
    sds = jax.ShapeDtypeStruct((s, n_heads * HEAD_DIM), BF16)
    return _call("attn_bwd", body, (n_heads, n_qb),
                 [qblk, pl.BlockSpec((s, HEAD_DIM), lambda h, i: (0, n_heads + h)),
                  pl.BlockSpec((s, HEAD_DIM), lambda h, i: (0, 2 * n_heads + h)), qblk],
                 [qblk, full, full], [sds, sds, sds],
                 [pltpu.VMEM((s, HEAD_DIM), F32), pltpu.VMEM((s, HEAD_DIM), F32),
                  pltpu.VMEM((n_kb, tq, TK), F32), pltpu.VMEM((n_kb, tq, TK), F32),
                  pltpu.VMEM((tq, HEAD_DIM), F32), pltpu.VMEM((tq, 1), F32)],
                 ("parallel", "arbitrary"), (proj, proj, proj, do_sb), comm)


def _adamw(name, w, m, v, parts, part_specs, tr, comm=None):
    r, c = w.shape
    n_parts = len(parts)

    def body(*refs):
        w_ref, m_ref, v_ref = refs[:3]
        p_refs = refs[3:3 + n_parts]
        g_ref, d_ref, nm_ref, nv_ref = refs[3 + n_parts:]
        g = p_refs[0][...].astype(F32)
        for p in p_refs[1:]:
            g = g + p[...].astype(F32)
        nm = ADAM_B1 * m_ref[...] + (1.0 - ADAM_B1) * g
        nv = ADAM_B2 * v_ref[...] + (1.0 - ADAM_B2) * jnp.square(g)
        m_hat = nm / (1.0 - ADAM_B1 ** ADAM_STEP)
        v_hat = nv / (1.0 - ADAM_B2 ** ADAM_STEP)
        g_ref[...] = g
        d_ref[...] = -ADAM_LR * (m_hat / (jnp.sqrt(v_hat) + ADAM_EPS) + ADAM_WD * w_ref[...])
        nm_ref[...] = nm
        nv_ref[...] = nv

    blk = pl.BlockSpec((tr, c), lambda i: (i, 0))
    sds = jax.ShapeDtypeStruct((r, c), F32)
    return _call(name, body, (r // tr,), [blk, blk, blk, *part_specs], [blk] * 4, [sds] * 4, [], ("parallel",),
                 (w, m, v, *parts), comm)


def _adamw_big(name, w, m, v, recv, comm=None):
    r, c = w.shape
    tr = _tile(r, 128)
    order = (3, 0, 1, 2)
    specs = [pl.BlockSpec((None, tr, c), functools.partial(lambda i, slot: (slot, i, 0), slot=sl)) for sl in order]
    return _adamw(name, w, m, v, [recv] * 4, specs, tr, comm)


def _adamw_small(name, w, m, v, g):
    r, c = w.shape
    return _adamw(name, w, m, v, [g], [pl.BlockSpec((r, c), lambda i: (0, 0))], r)


def kernel(x, g_pre_mix, w_in, b_in, w_dw, b_dw, g_conv_ln, b_conv_ln, w_sb_out, w_conv_out, w_o, g_post_mix, g_pre_mlp, w_up, w_down, g_post_mlp, loss_target, m_g_pre_mix, m_w_in, m_b_in, m_w_dw, m_b_dw, m_g_conv_ln, m_b_conv_ln, m_w_sb_out, m_w_conv_out, m_w_o, m_g_post_mix, m_g_pre_mlp, m_w_up, m_w_down, m_g_post_mlp, v_g_pre_mix, v_w_in, v_b_in, v_w_dw, v_b_dw, v_g_conv_ln, v_b_conv_ln, v_w_sb_out, v_w_conv_out, v_w_o, v_g_post_mix, v_g_pre_mlp, v_w_up, v_w_down, v_g_post_mlp):
    xs, tgt = x[0], loss_target[0]
    s, d = xs.shape
    d_half = d // 2
    n_heads = d_half // HEAD_DIM
    d_ff = NDEV * w_up.shape[2]
    core = lax.axis_index("c").astype(jnp.int32).reshape(1)
    dev = 4 * lax.axis_index("x") + 2 * lax.axis_index("y") + lax.axis_index("c")

    w_dw_pad = jnp.pad(w_dw[0], ((0, CONV_PAD - CONV_WIDTH), (0, 0)))
    sh_in, sh_sb, sh_cv, sh_o, sh_up, sh_down = [w[0].astype(BF16) for w in (w_in, w_sb_out, w_conv_out, w_o, w_up, w_down)]
    wg_in, wg_dw = _run_comm("all_gather_first", _ag_comm([sh_in, w_dw_pad]))
    wf_dw = wg_dw.transpose(1, 0, 2).reshape(CONV_PAD, d_half)

    ag_down = _Chunked(_ag_comm, [sh_down], sh_down.shape[0], 8)
    h, h_t = _prenorm(xs, g_pre_mix)
    proj, wg_sb, wg_cv, wg_o = _mm_cols("proj", h, wg_in, bias=b_in, comm=_ag_comm([sh_sb, sh_cv, sh_o]))
    o_sb, o_sb_t, wg_up = _attn_fwd(proj, n_heads, _ag_comm([sh_up]))
    wf_o = wg_o.reshape(d, d)
    u1, *part = _conv_fwd(proj, wf_dw, b_dw, d_half, ag_down.take())
    ag_down.done(part)
    u3, u3_t = _ln_silu(u1, g_conv_ln, b_conv_ln)
    o_sbp, *part = _mm_cols("sb_out", o_sb, wg_sb, comm=ag_down.take())
    ag_down.done(part)
    o_cv, *part = _mm_cols("conv_out", u3, wg_cv, comm=ag_down.take())
    ag_down.done(part)
    merged, merged_t, *part = _merge(proj, o_sbp, o_cv, d, ag_down.take())
    ag_down.done(part)
    y, *part = _mm_plain("w_o", merged, wf_o, False, F32, comm=ag_down.take())
    ag_down.done(part)
    x1, h2, h2_t, *part = _postnorm_mix(xs, y, g_post_mix, g_pre_mlp, ag_down.take())
    ag_down.done(part)

    tm_up = _tile(s, 1024)
    ns_up = wg_up.shape[2]
    tk_up = _tile(d, 2048)

    def up_epilogue(acc):
        f = jnp.square(jnp.maximum(acc, 0.0))
        return acc, f, f.T

    a_act, f, f_t, wg_down = _matmul(
        "w_up", h2, wg_up,
        [jax.ShapeDtypeStruct((s, d_ff), BF16), jax.ShapeDtypeStruct((s, d_ff), BF16), jax.ShapeDtypeStruct((d_ff, s), BF16)],
        (s // tm_up, NDEV, d // tk_up),
        pl.BlockSpec((tm_up, tk_up), lambda i, dd, kk: (i, kk)),
        pl.BlockSpec((None, tk_up, ns_up), lambda i, dd, kk: (dd, kk, 0)),
        [pl.BlockSpec((tm_up, ns_up), lambda i, dd, kk: (i, dd)), pl.BlockSpec((tm_up, ns_up), lambda i, dd, kk: (i, dd)),
         pl.BlockSpec((ns_up, tm_up), lambda i, dd, kk: (dd, i))],
        1, False, (tm_up, ns_up), epilogue=up_epilogue, comm=ag_down.take(2))
    wf_down = wg_down.reshape(d_ff, d)
    f2 = _mm_plain("w_down", f, wf_down, False, F32)[0]
    dx2, df2, dg_post_mlp, loss_part = _loss_head(x1, f2, tgt, g_post_mlp)

    tm_b, tn_b = _tile(s, 1024), _tile(d_ff, 1024)
    da = _mm_plain("w_down_bwd", df2, wf_down, True, BF16,
                   extra=(a_act,), extra_specs=(pl.BlockSpec((tm_b, tn_b), lambda i, j, kk: (i, j)),),
                   epilogue=lambda acc, av: (acc * (2.0 * jnp.maximum(av.astype(F32), 0.0)),),
                   outs=[jax.ShapeDtypeStruct((s, d_ff), BF16)],
                   out_specs=[pl.BlockSpec((tm_b, tn_b), lambda i, j, kk: (i, j))])[0]
    gw_down = _mm_plain("w_down_grad", f_t, df2, False, BF16)[0]
    big_down = gw_down.reshape(4, 2, d_ff // NDEV, d)
    gw_up, sib_down = _mm_dw_cols("w_up_grad", h2_t, da, comm=_sibling_comm([big_down]))
    big_up = gw_up.reshape(4, 2, d, d_ff // NDEV)
    dh2, sib_up = _mm_cols_t("w_up_bwd", da, wg_up, comm=_sibling_comm([big_up]))
    rs_down = _Chunked(_chips_comm, [_pair_sum(big_down, sib_down, core)], d_ff // NDEV, 8)
    rs_up = _Chunked(_chips_comm, [_pair_sum(big_up, sib_up, core)], d, 8)
    dx1, dy, dg_pre_mlp, dg_post_mix, *part = _midnorm_bwd(dx2, dh2, x1, y, g_post_mix, g_pre_mlp, rs_down.take())
    rs_down.done(part)
    gw_o, *part = _mm_plain("w_o_grad", merged_t, dy, False, BF16, comm=rs_down.take())
    rs_down.done(part)
    dmerged, *part = _mm_plain("w_o_bwd", dy, wf_o, True, F32, comm=rs_down.take())
    rs_down.done(part)
    do_sbp, do_cv, dgate_sb, dgate_cv, *part = _merge_bwd(dmerged, proj, o_sbp, o_cv, d, rs_down.take())
    rs_down.done(part)
    gw_cv, *part = _mm_dw_cols("conv_out_grad", u3_t, do_cv, comm=rs_down.take())
    rs_down.done(part)
    gw_sb, *part = _mm_dw_cols("sb_out_grad", o_sb_t, do_sbp, comm=rs_down.take())
    rs_down.done(part)
    du3, *part = _mm_cols_t("conv_out_bwd", do_cv, wg_cv, comm=rs_down.take())
    rs_down.done(part)
    do_sb, r_down = _mm_cols_t("sb_out_bwd", do_sbp, wg_sb, comm=rs_down.take())
    du1, dg_ln, db_ln = _ln_silu_bwd(du3, u1, g_conv_ln, b_conv_ln)
    dglu_a, dglu_b, dw_dw, db_dw, *part = _conv_bwd(du1, proj, wf_dw, d_half, rs_up.take(2))
    rs_up.done(part)
    big_mid = [gw_sb.reshape(4, 2, d_half, d // NDEV), gw_cv.reshape(4, 2, d_half, d // NDEV),
               gw_o.reshape(4, 2, d // NDEV, d)]
    sib_mid = _run_comm("exchange_sibling_mid", _sibling_comm(big_mid))
    sums_mid = [_pair_sum(g, r, core) for g, r in zip(big_mid, sib_mid)]
    up_rest = rs_up.take(6)
    mid = _chips_comm(sums_mid)
    n_up, n_mid = len(up_rest.ins), len(mid.ins)
    both = _Comm(
        up_rest.ins + mid.ins, up_rest.outs + mid.outs, up_rest.scratch + mid.scratch,
        lambda i, o, sc: (up_rest.start(i[:n_up], o[:1], sc[:3]), mid.start(i[n_up:], o[1:], sc[3:])),
        lambda i, o, sc: (up_rest.finish(i[:n_up], o[:1], sc[:3]), mid.finish(i[n_up:], o[1:], sc[3:])),
        up_rest.aliases)
    dq, dk, dv, r_up, r_sb, r_cv, r_o = _attn_bwd(proj, do_sb, n_heads, both)
    dproj = jnp.concatenate([dq, dk, dv, dglu_a, dglu_b, dgate_sb, dgate_cv], axis=1)
    db_in = _colsum(dproj)
    gw_in = _mm_dw_cols("w_in_grad", h_t, dproj)[0]
    big_in = gw_in.reshape(4, 2, d, gw_in.shape[2])
    sib_in, = _run_comm("exchange_sibling_in", _sibling_comm([big_in]))
    rs_in = _Chunked(_chips_comm, [_pair_sum(big_in, sib_in, core)], d, 8)
    dh, *part = _mm_cols_t("w_in_bwd", dproj, wg_in, comm=rs_in.take(5))
    rs_in.done(part)
    grad_x, dg_pre_mix, *part = _prenorm_bwd(dx1, dh, xs, g_pre_mix, rs_in.take())
    rs_in.done(part)

    small = [dg_pre_mix, db_in, dw_dw.reshape(1, -1), db_dw, dg_ln, db_ln, dg_post_mix, dg_pre_mlp, dg_post_mlp]
    sizes = [a.shape[1] for a in small]
    packed = jnp.concatenate(small, axis=1).reshape(-1, LANES)
    total = _sum_small(_all_gather_small(packed)).reshape(1, -1)
    offs = [0]
    for n in sizes:
        offs.append(offs[-1] + n)
    (g_g_pre_mix, g_b_in, g_w_dw_flat, g_b_dw, g_g_conv_ln, g_b_conv_ln, g_g_post_mix, g_g_pre_mlp,
     g_g_post_mlp) = [total[:, offs[k]:offs[k + 1]] for k in range(len(sizes))]
    ch = w_dw.shape[2]
    g_w_dw = lax.dynamic_slice_in_dim(g_w_dw_flat.reshape(CONV_PAD, d_half), dev * ch, ch, axis=1)[:CONV_WIDTH]

    loss = lax.psum(loss_part[0, 0], ("x", "y", "c"))

    res = {}
    *res["w_up"], part = _adamw_big("adamw_w_up", w_up[0], m_w_up[0], v_w_up[0], r_up, rs_in.take())
    rs_in.done([part])
    *res["w_down"], r_in = _adamw_big("adamw_w_down", w_down[0], m_w_down[0], v_w_down[0], r_down, rs_in.take())
    res["g_pre_mix"] = _adamw_small("adamw_g_pre_mix", g_pre_mix, m_g_pre_mix, v_g_pre_mix, g_g_pre_mix)
    res["w_in"] = _adamw_big("adamw_w_in", w_in[0], m_w_in[0], v_w_in[0], r_in)
    res["b_in"] = _adamw_small("adamw_b_in", b_in, m_b_in, v_b_in, g_b_in)
    res["w_dw"] = _adamw_small("adamw_w_dw", w_dw[0], m_w_dw[0], v_w_dw[0], g_w_dw)
    res["b_dw"] = _adamw_small("adamw_b_dw", b_dw, m_b_dw, v_b_dw, g_b_dw)
    res["g_conv_ln"] = _adamw_small("adamw_g_conv_ln", g_conv_ln, m_g_conv_ln, v_g_conv_ln, g_g_conv_ln)
    res["b_conv_ln"] = _adamw_small("adamw_b_conv_ln", b_conv_ln, m_b_conv_ln, v_b_conv_ln, g_b_conv_ln)
    res["w_sb_out"] = _adamw_big("adamw_w_sb_out", w_sb_out[0], m_w_sb_out[0], v_w_sb_out[0], r_sb)
    res["w_conv_out"] = _adamw_big("adamw_w_conv_out", w_conv_out[0], m_w_conv_out[0], v_w_conv_out[0], r_cv)
    res["w_o"] = _adamw_big("adamw_w_o", w_o[0], m_w_o[0], v_w_o[0], r_o)
    res["g_post_mix"] = _adamw_small("adamw_g_post_mix", g_post_mix, m_g_post_mix, v_g_post_mix, g_g_post_mix)
    res["g_pre_mlp"] = _adamw_small("adamw_g_pre_mlp", g_pre_mlp, m_g_pre_mlp, v_g_pre_mlp, g_g_pre_mlp)
    res["g_post_mlp"] = _adamw_small("adamw_g_post_mlp", g_post_mlp, m_g_post_mlp, v_g_post_mlp, g_g_post_mlp)

    names = ["g_pre_mix", "w_in", "b_in", "w_dw", "b_dw", "g_conv_ln", "b_conv_ln", "w_sb_out", "w_conv_out", "w_o",
             "g_post_mix", "g_pre_mlp", "w_up", "w_down", "g_post_mlp"]
    three_d = {"w_in", "w_dw", "w_sb_out", "w_conv_out", "w_o", "w_up", "w_down"}

    def shaped(nm, arr):
        return arr[None] if nm in three_d else arr

    out = [loss, grad_x[None]]
    for k in range(4):
        out += [shaped(nm, res[nm][k]) for nm in names]
    return tuple(out)
```

```python
import functools
import math

import jax
import jax.numpy as jnp
from jax import lax
from jax.experimental import pallas as pl
from jax.experimental.pallas import tpu as pltpu

F32 = jnp.float32
BF16 = jnp.bfloat16
NDEV = 8
LANES = 128
EPS = 1e-6
CONV_WIDTH = 31
CONV_PAD = 32
HEAD_DIM = 128
ADAM_LR = 0.001
ADAM_B1 = 0.9
ADAM_B2 = 0.999
ADAM_EPS = 1e-08
ADAM_WD = 0.01
ADAM_STEP = 10
VMEM_LIMIT = 56 * 1024 * 1024
MESH = pl.DeviceIdType.MESH
ANY = pl.BlockSpec(memory_space=pl.ANY)


def _tile(n, pref):
    t = min(n, pref)
    assert n % t == 0, (n, t)
    return t


def _sigmoid(v):
    return 1.0 / (1.0 + jnp.exp(-v))


def _position():
    return lax.axis_index("x"), lax.axis_index("y"), lax.axis_index("c")


class _Comm:
    def __init__(self, ins, outs, scratch, start, finish, aliases=None):
        self.ins, self.outs, self.scratch = list(ins), list(outs), list(scratch)
        self.start, self.finish, self.aliases = start, finish, dict(aliases or {})


_NO_COMM = _Comm([], [], [], None, None)


def _call(name, body, grid, in_specs, out_specs, out_shape, scratch_shapes, sem, args, comm=None):
    comm = comm or _NO_COMM
    n_in, n_out, n_scr = len(in_specs), len(out_specs), len(scratch_shapes)
    n_cin, n_cout = len(comm.ins), len(comm.outs)

    def edge(c_ins, c_outs, c_scr, at_start):
        pids = [pl.program_id(ax) for ax in range(len(grid))]
        conds = [p == (0 if at_start else g - 1) for p, g in zip(pids, grid)]

        @pl.when(functools.reduce(jnp.logical_and, conds))
        def _():
            (comm.start if at_start else comm.finish)(c_ins, c_outs, c_scr)

    def wrapped(*refs):
        ins, c_ins = refs[:n_in], refs[n_in:n_in + n_cin]
        pos = n_in + n_cin
        outs, c_outs = refs[pos:pos + n_out], refs[pos + n_out:pos + n_out + n_cout]
        pos += n_out + n_cout
        scr, c_scr = refs[pos:pos + n_scr], refs[pos + n_scr:]
        if n_cin:
            edge(c_ins, c_outs, c_scr, True)
        body(*ins, *outs, *scr)
        if n_cin:
            edge(c_ins, c_outs, c_scr, False)

    if n_cin:
        sem = ("arbitrary",) * len(grid)
    return pl.pallas_call(
        wrapped, name=name, grid=grid,
        in_specs=[*in_specs, *[ANY] * n_cin], out_specs=[*out_specs, *[ANY] * n_cout],
        out_shape=[*out_shape, *comm.outs], scratch_shapes=[*scratch_shapes, *comm.scratch],
        input_output_aliases={n_in + ci: n_out + co for ci, co in comm.aliases.items()},
        compiler_params=pltpu.CompilerParams(dimension_semantics=sem, vmem_limit_bytes=VMEM_LIMIT),
    )(*args, *comm.ins)


def _run_comm(name, comm):
    n_in, n_out = len(comm.ins), len(comm.outs)

    def body(*refs):
        ins, outs, scr = refs[:n_in], refs[n_in:n_in + n_out], refs[n_in + n_out:]
        comm.start(ins, outs, scr)
        comm.finish(ins, outs, scr)

    return pl.pallas_call(body, name=name, in_specs=[ANY] * n_in, out_specs=[ANY] * n_out, out_shape=comm.outs,
                          scratch_shapes=comm.scratch,
                          input_output_aliases=comm.aliases)(*comm.ins)


def _rows_of(ref, rows):
    return ref if rows is None else ref.at[pl.ds(rows[0], rows[1])]


def _ag_comm(shards, rows=None, into=None):
    n = len(shards)

    def parts(ins, outs, scr):
        send_sems, recv_sems, local_sems = scr
        x, y, c = _position()
        chips = [(1 - x, y), (x, 1 - y), (1 - x, 1 - y)]

        def copy(a, k, block, to, own=False):
            px, py, pc = block
            dst = _rows_of(outs[a].at[4 * px + 2 * py + pc], rows)
            return pltpu.make_async_remote_copy(src_ref=_rows_of(ins[a], rows) if own else dst, dst_ref=dst,
                                                send_sem=send_sems.at[a, k], recv_sem=recv_sems.at[a, k],
                                                device_id=to, device_id_type=MESH)

        mine = [pltpu.make_async_copy(_rows_of(ins[a], rows), _rows_of(outs[a].at[4 * x + 2 * y + c], rows),
                                      local_sems.at[a]) for a in range(n)]
        first = []
        for a in range(n):
            first.append(copy(a, 0, (x, y, c), (x, y, 1 - c), own=True))
            first += [copy(a, 1 + j, (x, y, c), (*chip, c), own=True) for j, chip in enumerate(chips)]
        return copy, mine, first, chips, (x, y, c), (x, y, 1 - c)

    def start(ins, outs, scr):
        _, mine, first, _, _, _ = parts(ins, outs, scr)
        for cp in mine + first:
            cp.start()

    def finish(ins, outs, scr):
        copy, mine, first, chips, me, sibling = parts(ins, outs, scr)
        c = me[2]
        passed = []
        for a in range(n):
            for j, chip in enumerate(chips):
                copy(a, 1 + j, (*chip, c), me).wait_recv()
                cp = copy(a, 4 + j, (*chip, c), sibling)
                cp.start()
                passed.append(cp)
        for a in range(n):
            copy(a, 0, sibling, me).wait_recv()
            for j, chip in enumerate(chips):
                copy(a, 4 + j, (*chip, 1 - c), me).wait_recv()
        for cp in first + passed:
            cp.wait_send()
        for cp in mine:
            cp.wait()

    return _Comm([*shards, *(into or [])], [jax.ShapeDtypeStruct((NDEV, *sh.shape), sh.dtype) for sh in shards],
                 [pltpu.SemaphoreType.DMA((n, 7)), pltpu.SemaphoreType.DMA((n, 7)), pltpu.SemaphoreType.DMA((n,))],
                 start, finish, {n + a: a for a in range(n)} if into else None)


def _sibling_comm(grads):
    n = len(grads)

    def copies(ins, outs, scr):
        send_sems, recv_sems = scr
        x, y, c = _position()
        return [pltpu.make_async_remote_copy(src_ref=ins[a].at[k, 1 - c], dst_ref=outs[a].at[k],
                                             send_sem=send_sems.at[a, k], recv_sem=recv_sems.at[a, k],
                                             device_id=(x, y, 1 - c), device_id_type=MESH)
                for a in range(n) for k in range(4)]

    def start(ins, outs, scr):
        for cp in copies(ins, outs, scr):
            cp.start()

    def finish(ins, outs, scr):
        for cp in copies(ins, outs, scr):
            cp.wait()

    return _Comm(grads, [jax.ShapeDtypeStruct((4, *g.shape[2:]), g.dtype) for g in grads],
                 [pltpu.SemaphoreType.DMA((n, 4)), pltpu.SemaphoreType.DMA((n, 4))], start, finish)


def _chips_comm(sums, rows=None, into=None):
    n = len(sums)

    def copies(ins, outs, scr):
        send_sems, recv_sems, local_sems = scr
        x, y, c = _position()
        chips = [(1 - x, y), (x, 1 - y), (1 - x, 1 - y)]
        own = [pltpu.make_async_copy(_rows_of(ins[a].at[2 * x + y], rows), _rows_of(outs[a].at[3], rows), local_sems.at[a])
               for a in range(n)]
        remote = [pltpu.make_async_remote_copy(src_ref=_rows_of(ins[a].at[2 * px + py], rows),
                                               dst_ref=_rows_of(outs[a].at[j], rows),
                                               send_sem=send_sems.at[a, j], recv_sem=recv_sems.at[a, j],
                                               device_id=(px, py, c), device_id_type=MESH)
                  for a in range(n) for j, (px, py) in enumerate(chips)]
        return own + remote

    def start(ins, outs, scr):
        for cp in copies(ins, outs, scr):
            cp.start()

    def finish(ins, outs, scr):
        for cp in copies(ins, outs, scr):
            cp.wait()

    return _Comm([*sums, *(into or [])], [jax.ShapeDtypeStruct(sm.shape, sm.dtype) for sm in sums],
                 [pltpu.SemaphoreType.DMA((n, 3)), pltpu.SemaphoreType.DMA((n, 3)), pltpu.SemaphoreType.DMA((n,))],
                 start, finish, {n + a: a for a in range(n)} if into else None)


class _Chunked:
    def __init__(self, make, arrays, n_rows, n_chunks):
        self.make, self.arrays, self.into = make, arrays, None
        step = n_rows // n_chunks
        assert step * n_chunks == n_rows
        self.todo = [(k * step, step) for k in range(n_chunks)]

    def take(self, count=1):
        r0, nr = self.todo[0][0], sum(t[1] for t in self.todo[:count])
        self.todo = self.todo[count:]
        return self.make(self.arrays, (r0, nr), self.into)

    def done(self, outs):
        self.into = list(outs)
        return self.into


def _all_gather_small(part):
    def body(in_ref, out_ref, send_sems, recv_sems, local_sem):
        x, y, c = _position()
        me = 4 * x + 2 * y + c
        mine = pltpu.make_async_copy(in_ref, out_ref.at[me], local_sem)
        mine.start()
        flips = [(fx, fy, fc) for fx in (0, 1) for fy in (0, 1) for fc in (0, 1)][1:]
        copies = []
        for k, (fx, fy, fc) in enumerate(flips):
            cp = pltpu.make_async_remote_copy(src_ref=in_ref, dst_ref=out_ref.at[me], send_sem=send_sems.at[k],
                                              recv_sem=recv_sems.at[k],
                                              device_id=(x ^ fx, y ^ fy, c ^ fc), device_id_type=MESH)
            cp.start()
            copies.append(cp)
        for k, (fx, fy, fc) in enumerate(flips):
            peer = 4 * (x ^ fx) + 2 * (y ^ fy) + (c ^ fc)
            pltpu.make_async_remote_copy(src_ref=in_ref, dst_ref=out_ref.at[peer], send_sem=send_sems.at[k],
                                         recv_sem=recv_sems.at[k], device_id=(x, y, c), device_id_type=MESH).wait_recv()
        for cp in copies:
            cp.wait_send()
        mine.wait()

    return pl.pallas_call(
        body, name="all_gather_small", in_specs=[ANY], out_specs=ANY,
        out_shape=jax.ShapeDtypeStruct((NDEV, *part.shape), part.dtype),
        scratch_shapes=[pltpu.SemaphoreType.DMA((7,)), pltpu.SemaphoreType.DMA((7,)), pltpu.SemaphoreType.DMA],
    )(part)


def _pair_sum(g, recv, core):
    _, _, r, c = g.shape
    tr = _tile(r, 512)

    def body(core_ref, g_ref, r_ref, o_ref):
        o_ref[...] = (g_ref[...].astype(F32) + r_ref[...].astype(F32)).astype(o_ref.dtype)

    return pl.pallas_call(
        body, name="pair_sum",
        grid_spec=pltpu.PrefetchScalarGridSpec(
            num_scalar_prefetch=1, grid=(4, r // tr),
            in_specs=[pl.BlockSpec((None, None, tr, c), lambda k, i, core_ref: (k, core_ref[0], i, 0)),
                      pl.BlockSpec((None, tr, c), lambda k, i, core_ref: (k, i, 0))],
            out_specs=pl.BlockSpec((None, tr, c), lambda k, i, core_ref: (k, i, 0))),
        out_shape=jax.ShapeDtypeStruct((4, r, c), g.dtype),
        compiler_params=pltpu.CompilerParams(dimension_semantics=("parallel", "parallel"), vmem_limit_bytes=VMEM_LIMIT),
    )(core, g, recv)


def _sum_small(gathered):
    _, r, l = gathered.shape

    def body(g_ref, o_ref):
        acc = g_ref[0]
        for d in range(1, NDEV):
            acc = acc + g_ref[d]
        o_ref[...] = acc

    return pl.pallas_call(
        body, name="sum_small", in_specs=[pl.BlockSpec((NDEV, r, l), lambda: (0, 0, 0))],
        out_specs=pl.BlockSpec((r, l), lambda: (0, 0)), out_shape=jax.ShapeDtypeStruct((r, l), F32),
    )(gathered)


def _matmul(name, a, b, outs, grid, a_spec, b_spec, out_specs, n_red, nt, acc_shape,
            extra=(), extra_specs=(), epilogue=None, comm=None):
    n_extra, n_out = len(extra), len(outs)
    red_axes = tuple(range(len(grid) - n_red, len(grid)))
    red_sizes = tuple(grid[ax] for ax in red_axes)
    single = all(sz == 1 for sz in red_sizes)
    dims = (((1,), (1,)), ((), ())) if nt else (((1,), (0,)), ((), ()))

    def body(*refs):
        a_ref, b_ref = refs[0], refs[1]
        ex_refs = refs[2:2 + n_extra]
        o_refs = refs[2 + n_extra:2 + n_extra + n_out]
        acc_ref = refs[-1]

        def write(acc):
            vals = (acc,) if epilogue is None else epilogue(acc, *[r[...] for r in ex_refs])
            for o_ref, val in zip(o_refs, vals):
                o_ref[...] = val.astype(o_ref.dtype)

        part = lax.dot_general(a_ref[...].astype(BF16), b_ref[...].astype(BF16), dims, preferred_element_type=F32)
        if single:
            write(part)
        else:
            ks = [pl.program_id(ax) for ax in red_axes]
            first = functools.reduce(jnp.logical_and, [k == 0 for k in ks])
            last = functools.reduce(jnp.logical_and, [k == sz - 1 for k, sz in zip(ks, red_sizes)])

            @pl.when(first)
            def _():
                acc_ref[...] = part

            @pl.when(jnp.logical_not(first))
            def _():
                acc_ref[...] += part

            @pl.when(last)
            def _():
                write(acc_ref[...])

    sem = ("parallel",) * (len(grid) - n_red) + ("arbitrary",) * n_red
    return _call(name, body, grid, [a_spec, b_spec, *extra_specs], list(out_specs), list(outs),
                 [pltpu.VMEM((8, LANES) if single else acc_shape, F32)], sem, (a, b, *extra), comm)


def _mm_cols(name, a, wg, bias=None, out_dtype=F32, comm=None):
    m, k = a.shape
    _, _, ns = wg.shape
    tm, tk = _tile(m, 1024), _tile(k, 2048)
    grid = (m // tm, NDEV, k // tk)
    extra, extra_specs, epi = (), (), None
    if bias is not None:
        extra, extra_specs = (bias,), (pl.BlockSpec((1, ns), lambda i, d, kk: (0, d)),)
        epi = lambda acc, bv: (acc + bv,)
    return _matmul(name, a, wg, [jax.ShapeDtypeStruct((m, NDEV * ns), out_dtype)], grid,
                   pl.BlockSpec((tm, tk), lambda i, d, kk: (i, kk)),
                   pl.BlockSpec((None, tk, ns), lambda i, d, kk: (d, kk, 0)),
                   [pl.BlockSpec((tm, ns), lambda i, d, kk: (i, d))], 1, False, (tm, ns),
                   extra, extra_specs, epi, comm)


def _mm_cols_t(name, a, wg, comm=None):
    m, _ = a.shape
    _, n, ns = wg.shape
    tm, tn = _tile(m, 1024), _tile(n, 1024)
    grid = (m // tm, n // tn, NDEV)
    return _matmul(name, a, wg, [jax.ShapeDtypeStruct((m, n), F32)], grid,
                   pl.BlockSpec((tm, ns), lambda i, j, d: (i, d)),
                   pl.BlockSpec((None, tn, ns), lambda i, j, d: (d, j, 0)),
                   [pl.BlockSpec((tm, tn), lambda i, j, d: (i, j))], 1, True, (tm, tn), comm=comm)


def _mm_dw_cols(name, at, g, comm=None):
    m, t = at.shape
    ns = g.shape[1] // NDEV
    tm, tk = _tile(m, 1024), _tile(t, 2048)
    grid = (m // tm, NDEV, t // tk)
    return _matmul(name, at, g, [jax.ShapeDtypeStruct((NDEV, m, ns), BF16)], grid,
                   pl.BlockSpec((tm, tk), lambda i, d, kk: (i, kk)),
                   pl.BlockSpec((tk, ns), lambda i, d, kk: (kk, d)),
                   [pl.BlockSpec((None, tm, ns), lambda i, d, kk: (d, i, 0))], 1, False, (tm, ns), comm=comm)


def _mm_plain(name, a, b, nt, out_dtype, extra=(), extra_specs=(), epilogue=None, outs=None, out_specs=None, comm=None):
    m, k = a.shape
    n = b.shape[0] if nt else b.shape[1]
    tm, tn, tk = _tile(m, 1024), _tile(n, 1024), _tile(k, 2048)
    grid = (m // tm, n // tn, k // tk)
    b_spec = (pl.BlockSpec((tn, tk), lambda i, j, kk: (j, kk)) if nt
              else pl.BlockSpec((tk, tn), lambda i, j, kk: (kk, j)))
    if outs is None:
        outs = [jax.ShapeDtypeStruct((m, n), out_dtype)]
        out_specs = [pl.BlockSpec((tm, tn), lambda i, j, kk: (i, j))]
    return _matmul(name, a, b, outs, grid, pl.BlockSpec((tm, tk), lambda i, j, kk: (i, kk)), b_spec,
                   out_specs, 1, nt, (tm, tn), extra, extra_specs, epilogue, comm)


def _rms_rows(v):
    return lax.rsqrt(jnp.mean(v * v, axis=-1, keepdims=True) + EPS)


def _prenorm(x, g):
    s, d = x.shape
    ts = _tile(s, 256)

    def body(x_ref, g_ref, h_ref, ht_ref):
        xv = x_ref[...]
        h = xv * _rms_rows(xv) * g_ref[...]
        h_ref[...] = h.astype(BF16)
        ht_ref[...] = h.T.astype(BF16)

    return _call("prenorm", body, (s // ts,),
                 [pl.BlockSpec((ts, d), lambda i: (i, 0)), pl.BlockSpec((1, d), lambda i: (0, 0))],
                 [pl.BlockSpec((ts, d), lambda i: (i, 0)), pl.BlockSpec((d, ts), lambda i: (0, i))],
                 [jax.ShapeDtypeStruct((s, d), BF16), jax.ShapeDtypeStruct((d, s), BF16)], [], ("parallel",), (x, g))


def _ln_silu(u1, g, b):
    s, c = u1.shape
    ts = _tile(s, 256)

    def body(u_ref, g_ref, b_ref, o_ref, ot_ref):
        u = u_ref[...]
        mu = jnp.mean(u, axis=-1, keepdims=True)
        var = jnp.mean(jnp.square(u - mu), axis=-1, keepdims=True)
        u2 = (u - mu) * lax.rsqrt(var + EPS) * g_ref[...] + b_ref[...]
        u3 = u2 * _sigmoid(u2)
        o_ref[...] = u3.astype(BF16)
        ot_ref[...] = u3.T.astype(BF16)

    vec = pl.BlockSpec((1, c), lambda i: (0, 0))
    return _call("ln_silu", body, (s // ts,), [pl.BlockSpec((ts, c), lambda i: (i, 0)), vec, vec],
                 [pl.BlockSpec((ts, c), lambda i: (i, 0)), pl.BlockSpec((c, ts), lambda i: (0, i))],
                 [jax.ShapeDtypeStruct((s, c), BF16), jax.ShapeDtypeStruct((c, s), BF16)], [], ("parallel",), (u1, g, b))


def _ln_silu_bwd(du3, u1, g, b, comm=None):
    s, c = u1.shape
    ts = _tile(s, 256)

    def body(d_ref, u_ref, g_ref, b_ref, du1_ref, dg_ref, db_ref):
        @pl.when(pl.program_id(0) == 0)
        def _():
            dg_ref[...] = jnp.zeros_like(dg_ref)
            db_ref[...] = jnp.zeros_like(db_ref)

        u = u_ref[...]
        mu = jnp.mean(u, axis=-1, keepdims=True)
        var = jnp.mean(jnp.square(u - mu), axis=-1, keepdims=True)
        rstd = lax.rsqrt(var + EPS)
        uhat = (u - mu) * rstd
        u2 = uhat * g_ref[...] + b_ref[...]
        sg = _sigmoid(u2)
        du2 = d_ref[...] * (sg * (1.0 + u2 * (1.0 - sg)))
        dg_ref[...] += jnp.sum(du2 * uhat, axis=0, keepdims=True)
        db_ref[...] += jnp.sum(du2, axis=0, keepdims=True)
        duh = du2 * g_ref[...]
        du1_ref[...] = rstd * (duh - jnp.mean(duh, axis=-1, keepdims=True)
                               - uhat * jnp.mean(duh * uhat, axis=-1, keepdims=True))

    row = pl.BlockSpec((ts, c), lambda i: (i, 0))
    vec = pl.BlockSpec((1, c), lambda i: (0, 0))
    return _call("ln_silu_bwd", body, (s // ts,), [row, row, vec, vec], [row, vec, vec],
                 [jax.ShapeDtypeStruct((s, c), F32), jax.ShapeDtypeStruct((1, c), F32), jax.ShapeDtypeStruct((1, c), F32)],
                 [], ("arbitrary",), (du3, u1, g, b), comm)


def _merge(proj, o_sbp, o_cv, d, comm=None):
    s = proj.shape[0]
    w = d // 2
    ts = _tile(s, 256)

    def body(gs_ref, gc_ref, a_ref, b_ref, m_ref, mt_ref):
        mg = _sigmoid(gs_ref[...]) * a_ref[...] + _sigmoid(gc_ref[...]) * b_ref[...]
        m_ref[...] = mg.astype(BF16)
        mt_ref[...] = mg.T.astype(BF16)

    blk = pl.BlockSpec((ts, w), lambda i, j: (i, j))
    return _call("merge", body, (s // ts, 2),
                 [pl.BlockSpec((ts, w), lambda i, j: (i, 5 + j)), pl.BlockSpec((ts, w), lambda i, j: (i, 7 + j)), blk, blk],
                 [blk, pl.BlockSpec((w, ts), lambda i, j: (j, i))],
                 [jax.ShapeDtypeStruct((s, d), BF16), jax.ShapeDtypeStruct((d, s), BF16)], [],
                 ("parallel", "parallel"), (proj, proj, o_sbp, o_cv), comm)


def _merge_bwd(dmerged, proj, o_sbp, o_cv, d, comm=None):
    s = proj.shape[0]
    w = d // 2
    ts = _tile(s, 256)

    def body(dm_ref, gs_ref, gc_ref, a_ref, b_ref, da_ref, db_ref, dgs_ref, dgc_ref):
        dm = dm_ref[...]
        ss = _sigmoid(gs_ref[...])
        sc = _sigmoid(gc_ref[...])
        da_ref[...] = (dm * ss).astype(BF16)
        db_ref[...] = (dm * sc).astype(BF16)
        dgs_ref[...] = (dm * a_ref[...] * ss * (1.0 - ss)).astype(BF16)
        dgc_ref[...] = (dm * b_ref[...] * sc * (1.0 - sc)).astype(BF16)

    blk = pl.BlockSpec((ts, w), lambda i, j: (i, j))
    sds = jax.ShapeDtypeStruct((s, d), BF16)
    return _call("merge_bwd", body, (s // ts, 2),
                 [blk, pl.BlockSpec((ts, w), lambda i, j: (i, 5 + j)), pl.BlockSpec((ts, w), lambda i, j: (i, 7 + j)),
                  blk, blk],
                 [blk, blk, blk, blk], [sds, sds, sds, sds], [], ("parallel", "parallel"),
                 (dmerged, proj, proj, o_sbp, o_cv), comm)


def _postnorm_mix(x, y, g_post, g_pre, comm=None):
    s, d = x.shape
    ts = _tile(s, 256)

    def body(x_ref, y_ref, gp_ref, gn_ref, x1_ref, h_ref, ht_ref):
        yv = y_ref[...]
        x1 = x_ref[...] + yv * _rms_rows(yv) * gp_ref[...]
        x1_ref[...] = x1
        h = x1 * _rms_rows(x1) * gn_ref[...]
        h_ref[...] = h.astype(BF16)
        ht_ref[...] = h.T.astype(BF16)

    row = pl.BlockSpec((ts, d), lambda i: (i, 0))
    vec = pl.BlockSpec((1, d), lambda i: (0, 0))
    return _call("postnorm_mix", body, (s // ts,), [row, row, vec, vec],
                 [row, row, pl.BlockSpec((d, ts), lambda i: (0, i))],
                 [jax.ShapeDtypeStruct((s, d), F32), jax.ShapeDtypeStruct((s, d), BF16), jax.ShapeDtypeStruct((d, s), BF16)],
                 [], ("parallel",), (x, y, g_post, g_pre), comm)


def _rms_bwd(dout, vin, g):
    r = _rms_rows(vin)
    vhat = vin * r
    dyh = dout * g
    dvin = r * (dyh - vhat * jnp.mean(dyh * vhat, axis=-1, keepdims=True))
    return dvin, jnp.sum(dout * vhat, axis=0, keepdims=True)


def _loss_head(x1, f2, tgt, g):
    s, d = x1.shape
    ts = _tile(s, 256)

    def body(x1_ref, f_ref, t_ref, g_ref, dx2_ref, df2_ref, dg_ref, loss_ref):
        @pl.when(pl.program_id(0) == 0)
        def _():
            dg_ref[...] = jnp.zeros_like(dg_ref)
            loss_ref[...] = jnp.zeros_like(loss_ref)

        fv = f_ref[...]
        x2 = x1_ref[...] + fv * _rms_rows(fv) * g_ref[...]
        err = x2 - t_ref[...]
        loss_ref[...] += 0.5 * jnp.sum(jnp.mean(err * err, axis=-1, keepdims=True), axis=0, keepdims=True)
        dx2 = err * (1.0 / d)
        dx2_ref[...] = dx2
        df2, dg = _rms_bwd(dx2, fv, g_ref[...])
        df2_ref[...] = df2.astype(BF16)
        dg_ref[...] += dg

    row = pl.BlockSpec((ts, d), lambda i: (i, 0))
    vec = pl.BlockSpec((1, d), lambda i: (0, 0))
    return _call("loss_head", body, (s // ts,), [row, row, row, vec],
                 [row, row, vec, pl.BlockSpec((1, LANES), lambda i: (0, 0))],
                 [jax.ShapeDtypeStruct((s, d), F32), jax.ShapeDtypeStruct((s, d), BF16),
                  jax.ShapeDtypeStruct((1, d), F32), jax.ShapeDtypeStruct((1, LANES), F32)],
                 [], ("arbitrary",), (x1, f2, tgt, g))


def _midnorm_bwd(dx2, dh2, x1, y, g_post, g_pre, comm=None):
    s, d = x1.shape
    ts = _tile(s, 256)

    def body(dx2_ref, dh_ref, x1_ref, y_ref, gp_ref, gn_ref, dx1_ref, dy_ref, dgn_ref, dgp_ref):
        @pl.when(pl.program_id(0) == 0)
        def _():
            dgn_ref[...] = jnp.zeros_like(dgn_ref)
            dgp_ref[...] = jnp.zeros_like(dgp_ref)

        dxa, dgn = _rms_bwd(dh_ref[...], x1_ref[...], gn_ref[...])
        dx1 = dx2_ref[...] + dxa
        dx1_ref[...] = dx1
        dy, dgp = _rms_bwd(dx1, y_ref[...], gp_ref[...])
        dy_ref[...] = dy.astype(BF16)
        dgn_ref[...] += dgn
        dgp_ref[...] += dgp

    row = pl.BlockSpec((ts, d), lambda i: (i, 0))
    vec = pl.BlockSpec((1, d), lambda i: (0, 0))
    return _call("midnorm_bwd", body, (s // ts,), [row, row, row, row, vec, vec], [row, row, vec, vec],
                 [jax.ShapeDtypeStruct((s, d), F32), jax.ShapeDtypeStruct((s, d), BF16),
                  jax.ShapeDtypeStruct((1, d), F32), jax.ShapeDtypeStruct((1, d), F32)],
                 [], ("arbitrary",), (dx2, dh2, x1, y, g_post, g_pre), comm)


def _prenorm_bwd(dx1, dh, x, g, comm=None):
    s, d = x.shape
    ts = _tile(s, 256)

    def body(dx1_ref, dh_ref, x_ref, g_ref, dx_ref, dg_ref):
        @pl.when(pl.program_id(0) == 0)
        def _():
            dg_ref[...] = jnp.zeros_like(dg_ref)

        dxa, dg = _rms_bwd(dh_ref[...], x_ref[...], g_ref[...])
        dx_ref[...] = dx1_ref[...] + dxa
        dg_ref[...] += dg

    row = pl.BlockSpec((ts, d), lambda i: (i, 0))
    vec = pl.BlockSpec((1, d), lambda i: (0, 0))
    return _call("prenorm_bwd", body, (s // ts,), [row, row, row, vec], [row, vec],
                 [jax.ShapeDtypeStruct((s, d), F32), jax.ShapeDtypeStruct((1, d), F32)],
                 [], ("arbitrary",), (dx1, dh, x, g), comm)


def _colsum(a):
    s, n = a.shape
    ts = _tile(s, 256)

    def body(a_ref, o_ref):
        @pl.when(pl.program_id(0) == 0)
        def _():
            o_ref[...] = jnp.zeros_like(o_ref)

        o_ref[...] += jnp.sum(a_ref[...].astype(F32), axis=0, keepdims=True)

    return _call("colsum", body, (s // ts,), [pl.BlockSpec((ts, n), lambda i: (i, 0))],
                 [pl.BlockSpec((1, n), lambda i: (0, 0))], [jax.ShapeDtypeStruct((1, n), F32)], [], ("arbitrary",), (a,))[0]


def _shift_rows(win, off, t):
    n = win.shape[0]
    if off == 0:
        return win[:t]
    return pltpu.roll(win, n - off, axis=0)[:t]


def _conv_fwd(proj, w_pad, b_dw, c_total, comm=None):
    s = proj.shape[0]
    nct = c_total // LANES
    t = _tile(s, 256)

    def body(ga_ref, gb_ref, w_ref, b_ref, o_ref, u0_ref):
        u0_ref[pl.ds(0, CONV_PAD), :] = jnp.zeros((CONV_PAD, LANES), F32)
        u0_ref[pl.ds(CONV_PAD, s), :] = ga_ref[...] * _sigmoid(gb_ref[...])
        wv = w_ref[...]

        def chunk(r, carry):
            r0 = pl.multiple_of(r * t, t)
            win = u0_ref[pl.ds(r0, t + CONV_PAD), :]
            acc = jnp.broadcast_to(b_ref[...], (t, LANES))
            for j in range(CONV_WIDTH):
                acc = acc + wv[j:j + 1, :] * _shift_rows(win, j + CONV_PAD - (CONV_WIDTH - 1), t)
            o_ref[pl.ds(r0, t), :] = acc
            return carry

        lax.fori_loop(0, s // t, chunk, 0)

    return _call("conv_fwd", body, (nct,),
                 [pl.BlockSpec((s, LANES), lambda c: (0, 3 * nct + c)), pl.BlockSpec((s, LANES), lambda c: (0, 4 * nct + c)),
                  pl.BlockSpec((CONV_PAD, LANES), lambda c: (0, c)), pl.BlockSpec((1, LANES), lambda c: (0, c))],
                 [pl.BlockSpec((s, LANES), lambda c: (0, c))], [jax.ShapeDtypeStruct((s, c_total), F32)],
                 [pltpu.VMEM((s + CONV_PAD, LANES), F32)], ("parallel",), (proj, proj, w_pad, b_dw), comm)


def _conv_bwd(du1, proj, w_pad, c_total, comm=None):
    s = proj.shape[0]
    nct = c_total // LANES
    t = _tile(s, 256)

    def body(d_ref, ga_ref, gb_ref, w_ref, dga_ref, dgb_ref, dw_ref, db_ref, u0_ref, dp_ref):
        sg = _sigmoid(gb_ref[...])
        u0_ref[pl.ds(0, CONV_PAD), :] = jnp.zeros((CONV_PAD, LANES), F32)
        u0_ref[pl.ds(CONV_PAD, s), :] = ga_ref[...] * sg
        dp_ref[pl.ds(0, s), :] = d_ref[...]
        dp_ref[pl.ds(s, CONV_PAD), :] = jnp.zeros((CONV_PAD, LANES), F32)
        dw_ref[...] = jnp.zeros_like(dw_ref)
        db_ref[...] = jnp.sum(d_ref[...], axis=0, keepdims=True)
        wv = w_ref[...]

        def chunk(r, carry):
            r0 = pl.multiple_of(r * t, t)
            win = u0_ref[pl.ds(r0, t + CONV_PAD), :]
            dwin = dp_ref[pl.ds(r0, t + CONV_PAD), :]
            dcur = dwin[:t]
            du0 = jnp.zeros((t, LANES), F32)
            for j in range(CONV_WIDTH):
                du0 = du0 + wv[j:j + 1, :] * _shift_rows(dwin, CONV_WIDTH - 1 - j, t)
                sh = _shift_rows(win, j + CONV_PAD - (CONV_WIDTH - 1), t)
                dw_ref[j:j + 1, :] += jnp.sum(dcur * sh, axis=0, keepdims=True)
            gav = ga_ref[pl.ds(r0, t), :]
            sgv = _sigmoid(gb_ref[pl.ds(r0, t), :])
            dga_ref[pl.ds(r0, t), :] = (du0 * sgv).astype(BF16)
            dgb_ref[pl.ds(r0, t), :] = (du0 * gav * sgv * (1.0 - sgv)).astype(BF16)
            return carry

        lax.fori_loop(0, s // t, chunk, 0)

    col = pl.BlockSpec((s, LANES), lambda c: (0, c))
    return _call("conv_bwd", body, (nct,),
                 [col, pl.BlockSpec((s, LANES), lambda c: (0, 3 * nct + c)),
                  pl.BlockSpec((s, LANES), lambda c: (0, 4 * nct + c)), pl.BlockSpec((CONV_PAD, LANES), lambda c: (0, c))],
                 [col, col, pl.BlockSpec((CONV_PAD, LANES), lambda c: (0, c)), pl.BlockSpec((1, LANES), lambda c: (0, c))],
                 [jax.ShapeDtypeStruct((s, c_total), BF16), jax.ShapeDtypeStruct((s, c_total), BF16),
                  jax.ShapeDtypeStruct((CONV_PAD, c_total), F32), jax.ShapeDtypeStruct((1, c_total), F32)],
                 [pltpu.VMEM((s + CONV_PAD, LANES), F32), pltpu.VMEM((s + CONV_PAD, LANES), F32)],
                 ("parallel",), (du1, proj, proj, w_pad), comm)


TQ_PREF = 256
NU = 4
NU_BWD = 2
TK = 256


def _split_dot(v, tri):
    hi = v.astype(BF16)
    lo = (v - hi.astype(F32)).astype(BF16)
    return (jnp.dot(hi, tri, preferred_element_type=F32) + jnp.dot(lo, tri, preferred_element_type=F32))


def _causal_mask(i, j, tq):
    tpos = i * tq + lax.broadcasted_iota(jnp.int32, (tq, TK), 0)
    spos = j * TK + lax.broadcasted_iota(jnp.int32, (tq, TK), 1)
    return spos < tpos


def _log_terms(z, mask):
    sp = jnp.log(1.0 + jnp.exp(-jnp.abs(z)))
    return jnp.minimum(z, 0.0) - sp, jnp.where(mask, -jnp.maximum(z, 0.0) - sp, 0.0)


def _tri(after):
    r = lax.broadcasted_iota(jnp.int32, (TK, TK), 0)
    c = lax.broadcasted_iota(jnp.int32, (TK, TK), 1)
    return (r > c).astype(BF16) if after else (r < c).astype(BF16)


def _attn_fwd(proj, n_heads, comm=None):
    s = proj.shape[0]
    tq = _tile(s, TQ_PREF)
    scale = 1.0 / math.sqrt(HEAD_DIM)
    ratio = tq // TK

    def body(q_ref, k_ref, v_ref, o_ref, ot_ref, acc_ref, *clms):
        i = pl.program_id(1)
        heads = [slice(u * HEAD_DIM, (u + 1) * HEAD_DIM) for u in range(NU)]
        qs = [q_ref[:, hs].astype(BF16) for hs in heads]
        tri_after = _tri(True)
        acc_ref[...] = jnp.zeros_like(acc_ref)
        for cr in clms:
            cr[...] = jnp.zeros_like(cr)
        nkb = (i + 1) * ratio

        def step(jj, carry):
            j = nkb - 1 - jj
            rows = pl.ds(pl.multiple_of(j * TK, TK), TK)
            mask = _causal_mask(i, j, tq)
            zs = [lax.dot_general(qs[u], k_ref[rows, hs].astype(BF16), (((1,), (1,)), ((), ())),
                                  preferred_element_type=F32) * scale for u, hs in enumerate(heads)]
            lls = [_log_terms(z, mask) for z in zs]
            sufs = [clms[u][...] + _split_dot(lls[u][1], tri_after) for u in range(NU)]
            for u, hs in enumerate(heads):
                a = jnp.where(mask, jnp.exp(lls[u][0] + sufs[u]), 0.0)
                acc_ref[:, hs] += jnp.dot(a.astype(BF16), v_ref[rows, hs].astype(BF16), preferred_element_type=F32)
                clms[u][...] += jnp.sum(lls[u][1], axis=1, keepdims=True)
            return carry

        lax.fori_loop(0, nkb, step, 0)
        o = acc_ref[...]
        o_ref[...] = o
        ot_ref[...] = o.T.astype(BF16)

    w = NU * HEAD_DIM
    ng = n_heads // NU
    return _call("attn_fwd", body, (ng, s // tq),
                 [pl.BlockSpec((tq, w), lambda h, i: (i, h)),
                  pl.BlockSpec((s, w), lambda h, i: (0, ng + h)),
                  pl.BlockSpec((s, w), lambda h, i: (0, 2 * ng + h))],
                 [pl.BlockSpec((tq, w), lambda h, i: (i, h)), pl.BlockSpec((w, tq), lambda h, i: (h, i))],
                 [jax.ShapeDtypeStruct((s, n_heads * HEAD_DIM), F32), jax.ShapeDtypeStruct((n_heads * HEAD_DIM, s), BF16)],
                 [pltpu.VMEM((tq, w), F32), *[pltpu.VMEM((tq, 1), F32)] * NU],
                 ("parallel", "arbitrary"), (proj, proj, proj), comm)


def _attn_bwd(proj, do_sb, n_heads, comm=None):
    s = proj.shape[0]
    tq = _tile(s, TQ_PREF)
    scale = 1.0 / math.sqrt(HEAD_DIM)
    ratio = tq // TK
    n_kb = s // TK
    n_qb = s // tq
    nu = NU_BWD
    heads = [slice(u * HEAD_DIM, (u + 1) * HEAD_DIM) for u in range(nu)]
    nt_dims = (((1,), (1,)), ((), ()))

    def body(q_ref, k_ref, v_ref, do_ref, dq_ref, dk_ref, dv_ref, dka_ref, dva_ref, dl_ref, be_ref, dqa_ref, *c_refs):
        i = pl.program_id(1)

        @pl.when(i == 0)
        def _():
            dka_ref[...] = jnp.zeros_like(dka_ref)
            dva_ref[...] = jnp.zeros_like(dva_ref)

        qs = [q_ref[:, hs].astype(BF16) for hs in heads]
        dobs = [do_ref[:, hs].astype(BF16) for hs in heads]
        tri_after = _tri(True)
        tri_before = _tri(False)
        nkb = (i + 1) * ratio

        for cr in c_refs:
            cr[...] = jnp.zeros_like(cr)

        def sweep_a(jj, carry):
            j = nkb - 1 - jj
            rows = pl.ds(pl.multiple_of(j * TK, TK), TK)
            mask = _causal_mask(i, j, tq)
            zs = [lax.dot_general(qs[u], k_ref[rows, hs].astype(BF16), nt_dims, preferred_element_type=F32) * scale
                  for u, hs in enumerate(heads)]
            das = [lax.dot_general(dobs[u], v_ref[rows, hs].astype(BF16), nt_dims, preferred_element_type=F32)
                   for u, hs in enumerate(heads)]
            lls = [_log_terms(z, mask) for z in zs]
            sufs = [c_refs[u][...] + _split_dot(lls[u][1], tri_after) for u in range(nu)]
            for u, hs in enumerate(heads):
                a = jnp.where(mask, jnp.exp(lls[u][0] + sufs[u]), 0.0)
                dl_ref[u, j] = das[u] * a
                be_ref[u, j] = jnp.exp(lls[u][0])
                dva_ref[rows, hs] += jnp.dot(a.T.astype(BF16), dobs[u], preferred_element_type=F32)
                c_refs[u][...] += jnp.sum(lls[u][1], axis=1, keepdims=True)
            return carry

        lax.fori_loop(0, nkb, sweep_a, 0)

        for cr in c_refs:
            cr[...] = jnp.zeros_like(cr)
        dqa_ref[...] = jnp.zeros_like(dqa_ref)

        def sweep_b(j, carry):
            rows = pl.ds(pl.multiple_of(j * TK, TK), TK)
            mask = _causal_mask(i, j, tq)
            dls = [dl_ref[u, j] for u in range(nu)]
            ps = [c_refs[u][...] + _split_dot(dls[u], tri_before) for u in range(nu)]
            for u, hs in enumerate(heads):
                beta = be_ref[u, j]
                dz = jnp.where(mask, (dls[u] * (1.0 - beta) - beta * ps[u]) * scale, 0.0)
                dqa_ref[:, hs] += jnp.dot(dz.astype(BF16), k_ref[rows, hs].astype(BF16), preferred_element_type=F32)
                dka_ref[rows, hs] += jnp.dot(dz.T.astype(BF16), qs[u], preferred_element_type=F32)
                c_refs[u][...] += jnp.sum(dls[u], axis=1, keepdims=True)
            return carry

        lax.fori_loop(0, nkb, sweep_b, 0)
        dq_ref[...] = dqa_ref[...].astype(BF16)

        @pl.when(i == n_qb - 1)
        def _():
            dk_ref[...] = dka_ref[...].astype(BF16)
            dv_ref[...] = dva_ref[...].astype(BF16)

    w = nu * HEAD_DIM
    ng = n_heads // nu
    qblk = pl.BlockSpec((tq, w), lambda h, i: (i, h))
    full = pl.BlockSpec((s, w), lambda h, i: (0, h))
    sds = jax.ShapeDtypeStruct((s, n_heads * HEAD_DIM), BF16)
    return _call("attn_bwd", body, (ng, n_qb),
                 [qblk, pl.BlockSpec((s, w), lambda h, i: (0, ng + h)), pl.BlockSpec((s, w), lambda h, i: (0, 2 * ng + h)),
                  qblk],
                 [qblk, full, full], [sds, sds, sds],
                 [pltpu.VMEM((s, w), F32), pltpu.VMEM((s, w), F32),
                  pltpu.VMEM((nu, n_kb, tq, TK), F32), pltpu.VMEM((nu, n_kb, tq, TK), F32),
                  pltpu.VMEM((tq, w), F32), *[pltpu.VMEM((tq, 1), F32)] * nu],
                 ("parallel", "arbitrary"), (proj, proj, proj, do_sb), comm)


def _adamw(name, w, m, v, parts, part_specs, tr, comm=None):
    r, c = w.shape
    n_parts = len(parts)

    def body(*refs):
        w_ref, m_ref, v_ref = refs[:3]
        p_refs = refs[3:3 + n_parts]
        g_ref, d_ref, nm_ref, nv_ref = refs[3 + n_parts:]
        g = p_refs[0][...].astype(F32)
        for p in p_refs[1:]:
            g = g + p[...].astype(F32)
        nm = ADAM_B1 * m_ref[...] + (1.0 - ADAM_B1) * g
        nv = ADAM_B2 * v_ref[...] + (1.0 - ADAM_B2) * jnp.square(g)
        m_hat = nm / (1.0 - ADAM_B1 ** ADAM_STEP)
        v_hat = nv / (1.0 - ADAM_B2 ** ADAM_STEP)
        g_ref[...] = g
        d_ref[...] = -ADAM_LR * (m_hat / (jnp.sqrt(v_hat) + ADAM_EPS) + ADAM_WD * w_ref[...])
        nm_ref[...] = nm
        nv_ref[...] = nv

    blk = pl.BlockSpec((tr, c), lambda i: (i, 0))
    sds = jax.ShapeDtypeStruct((r, c), F32)
    return _call(name, body, (r // tr,), [blk, blk, blk, *part_specs], [blk] * 4, [sds] * 4, [], ("parallel",),
                 (w, m, v, *parts), comm)


def _adamw_big(name, w, m, v, recv, comm=None):
    r, c = w.shape
    tr = _tile(r, 128)
    order = (3, 0, 1, 2)
    specs = [pl.BlockSpec((None, tr, c), functools.partial(lambda i, slot: (slot, i, 0), slot=sl)) for sl in order]
    return _adamw(name, w, m, v, [recv] * 4, specs, tr, comm)


def _adamw_small(name, w, m, v, g):
    r, c = w.shape
    return _adamw(name, w, m, v, [g], [pl.BlockSpec((r, c), lambda i: (0, 0))], r)


def kernel(x, g_pre_mix, w_in, b_in, w_dw, b_dw, g_conv_ln, b_conv_ln, w_sb_out, w_conv_out, w_o, g_post_mix, g_pre_mlp, w_up, w_down, g_post_mlp, loss_target, m_g_pre_mix, m_w_in, m_b_in, m_w_dw, m_b_dw, m_g_conv_ln, m_b_conv_ln, m_w_sb_out, m_w_conv_out, m_w_o, m_g_post_mix, m_g_pre_mlp, m_w_up, m_w_down, m_g_post_mlp, v_g_pre_mix, v_w_in, v_b_in, v_w_dw, v_b_dw, v_g_conv_ln, v_b_conv_ln, v_w_sb_out, v_w_conv_out, v_w_o, v_g_post_mix, v_g_pre_mlp, v_w_up, v_w_down, v_g_post_mlp):
    xs, tgt = x[0], loss_target[0]
    s, d = xs.shape
    d_half = d // 2
    n_heads = d_half // HEAD_DIM
    d_ff = NDEV * w_up.shape[2]
    core = lax.axis_index("c").astype(jnp.int32).reshape(1)
    dev = 4 * lax.axis_index("x") + 2 * lax.axis_index("y") + lax.axis_index("c")

    w_dw_pad = jnp.pad(w_dw[0], ((0, CONV_PAD - CONV_WIDTH), (0, 0)))
    sh_in, sh_sb, sh_cv, sh_o, sh_up, sh_down = [w[0].astype(BF16) for w in (w_in, w_sb_out, w_conv_out, w_o, w_up, w_down)]
    wg_in, wg_dw = _run_comm("all_gather_first", _ag_comm([sh_in, w_dw_pad]))
    wf_dw = wg_dw.transpose(1, 0, 2).reshape(CONV_PAD, d_half)

    ag_down = _Chunked(_ag_comm, [sh_down], sh_down.shape[0], 8)
    h, h_t = _prenorm(xs, g_pre_mix)
    proj, wg_sb, wg_cv, wg_o = _mm_cols("proj", h, wg_in, bias=b_in, comm=_ag_comm([sh_sb, sh_cv, sh_o]))
    o_sb, o_sb_t, wg_up = _attn_fwd(proj, n_heads, _ag_comm([sh_up]))
    wf_o = wg_o.reshape(d, d)
    u1, *part = _conv_fwd(proj, wf_dw, b_dw, d_half, ag_down.take())
    ag_down.done(part)
    u3, u3_t = _ln_silu(u1, g_conv_ln, b_conv_ln)
    o_sbp = _mm_cols("sb_out", o_sb, wg_sb)[0]
    o_cv = _mm_cols("conv_out", u3, wg_cv)[0]
    merged, merged_t, *part = _merge(proj, o_sbp, o_cv, d, ag_down.take())
    ag_down.done(part)
    y, *part = _mm_plain("w_o", merged, wf_o, False, F32, comm=ag_down.take())
    ag_down.done(part)
    x1, h2, h2_t, *part = _postnorm_mix(xs, y, g_post_mix, g_pre_mlp, ag_down.take())
    ag_down.done(part)

    tm_up = _tile(s, 1024)
    ns_up = wg_up.shape[2]
    tk_up = _tile(d, 2048)

    def up_epilogue(acc):
        f = jnp.square(jnp.maximum(acc, 0.0))
        return acc, f, f.T

    a_act, f, f_t, wg_down = _matmul(
        "w_up", h2, wg_up,
        [jax.ShapeDtypeStruct((s, d_ff), BF16), jax.ShapeDtypeStruct((s, d_ff), BF16), jax.ShapeDtypeStruct((d_ff, s), BF16)],
        (s // tm_up, NDEV, d // tk_up),
        pl.BlockSpec((tm_up, tk_up), lambda i, dd, kk: (i, kk)),
        pl.BlockSpec((None, tk_up, ns_up), lambda i, dd, kk: (dd, kk, 0)),
        [pl.BlockSpec((tm_up, ns_up), lambda i, dd, kk: (i, dd)), pl.BlockSpec((tm_up, ns_up), lambda i, dd, kk: (i, dd)),
         pl.BlockSpec((ns_up, tm_up), lambda i, dd, kk: (dd, i))],
        1, False, (tm_up, ns_up), epilogue=up_epilogue, comm=ag_down.take(4))
    wf_down = wg_down.reshape(d_ff, d)
    f2 = _mm_plain("w_down", f, wf_down, False, F32)[0]
    dx2, df2, dg_post_mlp, loss_part = _loss_head(x1, f2, tgt, g_post_mlp)

    tm_b, tn_b = _tile(s, 1024), _tile(d_ff, 1024)
    da = _mm_plain("w_down_bwd", df2, wf_down, True, BF16,
                   extra=(a_act,), extra_specs=(pl.BlockSpec((tm_b, tn_b), lambda i, j, kk: (i, j)),),
                   epilogue=lambda acc, av: (acc * (2.0 * jnp.maximum(av.astype(F32), 0.0)),),
                   outs=[jax.ShapeDtypeStruct((s, d_ff), BF16)],
                   out_specs=[pl.BlockSpec((tm_b, tn_b), lambda i, j, kk: (i, j))])[0]
    gw_down = _mm_plain("w_down_grad", f_t, df2, False, BF16)[0]
    big_down = gw_down.reshape(4, 2, d_ff // NDEV, d)
    gw_up, sib_down = _mm_dw_cols("w_up_grad", h2_t, da, comm=_sibling_comm([big_down]))
    big_up = gw_up.reshape(4, 2, d, d_ff // NDEV)
    dh2, sib_up = _mm_cols_t("w_up_bwd", da, wg_up, comm=_sibling_comm([big_up]))
    rs_down = _Chunked(_chips_comm, [_pair_sum(big_down, sib_down, core)], d_ff // NDEV, 8)
    rs_up = _Chunked(_chips_comm, [_pair_sum(big_up, sib_up, core)], d, 8)
    dx1, dy, dg_pre_mlp, dg_post_mix, *part = _midnorm_bwd(dx2, dh2, x1, y, g_post_mix, g_pre_mlp, rs_down.take())
    rs_down.done(part)
    gw_o, *part = _mm_plain("w_o_grad", merged_t, dy, False, BF16, comm=rs_down.take())
    rs_down.done(part)
    dmerged, *part = _mm_plain("w_o_bwd", dy, wf_o, True, F32, comm=rs_down.take())
    rs_down.done(part)
    do_sbp, do_cv, dgate_sb, dgate_cv, *part = _merge_bwd(dmerged, proj, o_sbp, o_cv, d, rs_down.take())
    rs_down.done(part)
    gw_cv = _mm_dw_cols("conv_out_grad", u3_t, do_cv)[0]
    gw_sb = _mm_dw_cols("sb_out_grad", o_sb_t, do_sbp)[0]
    du3 = _mm_cols_t("conv_out_bwd", do_cv, wg_cv)[0]
    do_sb = _mm_cols_t("sb_out_bwd", do_sbp, wg_sb)[0]
    du1, dg_ln, db_ln = _ln_silu_bwd(du3, u1, g_conv_ln, b_conv_ln)
    dglu_a, dglu_b, dw_dw, db_dw, *part = _conv_bwd(du1, proj, wf_dw, d_half, rs_up.take(2))
    rs_up.done(part)
    big_mid = [gw_sb.reshape(4, 2, d_half, d // NDEV), gw_cv.reshape(4, 2, d_half, d // NDEV),
               gw_o.reshape(4, 2, d // NDEV, d)]
    sib_mid = _run_comm("exchange_sibling_mid", _sibling_comm(big_mid))
    sums_mid = [_pair_sum(g, r, core) for g, r in zip(big_mid, sib_mid)]
    up_rest = rs_up.take(6)
    mid = _chips_comm(sums_mid)
    n_up, n_mid = len(up_rest.ins), len(mid.ins)
    both = _Comm(
        up_rest.ins + mid.ins, up_rest.outs + mid.outs, up_rest.scratch + mid.scratch,
        lambda i, o, sc: (up_rest.start(i[:n_up], o[:1], sc[:3]), mid.start(i[n_up:], o[1:], sc[3:])),
        lambda i, o, sc: (up_rest.finish(i[:n_up], o[:1], sc[:3]), mid.finish(i[n_up:], o[1:], sc[3:])),
        up_rest.aliases)
    dq, dk, dv, r_up, r_sb, r_cv, r_o = _attn_bwd(proj, do_sb, n_heads, both)
    dproj = jnp.concatenate([dq, dk, dv, dglu_a, dglu_b, dgate_sb, dgate_cv], axis=1)
    db_in = _colsum(dproj)
    gw_in, r_down = _mm_dw_cols("w_in_grad", h_t, dproj, comm=rs_down.take(4))
    big_in = gw_in.reshape(4, 2, d, gw_in.shape[2])
    sib_in, = _run_comm("exchange_sibling_in", _sibling_comm([big_in]))
    rs_in = _Chunked(_chips_comm, [_pair_sum(big_in, sib_in, core)], d, 8)
    dh, *part = _mm_cols_t("w_in_bwd", dproj, wg_in, comm=rs_in.take(5))
    rs_in.done(part)
    grad_x, dg_pre_mix, *part = _prenorm_bwd(dx1, dh, xs, g_pre_mix, rs_in.take())
    rs_in.done(part)

    small = [dg_pre_mix, db_in, dw_dw.reshape(1, -1), db_dw, dg_ln, db_ln, dg_post_mix, dg_pre_mlp, dg_post_mlp]
    sizes = [a.shape[1] for a in small]
    packed = jnp.concatenate(small, axis=1).reshape(-1, LANES)
    total = _sum_small(_all_gather_small(packed)).reshape(1, -1)
    offs = [0]
    for n in sizes:
        offs.append(offs[-1] + n)
    (g_g_pre_mix, g_b_in, g_w_dw_flat, g_b_dw, g_g_conv_ln, g_b_conv_ln, g_g_post_mix, g_g_pre_mlp,
     g_g_post_mlp) = [total[:, offs[k]:offs[k + 1]] for k in range(len(sizes))]
    ch = w_dw.shape[2]
    g_w_dw = lax.dynamic_slice_in_dim(g_w_dw_flat.reshape(CONV_PAD, d_half), dev * ch, ch, axis=1)[:CONV_WIDTH]

    loss = lax.psum(loss_part[0, 0], ("x", "y", "c"))

    res = {}
    *res["w_up"], part = _adamw_big("adamw_w_up", w_up[0], m_w_up[0], v_w_up[0], r_up, rs_in.take())
    rs_in.done([part])
    *res["w_down"], r_in = _adamw_big("adamw_w_down", w_down[0], m_w_down[0], v_w_down[0], r_down, rs_in.take())
    res["g_pre_mix"] = _adamw_small("adamw_g_pre_mix", g_pre_mix, m_g_pre_mix, v_g_pre_mix, g_g_pre_mix)
    res["w_in"] = _adamw_big("adamw_w_in", w_in[0], m_w_in[0], v_w_in[0], r_in)
    res["b_in"] = _adamw_small("adamw_b_in", b_in, m_b_in, v_b_in, g_b_in)
    res["w_dw"] = _adamw_small("adamw_w_dw", w_dw[0], m_w_dw[0], v_w_dw[0], g_w_dw)
    res["b_dw"] = _adamw_small("adamw_b_dw", b_dw, m_b_dw, v_b_dw, g_b_dw)
    res["g_conv_ln"] = _adamw_small("adamw_g_conv_ln", g_conv_ln, m_g_conv_ln, v_g_conv_ln, g_g_conv_ln)
    res["b_conv_ln"] = _adamw_small("adamw_b_conv_ln", b_conv_ln, m_b_conv_ln, v_b_conv_ln, g_b_conv_ln)
    res["w_sb_out"] = _adamw_big("adamw_w_sb_out", w_sb_out[0], m_w_sb_out[0], v_w_sb_out[0], r_sb)
    res["w_conv_out"] = _adamw_big("adamw_w_conv_out", w_conv_out[0], m_w_conv_out[0], v_w_conv_out[0], r_cv)
    res["w_o"] = _adamw_big("adamw_w_o", w_o[0], m_w_o[0], v_w_o[0], r_o)
    res["g_post_mix"] = _adamw_small("adamw_g_post_mix", g_post_mix, m_g_post_mix, v_g_post_mix, g_g_post_mix)
    res["g_pre_mlp"] = _adamw_small("adamw_g_pre_mlp", g_pre_mlp, m_g_pre_mlp, v_g_pre_mlp, g_g_pre_mlp)
    res["g_post_mlp"] = _adamw_small("adamw_g_post_mlp", g_post_mlp, m_g_post_mlp, v_g_post_mlp, g_g_post_mlp)

    names = ["g_pre_mix", "w_in", "b_in", "w_dw", "b_dw", "g_conv_ln", "b_conv_ln", "w_sb_out", "w_conv_out", "w_o",
             "g_post_mix", "g_pre_mlp", "w_up", "w_down", "g_post_mlp"]
    three_d = {"w_in", "w_dw", "w_sb_out", "w_conv_out", "w_o", "w_up", "w_down"}

    def shaped(nm, arr):
        return arr[None] if nm in three_d else arr

    out = [loss, grad_x[None]]
    for k in range(4):
        out += [shaped(nm, res[nm][k]) for nm in names]
    return tuple(out)
```

```python
import functools
import math

import jax
import jax.numpy as jnp
from jax import lax
from jax.experimental import pallas as pl
from jax.experimental.pallas import tpu as pltpu

F32 = jnp.float32
BF16 = jnp.bfloat16
NDEV = 8
LANES = 128
EPS = 1e-6
CONV_WIDTH = 31
CONV_PAD = 32
HEAD_DIM = 128
ADAM_LR = 0.001
ADAM_B1 = 0.9
ADAM_B2 = 0.999
ADAM_EPS = 1e-08
ADAM_WD = 0.01
ADAM_STEP = 10
VMEM_LIMIT = 56 * 1024 * 1024
MESH = pl.DeviceIdType.MESH
ANY = pl.BlockSpec(memory_space=pl.ANY)


def _tile(n, pref):
    t = min(n, pref)
    assert n % t == 0, (n, t)
    return t


def _sigmoid(v):
    return 1.0 / (1.0 + jnp.exp(-v))


def _position():
    return lax.axis_index("x"), lax.axis_index("y"), lax.axis_index("c")


class _Comm:
    def __init__(self, ins, outs, scratch, start, finish, aliases=None):
        self.ins, self.outs, self.scratch = list(ins), list(outs), list(scratch)
        self.start, self.finish, self.aliases = start, finish, dict(aliases or {})


_NO_COMM = _Comm([], [], [], None, None)


def _call(name, body, grid, in_specs, out_specs, out_shape, scratch_shapes, sem, args, comm=None):
    comm = comm or _NO_COMM
    n_in, n_out, n_scr = len(in_specs), len(out_specs), len(scratch_shapes)
    n_cin, n_cout = len(comm.ins), len(comm.outs)

    def edge(c_ins, c_outs, c_scr, at_start):
        pids = [pl.program_id(ax) for ax in range(len(grid))]
        conds = [p == (0 if at_start else g - 1) for p, g in zip(pids, grid)]

        @pl.when(functools.reduce(jnp.logical_and, conds))
        def _():
            (comm.start if at_start else comm.finish)(c_ins, c_outs, c_scr)

    def wrapped(*refs):
        ins, c_ins = refs[:n_in], refs[n_in:n_in + n_cin]
        pos = n_in + n_cin
        outs, c_outs = refs[pos:pos + n_out], refs[pos + n_out:pos + n_out + n_cout]
        pos += n_out + n_cout
        scr, c_scr = refs[pos:pos + n_scr], refs[pos + n_scr:]
        if n_cin:
            edge(c_ins, c_outs, c_scr, True)
        body(*ins, *outs, *scr)
        if n_cin:
            edge(c_ins, c_outs, c_scr, False)

    if n_cin:
        sem = ("arbitrary",) * len(grid)
    return pl.pallas_call(
        wrapped, name=name, grid=grid,
        in_specs=[*in_specs, *[ANY] * n_cin], out_specs=[*out_specs, *[ANY] * n_cout],
        out_shape=[*out_shape, *comm.outs], scratch_shapes=[*scratch_shapes, *comm.scratch],
        input_output_aliases={n_in + ci: n_out + co for ci, co in comm.aliases.items()},
        compiler_params=pltpu.CompilerParams(dimension_semantics=sem, vmem_limit_bytes=VMEM_LIMIT),
    )(*args, *comm.ins)


def _run_comm(name, comm):
    n_in, n_out = len(comm.ins), len(comm.outs)

    def body(*refs):
        ins, outs, scr = refs[:n_in], refs[n_in:n_in + n_out], refs[n_in + n_out:]
        comm.start(ins, outs, scr)
        comm.finish(ins, outs, scr)

    return pl.pallas_call(body, name=name, in_specs=[ANY] * n_in, out_specs=[ANY] * n_out, out_shape=comm.outs,
                          scratch_shapes=comm.scratch,
                          input_output_aliases=comm.aliases)(*comm.ins)


def _rows_of(ref, rows):
    return ref if rows is None else ref.at[pl.ds(rows[0], rows[1])]


def _ag_comm(shards, rows=None, into=None):
    n = len(shards)

    def parts(ins, outs, scr):
        send_sems, recv_sems, local_sems = scr
        x, y, c = _position()
        chips = [(1 - x, y), (x, 1 - y), (1 - x, 1 - y)]

        def copy(a, k, block, to, own=False):
            px, py, pc = block
            dst = _rows_of(outs[a].at[4 * px + 2 * py + pc], rows)
            return pltpu.make_async_remote_copy(src_ref=_rows_of(ins[a], rows) if own else dst, dst_ref=dst,
                                                send_sem=send_sems.at[a, k], recv_sem=recv_sems.at[a, k],
                                                device_id=to, device_id_type=MESH)

        mine = [pltpu.make_async_copy(_rows_of(ins[a], rows), _rows_of(outs[a].at[4 * x + 2 * y + c], rows),
                                      local_sems.at[a]) for a in range(n)]
        first = []
        for a in range(n):
            first.append(copy(a, 0, (x, y, c), (x, y, 1 - c), own=True))
            first += [copy(a, 1 + j, (x, y, c), (*chip, c), own=True) for j, chip in enumerate(chips)]
        return copy, mine, first, chips, (x, y, c), (x, y, 1 - c)

    def start(ins, outs, scr):
        _, mine, first, _, _, _ = parts(ins, outs, scr)
        for cp in mine + first:
            cp.start()

    def finish(ins, outs, scr):
        copy, mine, first, chips, me, sibling = parts(ins, outs, scr)
        c = me[2]
        passed = []
        for a in range(n):
            for j, chip in enumerate(chips):
                copy(a, 1 + j, (*chip, c), me).wait_recv()
                cp = copy(a, 4 + j, (*chip, c), sibling)
                cp.start()
                passed.append(cp)
        for a in range(n):
            copy(a, 0, sibling, me).wait_recv()
            for j, chip in enumerate(chips):
                copy(a, 4 + j, (*chip, 1 - c), me).wait_recv()
        for cp in first + passed:
            cp.wait_send()
        for cp in mine:
            cp.wait()

    return _Comm([*shards, *(into or [])], [jax.ShapeDtypeStruct((NDEV, *sh.shape), sh.dtype) for sh in shards],
                 [pltpu.SemaphoreType.DMA((n, 7)), pltpu.SemaphoreType.DMA((n, 7)), pltpu.SemaphoreType.DMA((n,))],
                 start, finish, {n + a: a for a in range(n)} if into else None)


def _sibling_comm(grads):
    n = len(grads)

    def copies(ins, outs, scr):
        send_sems, recv_sems = scr
        x, y, c = _position()
        return [pltpu.make_async_remote_copy(src_ref=ins[a].at[k, 1 - c], dst_ref=outs[a].at[k],
                                             send_sem=send_sems.at[a, k], recv_sem=recv_sems.at[a, k],
                                             device_id=(x, y, 1 - c), device_id_type=MESH)
                for a in range(n) for k in range(4)]

    def start(ins, outs, scr):
        for cp in copies(ins, outs, scr):
            cp.start()

    def finish(ins, outs, scr):
        for cp in copies(ins, outs, scr):
            cp.wait()

    return _Comm(grads, [jax.ShapeDtypeStruct((4, *g.shape[2:]), g.dtype) for g in grads],
                 [pltpu.SemaphoreType.DMA((n, 4)), pltpu.SemaphoreType.DMA((n, 4))], start, finish)


def _chips_comm(sums, rows=None, into=None):
    n = len(sums)

    def copies(ins, outs, scr):
        send_sems, recv_sems, local_sems = scr
        x, y, c = _position()
        chips = [(1 - x, y), (x, 1 - y), (1 - x, 1 - y)]
        own = [pltpu.make_async_copy(_rows_of(ins[a].at[2 * x + y], rows), _rows_of(outs[a].at[3], rows), local_sems.at[a])
               for a in range(n)]
        remote = [pltpu.make_async_remote_copy(src_ref=_rows_of(ins[a].at[2 * px + py], rows),
                                               dst_ref=_rows_of(outs[a].at[j], rows),
                                               send_sem=send_sems.at[a, j], recv_sem=recv_sems.at[a, j],
                                               device_id=(px, py, c), device_id_type=MESH)
                  for a in range(n) for j, (px, py) in enumerate(chips)]
        return own + remote

    def start(ins, outs, scr):
        for cp in copies(ins, outs, scr):
            cp.start()

    def finish(ins, outs, scr):
        for cp in copies(ins, outs, scr):
            cp.wait()

    return _Comm([*sums, *(into or [])], [jax.ShapeDtypeStruct(sm.shape, sm.dtype) for sm in sums],
                 [pltpu.SemaphoreType.DMA((n, 3)), pltpu.SemaphoreType.DMA((n, 3)), pltpu.SemaphoreType.DMA((n,))],
                 start, finish, {n + a: a for a in range(n)} if into else None)


def _join(c1, c2):
    n_in, n_out, n_scr = len(c1.ins), len(c1.outs), len(c1.scratch)
    aliases = dict(c1.aliases)
    aliases.update({n_in + ci: n_out + co for ci, co in c2.aliases.items()})

    def start(ins, outs, scr):
        c1.start(ins[:n_in], outs[:n_out], scr[:n_scr])
        c2.start(ins[n_in:], outs[n_out:], scr[n_scr:])

    def finish(ins, outs, scr):
        c1.finish(ins[:n_in], outs[:n_out], scr[:n_scr])
        c2.finish(ins[n_in:], outs[n_out:], scr[n_scr:])

    return _Comm(c1.ins + c2.ins, c1.outs + c2.outs, c1.scratch + c2.scratch, start, finish, aliases)


GATHER_FLIPS = ((0, 0, 0), (0, 0, 1), (1, 0, 0), (0, 1, 0), (1, 0, 1), (0, 1, 1), (1, 1, 0), (1, 1, 1))
GATHER_RECV_SEM = {1: 0, 2: 1, 3: 2, 4: 4, 5: 5, 6: 3, 7: 6}
GATHER_PASS_SEM = {2: 4, 3: 5, 6: 6}


def _proj_gather(h, shard, bias, order):
    s_len, d = h.shape
    _, ns = shard.shape
    n_steps = len(GATHER_FLIPS)

    def body(order_ref, h_ref, sh_ref, b_ref, o_ref, wg_ref, w_vmem, load_sem, send_sems, recv_sems, local_sem):
        step = pl.program_id(0)
        x, y, c = _position()
        sibling = (x, y, 1 - c)

        def block(flip):
            return wg_ref.at[4 * (x ^ flip[0]) + 2 * (y ^ flip[1]) + (c ^ flip[2])]

        def copy(k, flip, to, own=False):
            dst = block(flip)
            return pltpu.make_async_remote_copy(src_ref=sh_ref if own else dst, dst_ref=dst, send_sem=send_sems.at[k],
                                                recv_sem=recv_sems.at[k], device_id=to, device_id_type=MESH)

        mine = pltpu.make_async_copy(sh_ref, block(GATHER_FLIPS[0]), local_sem)
        first = [copy(0, GATHER_FLIPS[0], sibling, own=True)]
        first += [copy(1 + j, GATHER_FLIPS[0], (x ^ fx, y ^ fy, c), own=True)
                  for j, (fx, fy) in enumerate(((1, 0), (0, 1), (1, 1)))]
        passed = [copy(k, GATHER_FLIPS[t], sibling) for t, k in GATHER_PASS_SEM.items()]

        for t, flip in enumerate(GATHER_FLIPS):
            @pl.when(step == t)
            def _(t=t, flip=flip):
                if t == 0:
                    for cp in [mine] + first:
                        cp.start()
                    src = sh_ref
                else:
                    copy(GATHER_RECV_SEM[t], flip, (x, y, c)).wait_recv()
                    if t in GATHER_PASS_SEM:
                        copy(GATHER_PASS_SEM[t], flip, sibling).start()
                    src = block(flip)
                load = pltpu.make_async_copy(src, w_vmem, load_sem)
                load.start()
                load.wait()

        o_ref[...] = jnp.dot(h_ref[...], w_vmem[...], preferred_element_type=F32) + b_ref[...]

        @pl.when(step == n_steps - 1)
        def _():
            for cp in first + passed:
                cp.wait_send()
            mine.wait()

    return pl.pallas_call(
        body, name="proj_gather",
        grid_spec=pltpu.PrefetchScalarGridSpec(
            num_scalar_prefetch=1, grid=(n_steps,),
            in_specs=[pl.BlockSpec((s_len, d), lambda t, order_ref: (0, 0)), ANY,
                      pl.BlockSpec((1, ns), lambda t, order_ref: (0, order_ref[t]))],
            out_specs=[pl.BlockSpec((s_len, ns), lambda t, order_ref: (0, order_ref[t])), ANY],
            scratch_shapes=[pltpu.VMEM((d, ns), BF16), pltpu.SemaphoreType.DMA, pltpu.SemaphoreType.DMA((7,)),
                            pltpu.SemaphoreType.DMA((7,)), pltpu.SemaphoreType.DMA]),
        out_shape=[jax.ShapeDtypeStruct((s_len, NDEV * ns), F32), jax.ShapeDtypeStruct((NDEV, d, ns), BF16)],
        compiler_params=pltpu.CompilerParams(dimension_semantics=("arbitrary",), vmem_limit_bytes=VMEM_LIMIT),
    )(order, h, shard, bias)


class _Chunked:
    def __init__(self, make, arrays, n_rows, n_chunks):
        self.make, self.arrays, self.into = make, arrays, None
        step = n_rows // n_chunks
        assert step * n_chunks == n_rows
        self.todo = [(k * step, step) for k in range(n_chunks)]

    def take(self, count=1):
        r0, nr = self.todo[0][0], sum(t[1] for t in self.todo[:count])
        self.todo = self.todo[count:]
        return self.make(self.arrays, (r0, nr), self.into)

    def done(self, outs):
        self.into = list(outs)
        return self.into


def _all_gather_small(part):
    def body(in_ref, out_ref, send_sems, recv_sems, local_sem):
        x, y, c = _position()
        me = 4 * x + 2 * y + c
        mine = pltpu.make_async_copy(in_ref, out_ref.at[me], local_sem)
        mine.start()
        flips = [(fx, fy, fc) for fx in (0, 1) for fy in (0, 1) for fc in (0, 1)][1:]
        copies = []
        for k, (fx, fy, fc) in enumerate(flips):
            cp = pltpu.make_async_remote_copy(src_ref=in_ref, dst_ref=out_ref.at[me], send_sem=send_sems.at[k],
                                              recv_sem=recv_sems.at[k],
                                              device_id=(x ^ fx, y ^ fy, c ^ fc), device_id_type=MESH)
            cp.start()
            copies.append(cp)
        for k, (fx, fy, fc) in enumerate(flips):
            peer = 4 * (x ^ fx) + 2 * (y ^ fy) + (c ^ fc)
            pltpu.make_async_remote_copy(src_ref=in_ref, dst_ref=out_ref.at[peer], send_sem=send_sems.at[k],
                                         recv_sem=recv_sems.at[k], device_id=(x, y, c), device_id_type=MESH).wait_recv()
        for cp in copies:
            cp.wait_send()
        mine.wait()

    return pl.pallas_call(
        body, name="all_gather_small", in_specs=[ANY], out_specs=ANY,
        out_shape=jax.ShapeDtypeStruct((NDEV, *part.shape), part.dtype),
        scratch_shapes=[pltpu.SemaphoreType.DMA((7,)), pltpu.SemaphoreType.DMA((7,)), pltpu.SemaphoreType.DMA],
    )(part)


def _pair_sum(g, recv, core):
    _, _, r, c = g.shape
    tr = _tile(r, 512)

    def body(core_ref, g_ref, r_ref, o_ref):
        o_ref[...] = (g_ref[...].astype(F32) + r_ref[...].astype(F32)).astype(o_ref.dtype)

    return pl.pallas_call(
        body, name="pair_sum",
        grid_spec=pltpu.PrefetchScalarGridSpec(
            num_scalar_prefetch=1, grid=(4, r // tr),
            in_specs=[pl.BlockSpec((None, None, tr, c), lambda k, i, core_ref: (k, core_ref[0], i, 0)),
                      pl.BlockSpec((None, tr, c), lambda k, i, core_ref: (k, i, 0))],
            out_specs=pl.BlockSpec((None, tr, c), lambda k, i, core_ref: (k, i, 0))),
        out_shape=jax.ShapeDtypeStruct((4, r, c), g.dtype),
        compiler_params=pltpu.CompilerParams(dimension_semantics=("parallel", "parallel"), vmem_limit_bytes=VMEM_LIMIT),
    )(core, g, recv)


def _sum_small(gathered):
    _, r, l = gathered.shape

    def body(g_ref, o_ref):
        acc = g_ref[0]
        for d in range(1, NDEV):
            acc = acc + g_ref[d]
        o_ref[...] = acc

    return pl.pallas_call(
        body, name="sum_small", in_specs=[pl.BlockSpec((NDEV, r, l), lambda: (0, 0, 0))],
        out_specs=pl.BlockSpec((r, l), lambda: (0, 0)), out_shape=jax.ShapeDtypeStruct((r, l), F32),
    )(gathered)


def _matmul(name, a, b, outs, grid, a_spec, b_spec, out_specs, n_red, nt, acc_shape,
            extra=(), extra_specs=(), epilogue=None, comm=None):
    n_extra, n_out = len(extra), len(outs)
    red_axes = tuple(range(len(grid) - n_red, len(grid)))
    red_sizes = tuple(grid[ax] for ax in red_axes)
    single = all(sz == 1 for sz in red_sizes)
    dims = (((1,), (1,)), ((), ())) if nt else (((1,), (0,)), ((), ()))

    def body(*refs):
        a_ref, b_ref = refs[0], refs[1]
        ex_refs = refs[2:2 + n_extra]
        o_refs = refs[2 + n_extra:2 + n_extra + n_out]
        acc_ref = refs[-1]

        def write(acc):
            vals = (acc,) if epilogue is None else epilogue(acc, *[r[...] for r in ex_refs])
            for o_ref, val in zip(o_refs, vals):
                o_ref[...] = val.astype(o_ref.dtype)

        part = lax.dot_general(a_ref[...].astype(BF16), b_ref[...].astype(BF16), dims, preferred_element_type=F32)
        if single:
            write(part)
        else:
            ks = [pl.program_id(ax) for ax in red_axes]
            first = functools.reduce(jnp.logical_and, [k == 0 for k in ks])
            last = functools.reduce(jnp.logical_and, [k == sz - 1 for k, sz in zip(ks, red_sizes)])

            @pl.when(first)
            def _():
                acc_ref[...] = part

            @pl.when(jnp.logical_not(first))
            def _():
                acc_ref[...] += part

            @pl.when(last)
            def _():
                write(acc_ref[...])

    sem = ("parallel",) * (len(grid) - n_red) + ("arbitrary",) * n_red
    return _call(name, body, grid, [a_spec, b_spec, *extra_specs], list(out_specs), list(outs),
                 [pltpu.VMEM((8, LANES) if single else acc_shape, F32)], sem, (a, b, *extra), comm)


def _mm_cols(name, a, wg, bias=None, out_dtype=F32, comm=None):
    m, k = a.shape
    _, _, ns = wg.shape
    tm, tk = _tile(m, 1024), _tile(k, 2048)
    grid = (m // tm, NDEV, k // tk)
    extra, extra_specs, epi = (), (), None
    if bias is not None:
        extra, extra_specs = (bias,), (pl.BlockSpec((1, ns), lambda i, d, kk: (0, d)),)
        epi = lambda acc, bv: (acc + bv,)
    return _matmul(name, a, wg, [jax.ShapeDtypeStruct((m, NDEV * ns), out_dtype)], grid,
                   pl.BlockSpec((tm, tk), lambda i, d, kk: (i, kk)),
                   pl.BlockSpec((None, tk, ns), lambda i, d, kk: (d, kk, 0)),
                   [pl.BlockSpec((tm, ns), lambda i, d, kk: (i, d))], 1, False, (tm, ns),
                   extra, extra_specs, epi, comm)


def _mm_cols_t(name, a, wg, comm=None):
    m, _ = a.shape
    _, n, ns = wg.shape
    tm, tn = _tile(m, 1024), _tile(n, 1024)
    grid = (m // tm, n // tn, NDEV)
    return _matmul(name, a, wg, [jax.ShapeDtypeStruct((m, n), F32)], grid,
                   pl.BlockSpec((tm, ns), lambda i, j, d: (i, d)),
                   pl.BlockSpec((None, tn, ns), lambda i, j, d: (d, j, 0)),
                   [pl.BlockSpec((tm, tn), lambda i, j, d: (i, j))], 1, True, (tm, tn), comm=comm)


def _mm_dw_cols(name, at, g, comm=None):
    m, t = at.shape
    ns = g.shape[1] // NDEV
    tm, tk = _tile(m, 1024), _tile(t, 2048)
    grid = (m // tm, NDEV, t // tk)
    return _matmul(name, at, g, [jax.ShapeDtypeStruct((NDEV, m, ns), BF16)], grid,
                   pl.BlockSpec((tm, tk), lambda i, d, kk: (i, kk)),
                   pl.BlockSpec((tk, ns), lambda i, d, kk: (kk, d)),
                   [pl.BlockSpec((None, tm, ns), lambda i, d, kk: (d, i, 0))], 1, False, (tm, ns), comm=comm)


def _mm_plain(name, a, b, nt, out_dtype, extra=(), extra_specs=(), epilogue=None, outs=None, out_specs=None, comm=None):
    m, k = a.shape
    n = b.shape[0] if nt else b.shape[1]
    tm, tn, tk = _tile(m, 1024), _tile(n, 1024), _tile(k, 2048)
    grid = (m // tm, n // tn, k // tk)
    b_spec = (pl.BlockSpec((tn, tk), lambda i, j, kk: (j, kk)) if nt
              else pl.BlockSpec((tk, tn), lambda i, j, kk: (kk, j)))
    if outs is None:
        outs = [jax.ShapeDtypeStruct((m, n), out_dtype)]
        out_specs = [pl.BlockSpec((tm, tn), lambda i, j, kk: (i, j))]
    return _matmul(name, a, b, outs, grid, pl.BlockSpec((tm, tk), lambda i, j, kk: (i, kk)), b_spec,
                   out_specs, 1, nt, (tm, tn), extra, extra_specs, epilogue, comm)


def _rms_rows(v):
    return lax.rsqrt(jnp.mean(v * v, axis=-1, keepdims=True) + EPS)


def _prenorm(x, g):
    s, d = x.shape
    ts = _tile(s, 256)

    def body(x_ref, g_ref, h_ref, ht_ref):
        xv = x_ref[...]
        h = xv * _rms_rows(xv) * g_ref[...]
        h_ref[...] = h.astype(BF16)
        ht_ref[...] = h.T.astype(BF16)

    return _call("prenorm", body, (s // ts,),
                 [pl.BlockSpec((ts, d), lambda i: (i, 0)), pl.BlockSpec((1, d), lambda i: (0, 0))],
                 [pl.BlockSpec((ts, d), lambda i: (i, 0)), pl.BlockSpec((d, ts), lambda i: (0, i))],
                 [jax.ShapeDtypeStruct((s, d), BF16), jax.ShapeDtypeStruct((d, s), BF16)], [], ("parallel",), (x, g))


def _ln_silu(u1, g, b):
    s, c = u1.shape
    ts = _tile(s, 256)

    def body(u_ref, g_ref, b_ref, o_ref, ot_ref):
        u = u_ref[...]
        mu = jnp.mean(u, axis=-1, keepdims=True)
        var = jnp.mean(jnp.square(u - mu), axis=-1, keepdims=True)
        u2 = (u - mu) * lax.rsqrt(var + EPS) * g_ref[...] + b_ref[...]
        u3 = u2 * _sigmoid(u2)
        o_ref[...] = u3.astype(BF16)
        ot_ref[...] = u3.T.astype(BF16)

    vec = pl.BlockSpec((1, c), lambda i: (0, 0))
    return _call("ln_silu", body, (s // ts,), [pl.BlockSpec((ts, c), lambda i: (i, 0)), vec, vec],
                 [pl.BlockSpec((ts, c), lambda i: (i, 0)), pl.BlockSpec((c, ts), lambda i: (0, i))],
                 [jax.ShapeDtypeStruct((s, c), BF16), jax.ShapeDtypeStruct((c, s), BF16)], [], ("parallel",), (u1, g, b))


def _ln_silu_bwd(du3, u1, g, b, comm=None):
    s, c = u1.shape
    ts = _tile(s, 256)

    def body(d_ref, u_ref, g_ref, b_ref, du1_ref, dg_ref, db_ref):
        @pl.when(pl.program_id(0) == 0)
        def _():
            dg_ref[...] = jnp.zeros_like(dg_ref)
            db_ref[...] = jnp.zeros_like(db_ref)

        u = u_ref[...]
        mu = jnp.mean(u, axis=-1, keepdims=True)
        var = jnp.mean(jnp.square(u - mu), axis=-1, keepdims=True)
        rstd = lax.rsqrt(var + EPS)
        uhat = (u - mu) * rstd
        u2 = uhat * g_ref[...] + b_ref[...]
        sg = _sigmoid(u2)
        du2 = d_ref[...] * (sg * (1.0 + u2 * (1.0 - sg)))
        dg_ref[...] += jnp.sum(du2 * uhat, axis=0, keepdims=True)
        db_ref[...] += jnp.sum(du2, axis=0, keepdims=True)
        duh = du2 * g_ref[...]
        du1_ref[...] = rstd * (duh - jnp.mean(duh, axis=-1, keepdims=True)
                               - uhat * jnp.mean(duh * uhat, axis=-1, keepdims=True))

    row = pl.BlockSpec((ts, c), lambda i: (i, 0))
    vec = pl.BlockSpec((1, c), lambda i: (0, 0))
    return _call("ln_silu_bwd", body, (s // ts,), [row, row, vec, vec], [row, vec, vec],
                 [jax.ShapeDtypeStruct((s, c), F32), jax.ShapeDtypeStruct((1, c), F32), jax.ShapeDtypeStruct((1, c), F32)],
                 [], ("arbitrary",), (du3, u1, g, b), comm)


def _merge(proj, o_sbp, o_cv, d, comm=None):
    s = proj.shape[0]
    w = d // 2
    ts = _tile(s, 256)

    def body(gs_ref, gc_ref, a_ref, b_ref, m_ref, mt_ref):
        mg = _sigmoid(gs_ref[...]) * a_ref[...] + _sigmoid(gc_ref[...]) * b_ref[...]
        m_ref[...] = mg.astype(BF16)
        mt_ref[...] = mg.T.astype(BF16)

    blk = pl.BlockSpec((ts, w), lambda i, j: (i, j))
    return _call("merge", body, (s // ts, 2),
                 [pl.BlockSpec((ts, w), lambda i, j: (i, 5 + j)), pl.BlockSpec((ts, w), lambda i, j: (i, 7 + j)), blk, blk],
                 [blk, pl.BlockSpec((w, ts), lambda i, j: (j, i))],
                 [jax.ShapeDtypeStruct((s, d), BF16), jax.ShapeDtypeStruct((d, s), BF16)], [],
                 ("parallel", "parallel"), (proj, proj, o_sbp, o_cv), comm)


def _merge_bwd(dmerged, proj, o_sbp, o_cv, d, comm=None):
    s = proj.shape[0]
    w = d // 2
    ts = _tile(s, 256)

    def body(dm_ref, gs_ref, gc_ref, a_ref, b_ref, da_ref, db_ref, dgs_ref, dgc_ref):
        dm = dm_ref[...]
        ss = _sigmoid(gs_ref[...])
        sc = _sigmoid(gc_ref[...])
        da_ref[...] = (dm * ss).astype(BF16)
        db_ref[...] = (dm * sc).astype(BF16)
        dgs_ref[...] = (dm * a_ref[...] * ss * (1.0 - ss)).astype(BF16)
        dgc_ref[...] = (dm * b_ref[...] * sc * (1.0 - sc)).astype(BF16)

    blk = pl.BlockSpec((ts, w), lambda i, j: (i, j))
    sds = jax.ShapeDtypeStruct((s, d), BF16)
    return _call("merge_bwd", body, (s // ts, 2),
                 [blk, pl.BlockSpec((ts, w), lambda i, j: (i, 5 + j)), pl.BlockSpec((ts, w), lambda i, j: (i, 7 + j)),
                  blk, blk],
                 [blk, blk, blk, blk], [sds, sds, sds, sds], [], ("parallel", "parallel"),
                 (dmerged, proj, proj, o_sbp, o_cv), comm)


def _postnorm_mix(x, y, g_post, g_pre, comm=None):
    s, d = x.shape
    ts = _tile(s, 256)

    def body(x_ref, y_ref, gp_ref, gn_ref, x1_ref, h_ref, ht_ref):
        yv = y_ref[...]
        x1 = x_ref[...] + yv * _rms_rows(yv) * gp_ref[...]
        x1_ref[...] = x1
        h = x1 * _rms_rows(x1) * gn_ref[...]
        h_ref[...] = h.astype(BF16)
        ht_ref[...] = h.T.astype(BF16)

    row = pl.BlockSpec((ts, d), lambda i: (i, 0))
    vec = pl.BlockSpec((1, d), lambda i: (0, 0))
    return _call("postnorm_mix", body, (s // ts,), [row, row, vec, vec],
                 [row, row, pl.BlockSpec((d, ts), lambda i: (0, i))],
                 [jax.ShapeDtypeStruct((s, d), F32), jax.ShapeDtypeStruct((s, d), BF16), jax.ShapeDtypeStruct((d, s), BF16)],
                 [], ("parallel",), (x, y, g_post, g_pre), comm)


def _rms_bwd(dout, vin, g):
    r = _rms_rows(vin)
    vhat = vin * r
    dyh = dout * g
    dvin = r * (dyh - vhat * jnp.mean(dyh * vhat, axis=-1, keepdims=True))
    return dvin, jnp.sum(dout * vhat, axis=0, keepdims=True)


def _loss_head(x1, f2, tgt, g):
    s, d = x1.shape
    ts = _tile(s, 256)

    def body(x1_ref, f_ref, t_ref, g_ref, dx2_ref, df2_ref, dg_ref, loss_ref):
        @pl.when(pl.program_id(0) == 0)
        def _():
            dg_ref[...] = jnp.zeros_like(dg_ref)
            loss_ref[...] = jnp.zeros_like(loss_ref)

        fv = f_ref[...]
        x2 = x1_ref[...] + fv * _rms_rows(fv) * g_ref[...]
        err = x2 - t_ref[...]
        loss_ref[...] += 0.5 * jnp.sum(jnp.mean(err * err, axis=-1, keepdims=True), axis=0, keepdims=True)
        dx2 = err * (1.0 / d)
        dx2_ref[...] = dx2
        df2, dg = _rms_bwd(dx2, fv, g_ref[...])
        df2_ref[...] = df2.astype(BF16)
        dg_ref[...] += dg

    row = pl.BlockSpec((ts, d), lambda i: (i, 0))
    vec = pl.BlockSpec((1, d), lambda i: (0, 0))
    return _call("loss_head", body, (s // ts,), [row, row, row, vec],
                 [row, row, vec, pl.BlockSpec((1, LANES), lambda i: (0, 0))],
                 [jax.ShapeDtypeStruct((s, d), F32), jax.ShapeDtypeStruct((s, d), BF16),
                  jax.ShapeDtypeStruct((1, d), F32), jax.ShapeDtypeStruct((1, LANES), F32)],
                 [], ("arbitrary",), (x1, f2, tgt, g))


def _midnorm_bwd(dx2, dh2, x1, y, g_post, g_pre, comm=None):
    s, d = x1.shape
    ts = _tile(s, 256)

    def body(dx2_ref, dh_ref, x1_ref, y_ref, gp_ref, gn_ref, dx1_ref, dy_ref, dgn_ref, dgp_ref):
        @pl.when(pl.program_id(0) == 0)
        def _():
            dgn_ref[...] = jnp.zeros_like(dgn_ref)
            dgp_ref[...] = jnp.zeros_like(dgp_ref)

        dxa, dgn = _rms_bwd(dh_ref[...], x1_ref[...], gn_ref[...])
        dx1 = dx2_ref[...] + dxa
        dx1_ref[...] = dx1
        dy, dgp = _rms_bwd(dx1, y_ref[...], gp_ref[...])
        dy_ref[...] = dy.astype(BF16)
        dgn_ref[...] += dgn
        dgp_ref[...] += dgp

    row = pl.BlockSpec((ts, d), lambda i: (i, 0))
    vec = pl.BlockSpec((1, d), lambda i: (0, 0))
    return _call("midnorm_bwd", body, (s // ts,), [row, row, row, row, vec, vec], [row, row, vec, vec],
                 [jax.ShapeDtypeStruct((s, d), F32), jax.ShapeDtypeStruct((s, d), BF16),
                  jax.ShapeDtypeStruct((1, d), F32), jax.ShapeDtypeStruct((1, d), F32)],
                 [], ("arbitrary",), (dx2, dh2, x1, y, g_post, g_pre), comm)


def _prenorm_bwd(dx1, dh, x, g, comm=None):
    s, d = x.shape
    ts = _tile(s, 256)

    def body(dx1_ref, dh_ref, x_ref, g_ref, dx_ref, dg_ref):
        @pl.when(pl.program_id(0) == 0)
        def _():
            dg_ref[...] = jnp.zeros_like(dg_ref)

        dxa, dg = _rms_bwd(dh_ref[...], x_ref[...], g_ref[...])
        dx_ref[...] = dx1_ref[...] + dxa
        dg_ref[...] += dg

    row = pl.BlockSpec((ts, d), lambda i: (i, 0))
    vec = pl.BlockSpec((1, d), lambda i: (0, 0))
    return _call("prenorm_bwd", body, (s // ts,), [row, row, row, vec], [row, vec],
                 [jax.ShapeDtypeStruct((s, d), F32), jax.ShapeDtypeStruct((1, d), F32)],
                 [], ("arbitrary",), (dx1, dh, x, g), comm)


def _colsum(a):
    s, n = a.shape
    ts = _tile(s, 256)

    def body(a_ref, o_ref):
        @pl.when(pl.program_id(0) == 0)
        def _():
            o_ref[...] = jnp.zeros_like(o_ref)

        o_ref[...] += jnp.sum(a_ref[...].astype(F32), axis=0, keepdims=True)

    return _call("colsum", body, (s // ts,), [pl.BlockSpec((ts, n), lambda i: (i, 0))],
                 [pl.BlockSpec((1, n), lambda i: (0, 0))], [jax.ShapeDtypeStruct((1, n), F32)], [], ("arbitrary",), (a,))[0]


def _shift_rows(win, off, t):
    n = win.shape[0]
    if off == 0:
        return win[:t]
    return pltpu.roll(win, n - off, axis=0)[:t]


def _conv_fwd(proj, w_pad, b_dw, c_total, comm=None):
    s = proj.shape[0]
    nct = c_total // LANES
    t = _tile(s, 256)

    def body(ga_ref, gb_ref, w_ref, b_ref, o_ref, u0_ref):
        u0_ref[pl.ds(0, CONV_PAD), :] = jnp.zeros((CONV_PAD, LANES), F32)
        u0_ref[pl.ds(CONV_PAD, s), :] = ga_ref[...] * _sigmoid(gb_ref[...])
        wv = w_ref[...]

        def chunk(r, carry):
            r0 = pl.multiple_of(r * t, t)
            win = u0_ref[pl.ds(r0, t + CONV_PAD), :]
            acc = jnp.broadcast_to(b_ref[...], (t, LANES))
            for j in range(CONV_WIDTH):
                acc = acc + wv[j:j + 1, :] * _shift_rows(win, j + CONV_PAD - (CONV_WIDTH - 1), t)
            o_ref[pl.ds(r0, t), :] = acc
            return carry

        lax.fori_loop(0, s // t, chunk, 0)

    return _call("conv_fwd", body, (nct,),
                 [pl.BlockSpec((s, LANES), lambda c: (0, 3 * nct + c)), pl.BlockSpec((s, LANES), lambda c: (0, 4 * nct + c)),
                  pl.BlockSpec((CONV_PAD, LANES), lambda c: (0, c)), pl.BlockSpec((1, LANES), lambda c: (0, c))],
                 [pl.BlockSpec((s, LANES), lambda c: (0, c))], [jax.ShapeDtypeStruct((s, c_total), F32)],
                 [pltpu.VMEM((s + CONV_PAD, LANES), F32)], ("parallel",), (proj, proj, w_pad, b_dw), comm)


def _conv_bwd(du1, proj, w_pad, c_total, comm=None):
    s = proj.shape[0]
    nct = c_total // LANES
    t = _tile(s, 256)

    def body(d_ref, ga_ref, gb_ref, w_ref, dga_ref, dgb_ref, dw_ref, db_ref, u0_ref, dp_ref):
        sg = _sigmoid(gb_ref[...])
        u0_ref[pl.ds(0, CONV_PAD), :] = jnp.zeros((CONV_PAD, LANES), F32)
        u0_ref[pl.ds(CONV_PAD, s), :] = ga_ref[...] * sg
        dp_ref[pl.ds(0, s), :] = d_ref[...]
        dp_ref[pl.ds(s, CONV_PAD), :] = jnp.zeros((CONV_PAD, LANES), F32)
        dw_ref[...] = jnp.zeros_like(dw_ref)
        db_ref[...] = jnp.sum(d_ref[...], axis=0, keepdims=True)
        wv = w_ref[...]

        def chunk(r, carry):
            r0 = pl.multiple_of(r * t, t)
            win = u0_ref[pl.ds(r0, t + CONV_PAD), :]
            dwin = dp_ref[pl.ds(r0, t + CONV_PAD), :]
            dcur = dwin[:t]
            du0 = jnp.zeros((t, LANES), F32)
            for j in range(CONV_WIDTH):
                du0 = du0 + wv[j:j + 1, :] * _shift_rows(dwin, CONV_WIDTH - 1 - j, t)
                sh = _shift_rows(win, j + CONV_PAD - (CONV_WIDTH - 1), t)
                dw_ref[j:j + 1, :] += jnp.sum(dcur * sh, axis=0, keepdims=True)
            gav = ga_ref[pl.ds(r0, t), :]
            sgv = _sigmoid(gb_ref[pl.ds(r0, t), :])
            dga_ref[pl.ds(r0, t), :] = (du0 * sgv).astype(BF16)
            dgb_ref[pl.ds(r0, t), :] = (du0 * gav * sgv * (1.0 - sgv)).astype(BF16)
            return carry

        lax.fori_loop(0, s // t, chunk, 0)

    col = pl.BlockSpec((s, LANES), lambda c: (0, c))
    return _call("conv_bwd", body, (nct,),
                 [col, pl.BlockSpec((s, LANES), lambda c: (0, 3 * nct + c)),
                  pl.BlockSpec((s, LANES), lambda c: (0, 4 * nct + c)), pl.BlockSpec((CONV_PAD, LANES), lambda c: (0, c))],
                 [col, col, pl.BlockSpec((CONV_PAD, LANES), lambda c: (0, c)), pl.BlockSpec((1, LANES), lambda c: (0, c))],
                 [jax.ShapeDtypeStruct((s, c_total), BF16), jax.ShapeDtypeStruct((s, c_total), BF16),
                  jax.ShapeDtypeStruct((CONV_PAD, c_total), F32), jax.ShapeDtypeStruct((1, c_total), F32)],
                 [pltpu.VMEM((s + CONV_PAD, LANES), F32), pltpu.VMEM((s + CONV_PAD, LANES), F32)],
                 ("parallel",), (du1, proj, proj, w_pad), comm)


TQ_PREF = 256
NU = 4
NU_BWD = 2
TK = 256


def _split_dot(v, tri):
    hi = v.astype(BF16)
    lo = (v - hi.astype(F32)).astype(BF16)
    return (jnp.dot(hi, tri, preferred_element_type=F32) + jnp.dot(lo, tri, preferred_element_type=F32))


def _causal_mask(i, j, tq):
    tpos = i * tq + lax.broadcasted_iota(jnp.int32, (tq, TK), 0)
    spos = j * TK + lax.broadcasted_iota(jnp.int32, (tq, TK), 1)
    return spos < tpos


def _log_terms(z, mask):
    sp = jnp.log(1.0 + jnp.exp(-jnp.abs(z)))
    return jnp.minimum(z, 0.0) - sp, jnp.where(mask, -jnp.maximum(z, 0.0) - sp, 0.0)


def _tri(after):
    r = lax.broadcasted_iota(jnp.int32, (TK, TK), 0)
    c = lax.broadcasted_iota(jnp.int32, (TK, TK), 1)
    return (r > c).astype(BF16) if after else (r < c).astype(BF16)


def _attn_fwd(proj, n_heads, comm=None):
    s = proj.shape[0]
    tq = _tile(s, TQ_PREF)
    scale = 1.0 / math.sqrt(HEAD_DIM)
    ratio = tq // TK

    def body(q_ref, k_ref, v_ref, o_ref, ot_ref, acc_ref, *clms):
        i = pl.program_id(1)
        heads = [slice(u * HEAD_DIM, (u + 1) * HEAD_DIM) for u in range(NU)]
        qs = [q_ref[:, hs].astype(BF16) for hs in heads]
        tri_after = _tri(True)
        acc_ref[...] = jnp.zeros_like(acc_ref)
        for cr in clms:
            cr[...] = jnp.zeros_like(cr)
        nkb = (i + 1) * ratio

        def step(jj, carry):
            j = nkb - 1 - jj
            rows = pl.ds(pl.multiple_of(j * TK, TK), TK)
            mask = _causal_mask(i, j, tq)
            zs = [lax.dot_general(qs[u], k_ref[rows, hs].astype(BF16), (((1,), (1,)), ((), ())),
                                  preferred_element_type=F32) * scale for u, hs in enumerate(heads)]
            lls = [_log_terms(z, mask) for z in zs]
            sufs = [clms[u][...] + _split_dot(lls[u][1], tri_after) for u in range(NU)]
            for u, hs in enumerate(heads):
                a = jnp.where(mask, jnp.exp(lls[u][0] + sufs[u]), 0.0)
                acc_ref[:, hs] += jnp.dot(a.astype(BF16), v_ref[rows, hs].astype(BF16), preferred_element_type=F32)
                clms[u][...] += jnp.sum(lls[u][1], axis=1, keepdims=True)
            return carry

        lax.fori_loop(0, nkb, step, 0)
        o = acc_ref[...]
        o_ref[...] = o
        ot_ref[...] = o.T.astype(BF16)

    w = NU * HEAD_DIM
    ng = n_heads // NU
    return _call("attn_fwd", body, (ng, s // tq),
                 [pl.BlockSpec((tq, w), lambda h, i: (i, h)),
                  pl.BlockSpec((s, w), lambda h, i: (0, ng + h)),
                  pl.BlockSpec((s, w), lambda h, i: (0, 2 * ng + h))],
                 [pl.BlockSpec((tq, w), lambda h, i: (i, h)), pl.BlockSpec((w, tq), lambda h, i: (h, i))],
                 [jax.ShapeDtypeStruct((s, n_heads * HEAD_DIM), F32), jax.ShapeDtypeStruct((n_heads * HEAD_DIM, s), BF16)],
                 [pltpu.VMEM((tq, w), F32), *[pltpu.VMEM((tq, 1), F32)] * NU],
                 ("parallel", "arbitrary"), (proj, proj, proj), comm)


def _attn_bwd(proj, do_sb, n_heads, comm=None):
    s = proj.shape[0]
    tq = _tile(s, TQ_PREF)
    scale = 1.0 / math.sqrt(HEAD_DIM)
    ratio = tq // TK
    n_kb = s // TK
    n_qb = s // tq
    nu = NU_BWD
    heads = [slice(u * HEAD_DIM, (u + 1) * HEAD_DIM) for u in range(nu)]
    nt_dims = (((1,), (1,)), ((), ()))

    def body(q_ref, k_ref, v_ref, do_ref, dq_ref, dk_ref, dv_ref, dka_ref, dva_ref, dl_ref, be_ref, dqa_ref, *c_refs):
        i = pl.program_id(1)

        @pl.when(i == 0)
        def _():
            dka_ref[...] = jnp.zeros_like(dka_ref)
            dva_ref[...] = jnp.zeros_like(dva_ref)

        qs = [q_ref[:, hs].astype(BF16) for hs in heads]
        dobs = [do_ref[:, hs].astype(BF16) for hs in heads]
        tri_after = _tri(True)
        tri_before = _tri(False)
        nkb = (i + 1) * ratio

        for cr in c_refs:
            cr[...] = jnp.zeros_like(cr)

        def sweep_a(jj, carry):
            j = nkb - 1 - jj
            rows = pl.ds(pl.multiple_of(j * TK, TK), TK)
            mask = _causal_mask(i, j, tq)
            zs = [lax.dot_general(qs[u], k_ref[rows, hs].astype(BF16), nt_dims, preferred_element_type=F32) * scale
                  for u, hs in enumerate(heads)]
            das = [lax.dot_general(dobs[u], v_ref[rows, hs].astype(BF16), nt_dims, preferred_element_type=F32)
                   for u, hs in enumerate(heads)]
            lls = [_log_terms(z, mask) for z in zs]
            sufs = [c_refs[u][...] + _split_dot(lls[u][1], tri_after) for u in range(nu)]
            for u, hs in enumerate(heads):
                a = jnp.where(mask, jnp.exp(lls[u][0] + sufs[u]), 0.0)
                dl_ref[u, j] = das[u] * a
                be_ref[u, j] = jnp.exp(lls[u][0])
                dva_ref[rows, hs] += jnp.dot(a.T.astype(BF16), dobs[u], preferred_element_type=F32)
                c_refs[u][...] += jnp.sum(lls[u][1], axis=1, keepdims=True)
            return carry

        lax.fori_loop(0, nkb, sweep_a, 0)

        for cr in c_refs:
            cr[...] = jnp.zeros_like(cr)
        dqa_ref[...] = jnp.zeros_like(dqa_ref)

        def sweep_b(j, carry):
            rows = pl.ds(pl.multiple_of(j * TK, TK), TK)
            mask = _causal_mask(i, j, tq)
            dls = [dl_ref[u, j] for u in range(nu)]
            ps = [c_refs[u][...] + _split_dot(dls[u], tri_before) for u in range(nu)]
            for u, hs in enumerate(heads):
                beta = be_ref[u, j]
                dz = jnp.where(mask, (dls[u] * (1.0 - beta) - beta * ps[u]) * scale, 0.0)
                dqa_ref[:, hs] += jnp.dot(dz.astype(BF16), k_ref[rows, hs].astype(BF16), preferred_element_type=F32)
                dka_ref[rows, hs] += jnp.dot(dz.T.astype(BF16), qs[u], preferred_element_type=F32)
                c_refs[u][...] += jnp.sum(dls[u], axis=1, keepdims=True)
            return carry

        lax.fori_loop(0, nkb, sweep_b, 0)
        dq_ref[...] = dqa_ref[...].astype(BF16)

        @pl.when(i == n_qb - 1)
        def _():
            dk_ref[...] = dka_ref[...].astype(BF16)
            dv_ref[...] = dva_ref[...].astype(BF16)

    w = nu * HEAD_DIM
    ng = n_heads // nu
    qblk = pl.BlockSpec((tq, w), lambda h, i: (i, h))
    full = pl.BlockSpec((s, w), lambda h, i: (0, h))
    sds = jax.ShapeDtypeStruct((s, n_heads * HEAD_DIM), BF16)
    return _call("attn_bwd", body, (ng, n_qb),
                 [qblk, pl.BlockSpec((s, w), lambda h, i: (0, ng + h)), pl.BlockSpec((s, w), lambda h, i: (0, 2 * ng + h)),
                  qblk],
                 [qblk, full, full], [sds, sds, sds],
                 [pltpu.VMEM((s, w), F32), pltpu.VMEM((s, w), F32),
                  pltpu.VMEM((nu, n_kb, tq, TK), F32), pltpu.VMEM((nu, n_kb, tq, TK), F32),
                  pltpu.VMEM((tq, w), F32), *[pltpu.VMEM((tq, 1), F32)] * nu],
                 ("parallel", "arbitrary"), (proj, proj, proj, do_sb), comm)


def _adamw(name, w, m, v, parts, part_specs, tr, comm=None):
    r, c = w.shape
    n_parts = len(parts)

    def body(*refs):
        w_ref, m_ref, v_ref = refs[:3]
        p_refs = refs[3:3 + n_parts]
        g_ref, d_ref, nm_ref, nv_ref = refs[3 + n_parts:]
        g = p_refs[0][...].astype(F32)
        for p in p_refs[1:]:
            g = g + p[...].astype(F32)
        nm = ADAM_B1 * m_ref[...] + (1.0 - ADAM_B1) * g
        nv = ADAM_B2 * v_ref[...] + (1.0 - ADAM_B2) * jnp.square(g)
        m_hat = nm / (1.0 - ADAM_B1 ** ADAM_STEP)
        v_hat = nv / (1.0 - ADAM_B2 ** ADAM_STEP)
        g_ref[...] = g
        d_ref[...] = -ADAM_LR * (m_hat / (jnp.sqrt(v_hat) + ADAM_EPS) + ADAM_WD * w_ref[...])
        nm_ref[...] = nm
        nv_ref[...] = nv

    blk = pl.BlockSpec((tr, c), lambda i: (i, 0))
    sds = jax.ShapeDtypeStruct((r, c), F32)
    return _call(name, body, (r // tr,), [blk, blk, blk, *part_specs], [blk] * 4, [sds] * 4, [], ("parallel",),
                 (w, m, v, *parts), comm)


def _adamw_big(name, w, m, v, recv, comm=None):
    r, c = w.shape
    tr = _tile(r, 128)
    order = (3, 0, 1, 2)
    specs = [pl.BlockSpec((None, tr, c), functools.partial(lambda i, slot: (slot, i, 0), slot=sl)) for sl in order]
    return _adamw(name, w, m, v, [recv] * 4, specs, tr, comm)


def _adamw_small(name, w, m, v, g):
    r, c = w.shape
    return _adamw(name, w, m, v, [g], [pl.BlockSpec((r, c), lambda i: (0, 0))], r)


def kernel(x, g_pre_mix, w_in, b_in, w_dw, b_dw, g_conv_ln, b_conv_ln, w_sb_out, w_conv_out, w_o, g_post_mix, g_pre_mlp, w_up, w_down, g_post_mlp, loss_target, m_g_pre_mix, m_w_in, m_b_in, m_w_dw, m_b_dw, m_g_conv_ln, m_b_conv_ln, m_w_sb_out, m_w_conv_out, m_w_o, m_g_post_mix, m_g_pre_mlp, m_w_up, m_w_down, m_g_post_mlp, v_g_pre_mix, v_w_in, v_b_in, v_w_dw, v_b_dw, v_g_conv_ln, v_b_conv_ln, v_w_sb_out, v_w_conv_out, v_w_o, v_g_post_mix, v_g_pre_mlp, v_w_up, v_w_down, v_g_post_mlp):
    xs, tgt = x[0], loss_target[0]
    s, d = xs.shape
    d_half = d // 2
    n_heads = d_half // HEAD_DIM
    d_ff = NDEV * w_up.shape[2]
    core = lax.axis_index("c").astype(jnp.int32).reshape(1)
    dev = 4 * lax.axis_index("x") + 2 * lax.axis_index("y") + lax.axis_index("c")

    w_dw_pad = jnp.pad(w_dw[0], ((0, CONV_PAD - CONV_WIDTH), (0, 0)))
    sh_in, sh_sb, sh_cv, sh_o, sh_up, sh_down = [w[0].astype(BF16) for w in (w_in, w_sb_out, w_conv_out, w_o, w_up, w_down)]
    px, py, pc = _position()
    order = jnp.stack([4 * (px ^ fx) + 2 * (py ^ fy) + (pc ^ fc) for fx, fy, fc in GATHER_FLIPS]).astype(jnp.int32)

    ag_up = _Chunked(_ag_comm, [sh_up], sh_up.shape[0], 8)
    h, h_t = _prenorm(xs, g_pre_mix)
    proj, wg_in = _proj_gather(h, sh_in, b_in, order)
    o_sb, o_sb_t, wg_sb, wg_cv, wg_o, wg_dw, *part = _attn_fwd(
        proj, n_heads, _join(_ag_comm([sh_sb, sh_cv, sh_o, w_dw_pad]), ag_up.take(2)))
    ag_up.done(part)
    wf_dw = wg_dw.transpose(1, 0, 2).reshape(CONV_PAD, d_half)
    wf_o = wg_o.reshape(d, d)
    u1, *part = _conv_fwd(proj, wf_dw, b_dw, d_half, ag_up.take())
    ag_up.done(part)
    u3, u3_t = _ln_silu(u1, g_conv_ln, b_conv_ln)
    o_sbp, *part = _mm_cols("sb_out", o_sb, wg_sb, comm=ag_up.take())
    ag_up.done(part)
    o_cv, *part = _mm_cols("conv_out", u3, wg_cv, comm=ag_up.take())
    ag_up.done(part)
    merged, merged_t, *part = _merge(proj, o_sbp, o_cv, d, ag_up.take())
    ag_up.done(part)
    y, *part = _mm_plain("w_o", merged, wf_o, False, F32, comm=ag_up.take())
    ag_up.done(part)
    x1, h2, h2_t, wg_up = _postnorm_mix(xs, y, g_post_mix, g_pre_mlp, ag_up.take())

    tm_up = _tile(s, 1024)
    ns_up = wg_up.shape[2]
    tk_up = _tile(d, 2048)

    def up_epilogue(acc):
        f = jnp.square(jnp.maximum(acc, 0.0))
        return acc, f, f.T

    a_act, f, f_t, wg_down = _matmul(
        "w_up", h2, wg_up,
        [jax.ShapeDtypeStruct((s, d_ff), BF16), jax.ShapeDtypeStruct((s, d_ff), BF16), jax.ShapeDtypeStruct((d_ff, s), BF16)],
        (s // tm_up, NDEV, d // tk_up),
        pl.BlockSpec((tm_up, tk_up), lambda i, dd, kk: (i, kk)),
        pl.BlockSpec((None, tk_up, ns_up), lambda i, dd, kk: (dd, kk, 0)),
        [pl.BlockSpec((tm_up, ns_up), lambda i, dd, kk: (i, dd)), pl.BlockSpec((tm_up, ns_up), lambda i, dd, kk: (i, dd)),
         pl.BlockSpec((ns_up, tm_up), lambda i, dd, kk: (dd, i))],
        1, False, (tm_up, ns_up), epilogue=up_epilogue, comm=_ag_comm([sh_down]))
    wf_down = wg_down.reshape(d_ff, d)
    f2 = _mm_plain("w_down", f, wf_down, False, F32)[0]
    dx2, df2, dg_post_mlp, loss_part = _loss_head(x1, f2, tgt, g_post_mlp)

    tm_b, tn_b = _tile(s, 1024), _tile(d_ff, 1024)
    da = _mm_plain("w_down_bwd", df2, wf_down, True, BF16,
                   extra=(a_act,), extra_specs=(pl.BlockSpec((tm_b, tn_b), lambda i, j, kk: (i, j)),),
                   epilogue=lambda acc, av: (acc * (2.0 * jnp.maximum(av.astype(F32), 0.0)),),
                   outs=[jax.ShapeDtypeStruct((s, d_ff), BF16)],
                   out_specs=[pl.BlockSpec((tm_b, tn_b), lambda i, j, kk: (i, j))])[0]
    gw_down = _mm_plain("w_down_grad", f_t, df2, False, BF16)[0]
    big_down = gw_down.reshape(4, 2, d_ff // NDEV, d)
    gw_up, sib_down = _mm_dw_cols("w_up_grad", h2_t, da, comm=_sibling_comm([big_down]))
    big_up = gw_up.reshape(4, 2, d, d_ff // NDEV)
    dh2, sib_up = _mm_cols_t("w_up_bwd", da, wg_up, comm=_sibling_comm([big_up]))
    rs_down = _Chunked(_chips_comm, [_pair_sum(big_down, sib_down, core)], d_ff // NDEV, 8)
    rs_up = _Chunked(_chips_comm, [_pair_sum(big_up, sib_up, core)], d, 8)
    dx1, dy, dg_pre_mlp, dg_post_mix, *part = _midnorm_bwd(dx2, dh2, x1, y, g_post_mix, g_pre_mlp, rs_down.take())
    rs_down.done(part)
    gw_o, *part = _mm_plain("w_o_grad", merged_t, dy, False, BF16, comm=rs_down.take())
    rs_down.done(part)
    dmerged, *part = _mm_plain("w_o_bwd", dy, wf_o, True, F32, comm=rs_down.take())
    rs_down.done(part)
    do_sbp, do_cv, dgate_sb, dgate_cv, *part = _merge_bwd(dmerged, proj, o_sbp, o_cv, d, rs_down.take())
    rs_down.done(part)
    gw_cv = _mm_dw_cols("conv_out_grad", u3_t, do_cv)[0]
    gw_sb = _mm_dw_cols("sb_out_grad", o_sb_t, do_sbp)[0]
    du3 = _mm_cols_t("conv_out_bwd", do_cv, wg_cv)[0]
    do_sb = _mm_cols_t("sb_out_bwd", do_sbp, wg_sb)[0]
    du1, dg_ln, db_ln = _ln_silu_bwd(du3, u1, g_conv_ln, b_conv_ln)
    dglu_a, dglu_b, dw_dw, db_dw, *part = _conv_bwd(du1, proj, wf_dw, d_half, rs_up.take(2))
    rs_up.done(part)
    big_mid = [gw_sb.reshape(4, 2, d_half, d // NDEV), gw_cv.reshape(4, 2, d_half, d // NDEV),
               gw_o.reshape(4, 2, d // NDEV, d)]
    sib_mid = _run_comm("exchange_sibling_mid", _sibling_comm(big_mid))
    sums_mid = [_pair_sum(g, r, core) for g, r in zip(big_mid, sib_mid)]
    dq, dk, dv, r_up, r_sb, r_cv, r_o = _attn_bwd(proj, do_sb, n_heads, _join(rs_up.take(6), _chips_comm(sums_mid)))
    dproj = jnp.concatenate([dq, dk, dv, dglu_a, dglu_b, dgate_sb, dgate_cv], axis=1)
    db_in = _colsum(dproj)
    gw_in, r_down = _mm_dw_cols("w_in_grad", h_t, dproj, comm=rs_down.take(4))
    big_in = gw_in.reshape(4, 2, d, gw_in.shape[2])
    sib_in, = _run_comm("exchange_sibling_in", _sibling_comm([big_in]))
    rs_in = _Chunked(_chips_comm, [_pair_sum(big_in, sib_in, core)], d, 8)
    dh, *part = _mm_cols_t("w_in_bwd", dproj, wg_in, comm=rs_in.take(5))
    rs_in.done(part)
    grad_x, dg_pre_mix, *part = _prenorm_bwd(dx1, dh, xs, g_pre_mix, rs_in.take())
    rs_in.done(part)

    small = [dg_pre_mix, db_in, dw_dw.reshape(1, -1), db_dw, dg_ln, db_ln, dg_post_mix, dg_pre_mlp, dg_post_mlp]
    sizes = [a.shape[1] for a in small]
    packed = jnp.concatenate(small, axis=1).reshape(-1, LANES)
    total = _sum_small(_all_gather_small(packed)).reshape(1, -1)
    offs = [0]
    for n in sizes:
        offs.append(offs[-1] + n)
    (g_g_pre_mix, g_b_in, g_w_dw_flat, g_b_dw, g_g_conv_ln, g_b_conv_ln, g_g_post_mix, g_g_pre_mlp,
     g_g_post_mlp) = [total[:, offs[k]:offs[k + 1]] for k in range(len(sizes))]
    ch = w_dw.shape[2]
    g_w_dw = lax.dynamic_slice_in_dim(g_w_dw_flat.reshape(CONV_PAD, d_half), dev * ch, ch, axis=1)[:CONV_WIDTH]

    loss = lax.psum(loss_part[0, 0], ("x", "y", "c"))

    res = {}
    *res["w_up"], part = _adamw_big("adamw_w_up", w_up[0], m_w_up[0], v_w_up[0], r_up, rs_in.take())
    rs_in.done([part])
    *res["w_down"], r_in = _adamw_big("adamw_w_down", w_down[0], m_w_down[0], v_w_down[0], r_down, rs_in.take())
    res["g_pre_mix"] = _adamw_small("adamw_g_pre_mix", g_pre_mix, m_g_pre_mix, v_g_pre_mix, g_g_pre_mix)
    res["w_in"] = _adamw_big("adamw_w_in", w_in[0], m_w_in[0], v_w_in[0], r_in)
    res["b_in"] = _adamw_small("adamw_b_in", b_in, m_b_in, v_b_in, g_b_in)
    res["w_dw"] = _adamw_small("adamw_w_dw", w_dw[0], m_w_dw[0], v_w_dw[0], g_w_dw)
    res["b_dw"] = _adamw_small("adamw_b_dw", b_dw, m_b_dw, v_b_dw, g_b_dw)
    res["g_conv_ln"] = _adamw_small("adamw_g_conv_ln", g_conv_ln, m_g_conv_ln, v_g_conv_ln, g_g_conv_ln)
    res["b_conv_ln"] = _adamw_small("adamw_b_conv_ln", b_conv_ln, m_b_conv_ln, v_b_conv_ln, g_b_conv_ln)
    res["w_sb_out"] = _adamw_big("adamw_w_sb_out", w_sb_out[0], m_w_sb_out[0], v_w_sb_out[0], r_sb)
    res["w_conv_out"] = _adamw_big("adamw_w_conv_out", w_conv_out[0], m_w_conv_out[0], v_w_conv_out[0], r_cv)
    res["w_o"] = _adamw_big("adamw_w_o", w_o[0], m_w_o[0], v_w_o[0], r_o)
    res["g_post_mix"] = _adamw_small("adamw_g_post_mix", g_post_mix, m_g_post_mix, v_g_post_mix, g_g_post_mix)
    res["g_pre_mlp"] = _adamw_small("adamw_g_pre_mlp", g_pre_mlp, m_g_pre_mlp, v_g_pre_mlp, g_g_pre_mlp)
    res["g_post_mlp"] = _adamw_small("adamw_g_post_mlp", g_post_mlp, m_g_post_mlp, v_g_post_mlp, g_g_post_mlp)

    names = ["g_pre_mix", "w_in", "b_in", "w_dw", "b_dw", "g_conv_ln", "b_conv_ln", "w_sb_out", "w_conv_out", "w_o",
             "g_post_mix", "g_pre_mlp", "w_up", "w_down", "g_post_mlp"]
    three_d = {"w_in", "w_dw", "w_sb_out", "w_conv_out", "w_o", "w_up", "w_down"}

    def shaped(nm, arr):
        return arr[None] if nm in three_d else arr

    out = [loss, grad_x[None]]
    for k in range(4):
        out += [shaped(nm, res[nm][k]) for nm in names]
    return tuple(out)
```

```python
import functools
import math

import jax
import jax.numpy as jnp
from jax import lax
from jax.experimental import pallas as pl
from jax.experimental.pallas import tpu as pltpu

F32 = jnp.float32
BF16 = jnp.bfloat16
NDEV = 8
LANES = 128
EPS = 1e-6
CONV_WIDTH = 31
CONV_PAD = 32
HEAD_DIM = 128
ADAM_LR = 0.001
ADAM_B1 = 0.9
ADAM_B2 = 0.999
ADAM_EPS = 1e-08
ADAM_WD = 0.01
ADAM_STEP = 10
VMEM_LIMIT = 56 * 1024 * 1024
MESH = pl.DeviceIdType.MESH
ANY = pl.BlockSpec(memory_space=pl.ANY)


def _tile(n, pref):
    t = min(n, pref)
    assert n % t == 0, (n, t)
    return t


def _sigmoid(v):
    return 1.0 / (1.0 + jnp.exp(-v))


def _position():
    return lax.axis_index("x"), lax.axis_index("y"), lax.axis_index("c")


class _Comm:
    def __init__(self, ins, outs, scratch, start, finish, aliases=None):
        self.ins, self.outs, self.scratch = list(ins), list(outs), list(scratch)
        self.start, self.finish, self.aliases = start, finish, dict(aliases or {})


_NO_COMM = _Comm([], [], [], None, None)


def _call(name, body, grid, in_specs, out_specs, out_shape, scratch_shapes, sem, args, comm=None):
    comm = comm or _NO_COMM
    n_in, n_out, n_scr = len(in_specs), len(out_specs), len(scratch_shapes)
    n_cin, n_cout = len(comm.ins), len(comm.outs)

    def edge(c_ins, c_outs, c_scr, at_start):
        pids = [pl.program_id(ax) for ax in range(len(grid))]
        conds = [p == (0 if at_start else g - 1) for p, g in zip(pids, grid)]

        @pl.when(functools.reduce(jnp.logical_and, conds))
        def _():
            (comm.start if at_start else comm.finish)(c_ins, c_outs, c_scr)

    def wrapped(*refs):
        ins, c_ins = refs[:n_in], refs[n_in:n_in + n_cin]
        pos = n_in + n_cin
        outs, c_outs = refs[pos:pos + n_out], refs[pos + n_out:pos + n_out + n_cout]
        pos += n_out + n_cout
        scr, c_scr = refs[pos:pos + n_scr], refs[pos + n_scr:]
        if n_cin:
            edge(c_ins, c_outs, c_scr, True)
        body(*ins, *outs, *scr)
        if n_cin:
            edge(c_ins, c_outs, c_scr, False)

    if n_cin:
        sem = ("arbitrary",) * len(grid)
    return pl.pallas_call(
        wrapped, name=name, grid=grid,
        in_specs=[*in_specs, *[ANY] * n_cin], out_specs=[*out_specs, *[ANY] * n_cout],
        out_shape=[*out_shape, *comm.outs], scratch_shapes=[*scratch_shapes, *comm.scratch],
        input_output_aliases={n_in + ci: n_out + co for ci, co in comm.aliases.items()},
        compiler_params=pltpu.CompilerParams(dimension_semantics=sem, vmem_limit_bytes=VMEM_LIMIT),
    )(*args, *comm.ins)


def _run_comm(name, comm):
    n_in, n_out = len(comm.ins), len(comm.outs)

    def body(*refs):
        ins, outs, scr = refs[:n_in], refs[n_in:n_in + n_out], refs[n_in + n_out:]
        comm.start(ins, outs, scr)
        comm.finish(ins, outs, scr)

    return pl.pallas_call(body, name=name, in_specs=[ANY] * n_in, out_specs=[ANY] * n_out, out_shape=comm.outs,
                          scratch_shapes=comm.scratch,
                          input_output_aliases=comm.aliases)(*comm.ins)


def _rows_of(ref, rows):
    return ref if rows is None else ref.at[pl.ds(rows[0], rows[1])]


def _ag_comm(shards, rows=None, into=None):
    n = len(shards)

    def parts(ins, outs, scr):
        send_sems, recv_sems, local_sems = scr
        x, y, c = _position()
        chips = [(1 - x, y), (x, 1 - y), (1 - x, 1 - y)]

        def copy(a, k, block, to, own=False):
            px, py, pc = block
            dst = _rows_of(outs[a].at[4 * px + 2 * py + pc], rows)
            return pltpu.make_async_remote_copy(src_ref=_rows_of(ins[a], rows) if own else dst, dst_ref=dst,
                                                send_sem=send_sems.at[a, k], recv_sem=recv_sems.at[a, k],
                                                device_id=to, device_id_type=MESH)

        mine = [pltpu.make_async_copy(_rows_of(ins[a], rows), _rows_of(outs[a].at[4 * x + 2 * y + c], rows),
                                      local_sems.at[a]) for a in range(n)]
        first = []
        for a in range(n):
            first.append(copy(a, 0, (x, y, c), (x, y, 1 - c), own=True))
            first += [copy(a, 1 + j, (x, y, c), (*chip, c), own=True) for j, chip in enumerate(chips)]
        return copy, mine, first, chips, (x, y, c), (x, y, 1 - c)

    def start(ins, outs, scr):
        _, mine, first, _, _, _ = parts(ins, outs, scr)
        for cp in mine + first:
            cp.start()

    def finish(ins, outs, scr):
        copy, mine, first, chips, me, sibling = parts(ins, outs, scr)
        c = me[2]
        passed = []
        for a in range(n):
            for j, chip in enumerate(chips):
                copy(a, 1 + j, (*chip, c), me).wait_recv()
                cp = copy(a, 4 + j, (*chip, c), sibling)
                cp.start()
                passed.append(cp)
        for a in range(n):
            copy(a, 0, sibling, me).wait_recv()
            for j, chip in enumerate(chips):
                copy(a, 4 + j, (*chip, 1 - c), me).wait_recv()
        for cp in first + passed:
            cp.wait_send()
        for cp in mine:
            cp.wait()

    return _Comm([*shards, *(into or [])], [jax.ShapeDtypeStruct((NDEV, *sh.shape), sh.dtype) for sh in shards],
                 [pltpu.SemaphoreType.DMA((n, 7)), pltpu.SemaphoreType.DMA((n, 7)), pltpu.SemaphoreType.DMA((n,))],
                 start, finish, {n + a: a for a in range(n)} if into else None)


def _sibling_comm(grads):
    n = len(grads)

    def copies(ins, outs, scr):
        send_sems, recv_sems = scr
        x, y, c = _position()
        return [pltpu.make_async_remote_copy(src_ref=ins[a].at[k, 1 - c], dst_ref=outs[a].at[k],
                                             send_sem=send_sems.at[a, k], recv_sem=recv_sems.at[a, k],
                                             device_id=(x, y, 1 - c), device_id_type=MESH)
                for a in range(n) for k in range(4)]

    def start(ins, outs, scr):
        for cp in copies(ins, outs, scr):
            cp.start()

    def finish(ins, outs, scr):
        for cp in copies(ins, outs, scr):
            cp.wait()

    return _Comm(grads, [jax.ShapeDtypeStruct((4, *g.shape[2:]), g.dtype) for g in grads],
                 [pltpu.SemaphoreType.DMA((n, 4)), pltpu.SemaphoreType.DMA((n, 4))], start, finish)


def _chips_comm(sums, rows=None, into=None):
    n = len(sums)

    def copies(ins, outs, scr):
        send_sems, recv_sems, local_sems = scr
        x, y, c = _position()
        chips = [(1 - x, y), (x, 1 - y), (1 - x, 1 - y)]
        own = [pltpu.make_async_copy(_rows_of(ins[a].at[2 * x + y], rows), _rows_of(outs[a].at[3], rows), local_sems.at[a])
               for a in range(n)]
        remote = [pltpu.make_async_remote_copy(src_ref=_rows_of(ins[a].at[2 * px + py], rows),
                                               dst_ref=_rows_of(outs[a].at[j], rows),
                                               send_sem=send_sems.at[a, j], recv_sem=recv_sems.at[a, j],
                                               device_id=(px, py, c), device_id_type=MESH)
                  for a in range(n) for j, (px, py) in enumerate(chips)]
        return own + remote

    def start(ins, outs, scr):
        for cp in copies(ins, outs, scr):
            cp.start()

    def finish(ins, outs, scr):
        for cp in copies(ins, outs, scr):
            cp.wait()

    return _Comm([*sums, *(into or [])], [jax.ShapeDtypeStruct(sm.shape, sm.dtype) for sm in sums],
                 [pltpu.SemaphoreType.DMA((n, 3)), pltpu.SemaphoreType.DMA((n, 3)), pltpu.SemaphoreType.DMA((n,))],
                 start, finish, {n + a: a for a in range(n)} if into else None)


def _join(c1, c2):
    n_in, n_out, n_scr = len(c1.ins), len(c1.outs), len(c1.scratch)
    aliases = dict(c1.aliases)
    aliases.update({n_in + ci: n_out + co for ci, co in c2.aliases.items()})

    def start(ins, outs, scr):
        c1.start(ins[:n_in], outs[:n_out], scr[:n_scr])
        c2.start(ins[n_in:], outs[n_out:], scr[n_scr:])

    def finish(ins, outs, scr):
        c1.finish(ins[:n_in], outs[:n_out], scr[:n_scr])
        c2.finish(ins[n_in:], outs[n_out:], scr[n_scr:])

    return _Comm(c1.ins + c2.ins, c1.outs + c2.outs, c1.scratch + c2.scratch, start, finish, aliases)


GATHER_FLIPS = ((0, 0, 0), (0, 0, 1), (1, 0, 0), (0, 1, 0), (1, 0, 1), (0, 1, 1), (1, 1, 0), (1, 1, 1))
GATHER_RECV_SEM = {1: 0, 2: 1, 3: 2, 4: 4, 5: 5, 6: 3, 7: 6}
GATHER_PASS_SEM = {2: 4, 3: 5, 6: 6}


def _proj_gather(h, shard, bias, order):
    s_len, d = h.shape
    _, ns = shard.shape
    n_steps = len(GATHER_FLIPS)

    def body(order_ref, h_ref, sh_ref, b_ref, o_ref, wg_ref, w_vmem, load_sem, send_sems, recv_sems, local_sem):
        step = pl.program_id(0)
        x, y, c = _position()
        sibling = (x, y, 1 - c)

        def block(flip):
            return wg_ref.at[4 * (x ^ flip[0]) + 2 * (y ^ flip[1]) + (c ^ flip[2])]

        def copy(k, flip, to, own=False):
            dst = block(flip)
            return pltpu.make_async_remote_copy(src_ref=sh_ref if own else dst, dst_ref=dst, send_sem=send_sems.at[k],
                                                recv_sem=recv_sems.at[k], device_id=to, device_id_type=MESH)

        mine = pltpu.make_async_copy(sh_ref, block(GATHER_FLIPS[0]), local_sem)
        first = [copy(0, GATHER_FLIPS[0], sibling, own=True)]
        first += [copy(1 + j, GATHER_FLIPS[0], (x ^ fx, y ^ fy, c), own=True)
                  for j, (fx, fy) in enumerate(((1, 0), (0, 1), (1, 1)))]
        passed = [copy(k, GATHER_FLIPS[t], sibling) for t, k in GATHER_PASS_SEM.items()]

        for t, flip in enumerate(GATHER_FLIPS):
            @pl.when(step == t)
            def _(t=t, flip=flip):
                if t == 0:
                    for cp in [mine] + first:
                        cp.start()
                    src = sh_ref
                else:
                    copy(GATHER_RECV_SEM[t], flip, (x, y, c)).wait_recv()
                    if t in GATHER_PASS_SEM:
                        copy(GATHER_PASS_SEM[t], flip, sibling).start()
                    src = block(flip)
                load = pltpu.make_async_copy(src, w_vmem, load_sem)
                load.start()
                load.wait()

        o_ref[...] = jnp.dot(h_ref[...], w_vmem[...], preferred_element_type=F32) + b_ref[...]

        @pl.when(step == n_steps - 1)
        def _():
            for cp in first + passed:
                cp.wait_send()
            mine.wait()

    return pl.pallas_call(
        body, name="proj_gather",
        grid_spec=pltpu.PrefetchScalarGridSpec(
            num_scalar_prefetch=1, grid=(n_steps,),
            in_specs=[pl.BlockSpec((s_len, d), lambda t, order_ref: (0, 0)), ANY,
                      pl.BlockSpec((1, ns), lambda t, order_ref: (0, order_ref[t]))],
            out_specs=[pl.BlockSpec((s_len, ns), lambda t, order_ref: (0, order_ref[t])), ANY],
            scratch_shapes=[pltpu.VMEM((d, ns), BF16), pltpu.SemaphoreType.DMA, pltpu.SemaphoreType.DMA((7,)),
                            pltpu.SemaphoreType.DMA((7,)), pltpu.SemaphoreType.DMA]),
        out_shape=[jax.ShapeDtypeStruct((s_len, NDEV * ns), F32), jax.ShapeDtypeStruct((NDEV, d, ns), BF16)],
        compiler_params=pltpu.CompilerParams(dimension_semantics=("arbitrary",), vmem_limit_bytes=VMEM_LIMIT),
    )(order, h, shard, bias)


class _Chunked:
    def __init__(self, make, arrays, n_rows, n_chunks):
        self.make, self.arrays, self.into = make, arrays, None
        step = n_rows // n_chunks
        assert step * n_chunks == n_rows
        self.todo = [(k * step, step) for k in range(n_chunks)]

    def take(self, count=1):
        r0, nr = self.todo[0][0], sum(t[1] for t in self.todo[:count])
        self.todo = self.todo[count:]
        return self.make(self.arrays, (r0, nr), self.into)

    def done(self, outs):
        self.into = list(outs)
        return self.into


def _all_gather_small(part):
    def body(in_ref, out_ref, send_sems, recv_sems, local_sem):
        x, y, c = _position()
        me = 4 * x + 2 * y + c
        mine = pltpu.make_async_copy(in_ref, out_ref.at[me], local_sem)
        mine.start()
        flips = [(fx, fy, fc) for fx in (0, 1) for fy in (0, 1) for fc in (0, 1)][1:]
        copies = []
        for k, (fx, fy, fc) in enumerate(flips):
            cp = pltpu.make_async_remote_copy(src_ref=in_ref, dst_ref=out_ref.at[me], send_sem=send_sems.at[k],
                                              recv_sem=recv_sems.at[k],
                                              device_id=(x ^ fx, y ^ fy, c ^ fc), device_id_type=MESH)
            cp.start()
            copies.append(cp)
        for k, (fx, fy, fc) in enumerate(flips):
            peer = 4 * (x ^ fx) + 2 * (y ^ fy) + (c ^ fc)
            pltpu.make_async_remote_copy(src_ref=in_ref, dst_ref=out_ref.at[peer], send_sem=send_sems.at[k],
                                         recv_sem=recv_sems.at[k], device_id=(x, y, c), device_id_type=MESH).wait_recv()
        for cp in copies:
            cp.wait_send()
        mine.wait()

    return pl.pallas_call(
        body, name="all_gather_small", in_specs=[ANY], out_specs=ANY,
        out_shape=jax.ShapeDtypeStruct((NDEV, *part.shape), part.dtype),
        scratch_shapes=[pltpu.SemaphoreType.DMA((7,)), pltpu.SemaphoreType.DMA((7,)), pltpu.SemaphoreType.DMA],
    )(part)


def _pair_sum(g, recv, core):
    _, _, r, c = g.shape
    tr = _tile(r, 512)

    def body(core_ref, g_ref, r_ref, o_ref):
        o_ref[...] = (g_ref[...].astype(F32) + r_ref[...].astype(F32)).astype(o_ref.dtype)

    return pl.pallas_call(
        body, name="pair_sum",
        grid_spec=pltpu.PrefetchScalarGridSpec(
            num_scalar_prefetch=1, grid=(4, r // tr),
            in_specs=[pl.BlockSpec((None, None, tr, c), lambda k, i, core_ref: (k, core_ref[0], i, 0)),
                      pl.BlockSpec((None, tr, c), lambda k, i, core_ref: (k, i, 0))],
            out_specs=pl.BlockSpec((None, tr, c), lambda k, i, core_ref: (k, i, 0))),
        out_shape=jax.ShapeDtypeStruct((4, r, c), g.dtype),
        compiler_params=pltpu.CompilerParams(dimension_semantics=("parallel", "parallel"), vmem_limit_bytes=VMEM_LIMIT),
    )(core, g, recv)


def _sum_small(gathered):
    _, r, l = gathered.shape

    def body(g_ref, o_ref):
        acc = g_ref[0]
        for d in range(1, NDEV):
            acc = acc + g_ref[d]
        o_ref[...] = acc

    return pl.pallas_call(
        body, name="sum_small", in_specs=[pl.BlockSpec((NDEV, r, l), lambda: (0, 0, 0))],
        out_specs=pl.BlockSpec((r, l), lambda: (0, 0)), out_shape=jax.ShapeDtypeStruct((r, l), F32),
    )(gathered)


def _matmul(name, a, b, outs, grid, a_spec, b_spec, out_specs, n_red, nt, acc_shape,
            extra=(), extra_specs=(), epilogue=None, comm=None):
    n_extra, n_out = len(extra), len(outs)
    red_axes = tuple(range(len(grid) - n_red, len(grid)))
    red_sizes = tuple(grid[ax] for ax in red_axes)
    single = all(sz == 1 for sz in red_sizes)
    dims = (((1,), (1,)), ((), ())) if nt else (((1,), (0,)), ((), ()))

    def body(*refs):
        a_ref, b_ref = refs[0], refs[1]
        ex_refs = refs[2:2 + n_extra]
        o_refs = refs[2 + n_extra:2 + n_extra + n_out]
        acc_ref = refs[-1]

        def write(acc):
            vals = (acc,) if epilogue is None else epilogue(acc, *[r[...] for r in ex_refs])
            for o_ref, val in zip(o_refs, vals):
                o_ref[...] = val.astype(o_ref.dtype)

        if len(b_ref.shape) == 3:
            w = b_ref.shape[2]
            part = sum(lax.dot_general(a_ref[:, p * w:(p + 1) * w].astype(BF16), b_ref[p].astype(BF16), dims,
                                       preferred_element_type=F32) for p in range(b_ref.shape[0]))
        else:
            part = lax.dot_general(a_ref[...].astype(BF16), b_ref[...].astype(BF16), dims, preferred_element_type=F32)
        if single:
            write(part)
        else:
            ks = [pl.program_id(ax) for ax in red_axes]
            first = functools.reduce(jnp.logical_and, [k == 0 for k in ks])
            last = functools.reduce(jnp.logical_and, [k == sz - 1 for k, sz in zip(ks, red_sizes)])

            @pl.when(first)
            def _():
                acc_ref[...] = part

            @pl.when(jnp.logical_not(first))
            def _():
                acc_ref[...] += part

            @pl.when(last)
            def _():
                write(acc_ref[...])

    sem = ("parallel",) * (len(grid) - n_red) + ("arbitrary",) * n_red
    return _call(name, body, grid, [a_spec, b_spec, *extra_specs], list(out_specs), list(outs),
                 [pltpu.VMEM((8, LANES) if single else acc_shape, F32)], sem, (a, b, *extra), comm)


def _mm_cols(name, a, wg, bias=None, out_dtype=F32, comm=None):
    m, k = a.shape
    _, _, ns = wg.shape
    tm, tk = _tile(m, 1024), _tile(k, 2048)
    grid = (m // tm, NDEV, k // tk)
    extra, extra_specs, epi = (), (), None
    if bias is not None:
        extra, extra_specs = (bias,), (pl.BlockSpec((1, ns), lambda i, d, kk: (0, d)),)
        epi = lambda acc, bv: (acc + bv,)
    return _matmul(name, a, wg, [jax.ShapeDtypeStruct((m, NDEV * ns), out_dtype)], grid,
                   pl.BlockSpec((tm, tk), lambda i, d, kk: (i, kk)),
                   pl.BlockSpec((None, tk, ns), lambda i, d, kk: (d, kk, 0)),
                   [pl.BlockSpec((tm, ns), lambda i, d, kk: (i, d))], 1, False, (tm, ns),
                   extra, extra_specs, epi, comm)


def _mm_cols_t(name, a, wg, comm=None):
    m, _ = a.shape
    _, n, ns = wg.shape
    tm, tn = _tile(m, 1024), _tile(n, 1024)
    per = 2
    grid = (m // tm, n // tn, NDEV // per)
    return _matmul(name, a, wg, [jax.ShapeDtypeStruct((m, n), F32)], grid,
                   pl.BlockSpec((tm, per * ns), lambda i, j, d: (i, d)),
                   pl.BlockSpec((per, tn, ns), lambda i, j, d: (d, j, 0)),
                   [pl.BlockSpec((tm, tn), lambda i, j, d: (i, j))], 1, True, (tm, tn), comm=comm)


def _mm_dw_cols(name, at, g, comm=None):
    m, t = at.shape
    ns = g.shape[1] // NDEV
    tm, tk = _tile(m, 1024), _tile(t, 2048)
    grid = (m // tm, NDEV, t // tk)
    return _matmul(name, at, g, [jax.ShapeDtypeStruct((NDEV, m, ns), BF16)], grid,
                   pl.BlockSpec((tm, tk), lambda i, d, kk: (i, kk)),
                   pl.BlockSpec((tk, ns), lambda i, d, kk: (kk, d)),
                   [pl.BlockSpec((None, tm, ns), lambda i, d, kk: (d, i, 0))], 1, False, (tm, ns), comm=comm)


def _mm_plain(name, a, b, nt, out_dtype, extra=(), extra_specs=(), epilogue=None, outs=None, out_specs=None, comm=None):
    m, k = a.shape
    n = b.shape[0] if nt else b.shape[1]
    tm, tn, tk = _tile(m, 1024), _tile(n, 1024), _tile(k, 2048)
    grid = (m // tm, n // tn, k // tk)
    b_spec = (pl.BlockSpec((tn, tk), lambda i, j, kk: (j, kk)) if nt
              else pl.BlockSpec((tk, tn), lambda i, j, kk: (kk, j)))
    if outs is None:
        outs = [jax.ShapeDtypeStruct((m, n), out_dtype)]
        out_specs = [pl.BlockSpec((tm, tn), lambda i, j, kk: (i, j))]
    return _matmul(name, a, b, outs, grid, pl.BlockSpec((tm, tk), lambda i, j, kk: (i, kk)), b_spec,
                   out_specs, 1, nt, (tm, tn), extra, extra_specs, epilogue, comm)


def _rms_rows(v):
    return lax.rsqrt(jnp.mean(v * v, axis=-1, keepdims=True) + EPS)


def _prenorm(x, g):
    s, d = x.shape
    ts = _tile(s, 256)

    def body(x_ref, g_ref, h_ref, ht_ref):
        xv = x_ref[...]
        h = xv * _rms_rows(xv) * g_ref[...]
        h_ref[...] = h.astype(BF16)
        ht_ref[...] = h.T.astype(BF16)

    return _call("prenorm", body, (s // ts,),
                 [pl.BlockSpec((ts, d), lambda i: (i, 0)), pl.BlockSpec((1, d), lambda i: (0, 0))],
                 [pl.BlockSpec((ts, d), lambda i: (i, 0)), pl.BlockSpec((d, ts), lambda i: (0, i))],
                 [jax.ShapeDtypeStruct((s, d), BF16), jax.ShapeDtypeStruct((d, s), BF16)], [], ("parallel",), (x, g))


def _ln_silu(u1, g, b):
    s, c = u1.shape
    ts = _tile(s, 256)

    def body(u_ref, g_ref, b_ref, o_ref, ot_ref):
        u = u_ref[...]
        mu = jnp.mean(u, axis=-1, keepdims=True)
        var = jnp.mean(jnp.square(u - mu), axis=-1, keepdims=True)
        u2 = (u - mu) * lax.rsqrt(var + EPS) * g_ref[...] + b_ref[...]
        u3 = u2 * _sigmoid(u2)
        o_ref[...] = u3.astype(BF16)
        ot_ref[...] = u3.T.astype(BF16)

    vec = pl.BlockSpec((1, c), lambda i: (0, 0))
    return _call("ln_silu", body, (s // ts,), [pl.BlockSpec((ts, c), lambda i: (i, 0)), vec, vec],
                 [pl.BlockSpec((ts, c), lambda i: (i, 0)), pl.BlockSpec((c, ts), lambda i: (0, i))],
                 [jax.ShapeDtypeStruct((s, c), BF16), jax.ShapeDtypeStruct((c, s), BF16)], [], ("parallel",), (u1, g, b))


def _ln_silu_bwd(du3, u1, g, b, comm=None):
    s, c = u1.shape
    ts = _tile(s, 256)

    def body(d_ref, u_ref, g_ref, b_ref, du1_ref, dg_ref, db_ref):
        @pl.when(pl.program_id(0) == 0)
        def _():
            dg_ref[...] = jnp.zeros_like(dg_ref)
            db_ref[...] = jnp.zeros_like(db_ref)

        u = u_ref[...]
        mu = jnp.mean(u, axis=-1, keepdims=True)
        var = jnp.mean(jnp.square(u - mu), axis=-1, keepdims=True)
        rstd = lax.rsqrt(var + EPS)
        uhat = (u - mu) * rstd
        u2 = uhat * g_ref[...] + b_ref[...]
        sg = _sigmoid(u2)
        du2 = d_ref[...] * (sg * (1.0 + u2 * (1.0 - sg)))
        dg_ref[...] += jnp.sum(du2 * uhat, axis=0, keepdims=True)
        db_ref[...] += jnp.sum(du2, axis=0, keepdims=True)
        duh = du2 * g_ref[...]
        du1_ref[...] = rstd * (duh - jnp.mean(duh, axis=-1, keepdims=True)
                               - uhat * jnp.mean(duh * uhat, axis=-1, keepdims=True))

    row = pl.BlockSpec((ts, c), lambda i: (i, 0))
    vec = pl.BlockSpec((1, c), lambda i: (0, 0))
    return _call("ln_silu_bwd", body, (s // ts,), [row, row, vec, vec], [row, vec, vec],
                 [jax.ShapeDtypeStruct((s, c), F32), jax.ShapeDtypeStruct((1, c), F32), jax.ShapeDtypeStruct((1, c), F32)],
                 [], ("arbitrary",), (du3, u1, g, b), comm)


def _merge(proj, o_sbp, o_cv, d, comm=None):
    s = proj.shape[0]
    w = d // 2
    ts = _tile(s, 256)

    def body(gs_ref, gc_ref, a_ref, b_ref, m_ref, mt_ref):
        mg = _sigmoid(gs_ref[...]) * a_ref[...] + _sigmoid(gc_ref[...]) * b_ref[...]
        m_ref[...] = mg.astype(BF16)
        mt_ref[...] = mg.T.astype(BF16)

    blk = pl.BlockSpec((ts, w), lambda i, j: (i, j))
    return _call("merge", body, (s // ts, 2),
                 [pl.BlockSpec((ts, w), lambda i, j: (i, 5 + j)), pl.BlockSpec((ts, w), lambda i, j: (i, 7 + j)), blk, blk],
                 [blk, pl.BlockSpec((w, ts), lambda i, j: (j, i))],
                 [jax.ShapeDtypeStruct((s, d), BF16), jax.ShapeDtypeStruct((d, s), BF16)], [],
                 ("parallel", "parallel"), (proj, proj, o_sbp, o_cv), comm)


def _merge_bwd(dmerged, proj, o_sbp, o_cv, d, comm=None):
    s = proj.shape[0]
    w = d // 2
    ts = _tile(s, 256)

    def body(dm_ref, gs_ref, gc_ref, a_ref, b_ref, da_ref, db_ref, dgs_ref, dgc_ref):
        dm = dm_ref[...]
        ss = _sigmoid(gs_ref[...])
        sc = _sigmoid(gc_ref[...])
        da_ref[...] = (dm * ss).astype(BF16)
        db_ref[...] = (dm * sc).astype(BF16)
        dgs_ref[...] = (dm * a_ref[...] * ss * (1.0 - ss)).astype(BF16)
        dgc_ref[...] = (dm * b_ref[...] * sc * (1.0 - sc)).astype(BF16)

    blk = pl.BlockSpec((ts, w), lambda i, j: (i, j))
    sds = jax.ShapeDtypeStruct((s, d), BF16)
    return _call("merge_bwd", body, (s // ts, 2),
                 [blk, pl.BlockSpec((ts, w), lambda i, j: (i, 5 + j)), pl.BlockSpec((ts, w), lambda i, j: (i, 7 + j)),
                  blk, blk],
                 [blk, blk, blk, blk], [sds, sds, sds, sds], [], ("parallel", "parallel"),
                 (dmerged, proj, proj, o_sbp, o_cv), comm)


def _postnorm_mix(x, y, g_post, g_pre, comm=None):
    s, d = x.shape
    ts = _tile(s, 256)

    def body(x_ref, y_ref, gp_ref, gn_ref, x1_ref, h_ref, ht_ref):
        yv = y_ref[...]
        x1 = x_ref[...] + yv * _rms_rows(yv) * gp_ref[...]
        x1_ref[...] = x1
        h = x1 * _rms_rows(x1) * gn_ref[...]
        h_ref[...] = h.astype(BF16)
        ht_ref[...] = h.T.astype(BF16)

    row = pl.BlockSpec((ts, d), lambda i: (i, 0))
    vec = pl.BlockSpec((1, d), lambda i: (0, 0))
    return _call("postnorm_mix", body, (s // ts,), [row, row, vec, vec],
                 [row, row, pl.BlockSpec((d, ts), lambda i: (0, i))],
                 [jax.ShapeDtypeStruct((s, d), F32), jax.ShapeDtypeStruct((s, d), BF16), jax.ShapeDtypeStruct((d, s), BF16)],
                 [], ("parallel",), (x, y, g_post, g_pre), comm)


def _rms_bwd(dout, vin, g):
    r = _rms_rows(vin)
    vhat = vin * r
    dyh = dout * g
    dvin = r * (dyh - vhat * jnp.mean(dyh * vhat, axis=-1, keepdims=True))
    return dvin, jnp.sum(dout * vhat, axis=0, keepdims=True)


def _loss_head(x1, f2, tgt, g):
    s, d = x1.shape
    ts = _tile(s, 256)

    def body(x1_ref, f_ref, t_ref, g_ref, dx2_ref, df2_ref, dg_ref, loss_ref):
        @pl.when(pl.program_id(0) == 0)
        def _():
            dg_ref[...] = jnp.zeros_like(dg_ref)
            loss_ref[...] = jnp.zeros_like(loss_ref)

        fv = f_ref[...]
        x2 = x1_ref[...] + fv * _rms_rows(fv) * g_ref[...]
        err = x2 - t_ref[...]
        loss_ref[...] += 0.5 * jnp.sum(jnp.mean(err * err, axis=-1, keepdims=True), axis=0, keepdims=True)
        dx2 = err * (1.0 / d)
        dx2_ref[...] = dx2
        df2, dg = _rms_bwd(dx2, fv, g_ref[...])
        df2_ref[...] = df2.astype(BF16)
        dg_ref[...] += dg

    row = pl.BlockSpec((ts, d), lambda i: (i, 0))
    vec = pl.BlockSpec((1, d), lambda i: (0, 0))
    return _call("loss_head", body, (s // ts,), [row, row, row, vec],
                 [row, row, vec, pl.BlockSpec((1, LANES), lambda i: (0, 0))],
                 [jax.ShapeDtypeStruct((s, d), F32), jax.ShapeDtypeStruct((s, d), BF16),
                  jax.ShapeDtypeStruct((1, d), F32), jax.ShapeDtypeStruct((1, LANES), F32)],
                 [], ("arbitrary",), (x1, f2, tgt, g))


def _midnorm_bwd(dx2, dh2, x1, y, g_post, g_pre, comm=None):
    s, d = x1.shape
    ts = _tile(s, 256)

    def body(dx2_ref, dh_ref, x1_ref, y_ref, gp_ref, gn_ref, dx1_ref, dy_ref, dgn_ref, dgp_ref):
        @pl.when(pl.program_id(0) == 0)
        def _():
            dgn_ref[...] = jnp.zeros_like(dgn_ref)
            dgp_ref[...] = jnp.zeros_like(dgp_ref)

        dxa, dgn = _rms_bwd(dh_ref[...], x1_ref[...], gn_ref[...])
        dx1 = dx2_ref[...] + dxa
        dx1_ref[...] = dx1
        dy, dgp = _rms_bwd(dx1, y_ref[...], gp_ref[...])
        dy_ref[...] = dy.astype(BF16)
        dgn_ref[...] += dgn
        dgp_ref[...] += dgp

    row = pl.BlockSpec((ts, d), lambda i: (i, 0))
    vec = pl.BlockSpec((1, d), lambda i: (0, 0))
    return _call("midnorm_bwd", body, (s // ts,), [row, row, row, row, vec, vec], [row, row, vec, vec],
                 [jax.ShapeDtypeStruct((s, d), F32), jax.ShapeDtypeStruct((s, d), BF16),
                  jax.ShapeDtypeStruct((1, d), F32), jax.ShapeDtypeStruct((1, d), F32)],
                 [], ("arbitrary",), (dx2, dh2, x1, y, g_post, g_pre), comm)


def _prenorm_bwd(dx1, dh, x, g, comm=None):
    s, d = x.shape
    ts = _tile(s, 256)

    def body(dx1_ref, dh_ref, x_ref, g_ref, dx_ref, dg_ref):
        @pl.when(pl.program_id(0) == 0)
        def _():
            dg_ref[...] = jnp.zeros_like(dg_ref)

        dxa, dg = _rms_bwd(dh_ref[...], x_ref[...], g_ref[...])
        dx_ref[...] = dx1_ref[...] + dxa
        dg_ref[...] += dg

    row = pl.BlockSpec((ts, d), lambda i: (i, 0))
    vec = pl.BlockSpec((1, d), lambda i: (0, 0))
    return _call("prenorm_bwd", body, (s // ts,), [row, row, row, vec], [row, vec],
                 [jax.ShapeDtypeStruct((s, d), F32), jax.ShapeDtypeStruct((1, d), F32)],
                 [], ("arbitrary",), (dx1, dh, x, g), comm)


def _colsum(a):
    s, n = a.shape
    ts = _tile(s, 256)

    def body(a_ref, o_ref):
        @pl.when(pl.program_id(0) == 0)
        def _():
            o_ref[...] = jnp.zeros_like(o_ref)

        o_ref[...] += jnp.sum(a_ref[...].astype(F32), axis=0, keepdims=True)

    return _call("colsum", body, (s // ts,), [pl.BlockSpec((ts, n), lambda i: (i, 0))],
                 [pl.BlockSpec((1, n), lambda i: (0, 0))], [jax.ShapeDtypeStruct((1, n), F32)], [], ("arbitrary",), (a,))[0]


def _shift_rows(win, off, t):
    n = win.shape[0]
    if off == 0:
        return win[:t]
    return pltpu.roll(win, n - off, axis=0)[:t]


def _conv_fwd(proj, w_pad, b_dw, c_total, comm=None):
    s = proj.shape[0]
    nct = c_total // LANES
    t = _tile(s, 256)

    def body(ga_ref, gb_ref, w_ref, b_ref, o_ref, u0_ref):
        u0_ref[pl.ds(0, CONV_PAD), :] = jnp.zeros((CONV_PAD, LANES), F32)
        u0_ref[pl.ds(CONV_PAD, s), :] = ga_ref[...] * _sigmoid(gb_ref[...])
        wv = w_ref[...]

        def chunk(r, carry):
            r0 = pl.multiple_of(r * t, t)
            win = u0_ref[pl.ds(r0, t + CONV_PAD), :]
            acc = jnp.broadcast_to(b_ref[...], (t, LANES))
            for j in range(CONV_WIDTH):
                acc = acc + wv[j:j + 1, :] * _shift_rows(win, j + CONV_PAD - (CONV_WIDTH - 1), t)
            o_ref[pl.ds(r0, t), :] = acc
            return carry

        lax.fori_loop(0, s // t, chunk, 0)

    return _call("conv_fwd", body, (nct,),
                 [pl.BlockSpec((s, LANES), lambda c: (0, 3 * nct + c)), pl.BlockSpec((s, LANES), lambda c: (0, 4 * nct + c)),
                  pl.BlockSpec((CONV_PAD, LANES), lambda c: (0, c)), pl.BlockSpec((1, LANES), lambda c: (0, c))],
                 [pl.BlockSpec((s, LANES), lambda c: (0, c))], [jax.ShapeDtypeStruct((s, c_total), F32)],
                 [pltpu.VMEM((s + CONV_PAD, LANES), F32)], ("parallel",), (proj, proj, w_pad, b_dw), comm)


def _conv_bwd(du1, proj, w_pad, c_total, comm=None):
    s = proj.shape[0]
    nct = c_total // LANES
    t = _tile(s, 256)

    def body(d_ref, ga_ref, gb_ref, w_ref, dga_ref, dgb_ref, dw_ref, db_ref, u0_ref, dp_ref):
        sg = _sigmoid(gb_ref[...])
        u0_ref[pl.ds(0, CONV_PAD), :] = jnp.zeros((CONV_PAD, LANES), F32)
        u0_ref[pl.ds(CONV_PAD, s), :] = ga_ref[...] * sg
        dp_ref[pl.ds(0, s), :] = d_ref[...]
        dp_ref[pl.ds(s, CONV_PAD), :] = jnp.zeros((CONV_PAD, LANES), F32)
        dw_ref[...] = jnp.zeros_like(dw_ref)
        db_ref[...] = jnp.sum(d_ref[...], axis=0, keepdims=True)
        wv = w_ref[...]

        def chunk(r, carry):
            r0 = pl.multiple_of(r * t, t)
            win = u0_ref[pl.ds(r0, t + CONV_PAD), :]
            dwin = dp_ref[pl.ds(r0, t + CONV_PAD), :]
            dcur = dwin[:t]
            du0 = jnp.zeros((t, LANES), F32)
            for j in range(CONV_WIDTH):
                du0 = du0 + wv[j:j + 1, :] * _shift_rows(dwin, CONV_WIDTH - 1 - j, t)
                sh = _shift_rows(win, j + CONV_PAD - (CONV_WIDTH - 1), t)
                dw_ref[j:j + 1, :] += jnp.sum(dcur * sh, axis=0, keepdims=True)
            gav = ga_ref[pl.ds(r0, t), :]
            sgv = _sigmoid(gb_ref[pl.ds(r0, t), :])
            dga_ref[pl.ds(r0, t), :] = (du0 * sgv).astype(BF16)
            dgb_ref[pl.ds(r0, t), :] = (du0 * gav * sgv * (1.0 - sgv)).astype(BF16)
            return carry

        lax.fori_loop(0, s // t, chunk, 0)

    col = pl.BlockSpec((s, LANES), lambda c: (0, c))
    return _call("conv_bwd", body, (nct,),
                 [col, pl.BlockSpec((s, LANES), lambda c: (0, 3 * nct + c)),
                  pl.BlockSpec((s, LANES), lambda c: (0, 4 * nct + c)), pl.BlockSpec((CONV_PAD, LANES), lambda c: (0, c))],
                 [col, col, pl.BlockSpec((CONV_PAD, LANES), lambda c: (0, c)), pl.BlockSpec((1, LANES), lambda c: (0, c))],
                 [jax.ShapeDtypeStruct((s, c_total), BF16), jax.ShapeDtypeStruct((s, c_total), BF16),
                  jax.ShapeDtypeStruct((CONV_PAD, c_total), F32), jax.ShapeDtypeStruct((1, c_total), F32)],
                 [pltpu.VMEM((s + CONV_PAD, LANES), F32), pltpu.VMEM((s + CONV_PAD, LANES), F32)],
                 ("parallel",), (du1, proj, proj, w_pad), comm)


TQ_PREF = 256
NU = 4
NU_BWD = 2
TK = 256


def _split_dot(v, tri):
    hi = v.astype(BF16)
    lo = (v - hi.astype(F32)).astype(BF16)
    return (jnp.dot(hi, tri, preferred_element_type=F32) + jnp.dot(lo, tri, preferred_element_type=F32))


def _causal_mask(i, j, tq):
    tpos = i * tq + lax.broadcasted_iota(jnp.int32, (tq, TK), 0)
    spos = j * TK + lax.broadcasted_iota(jnp.int32, (tq, TK), 1)
    return spos < tpos


def _log_terms(z, mask):
    sp = jnp.log(1.0 + jnp.exp(-jnp.abs(z)))
    return jnp.minimum(z, 0.0) - sp, jnp.where(mask, -jnp.maximum(z, 0.0) - sp, 0.0)


def _tri(after):
    r = lax.broadcasted_iota(jnp.int32, (TK, TK), 0)
    c = lax.broadcasted_iota(jnp.int32, (TK, TK), 1)
    return (r > c).astype(BF16) if after else (r < c).astype(BF16)


def _attn_fwd(proj, n_heads, comm=None):
    s = proj.shape[0]
    tq = _tile(s, TQ_PREF)
    scale = 1.0 / math.sqrt(HEAD_DIM)
    ratio = tq // TK

    def body(q_ref, k_ref, v_ref, o_ref, ot_ref, acc_ref, *clms):
        i = pl.program_id(1)
        heads = [slice(u * HEAD_DIM, (u + 1) * HEAD_DIM) for u in range(NU)]
        qs = [q_ref[:, hs].astype(BF16) for hs in heads]
        tri_after = _tri(True)
        acc_ref[...] = jnp.zeros_like(acc_ref)
        for cr in clms:
            cr[...] = jnp.zeros_like(cr)
        nkb = (i + 1) * ratio

        def step(jj, carry):
            j = nkb - 1 - jj
            rows = pl.ds(pl.multiple_of(j * TK, TK), TK)
            mask = _causal_mask(i, j, tq)
            zs = [lax.dot_general(qs[u], k_ref[rows, hs].astype(BF16), (((1,), (1,)), ((), ())),
                                  preferred_element_type=F32) * scale for u, hs in enumerate(heads)]
            lls = [_log_terms(z, mask) for z in zs]
            sufs = [clms[u][...] + _split_dot(lls[u][1], tri_after) for u in range(NU)]
            for u, hs in enumerate(heads):
                a = jnp.where(mask, jnp.exp(lls[u][0] + sufs[u]), 0.0)
                acc_ref[:, hs] += jnp.dot(a.astype(BF16), v_ref[rows, hs].astype(BF16), preferred_element_type=F32)
                clms[u][...] += jnp.sum(lls[u][1], axis=1, keepdims=True)
            return carry

        lax.fori_loop(0, nkb, step, 0)
        o = acc_ref[...]
        o_ref[...] = o
        ot_ref[...] = o.T.astype(BF16)

    w = NU * HEAD_DIM
    ng = n_heads // NU
    return _call("attn_fwd", body, (ng, s // tq),
                 [pl.BlockSpec((tq, w), lambda h, i: (i, h)),
                  pl.BlockSpec((s, w), lambda h, i: (0, ng + h)),
                  pl.BlockSpec((s, w), lambda h, i: (0, 2 * ng + h))],
                 [pl.BlockSpec((tq, w), lambda h, i: (i, h)), pl.BlockSpec((w, tq), lambda h, i: (h, i))],
                 [jax.ShapeDtypeStruct((s, n_heads * HEAD_DIM), F32), jax.ShapeDtypeStruct((n_heads * HEAD_DIM, s), BF16)],
                 [pltpu.VMEM((tq, w), F32), *[pltpu.VMEM((tq, 1), F32)] * NU],
                 ("parallel", "arbitrary"), (proj, proj, proj), comm)


def _attn_bwd(proj, do_sb, n_heads, comm=None):
    s = proj.shape[0]
    tq = _tile(s, TQ_PREF)
    scale = 1.0 / math.sqrt(HEAD_DIM)
    ratio = tq // TK
    n_kb = s // TK
    n_qb = s // tq
    nu = NU_BWD
    heads = [slice(u * HEAD_DIM, (u + 1) * HEAD_DIM) for u in range(nu)]
    nt_dims = (((1,), (1,)), ((), ()))

    def body(q_ref, k_ref, v_ref, do_ref, dq_ref, dk_ref, dv_ref, dka_ref, dva_ref, dl_ref, be_ref, dqa_ref, *c_refs):
        i = pl.program_id(1)

        @pl.when(i == 0)
        def _():
            dka_ref[...] = jnp.zeros_like(dka_ref)
            dva_ref[...] = jnp.zeros_like(dva_ref)

        qs = [q_ref[:, hs].astype(BF16) for hs in heads]
        dobs = [do_ref[:, hs].astype(BF16) for hs in heads]
        tri_after = _tri(True)
        tri_before = _tri(False)
        nkb = (i + 1) * ratio

        for cr in c_refs:
            cr[...] = jnp.zeros_like(cr)

        def sweep_a(jj, carry):
            j = nkb - 1 - jj
            rows = pl.ds(pl.multiple_of(j * TK, TK), TK)
            mask = _causal_mask(i, j, tq)
            zs = [lax.dot_general(qs[u], k_ref[rows, hs].astype(BF16), nt_dims, preferred_element_type=F32) * scale
                  for u, hs in enumerate(heads)]
            das = [lax.dot_general(dobs[u], v_ref[rows, hs].astype(BF16), nt_dims, preferred_element_type=F32)
                   for u, hs in enumerate(heads)]
            lls = [_log_terms(z, mask) for z in zs]
            sufs = [c_refs[u][...] + _split_dot(lls[u][1], tri_after) for u in range(nu)]
            for u, hs in enumerate(heads):
                a = jnp.where(mask, jnp.exp(lls[u][0] + sufs[u]), 0.0)
                dl_ref[u, j] = das[u] * a
                be_ref[u, j] = jnp.exp(lls[u][0])
                dva_ref[rows, hs] += jnp.dot(a.T.astype(BF16), dobs[u], preferred_element_type=F32)
                c_refs[u][...] += jnp.sum(lls[u][1], axis=1, keepdims=True)
            return carry

        lax.fori_loop(0, nkb, sweep_a, 0)

        for cr in c_refs:
            cr[...] = jnp.zeros_like(cr)
        dqa_ref[...] = jnp.zeros_like(dqa_ref)

        def sweep_b(j, carry):
            rows = pl.ds(pl.multiple_of(j * TK, TK), TK)
            mask = _causal_mask(i, j, tq)
            dls = [dl_ref[u, j] for u in range(nu)]
            ps = [c_refs[u][...] + _split_dot(dls[u], tri_before) for u in range(nu)]
            for u, hs in enumerate(heads):
                beta = be_ref[u, j]
                dz = jnp.where(mask, (dls[u] * (1.0 - beta) - beta * ps[u]) * scale, 0.0)
                dqa_ref[:, hs] += jnp.dot(dz.astype(BF16), k_ref[rows, hs].astype(BF16), preferred_element_type=F32)
                dka_ref[rows, hs] += jnp.dot(dz.T.astype(BF16), qs[u], preferred_element_type=F32)
                c_refs[u][...] += jnp.sum(dls[u], axis=1, keepdims=True)
            return carry

        lax.fori_loop(0, nkb, sweep_b, 0)
        dq_ref[...] = dqa_ref[...].astype(BF16)

        @pl.when(i == n_qb - 1)
        def _():
            dk_ref[...] = dka_ref[...].astype(BF16)
            dv_ref[...] = dva_ref[...].astype(BF16)

    w = nu * HEAD_DIM
    ng = n_heads // nu
    qblk = pl.BlockSpec((tq, w), lambda h, i: (i, h))
    full = pl.BlockSpec((s, w), lambda h, i: (0, h))
    sds = jax.ShapeDtypeStruct((s, n_heads * HEAD_DIM), BF16)
    return _call("attn_bwd", body, (ng, n_qb),
                 [qblk, pl.BlockSpec((s, w), lambda h, i: (0, ng + h)), pl.BlockSpec((s, w), lambda h, i: (0, 2 * ng + h)),
                  qblk],
                 [qblk, full, full], [sds, sds, sds],
                 [pltpu.VMEM((s, w), F32), pltpu.VMEM((s, w), F32),
                  pltpu.VMEM((nu, n_kb, tq, TK), F32), pltpu.VMEM((nu, n_kb, tq, TK), F32),
                  pltpu.VMEM((tq, w), F32), *[pltpu.VMEM((tq, 1), F32)] * nu],
                 ("parallel", "arbitrary"), (proj, proj, proj, do_sb), comm)


def _adamw(name, w, m, v, parts, part_specs, tr, comm=None):
    r, c = w.shape
    n_parts = len(parts)

    def body(*refs):
        w_ref, m_ref, v_ref = refs[:3]
        p_refs = refs[3:3 + n_parts]
        g_ref, d_ref, nm_ref, nv_ref = refs[3 + n_parts:]
        g = p_refs[0][...].astype(F32)
        for p in p_refs[1:]:
            g = g + p[...].astype(F32)
        nm = ADAM_B1 * m_ref[...] + (1.0 - ADAM_B1) * g
        nv = ADAM_B2 * v_ref[...] + (1.0 - ADAM_B2) * jnp.square(g)
        m_hat = nm / (1.0 - ADAM_B1 ** ADAM_STEP)
        v_hat = nv / (1.0 - ADAM_B2 ** ADAM_STEP)
        g_ref[...] = g
        d_ref[...] = -ADAM_LR * (m_hat / (jnp.sqrt(v_hat) + ADAM_EPS) + ADAM_WD * w_ref[...])
        nm_ref[...] = nm
        nv_ref[...] = nv

    blk = pl.BlockSpec((tr, c), lambda i: (i, 0))
    sds = jax.ShapeDtypeStruct((r, c), F32)
    return _call(name, body, (r // tr,), [blk, blk, blk, *part_specs], [blk] * 4, [sds] * 4, [], ("parallel",),
                 (w, m, v, *parts), comm)


def _adamw_big(name, w, m, v, recv, comm=None):
    r, c = w.shape
    tr = _tile(r, 128)
    order = (3, 0, 1, 2)
    specs = [pl.BlockSpec((None, tr, c), functools.partial(lambda i, slot: (slot, i, 0), slot=sl)) for sl in order]
    return _adamw(name, w, m, v, [recv] * 4, specs, tr, comm)


def _adamw_small(name, w, m, v, g):
    r, c = w.shape
    return _adamw(name, w, m, v, [g], [pl.BlockSpec((r, c), lambda i: (0, 0))], r)


def kernel(x, g_pre_mix, w_in, b_in, w_dw, b_dw, g_conv_ln, b_conv_ln, w_sb_out, w_conv_out, w_o, g_post_mix, g_pre_mlp, w_up, w_down, g_post_mlp, loss_target, m_g_pre_mix, m_w_in, m_b_in, m_w_dw, m_b_dw, m_g_conv_ln, m_b_conv_ln, m_w_sb_out, m_w_conv_out, m_w_o, m_g_post_mix, m_g_pre_mlp, m_w_up, m_w_down, m_g_post_mlp, v_g_pre_mix, v_w_in, v_b_in, v_w_dw, v_b_dw, v_g_conv_ln, v_b_conv_ln, v_w_sb_out, v_w_conv_out, v_w_o, v_g_post_mix, v_g_pre_mlp, v_w_up, v_w_down, v_g_post_mlp):
    xs, tgt = x[0], loss_target[0]
    s, d = xs.shape
    d_half = d // 2
    n_heads = d_half // HEAD_DIM
    d_ff = NDEV * w_up.shape[2]
    core = lax.axis_index("c").astype(jnp.int32).reshape(1)
    dev = 4 * lax.axis_index("x") + 2 * lax.axis_index("y") + lax.axis_index("c")

    w_dw_pad = jnp.pad(w_dw[0], ((0, CONV_PAD - CONV_WIDTH), (0, 0)))
    sh_in, sh_sb, sh_cv, sh_o, sh_up, sh_down = [w[0].astype(BF16) for w in (w_in, w_sb_out, w_conv_out, w_o, w_up, w_down)]
    px, py, pc = _position()
    order = jnp.stack([4 * (px ^ fx) + 2 * (py ^ fy) + (pc ^ fc) for fx, fy, fc in GATHER_FLIPS]).astype(jnp.int32)

    ag_up = _Chunked(_ag_comm, [sh_up], sh_up.shape[0], 8)
    h, h_t = _prenorm(xs, g_pre_mix)
    proj, wg_in = _proj_gather(h, sh_in, b_in, order)
    o_sb, o_sb_t, wg_sb, wg_cv, wg_o, wg_dw, *part = _attn_fwd(
        proj, n_heads, _join(_ag_comm([sh_sb, sh_cv, sh_o, w_dw_pad]), ag_up.take(4)))
    ag_up.done(part)
    wf_dw = wg_dw.transpose(1, 0, 2).reshape(CONV_PAD, d_half)
    wf_o = wg_o.reshape(d, d)
    u1, *part = _conv_fwd(proj, wf_dw, b_dw, d_half, ag_up.take())
    ag_up.done(part)
    u3, u3_t = _ln_silu(u1, g_conv_ln, b_conv_ln)
    o_sbp = _mm_cols("sb_out", o_sb, wg_sb)[0]
    o_cv = _mm_cols("conv_out", u3, wg_cv)[0]
    merged, merged_t, *part = _merge(proj, o_sbp, o_cv, d, ag_up.take())
    ag_up.done(part)
    y, *part = _mm_plain("w_o", merged, wf_o, False, F32, comm=ag_up.take())
    ag_up.done(part)
    x1, h2, h2_t, wg_up = _postnorm_mix(xs, y, g_post_mix, g_pre_mlp, ag_up.take())

    tm_up = _tile(s, 1024)
    ns_up = wg_up.shape[2]
    tk_up = _tile(d, 2048)

    def up_epilogue(acc):
        f = jnp.square(jnp.maximum(acc, 0.0))
        return acc, f, f.T

    a_act, f, f_t, wg_down = _matmul(
        "w_up", h2, wg_up,
        [jax.ShapeDtypeStruct((s, d_ff), BF16), jax.ShapeDtypeStruct((s, d_ff), BF16), jax.ShapeDtypeStruct((d_ff, s), BF16)],
        (s // tm_up, NDEV, d // tk_up),
        pl.BlockSpec((tm_up, tk_up), lambda i, dd, kk: (i, kk)),
        pl.BlockSpec((None, tk_up, ns_up), lambda i, dd, kk: (dd, kk, 0)),
        [pl.BlockSpec((tm_up, ns_up), lambda i, dd, kk: (i, dd)), pl.BlockSpec((tm_up, ns_up), lambda i, dd, kk: (i, dd)),
         pl.BlockSpec((ns_up, tm_up), lambda i, dd, kk: (dd, i))],
        1, False, (tm_up, ns_up), epilogue=up_epilogue, comm=_ag_comm([sh_down]))
    wf_down = wg_down.reshape(d_ff, d)
    f2 = _mm_plain("w_down", f, wf_down, False, F32)[0]
    dx2, df2, dg_post_mlp, loss_part = _loss_head(x1, f2, tgt, g_post_mlp)

    tm_b, tn_b = _tile(s, 1024), _tile(d_ff, 1024)
    da = _mm_plain("w_down_bwd", df2, wf_down, True, BF16,
                   extra=(a_act,), extra_specs=(pl.BlockSpec((tm_b, tn_b), lambda i, j, kk: (i, j)),),
                   epilogue=lambda acc, av: (acc * (2.0 * jnp.maximum(av.astype(F32), 0.0)),),
                   outs=[jax.ShapeDtypeStruct((s, d_ff), BF16)],
                   out_specs=[pl.BlockSpec((tm_b, tn_b), lambda i, j, kk: (i, j))])[0]
    gw_down = _mm_plain("w_down_grad", f_t, df2, False, BF16)[0]
    big_down = gw_down.reshape(4, 2, d_ff // NDEV, d)
    gw_up, sib_down = _mm_dw_cols("w_up_grad", h2_t, da, comm=_sibling_comm([big_down]))
    big_up = gw_up.reshape(4, 2, d, d_ff // NDEV)
    dh2, sib_up = _mm_cols_t("w_up_bwd", da, wg_up, comm=_sibling_comm([big_up]))
    rs_down = _Chunked(_chips_comm, [_pair_sum(big_down, sib_down, core)], d_ff // NDEV, 8)
    rs_up = _Chunked(_chips_comm, [_pair_sum(big_up, sib_up, core)], d, 8)
    dx1, dy, dg_pre_mlp, dg_post_mix, *part = _midnorm_bwd(dx2, dh2, x1, y, g_post_mix, g_pre_mlp, rs_down.take())
    rs_down.done(part)
    gw_o, *part = _mm_plain("w_o_grad", merged_t, dy, False, BF16, comm=rs_down.take())
    rs_down.done(part)
    dmerged, *part = _mm_plain("w_o_bwd", dy, wf_o, True, F32, comm=rs_down.take())
    rs_down.done(part)
    do_sbp, do_cv, dgate_sb, dgate_cv, *part = _merge_bwd(dmerged, proj, o_sbp, o_cv, d, rs_down.take())
    rs_down.done(part)
    gw_cv = _mm_dw_cols("conv_out_grad", u3_t, do_cv)[0]
    gw_sb = _mm_dw_cols("sb_out_grad", o_sb_t, do_sbp)[0]
    du3 = _mm_cols_t("conv_out_bwd", do_cv, wg_cv)[0]
    do_sb = _mm_cols_t("sb_out_bwd", do_sbp, wg_sb)[0]
    du1, dg_ln, db_ln = _ln_silu_bwd(du3, u1, g_conv_ln, b_conv_ln)
    dglu_a, dglu_b, dw_dw, db_dw, *part = _conv_bwd(du1, proj, wf_dw, d_half, rs_up.take(2))
    rs_up.done(part)
    big_mid = [gw_sb.reshape(4, 2, d_half, d // NDEV), gw_cv.reshape(4, 2, d_half, d // NDEV),
               gw_o.reshape(4, 2, d // NDEV, d)]
    sib_mid = _run_comm("exchange_sibling_mid", _sibling_comm(big_mid))
    sums_mid = [_pair_sum(g, r, core) for g, r in zip(big_mid, sib_mid)]
    dq, dk, dv, r_up, r_sb, r_cv, r_o = _attn_bwd(proj, do_sb, n_heads, _join(rs_up.take(6), _chips_comm(sums_mid)))
    dproj = jnp.concatenate([dq, dk, dv, dglu_a, dglu_b, dgate_sb, dgate_cv], axis=1)
    db_in = _colsum(dproj)
    gw_in, r_down = _mm_dw_cols("w_in_grad", h_t, dproj, comm=rs_down.take(4))
    big_in = gw_in.reshape(4, 2, d, gw_in.shape[2])
    sib_in, = _run_comm("exchange_sibling_in", _sibling_comm([big_in]))
    dh, r_in = _mm_cols_t("w_in_bwd", dproj, wg_in, comm=_chips_comm([_pair_sum(big_in, sib_in, core)]))
    grad_x, dg_pre_mix = _prenorm_bwd(dx1, dh, xs, g_pre_mix)

    small = [dg_pre_mix, db_in, dw_dw.reshape(1, -1), db_dw, dg_ln, db_ln, dg_post_mix, dg_pre_mlp, dg_post_mlp]
    sizes = [a.shape[1] for a in small]
    packed = jnp.concatenate(small, axis=1).reshape(-1, LANES)
    total = _sum_small(_all_gather_small(packed)).reshape(1, -1)
    offs = [0]
    for n in sizes:
        offs.append(offs[-1] + n)
    (g_g_pre_mix, g_b_in, g_w_dw_flat, g_b_dw, g_g_conv_ln, g_b_conv_ln, g_g_post_mix, g_g_pre_mlp,
     g_g_post_mlp) = [total[:, offs[k]:offs[k + 1]] for k in range(len(sizes))]
    ch = w_dw.shape[2]
    g_w_dw = lax.dynamic_slice_in_dim(g_w_dw_flat.reshape(CONV_PAD, d_half), dev * ch, ch, axis=1)[:CONV_WIDTH]

    loss = lax.psum(loss_part[0, 0], ("x", "y", "c"))

    res = {}
    res["w_up"] = _adamw_big("adamw_w_up", w_up[0], m_w_up[0], v_w_up[0], r_up)
    res["w_down"] = _adamw_big("adamw_w_down", w_down[0], m_w_down[0], v_w_down[0], r_down)
    res["g_pre_mix"] = _adamw_small("adamw_g_pre_mix", g_pre_mix, m_g_pre_mix, v_g_pre_mix, g_g_pre_mix)
    res["w_in"] = _adamw_big("adamw_w_in", w_in[0], m_w_in[0], v_w_in[0], r_in)
    res["b_in"] = _adamw_small("adamw_b_in", b_in, m_b_in, v_b_in, g_b_in)
    res["w_dw"] = _adamw_small("adamw_w_dw", w_dw[0], m_w_dw[0], v_w_dw[0], g_w_dw)
    res["b_dw"] = _adamw_small("adamw_b_dw", b_dw, m_b_dw, v_b_dw, g_b_dw)
    res["g_conv_ln"] = _adamw_small("adamw_g_conv_ln", g_conv_ln, m_g_conv_ln, v_g_conv_ln, g_g_conv_ln)
    res["b_conv_ln"] = _adamw_small("adamw_b_conv_ln", b_conv_ln, m_b_conv_ln, v_b_conv_ln, g_b_conv_ln)
    res["w_sb_out"] = _adamw_big("adamw_w_sb_out", w_sb_out[0], m_w_sb_out[0], v_w_sb_out[0], r_sb)
    res["w_conv_out"] = _adamw_big("adamw_w_conv_out", w_conv_out[0], m_w_conv_out[0], v_w_conv_out[0], r_cv)
    res["w_o"] = _adamw_big("adamw_w_o", w_o[0], m_w_o[0], v_w_o[0], r_o)
    res["g_post_mix"] = _adamw_small("adamw_g_post_mix", g_post_mix, m_g_post_mix, v_g_post_mix, g_g_post_mix)
    res["g_pre_mlp"] = _adamw_small("adamw_g_pre_mlp", g_pre_mlp, m_g_pre_mlp, v_g_pre_mlp, g_g_pre_mlp)
    res["g_post_mlp"] = _adamw_small("adamw_g_post_mlp", g_post_mlp, m_g_post_mlp, v_g_post_mlp, g_g_post_mlp)

    names = ["g_pre_mix", "w_in", "b_in", "w_dw", "b_dw", "g_conv_ln", "b_conv_ln", "w_sb_out", "w_conv_out", "w_o",
             "g_post_mix", "g_pre_mlp", "w_up", "w_down", "g_post_mlp"]
    three_d = {"w_in", "w_dw", "w_sb_out", "w_conv_out", "w_o", "w_up", "w_down"}

    def shaped(nm, arr):
        return arr[None] if nm in three_d else arr

    out = [loss, grad_x[None]]
    for k in range(4):
        out += [shaped(nm, res[nm][k]) for nm in names]
    return tuple(out)
```

```python
import functools
import math

import jax
import jax.numpy as jnp
from jax import lax
from jax.experimental import pallas as pl
from jax.experimental.pallas import tpu as pltpu

F32 = jnp.float32
BF16 = jnp.bfloat16
NDEV = 8
LANES = 128
EPS = 1e-6
CONV_WIDTH = 31
CONV_PAD = 32
HEAD_DIM = 128
ADAM_LR = 0.001
ADAM_B1 = 0.9
ADAM_B2 = 0.999
ADAM_EPS = 1e-08
ADAM_WD = 0.01
ADAM_STEP = 10
VMEM_LIMIT = 56 * 1024 * 1024
MESH = pl.DeviceIdType.MESH
ANY = pl.BlockSpec(memory_space=pl.ANY)


def _tile(n, pref):
    t = min(n, pref)
    assert n % t == 0, (n, t)
    return t


def _sigmoid(v):
    return 1.0 / (1.0 + jnp.exp(-v))


def _position():
    return lax.axis_index("x"), lax.axis_index("y"), lax.axis_index("c")


class _Comm:
    def __init__(self, ins, outs, scratch, start, finish, aliases=None, mid=None):
        self.ins, self.outs, self.scratch = list(ins), list(outs), list(scratch)
        self.start, self.finish, self.aliases = start, finish, dict(aliases or {})
        self.mid = mid


_NO_COMM = _Comm([], [], [], None, None)


def _call(name, body, grid, in_specs, out_specs, out_shape, scratch_shapes, sem, args, comm=None):
    comm = comm or _NO_COMM
    n_in, n_out, n_scr = len(in_specs), len(out_specs), len(scratch_shapes)
    n_cin, n_cout = len(comm.ins), len(comm.outs)

    n_steps = math.prod(grid)
    mid_step = n_steps - max(1, n_steps // 8) if (comm.mid is not None and n_steps >= 4) else None

    def edge(c_ins, c_outs, c_scr, at_step, actions):
        linear = 0
        for ax, g in enumerate(grid):
            linear = linear * g + pl.program_id(ax)

        @pl.when(linear == at_step)
        def _():
            for act in actions:
                act(c_ins, c_outs, c_scr)

    def wrapped(*refs):
        ins, c_ins = refs[:n_in], refs[n_in:n_in + n_cin]
        pos = n_in + n_cin
        outs, c_outs = refs[pos:pos + n_out], refs[pos + n_out:pos + n_out + n_cout]
        pos += n_out + n_cout
        scr, c_scr = refs[pos:pos + n_scr], refs[pos + n_scr:]
        if n_cin:
            edge(c_ins, c_outs, c_scr, 0, [comm.start])
            if mid_step is not None:
                edge(c_ins, c_outs, c_scr, mid_step, [comm.mid])
        body(*ins, *outs, *scr)
        if n_cin:
            late = [comm.finish] if (mid_step is not None or comm.mid is None) else [comm.mid, comm.finish]
            edge(c_ins, c_outs, c_scr, n_steps - 1, late)

    if n_cin:
        sem = ("arbitrary",) * len(grid)
    return pl.pallas_call(
        wrapped, name=name, grid=grid,
        in_specs=[*in_specs, *[ANY] * n_cin], out_specs=[*out_specs, *[ANY] * n_cout],
        out_shape=[*out_shape, *comm.outs], scratch_shapes=[*scratch_shapes, *comm.scratch],
        input_output_aliases={n_in + ci: n_out + co for ci, co in comm.aliases.items()},
        compiler_params=pltpu.CompilerParams(dimension_semantics=sem, vmem_limit_bytes=VMEM_LIMIT),
    )(*args, *comm.ins)


def _run_comm(name, comm):
    n_in, n_out = len(comm.ins), len(comm.outs)

    def body(*refs):
        ins, outs, scr = refs[:n_in], refs[n_in:n_in + n_out], refs[n_in + n_out:]
        comm.start(ins, outs, scr)
        if comm.mid is not None:
            comm.mid(ins, outs, scr)
        comm.finish(ins, outs, scr)

    return pl.pallas_call(body, name=name, in_specs=[ANY] * n_in, out_specs=[ANY] * n_out, out_shape=comm.outs,
                          scratch_shapes=comm.scratch,
                          input_output_aliases=comm.aliases)(*comm.ins)


def _rows_of(ref, rows):
    return ref if rows is None else ref.at[pl.ds(rows[0], rows[1])]


def _ag_comm(shards, rows=None, into=None):
    n = len(shards)

    def parts(ins, outs, scr):
        send_sems, recv_sems, local_sems = scr
        x, y, c = _position()
        chips = [(1 - x, y), (x, 1 - y), (1 - x, 1 - y)]

        def copy(a, k, block, to, own=False):
            px, py, pc = block
            dst = _rows_of(outs[a].at[4 * px + 2 * py + pc], rows)
            return pltpu.make_async_remote_copy(src_ref=_rows_of(ins[a], rows) if own else dst, dst_ref=dst,
                                                send_sem=send_sems.at[a, k], recv_sem=recv_sems.at[a, k],
                                                device_id=to, device_id_type=MESH)

        mine = [pltpu.make_async_copy(_rows_of(ins[a], rows), _rows_of(outs[a].at[4 * x + 2 * y + c], rows),
                                      local_sems.at[a]) for a in range(n)]
        first = []
        for a in range(n):
            first.append(copy(a, 0, (x, y, c), (x, y, 1 - c), own=True))
            first += [copy(a, 1 + j, (x, y, c), (*chip, c), own=True) for j, chip in enumerate(chips)]
        return copy, mine, first, chips, (x, y, c), (x, y, 1 - c)

    def start(ins, outs, scr):
        _, mine, first, _, _, _ = parts(ins, outs, scr)
        for cp in mine + first:
            cp.start()

    def mid(ins, outs, scr):
        copy, _, _, chips, me, sibling = parts(ins, outs, scr)
        for a in range(n):
            for j, chip in enumerate(chips):
                copy(a, 1 + j, (*chip, me[2]), me).wait_recv()
                copy(a, 4 + j, (*chip, me[2]), sibling).start()

    def finish(ins, outs, scr):
        copy, mine, first, chips, me, sibling = parts(ins, outs, scr)
        c = me[2]
        passed = [copy(a, 4 + j, (*chip, c), sibling) for a in range(n) for j, chip in enumerate(chips)]
        for a in range(n):
            copy(a, 0, sibling, me).wait_recv()
            for j, chip in enumerate(chips):
                copy(a, 4 + j, (*chip, 1 - c), me).wait_recv()
        for cp in first + passed:
            cp.wait_send()
        for cp in mine:
            cp.wait()

    return _Comm([*shards, *(into or [])], [jax.ShapeDtypeStruct((NDEV, *sh.shape), sh.dtype) for sh in shards],
                 [pltpu.SemaphoreType.DMA((n, 7)), pltpu.SemaphoreType.DMA((n, 7)), pltpu.SemaphoreType.DMA((n,))],
                 start, finish, {n + a: a for a in range(n)} if into else None, mid)


def _sibling_comm(grads):
    n = len(grads)

    def copies(ins, outs, scr):
        send_sems, recv_sems = scr
        x, y, c = _position()
        return [pltpu.make_async_remote_copy(src_ref=ins[a].at[k, 1 - c], dst_ref=outs[a].at[k],
                                             send_sem=send_sems.at[a, k], recv_sem=recv_sems.at[a, k],
                                             device_id=(x, y, 1 - c), device_id_type=MESH)
                for a in range(n) for k in range(4)]

    def start(ins, outs, scr):
        for cp in copies(ins, outs, scr):
            cp.start()

    def finish(ins, outs, scr):
        for cp in copies(ins, outs, scr):
            cp.wait()

    return _Comm(grads, [jax.ShapeDtypeStruct((4, *g.shape[2:]), g.dtype) for g in grads],
                 [pltpu.SemaphoreType.DMA((n, 4)), pltpu.SemaphoreType.DMA((n, 4))], start, finish)


def _chips_comm(sums, rows=None, into=None):
    n = len(sums)

    def copies(ins, outs, scr):
        send_sems, recv_sems, local_sems = scr
        x, y, c = _position()
        chips = [(1 - x, y), (x, 1 - y), (1 - x, 1 - y)]
        own = [pltpu.make_async_copy(_rows_of(ins[a].at[2 * x + y], rows), _rows_of(outs[a].at[3], rows), local_sems.at[a])
               for a in range(n)]
        remote = [pltpu.make_async_remote_copy(src_ref=_rows_of(ins[a].at[2 * px + py], rows),
                                               dst_ref=_rows_of(outs[a].at[j], rows),
                                               send_sem=send_sems.at[a, j], recv_sem=recv_sems.at[a, j],
                                               device_id=(px, py, c), device_id_type=MESH)
                  for a in range(n) for j, (px, py) in enumerate(chips)]
        return own + remote

    def start(ins, outs, scr):
        for cp in copies(ins, outs, scr):
            cp.start()

    def finish(ins, outs, scr):
        for cp in copies(ins, outs, scr):
            cp.wait()

    return _Comm([*sums, *(into or [])], [jax.ShapeDtypeStruct(sm.shape, sm.dtype) for sm in sums],
                 [pltpu.SemaphoreType.DMA((n, 3)), pltpu.SemaphoreType.DMA((n, 3)), pltpu.SemaphoreType.DMA((n,))],
                 start, finish, {n + a: a for a in range(n)} if into else None)


def _join(c1, c2):
    n_in, n_out, n_scr = len(c1.ins), len(c1.outs), len(c1.scratch)
    aliases = dict(c1.aliases)
    aliases.update({n_in + ci: n_out + co for ci, co in c2.aliases.items()})

    def start(ins, outs, scr):
        c1.start(ins[:n_in], outs[:n_out], scr[:n_scr])
        c2.start(ins[n_in:], outs[n_out:], scr[n_scr:])

    def mid(ins, outs, scr):
        if c1.mid is not None:
            c1.mid(ins[:n_in], outs[:n_out], scr[:n_scr])
        if c2.mid is not None:
            c2.mid(ins[n_in:], outs[n_out:], scr[n_scr:])

    def finish(ins, outs, scr):
        c1.finish(ins[:n_in], outs[:n_out], scr[:n_scr])
        c2.finish(ins[n_in:], outs[n_out:], scr[n_scr:])

    return _Comm(c1.ins + c2.ins, c1.outs + c2.outs, c1.scratch + c2.scratch, start, finish, aliases,
                 mid if (c1.mid is not None or c2.mid is not None) else None)


GATHER_FLIPS = ((0, 0, 0), (0, 0, 1), (1, 0, 0), (0, 1, 0), (1, 0, 1), (0, 1, 1), (1, 1, 0), (1, 1, 1))
GATHER_RECV_SEM = {1: 0, 2: 1, 3: 2, 4: 4, 5: 5, 6: 3, 7: 6}
GATHER_PASS_SEM = {2: 4, 3: 5, 6: 6}


def _proj_gather(h, shard, bias, order):
    s_len, d = h.shape
    _, ns = shard.shape
    n_steps = len(GATHER_FLIPS)

    def body(order_ref, h_ref, sh_ref, b_ref, o_ref, wg_ref, w_vmem, load_sem, send_sems, recv_sems, local_sem):
        step = pl.program_id(0)
        x, y, c = _position()
        sibling = (x, y, 1 - c)

        def block(flip):
            return wg_ref.at[4 * (x ^ flip[0]) + 2 * (y ^ flip[1]) + (c ^ flip[2])]

        def copy(k, flip, to, own=False):
            dst = block(flip)
            return pltpu.make_async_remote_copy(src_ref=sh_ref if own else dst, dst_ref=dst, send_sem=send_sems.at[k],
                                                recv_sem=recv_sems.at[k], device_id=to, device_id_type=MESH)

        mine = pltpu.make_async_copy(sh_ref, block(GATHER_FLIPS[0]), local_sem)
        first = [copy(0, GATHER_FLIPS[0], sibling, own=True)]
        first += [copy(1 + j, GATHER_FLIPS[0], (x ^ fx, y ^ fy, c), own=True)
                  for j, (fx, fy) in enumerate(((1, 0), (0, 1), (1, 1)))]
        passed = [copy(k, GATHER_FLIPS[t], sibling) for t, k in GATHER_PASS_SEM.items()]

        for t, flip in enumerate(GATHER_FLIPS):
            @pl.when(step == t)
            def _(t=t, flip=flip):
                if t == 0:
                    for cp in [mine] + first:
                        cp.start()
                    src = sh_ref
                else:
                    copy(GATHER_RECV_SEM[t], flip, (x, y, c)).wait_recv()
                    if t in GATHER_PASS_SEM:
                        copy(GATHER_PASS_SEM[t], flip, sibling).start()
                    src = block(flip)
                load = pltpu.make_async_copy(src, w_vmem, load_sem)
                load.start()
                load.wait()

        o_ref[...] = jnp.dot(h_ref[...], w_vmem[...], preferred_element_type=F32) + b_ref[...]

        @pl.when(step == n_steps - 1)
        def _():
            for cp in first + passed:
                cp.wait_send()
            mine.wait()

    return pl.pallas_call(
        body, name="proj_gather",
        grid_spec=pltpu.PrefetchScalarGridSpec(
            num_scalar_prefetch=1, grid=(n_steps,),
            in_specs=[pl.BlockSpec((s_len, d), lambda t, order_ref: (0, 0)), ANY,
                      pl.BlockSpec((1, ns), lambda t, order_ref: (0, order_ref[t]))],
            out_specs=[pl.BlockSpec((s_len, ns), lambda t, order_ref: (0, order_ref[t])), ANY],
            scratch_shapes=[pltpu.VMEM((d, ns), BF16), pltpu.SemaphoreType.DMA, pltpu.SemaphoreType.DMA((7,)),
                            pltpu.SemaphoreType.DMA((7,)), pltpu.SemaphoreType.DMA]),
        out_shape=[jax.ShapeDtypeStruct((s_len, NDEV * ns), F32), jax.ShapeDtypeStruct((NDEV, d, ns), BF16)],
        compiler_params=pltpu.CompilerParams(dimension_semantics=("arbitrary",), vmem_limit_bytes=VMEM_LIMIT),
    )(order, h, shard, bias)


class _Chunked:
    def __init__(self, make, arrays, n_rows, n_chunks):
        self.make, self.arrays, self.into = make, arrays, None
        step = n_rows // n_chunks
        assert step * n_chunks == n_rows
        self.todo = [(k * step, step) for k in range(n_chunks)]

    def take(self, count=1):
        r0, nr = self.todo[0][0], sum(t[1] for t in self.todo[:count])
        self.todo = self.todo[count:]
        return self.make(self.arrays, (r0, nr), self.into)

    def done(self, outs):
        self.into = list(outs)
        return self.into


def _all_gather_small(part):
    def body(in_ref, out_ref, send_sems, recv_sems, local_sem):
        x, y, c = _position()
        me = 4 * x + 2 * y + c
        mine = pltpu.make_async_copy(in_ref, out_ref.at[me], local_sem)
        mine.start()
        flips = [(fx, fy, fc) for fx in (0, 1) for fy in (0, 1) for fc in (0, 1)][1:]
        copies = []
        for k, (fx, fy, fc) in enumerate(flips):
            cp = pltpu.make_async_remote_copy(src_ref=in_ref, dst_ref=out_ref.at[me], send_sem=send_sems.at[k],
                                              recv_sem=recv_sems.at[k],
                                              device_id=(x ^ fx, y ^ fy, c ^ fc), device_id_type=MESH)
            cp.start()
            copies.append(cp)
        for k, (fx, fy, fc) in enumerate(flips):
            peer = 4 * (x ^ fx) + 2 * (y ^ fy) + (c ^ fc)
            pltpu.make_async_remote_copy(src_ref=in_ref, dst_ref=out_ref.at[peer], send_sem=send_sems.at[k],
                                         recv_sem=recv_sems.at[k], device_id=(x, y, c), device_id_type=MESH).wait_recv()
        for cp in copies:
            cp.wait_send()
        mine.wait()

    return pl.pallas_call(
        body, name="all_gather_small", in_specs=[ANY], out_specs=ANY,
        out_shape=jax.ShapeDtypeStruct((NDEV, *part.shape), part.dtype),
        scratch_shapes=[pltpu.SemaphoreType.DMA((7,)), pltpu.SemaphoreType.DMA((7,)), pltpu.SemaphoreType.DMA],
    )(part)


def _pair_sum(g, recv, core):
    _, _, r, c = g.shape
    tr = _tile(r, 512)

    def body(core_ref, g_ref, r_ref, o_ref):
        o_ref[...] = (g_ref[...].astype(F32) + r_ref[...].astype(F32)).astype(o_ref.dtype)

    return pl.pallas_call(
        body, name="pair_sum",
        grid_spec=pltpu.PrefetchScalarGridSpec(
            num_scalar_prefetch=1, grid=(4, r // tr),
            in_specs=[pl.BlockSpec((None, None, tr, c), lambda k, i, core_ref: (k, core_ref[0], i, 0)),
                      pl.BlockSpec((None, tr, c), lambda k, i, core_ref: (k, i, 0))],
            out_specs=pl.BlockSpec((None, tr, c), lambda k, i, core_ref: (k, i, 0))),
        out_shape=jax.ShapeDtypeStruct((4, r, c), g.dtype),
        compiler_params=pltpu.CompilerParams(dimension_semantics=("parallel", "parallel"), vmem_limit_bytes=VMEM_LIMIT),
    )(core, g, recv)


def _sum_small(gathered):
    _, r, l = gathered.shape

    def body(g_ref, o_ref):
        acc = g_ref[0]
        for d in range(1, NDEV):
            acc = acc + g_ref[d]
        o_ref[...] = acc

    return pl.pallas_call(
        body, name="sum_small", in_specs=[pl.BlockSpec((NDEV, r, l), lambda: (0, 0, 0))],
        out_specs=pl.BlockSpec((r, l), lambda: (0, 0)), out_shape=jax.ShapeDtypeStruct((r, l), F32),
    )(gathered)


def _matmul(name, a, b, outs, grid, a_spec, b_spec, out_specs, n_red, nt, acc_shape,
            extra=(), extra_specs=(), epilogue=None, comm=None):
    n_extra, n_out = len(extra), len(outs)
    red_axes = tuple(range(len(grid) - n_red, len(grid)))
    red_sizes = tuple(grid[ax] for ax in red_axes)
    single = all(sz == 1 for sz in red_sizes)
    dims = (((1,), (1,)), ((), ())) if nt else (((1,), (0,)), ((), ()))

    def body(*refs):
        a_ref, b_ref = refs[0], refs[1]
        ex_refs = refs[2:2 + n_extra]
        o_refs = refs[2 + n_extra:2 + n_extra + n_out]
        acc_ref = refs[-1]

        def write(acc):
            vals = (acc,) if epilogue is None else epilogue(acc, *[r[...] for r in ex_refs])
            for o_ref, val in zip(o_refs, vals):
                o_ref[...] = val.astype(o_ref.dtype)

        if len(b_ref.shape) == 3:
            w = b_ref.shape[2]
            part = sum(lax.dot_general(a_ref[:, p * w:(p + 1) * w].astype(BF16), b_ref[p].astype(BF16), dims,
                                       preferred_element_type=F32) for p in range(b_ref.shape[0]))
        else:
            part = lax.dot_general(a_ref[...].astype(BF16), b_ref[...].astype(BF16), dims, preferred_element_type=F32)
        if single:
            write(part)
        else:
            ks = [pl.program_id(ax) for ax in red_axes]
            first = functools.reduce(jnp.logical_and, [k == 0 for k in ks])
            last = functools.reduce(jnp.logical_and, [k == sz - 1 for k, sz in zip(ks, red_sizes)])

            @pl.when(first)
            def _():
                acc_ref[...] = part

            @pl.when(jnp.logical_not(first))
            def _():
                acc_ref[...] += part

            @pl.when(last)
            def _():
                write(acc_ref[...])

    sem = ("parallel",) * (len(grid) - n_red) + ("arbitrary",) * n_red
    return _call(name, body, grid, [a_spec, b_spec, *extra_specs], list(out_specs), list(outs),
                 [pltpu.VMEM((8, LANES) if single else acc_shape, F32)], sem, (a, b, *extra), comm)


def _mm_cols(name, a, wg, bias=None, out_dtype=F32, comm=None):
    m, k = a.shape
    _, _, ns = wg.shape
    tm, tk = _tile(m, 1024), _tile(k, 2048)
    grid = (m // tm, NDEV, k // tk)
    extra, extra_specs, epi = (), (), None
    if bias is not None:
        extra, extra_specs = (bias,), (pl.BlockSpec((1, ns), lambda i, d, kk: (0, d)),)
        epi = lambda acc, bv: (acc + bv,)
    return _matmul(name, a, wg, [jax.ShapeDtypeStruct((m, NDEV * ns), out_dtype)], grid,
                   pl.BlockSpec((tm, tk), lambda i, d, kk: (i, kk)),
                   pl.BlockSpec((None, tk, ns), lambda i, d, kk: (d, kk, 0)),
                   [pl.BlockSpec((tm, ns), lambda i, d, kk: (i, d))], 1, False, (tm, ns),
                   extra, extra_specs, epi, comm)


def _mm_cols_t(name, a, wg, comm=None):
    m, _ = a.shape
    _, n, ns = wg.shape
    tm, tn = _tile(m, 1024), _tile(n, 1024)
    per = 2
    grid = (m // tm, n // tn, NDEV // per)
    return _matmul(name, a, wg, [jax.ShapeDtypeStruct((m, n), F32)], grid,
                   pl.BlockSpec((tm, per * ns), lambda i, j, d: (i, d)),
                   pl.BlockSpec((per, tn, ns), lambda i, j, d: (d, j, 0)),
                   [pl.BlockSpec((tm, tn), lambda i, j, d: (i, j))], 1, True, (tm, tn), comm=comm)


def _mm_dw_cols(name, at, g, comm=None):
    m, t = at.shape
    ns = g.shape[1] // NDEV
    tm, tk = _tile(m, 1024), _tile(t, 2048)
    grid = (m // tm, NDEV, t // tk)
    return _matmul(name, at, g, [jax.ShapeDtypeStruct((NDEV, m, ns), BF16)], grid,
                   pl.BlockSpec((tm, tk), lambda i, d, kk: (i, kk)),
                   pl.BlockSpec((tk, ns), lambda i, d, kk: (kk, d)),
                   [pl.BlockSpec((None, tm, ns), lambda i, d, kk: (d, i, 0))], 1, False, (tm, ns), comm=comm)


def _mm_plain(name, a, b, nt, out_dtype, extra=(), extra_specs=(), epilogue=None, outs=None, out_specs=None, comm=None):
    m, k = a.shape
    n = b.shape[0] if nt else b.shape[1]
    tm, tn, tk = _tile(m, 1024), _tile(n, 1024), _tile(k, 2048)
    grid = (m // tm, n // tn, k // tk)
    b_spec = (pl.BlockSpec((tn, tk), lambda i, j, kk: (j, kk)) if nt
              else pl.BlockSpec((tk, tn), lambda i, j, kk: (kk, j)))
    if outs is None:
        outs = [jax.ShapeDtypeStruct((m, n), out_dtype)]
        out_specs = [pl.BlockSpec((tm, tn), lambda i, j, kk: (i, j))]
    return _matmul(name, a, b, outs, grid, pl.BlockSpec((tm, tk), lambda i, j, kk: (i, kk)), b_spec,
                   out_specs, 1, nt, (tm, tn), extra, extra_specs, epilogue, comm)


def _rms_rows(v):
    return lax.rsqrt(jnp.mean(v * v, axis=-1, keepdims=True) + EPS)


def _prenorm(x, g):
    s, d = x.shape
    ts = _tile(s, 256)

    def body(x_ref, g_ref, h_ref, ht_ref):
        xv = x_ref[...]
        h = xv * _rms_rows(xv) * g_ref[...]
        h_ref[...] = h.astype(BF16)
        ht_ref[...] = h.T.astype(BF16)

    return _call("prenorm", body, (s // ts,),
                 [pl.BlockSpec((ts, d), lambda i: (i, 0)), pl.BlockSpec((1, d), lambda i: (0, 0))],
                 [pl.BlockSpec((ts, d), lambda i: (i, 0)), pl.BlockSpec((d, ts), lambda i: (0, i))],
                 [jax.ShapeDtypeStruct((s, d), BF16), jax.ShapeDtypeStruct((d, s), BF16)], [], ("parallel",), (x, g))


def _ln_silu(u1, g, b):
    s, c = u1.shape
    ts = _tile(s, 256)

    def body(u_ref, g_ref, b_ref, o_ref, ot_ref):
        u = u_ref[...]
        mu = jnp.mean(u, axis=-1, keepdims=True)
        var = jnp.mean(jnp.square(u - mu), axis=-1, keepdims=True)
        u2 = (u - mu) * lax.rsqrt(var + EPS) * g_ref[...] + b_ref[...]
        u3 = u2 * _sigmoid(u2)
        o_ref[...] = u3.astype(BF16)
        ot_ref[...] = u3.T.astype(BF16)

    vec = pl.BlockSpec((1, c), lambda i: (0, 0))
    return _call("ln_silu", body, (s // ts,), [pl.BlockSpec((ts, c), lambda i: (i, 0)), vec, vec],
                 [pl.BlockSpec((ts, c), lambda i: (i, 0)), pl.BlockSpec((c, ts), lambda i: (0, i))],
                 [jax.ShapeDtypeStruct((s, c), BF16), jax.ShapeDtypeStruct((c, s), BF16)], [], ("parallel",), (u1, g, b))


def _ln_silu_bwd(du3, u1, g, b, comm=None):
    s, c = u1.shape
    ts = _tile(s, 256)

    def body(d_ref, u_ref, g_ref, b_ref, du1_ref, dg_ref, db_ref):
        @pl.when(pl.program_id(0) == 0)
        def _():
            dg_ref[...] = jnp.zeros_like(dg_ref)
            db_ref[...] = jnp.zeros_like(db_ref)

        u = u_ref[...]
        mu = jnp.mean(u, axis=-1, keepdims=True)
        var = jnp.mean(jnp.square(u - mu), axis=-1, keepdims=True)
        rstd = lax.rsqrt(var + EPS)
        uhat = (u - mu) * rstd
        u2 = uhat * g_ref[...] + b_ref[...]
        sg = _sigmoid(u2)
        du2 = d_ref[...] * (sg * (1.0 + u2 * (1.0 - sg)))
        dg_ref[...] += jnp.sum(du2 * uhat, axis=0, keepdims=True)
        db_ref[...] += jnp.sum(du2, axis=0, keepdims=True)
        duh = du2 * g_ref[...]
        du1_ref[...] = rstd * (duh - jnp.mean(duh, axis=-1, keepdims=True)
                               - uhat * jnp.mean(duh * uhat, axis=-1, keepdims=True))

    row = pl.BlockSpec((ts, c), lambda i: (i, 0))
    vec = pl.BlockSpec((1, c), lambda i: (0, 0))
    return _call("ln_silu_bwd", body, (s // ts,), [row, row, vec, vec], [row, vec, vec],
                 [jax.ShapeDtypeStruct((s, c), F32), jax.ShapeDtypeStruct((1, c), F32), jax.ShapeDtypeStruct((1, c), F32)],
                 [], ("arbitrary",), (du3, u1, g, b), comm)


def _merge(proj, o_sbp, o_cv, d, comm=None):
    s = proj.shape[0]
    w = d // 2
    ts = _tile(s, 256)

    def body(gs_ref, gc_ref, a_ref, b_ref, m_ref, mt_ref):
        mg = _sigmoid(gs_ref[...]) * a_ref[...] + _sigmoid(gc_ref[...]) * b_ref[...]
        m_ref[...] = mg.astype(BF16)
        mt_ref[...] = mg.T.astype(BF16)

    blk = pl.BlockSpec((ts, w), lambda i, j: (i, j))
    return _call("merge", body, (s // ts, 2),
                 [pl.BlockSpec((ts, w), lambda i, j: (i, 5 + j)), pl.BlockSpec((ts, w), lambda i, j: (i, 7 + j)), blk, blk],
                 [blk, pl.BlockSpec((w, ts), lambda i, j: (j, i))],
                 [jax.ShapeDtypeStruct((s, d), BF16), jax.ShapeDtypeStruct((d, s), BF16)], [],
                 ("parallel", "parallel"), (proj, proj, o_sbp, o_cv), comm)


def _merge_bwd(dmerged, proj, o_sbp, o_cv, d, comm=None):
    s = proj.shape[0]
    w = d // 2
    ts = _tile(s, 256)

    def body(dm_ref, gs_ref, gc_ref, a_ref, b_ref, da_ref, db_ref, dgs_ref, dgc_ref):
        dm = dm_ref[...]
        ss = _sigmoid(gs_ref[...])
        sc = _sigmoid(gc_ref[...])
        da_ref[...] = (dm * ss).astype(BF16)
        db_ref[...] = (dm * sc).astype(BF16)
        dgs_ref[...] = (dm * a_ref[...] * ss * (1.0 - ss)).astype(BF16)
        dgc_ref[...] = (dm * b_ref[...] * sc * (1.0 - sc)).astype(BF16)

    blk = pl.BlockSpec((ts, w), lambda i, j: (i, j))
    sds = jax.ShapeDtypeStruct((s, d), BF16)
    return _call("merge_bwd", body, (s // ts, 2),
                 [blk, pl.BlockSpec((ts, w), lambda i, j: (i, 5 + j)), pl.BlockSpec((ts, w), lambda i, j: (i, 7 + j)),
                  blk, blk],
                 [blk, blk, blk, blk], [sds, sds, sds, sds], [], ("parallel", "parallel"),
                 (dmerged, proj, proj, o_sbp, o_cv), comm)


def _postnorm_mix(x, y, g_post, g_pre, comm=None):
    s, d = x.shape
    ts = _tile(s, 256)

    def body(x_ref, y_ref, gp_ref, gn_ref, x1_ref, h_ref, ht_ref):
        yv = y_ref[...]
        x1 = x_ref[...] + yv * _rms_rows(yv) * gp_ref[...]
        x1_ref[...] = x1
        h = x1 * _rms_rows(x1) * gn_ref[...]
        h_ref[...] = h.astype(BF16)
        ht_ref[...] = h.T.astype(BF16)

    row = pl.BlockSpec((ts, d), lambda i: (i, 0))
    vec = pl.BlockSpec((1, d), lambda i: (0, 0))
    return _call("postnorm_mix", body, (s // ts,), [row, row, vec, vec],
                 [row, row, pl.BlockSpec((d, ts), lambda i: (0, i))],
                 [jax.ShapeDtypeStruct((s, d), F32), jax.ShapeDtypeStruct((s, d), BF16), jax.ShapeDtypeStruct((d, s), BF16)],
                 [], ("parallel",), (x, y, g_post, g_pre), comm)


def _rms_bwd(dout, vin, g):
    r = _rms_rows(vin)
    vhat = vin * r
    dyh = dout * g
    dvin = r * (dyh - vhat * jnp.mean(dyh * vhat, axis=-1, keepdims=True))
    return dvin, jnp.sum(dout * vhat, axis=0, keepdims=True)


def _loss_head(x1, f2, tgt, g):
    s, d = x1.shape
    ts = _tile(s, 256)

    def body(x1_ref, f_ref, t_ref, g_ref, dx2_ref, df2_ref, dg_ref, loss_ref):
        @pl.when(pl.program_id(0) == 0)
        def _():
            dg_ref[...] = jnp.zeros_like(dg_ref)
            loss_ref[...] = jnp.zeros_like(loss_ref)

        fv = f_ref[...]
        x2 = x1_ref[...] + fv * _rms_rows(fv) * g_ref[...]
        err = x2 - t_ref[...]
        loss_ref[...] += 0.5 * jnp.sum(jnp.mean(err * err, axis=-1, keepdims=True), axis=0, keepdims=True)
        dx2 = err * (1.0 / d)
        dx2_ref[...] = dx2
        df2, dg = _rms_bwd(dx2, fv, g_ref[...])
        df2_ref[...] = df2.astype(BF16)
        dg_ref[...] += dg

    row = pl.BlockSpec((ts, d), lambda i: (i, 0))
    vec = pl.BlockSpec((1, d), lambda i: (0, 0))
    return _call("loss_head", body, (s // ts,), [row, row, row, vec],
                 [row, row, vec, pl.BlockSpec((1, LANES), lambda i: (0, 0))],
                 [jax.ShapeDtypeStruct((s, d), F32), jax.ShapeDtypeStruct((s, d), BF16),
                  jax.ShapeDtypeStruct((1, d), F32), jax.ShapeDtypeStruct((1, LANES), F32)],
                 [], ("arbitrary",), (x1, f2, tgt, g))


def _midnorm_bwd(dx2, dh2, x1, y, g_post, g_pre, comm=None):
    s, d = x1.shape
    ts = _tile(s, 256)

    def body(dx2_ref, dh_ref, x1_ref, y_ref, gp_ref, gn_ref, dx1_ref, dy_ref, dgn_ref, dgp_ref):
        @pl.when(pl.program_id(0) == 0)
        def _():
            dgn_ref[...] = jnp.zeros_like(dgn_ref)
            dgp_ref[...] = jnp.zeros_like(dgp_ref)

        dxa, dgn = _rms_bwd(dh_ref[...], x1_ref[...], gn_ref[...])
        dx1 = dx2_ref[...] + dxa
        dx1_ref[...] = dx1
        dy, dgp = _rms_bwd(dx1, y_ref[...], gp_ref[...])
        dy_ref[...] = dy.astype(BF16)
        dgn_ref[...] += dgn
        dgp_ref[...] += dgp

    row = pl.BlockSpec((ts, d), lambda i: (i, 0))
    vec = pl.BlockSpec((1, d), lambda i: (0, 0))
    return _call("midnorm_bwd", body, (s // ts,), [row, row, row, row, vec, vec], [row, row, vec, vec],
                 [jax.ShapeDtypeStruct((s, d), F32), jax.ShapeDtypeStruct((s, d), BF16),
                  jax.ShapeDtypeStruct((1, d), F32), jax.ShapeDtypeStruct((1, d), F32)],
                 [], ("arbitrary",), (dx2, dh2, x1, y, g_post, g_pre), comm)


def _prenorm_bwd(dx1, dh, x, g, comm=None):
    s, d = x.shape
    ts = _tile(s, 256)

    def body(dx1_ref, dh_ref, x_ref, g_ref, dx_ref, dg_ref):
        @pl.when(pl.program_id(0) == 0)
        def _():
            dg_ref[...] = jnp.zeros_like(dg_ref)

        dxa, dg = _rms_bwd(dh_ref[...], x_ref[...], g_ref[...])
        dx_ref[...] = dx1_ref[...] + dxa
        dg_ref[...] += dg

    row = pl.BlockSpec((ts, d), lambda i: (i, 0))
    vec = pl.BlockSpec((1, d), lambda i: (0, 0))
    return _call("prenorm_bwd", body, (s // ts,), [row, row, row, vec], [row, vec],
                 [jax.ShapeDtypeStruct((s, d), F32), jax.ShapeDtypeStruct((1, d), F32)],
                 [], ("arbitrary",), (dx1, dh, x, g), comm)


def _colsum(a):
    s, n = a.shape
    ts = _tile(s, 256)

    def body(a_ref, o_ref):
        @pl.when(pl.program_id(0) == 0)
        def _():
            o_ref[...] = jnp.zeros_like(o_ref)

        o_ref[...] += jnp.sum(a_ref[...].astype(F32), axis=0, keepdims=True)

    return _call("colsum", body, (s // ts,), [pl.BlockSpec((ts, n), lambda i: (i, 0))],
                 [pl.BlockSpec((1, n), lambda i: (0, 0))], [jax.ShapeDtypeStruct((1, n), F32)], [], ("arbitrary",), (a,))[0]


def _shift_rows(win, off, t):
    n = win.shape[0]
    if off == 0:
        return win[:t]
    return pltpu.roll(win, n - off, axis=0)[:t]


def _conv_fwd(proj, w_pad, b_dw, c_total, comm=None):
    s = proj.shape[0]
    nct = c_total // LANES
    t = _tile(s, 256)

    def body(ga_ref, gb_ref, w_ref, b_ref, o_ref, u0_ref):
        u0_ref[pl.ds(0, CONV_PAD), :] = jnp.zeros((CONV_PAD, LANES), F32)
        u0_ref[pl.ds(CONV_PAD, s), :] = ga_ref[...] * _sigmoid(gb_ref[...])
        wv = w_ref[...]

        def chunk(r, carry):
            r0 = pl.multiple_of(r * t, t)
            win = u0_ref[pl.ds(r0, t + CONV_PAD), :]
            acc = jnp.broadcast_to(b_ref[...], (t, LANES))
            for j in range(CONV_WIDTH):
                acc = acc + wv[j:j + 1, :] * _shift_rows(win, j + CONV_PAD - (CONV_WIDTH - 1), t)
            o_ref[pl.ds(r0, t), :] = acc
            return carry

        lax.fori_loop(0, s // t, chunk, 0)

    return _call("conv_fwd", body, (nct,),
                 [pl.BlockSpec((s, LANES), lambda c: (0, 3 * nct + c)), pl.BlockSpec((s, LANES), lambda c: (0, 4 * nct + c)),
                  pl.BlockSpec((CONV_PAD, LANES), lambda c: (0, c)), pl.BlockSpec((1, LANES), lambda c: (0, c))],
                 [pl.BlockSpec((s, LANES), lambda c: (0, c))], [jax.ShapeDtypeStruct((s, c_total), F32)],
                 [pltpu.VMEM((s + CONV_PAD, LANES), F32)], ("parallel",), (proj, proj, w_pad, b_dw), comm)


def _conv_bwd(du1, proj, w_pad, c_total, comm=None):
    s = proj.shape[0]
    nct = c_total // LANES
    t = _tile(s, 256)

    def body(d_ref, ga_ref, gb_ref, w_ref, dga_ref, dgb_ref, dw_ref, db_ref, u0_ref, dp_ref):
        sg = _sigmoid(gb_ref[...])
        u0_ref[pl.ds(0, CONV_PAD), :] = jnp.zeros((CONV_PAD, LANES), F32)
        u0_ref[pl.ds(CONV_PAD, s), :] = ga_ref[...] * sg
        dp_ref[pl.ds(0, s), :] = d_ref[...]
        dp_ref[pl.ds(s, CONV_PAD), :] = jnp.zeros((CONV_PAD, LANES), F32)
        dw_ref[...] = jnp.zeros_like(dw_ref)
        db_ref[...] = jnp.sum(d_ref[...], axis=0, keepdims=True)
        wv = w_ref[...]

        def chunk(r, carry):
            r0 = pl.multiple_of(r * t, t)
            win = u0_ref[pl.ds(r0, t + CONV_PAD), :]
            dwin = dp_ref[pl.ds(r0, t + CONV_PAD), :]
            dcur = dwin[:t]
            du0 = jnp.zeros((t, LANES), F32)
            for j in range(CONV_WIDTH):
                du0 = du0 + wv[j:j + 1, :] * _shift_rows(dwin, CONV_WIDTH - 1 - j, t)
                sh = _shift_rows(win, j + CONV_PAD - (CONV_WIDTH - 1), t)
                dw_ref[j:j + 1, :] += jnp.sum(dcur * sh, axis=0, keepdims=True)
            gav = ga_ref[pl.ds(r0, t), :]
            sgv = _sigmoid(gb_ref[pl.ds(r0, t), :])
            dga_ref[pl.ds(r0, t), :] = (du0 * sgv).astype(BF16)
            dgb_ref[pl.ds(r0, t), :] = (du0 * gav * sgv * (1.0 - sgv)).astype(BF16)
            return carry

        lax.fori_loop(0, s // t, chunk, 0)

    col = pl.BlockSpec((s, LANES), lambda c: (0, c))
    return _call("conv_bwd", body, (nct,),
                 [col, pl.BlockSpec((s, LANES), lambda c: (0, 3 * nct + c)),
                  pl.BlockSpec((s, LANES), lambda c: (0, 4 * nct + c)), pl.BlockSpec((CONV_PAD, LANES), lambda c: (0, c))],
                 [col, col, pl.BlockSpec((CONV_PAD, LANES), lambda c: (0, c)), pl.BlockSpec((1, LANES), lambda c: (0, c))],
                 [jax.ShapeDtypeStruct((s, c_total), BF16), jax.ShapeDtypeStruct((s, c_total), BF16),
                  jax.ShapeDtypeStruct((CONV_PAD, c_total), F32), jax.ShapeDtypeStruct((1, c_total), F32)],
                 [pltpu.VMEM((s + CONV_PAD, LANES), F32), pltpu.VMEM((s + CONV_PAD, LANES), F32)],
                 ("parallel",), (du1, proj, proj, w_pad), comm)


TQ_PREF = 256
NU = 4
NU_BWD = 2
TK = 256


def _split_dot(v, tri):
    hi = v.astype(BF16)
    lo = (v - hi.astype(F32)).astype(BF16)
    return (jnp.dot(hi, tri, preferred_element_type=F32) + jnp.dot(lo, tri, preferred_element_type=F32))


def _causal_mask(i, j, tq):
    tpos = i * tq + lax.broadcasted_iota(jnp.int32, (tq, TK), 0)
    spos = j * TK + lax.broadcasted_iota(jnp.int32, (tq, TK), 1)
    return spos < tpos


def _log_terms(z, mask):
    sp = jnp.log(1.0 + jnp.exp(-jnp.abs(z)))
    return jnp.minimum(z, 0.0) - sp, jnp.where(mask, -jnp.maximum(z, 0.0) - sp, 0.0)


def _tri(after):
    r = lax.broadcasted_iota(jnp.int32, (TK, TK), 0)
    c = lax.broadcasted_iota(jnp.int32, (TK, TK), 1)
    return (r > c).astype(BF16) if after else (r < c).astype(BF16)


def _attn_fwd(proj, n_heads, comm=None):
    s = proj.shape[0]
    tq = _tile(s, TQ_PREF)
    scale = 1.0 / math.sqrt(HEAD_DIM)
    ratio = tq // TK

    def body(q_ref, k_ref, v_ref, o_ref, ot_ref, acc_ref, *clms):
        i = pl.program_id(1)
        heads = [slice(u * HEAD_DIM, (u + 1) * HEAD_DIM) for u in range(NU)]
        qs = [q_ref[:, hs].astype(BF16) for hs in heads]
        tri_after = _tri(True)
        acc_ref[...] = jnp.zeros_like(acc_ref)
        for cr in clms:
            cr[...] = jnp.zeros_like(cr)
        nkb = (i + 1) * ratio

        def step(jj, carry):
            j = nkb - 1 - jj
            rows = pl.ds(pl.multiple_of(j * TK, TK), TK)
            mask = _causal_mask(i, j, tq)
            zs = [lax.dot_general(qs[u], k_ref[rows, hs].astype(BF16), (((1,), (1,)), ((), ())),
                                  preferred_element_type=F32) * scale for u, hs in enumerate(heads)]
            lls = [_log_terms(z, mask) for z in zs]
            sufs = [clms[u][...] + _split_dot(lls[u][1], tri_after) for u in range(NU)]
            for u, hs in enumerate(heads):
                a = jnp.where(mask, jnp.exp(lls[u][0] + sufs[u]), 0.0)
                acc_ref[:, hs] += jnp.dot(a.astype(BF16), v_ref[rows, hs].astype(BF16), preferred_element_type=F32)
                clms[u][...] += jnp.sum(lls[u][1], axis=1, keepdims=True)
            return carry

        lax.fori_loop(0, nkb, step, 0)
        o = acc_ref[...]
        o_ref[...] = o
        ot_ref[...] = o.T.astype(BF16)

    w = NU * HEAD_DIM
    ng = n_heads // NU
    return _call("attn_fwd", body, (ng, s // tq),
                 [pl.BlockSpec((tq, w), lambda h, i: (i, h)),
                  pl.BlockSpec((s, w), lambda h, i: (0, ng + h)),
                  pl.BlockSpec((s, w), lambda h, i: (0, 2 * ng + h))],
                 [pl.BlockSpec((tq, w), lambda h, i: (i, h)), pl.BlockSpec((w, tq), lambda h, i: (h, i))],
                 [jax.ShapeDtypeStruct((s, n_heads * HEAD_DIM), F32), jax.ShapeDtypeStruct((n_heads * HEAD_DIM, s), BF16)],
                 [pltpu.VMEM((tq, w), F32), *[pltpu.VMEM((tq, 1), F32)] * NU],
                 ("parallel", "arbitrary"), (proj, proj, proj), comm)


def _attn_bwd(proj, do_sb, n_heads, comm=None):
    s = proj.shape[0]
    tq = _tile(s, TQ_PREF)
    scale = 1.0 / math.sqrt(HEAD_DIM)
    ratio = tq // TK
    n_kb = s // TK
    n_qb = s // tq
    nu = NU_BWD
    heads = [slice(u * HEAD_DIM, (u + 1) * HEAD_DIM) for u in range(nu)]
    nt_dims = (((1,), (1,)), ((), ()))

    def body(q_ref, k_ref, v_ref, do_ref, dq_ref, dk_ref, dv_ref, dka_ref, dva_ref, dl_ref, be_ref, dqa_ref, *c_refs):
        i = pl.program_id(1)

        @pl.when(i == 0)
        def _():
            dka_ref[...] = jnp.zeros_like(dka_ref)
            dva_ref[...] = jnp.zeros_like(dva_ref)

        qs = [q_ref[:, hs].astype(BF16) for hs in heads]
        dobs = [do_ref[:, hs].astype(BF16) for hs in heads]
        tri_after = _tri(True)
        tri_before = _tri(False)
        nkb = (i + 1) * ratio

        for cr in c_refs:
            cr[...] = jnp.zeros_like(cr)

        def sweep_a(jj, carry):
            j = nkb - 1 - jj
            rows = pl.ds(pl.multiple_of(j * TK, TK), TK)
            mask = _causal_mask(i, j, tq)
            zs = [lax.dot_general(qs[u], k_ref[rows, hs].astype(BF16), nt_dims, preferred_element_type=F32) * scale
                  for u, hs in enumerate(heads)]
            das = [lax.dot_general(dobs[u], v_ref[rows, hs].astype(BF16), nt_dims, preferred_element_type=F32)
                   for u, hs in enumerate(heads)]
            lls = [_log_terms(z, mask) for z in zs]
            sufs = [c_refs[u][...] + _split_dot(lls[u][1], tri_after) for u in range(nu)]
            for u, hs in enumerate(heads):
                a = jnp.where(mask, jnp.exp(lls[u][0] + sufs[u]), 0.0)
                dl_ref[u, j] = das[u] * a
                be_ref[u, j] = jnp.exp(lls[u][0])
                dva_ref[rows, hs] += jnp.dot(a.T.astype(BF16), dobs[u], preferred_element_type=F32)
                c_refs[u][...] += jnp.sum(lls[u][1], axis=1, keepdims=True)
            return carry

        lax.fori_loop(0, nkb, sweep_a, 0)

        for cr in c_refs:
            cr[...] = jnp.zeros_like(cr)
        dqa_ref[...] = jnp.zeros_like(dqa_ref)

        def sweep_b(j, carry):
            rows = pl.ds(pl.multiple_of(j * TK, TK), TK)
            mask = _causal_mask(i, j, tq)
            dls = [dl_ref[u, j] for u in range(nu)]
            ps = [c_refs[u][...] + _split_dot(dls[u], tri_before) for u in range(nu)]
            for u, hs in enumerate(heads):
                beta = be_ref[u, j]
                dz = jnp.where(mask, (dls[u] * (1.0 - beta) - beta * ps[u]) * scale, 0.0)
                dqa_ref[:, hs] += jnp.dot(dz.astype(BF16), k_ref[rows, hs].astype(BF16), preferred_element_type=F32)
                dka_ref[rows, hs] += jnp.dot(dz.T.astype(BF16), qs[u], preferred_element_type=F32)
                c_refs[u][...] += jnp.sum(dls[u], axis=1, keepdims=True)
            return carry

        lax.fori_loop(0, nkb, sweep_b, 0)
        dq_ref[...] = dqa_ref[...].astype(BF16)

        @pl.when(i == n_qb - 1)
        def _():
            dk_ref[...] = dka_ref[...].astype(BF16)
            dv_ref[...] = dva_ref[...].astype(BF16)

    w = nu * HEAD_DIM
    ng = n_heads // nu
    qblk = pl.BlockSpec((tq, w), lambda h, i: (i, h))
    full = pl.BlockSpec((s, w), lambda h, i: (0, h))
    sds = jax.ShapeDtypeStruct((s, n_heads * HEAD_DIM), BF16)
    return _call("attn_bwd", body, (ng, n_qb),
                 [qblk, pl.BlockSpec((s, w), lambda h, i: (0, ng + h)), pl.BlockSpec((s, w), lambda h, i: (0, 2 * ng + h)),
                  qblk],
                 [qblk, full, full], [sds, sds, sds],
                 [pltpu.VMEM((s, w), F32), pltpu.VMEM((s, w), F32),
                  pltpu.VMEM((nu, n_kb, tq, TK), F32), pltpu.VMEM((nu, n_kb, tq, TK), F32),
                  pltpu.VMEM((tq, w), F32), *[pltpu.VMEM((tq, 1), F32)] * nu],
                 ("parallel", "arbitrary"), (proj, proj, proj, do_sb), comm)


def _adamw(name, w, m, v, parts, part_specs, tr, comm=None):
    r, c = w.shape
    n_parts = len(parts)

    def body(*refs):
        w_ref, m_ref, v_ref = refs[:3]
        p_refs = refs[3:3 + n_parts]
        g_ref, d_ref, nm_ref, nv_ref = refs[3 + n_parts:]
        g = p_refs[0][...].astype(F32)
        for p in p_refs[1:]:
            g = g + p[...].astype(F32)
        nm = ADAM_B1 * m_ref[...] + (1.0 - ADAM_B1) * g
        nv = ADAM_B2 * v_ref[...] + (1.0 - ADAM_B2) * jnp.square(g)
        m_hat = nm / (1.0 - ADAM_B1 ** ADAM_STEP)
        v_hat = nv / (1.0 - ADAM_B2 ** ADAM_STEP)
        g_ref[...] = g
        d_ref[...] = -ADAM_LR * (m_hat / (jnp.sqrt(v_hat) + ADAM_EPS) + ADAM_WD * w_ref[...])
        nm_ref[...] = nm
        nv_ref[...] = nv

    blk = pl.BlockSpec((tr, c), lambda i: (i, 0))
    sds = jax.ShapeDtypeStruct((r, c), F32)
    return _call(name, body, (r // tr,), [blk, blk, blk, *part_specs], [blk] * 4, [sds] * 4, [], ("parallel",),
                 (w, m, v, *parts), comm)


def _adamw_big(name, w, m, v, recv, comm=None):
    r, c = w.shape
    tr = _tile(r, 128)
    order = (3, 0, 1, 2)
    specs = [pl.BlockSpec((None, tr, c), functools.partial(lambda i, slot: (slot, i, 0), slot=sl)) for sl in order]
    return _adamw(name, w, m, v, [recv] * 4, specs, tr, comm)


def _adamw_small(name, w, m, v, g):
    r, c = w.shape
    return _adamw(name, w, m, v, [g], [pl.BlockSpec((r, c), lambda i: (0, 0))], r)


def kernel(x, g_pre_mix, w_in, b_in, w_dw, b_dw, g_conv_ln, b_conv_ln, w_sb_out, w_conv_out, w_o, g_post_mix, g_pre_mlp, w_up, w_down, g_post_mlp, loss_target, m_g_pre_mix, m_w_in, m_b_in, m_w_dw, m_b_dw, m_g_conv_ln, m_b_conv_ln, m_w_sb_out, m_w_conv_out, m_w_o, m_g_post_mix, m_g_pre_mlp, m_w_up, m_w_down, m_g_post_mlp, v_g_pre_mix, v_w_in, v_b_in, v_w_dw, v_b_dw, v_g_conv_ln, v_b_conv_ln, v_w_sb_out, v_w_conv_out, v_w_o, v_g_post_mix, v_g_pre_mlp, v_w_up, v_w_down, v_g_post_mlp):
    xs, tgt = x[0], loss_target[0]
    s, d = xs.shape
    d_half = d // 2
    n_heads = d_half // HEAD_DIM
    d_ff = NDEV * w_up.shape[2]
    core = lax.axis_index("c").astype(jnp.int32).reshape(1)
    dev = 4 * lax.axis_index("x") + 2 * lax.axis_index("y") + lax.axis_index("c")

    w_dw_pad = jnp.pad(w_dw[0], ((0, CONV_PAD - CONV_WIDTH), (0, 0)))
    sh_in, sh_sb, sh_cv, sh_o, sh_up, sh_down = [w[0].astype(BF16) for w in (w_in, w_sb_out, w_conv_out, w_o, w_up, w_down)]
    px, py, pc = _position()
    order = jnp.stack([4 * (px ^ fx) + 2 * (py ^ fy) + (pc ^ fc) for fx, fy, fc in GATHER_FLIPS]).astype(jnp.int32)

    ag_up = _Chunked(_ag_comm, [sh_up], sh_up.shape[0], 8)
    h, h_t = _prenorm(xs, g_pre_mix)
    proj, wg_in = _proj_gather(h, sh_in, b_in, order)
    o_sb, o_sb_t, wg_sb, wg_cv, wg_o, wg_dw, *part = _attn_fwd(
        proj, n_heads, _join(_ag_comm([sh_sb, sh_cv, sh_o, w_dw_pad]), ag_up.take(4)))
    ag_up.done(part)
    wf_dw = wg_dw.transpose(1, 0, 2).reshape(CONV_PAD, d_half)
    wf_o = wg_o.reshape(d, d)
    u1, *part = _conv_fwd(proj, wf_dw, b_dw, d_half, ag_up.take())
    ag_up.done(part)
    u3, u3_t = _ln_silu(u1, g_conv_ln, b_conv_ln)
    o_sbp = _mm_cols("sb_out", o_sb, wg_sb)[0]
    o_cv = _mm_cols("conv_out", u3, wg_cv)[0]
    merged, merged_t, *part = _merge(proj, o_sbp, o_cv, d, ag_up.take())
    ag_up.done(part)
    y, *part = _mm_plain("w_o", merged, wf_o, False, F32, comm=ag_up.take())
    ag_up.done(part)
    x1, h2, h2_t, wg_up = _postnorm_mix(xs, y, g_post_mix, g_pre_mlp, ag_up.take())

    tm_up = _tile(s, 1024)
    ns_up = wg_up.shape[2]
    tk_up = _tile(d, 2048)

    def up_epilogue(acc):
        f = jnp.square(jnp.maximum(acc, 0.0))
        return acc, f, f.T

    a_act, f, f_t, wg_down = _matmul(
        "w_up", h2, wg_up,
        [jax.ShapeDtypeStruct((s, d_ff), BF16), jax.ShapeDtypeStruct((s, d_ff), BF16), jax.ShapeDtypeStruct((d_ff, s), BF16)],
        (s // tm_up, NDEV, d // tk_up),
        pl.BlockSpec((tm_up, tk_up), lambda i, dd, kk: (i, kk)),
        pl.BlockSpec((None, tk_up, ns_up), lambda i, dd, kk: (dd, kk, 0)),
        [pl.BlockSpec((tm_up, ns_up), lambda i, dd, kk: (i, dd)), pl.BlockSpec((tm_up, ns_up), lambda i, dd, kk: (i, dd)),
         pl.BlockSpec((ns_up, tm_up), lambda i, dd, kk: (dd, i))],
        1, False, (tm_up, ns_up), epilogue=up_epilogue, comm=_ag_comm([sh_down]))
    wf_down = wg_down.reshape(d_ff, d)
    f2 = _mm_plain("w_down", f, wf_down, False, F32)[0]
    dx2, df2, dg_post_mlp, loss_part = _loss_head(x1, f2, tgt, g_post_mlp)

    tm_b, tn_b = _tile(s, 1024), _tile(d_ff, 1024)
    da = _mm_plain("w_down_bwd", df2, wf_down, True, BF16,
                   extra=(a_act,), extra_specs=(pl.BlockSpec((tm_b, tn_b), lambda i, j, kk: (i, j)),),
                   epilogue=lambda acc, av: (acc * (2.0 * jnp.maximum(av.astype(F32), 0.0)),),
                   outs=[jax.ShapeDtypeStruct((s, d_ff), BF16)],
                   out_specs=[pl.BlockSpec((tm_b, tn_b), lambda i, j, kk: (i, j))])[0]
    gw_down = _mm_plain("w_down_grad", f_t, df2, False, BF16)[0]
    big_down = gw_down.reshape(4, 2, d_ff // NDEV, d)
    gw_up, sib_down = _mm_dw_cols("w_up_grad", h2_t, da, comm=_sibling_comm([big_down]))
    big_up = gw_up.reshape(4, 2, d, d_ff // NDEV)
    dh2, sib_up = _mm_cols_t("w_up_bwd", da, wg_up, comm=_sibling_comm([big_up]))
    rs_down = _Chunked(_chips_comm, [_pair_sum(big_down, sib_down, core)], d_ff // NDEV, 8)
    rs_up = _Chunked(_chips_comm, [_pair_sum(big_up, sib_up, core)], d, 8)
    dx1, dy, dg_pre_mlp, dg_post_mix, *part = _midnorm_bwd(dx2, dh2, x1, y, g_post_mix, g_pre_mlp, rs_down.take())
    rs_down.done(part)
    gw_o, *part = _mm_plain("w_o_grad", merged_t, dy, False, BF16, comm=rs_down.take())
    rs_down.done(part)
    dmerged, *part = _mm_plain("w_o_bwd", dy, wf_o, True, F32, comm=rs_down.take())
    rs_down.done(part)
    do_sbp, do_cv, dgate_sb, dgate_cv, *part = _merge_bwd(dmerged, proj, o_sbp, o_cv, d, rs_down.take())
    rs_down.done(part)
    gw_cv = _mm_dw_cols("conv_out_grad", u3_t, do_cv)[0]
    gw_sb = _mm_dw_cols("sb_out_grad", o_sb_t, do_sbp)[0]
    du3 = _mm_cols_t("conv_out_bwd", do_cv, wg_cv)[0]
    do_sb = _mm_cols_t("sb_out_bwd", do_sbp, wg_sb)[0]
    du1, dg_ln, db_ln = _ln_silu_bwd(du3, u1, g_conv_ln, b_conv_ln)
    dglu_a, dglu_b, dw_dw, db_dw, *part = _conv_bwd(du1, proj, wf_dw, d_half, rs_up.take(2))
    rs_up.done(part)
    big_mid = [gw_sb.reshape(4, 2, d_half, d // NDEV), gw_cv.reshape(4, 2, d_half, d // NDEV),
               gw_o.reshape(4, 2, d // NDEV, d)]
    sib_mid = _run_comm("exchange_sibling_mid", _sibling_comm(big_mid))
    sums_mid = [_pair_sum(g, r, core) for g, r in zip(big_mid, sib_mid)]
    dq, dk, dv, r_up, r_sb, r_cv, r_o = _attn_bwd(proj, do_sb, n_heads, _join(rs_up.take(6), _chips_comm(sums_mid)))
    dproj = jnp.concatenate([dq, dk, dv, dglu_a, dglu_b, dgate_sb, dgate_cv], axis=1)
    db_in = _colsum(dproj)
    gw_in, r_down = _mm_dw_cols("w_in_grad", h_t, dproj, comm=rs_down.take(4))
    big_in = gw_in.reshape(4, 2, d, gw_in.shape[2])
    sib_in, = _run_comm("exchange_sibling_in", _sibling_comm([big_in]))
    dh, r_in = _mm_cols_t("w_in_bwd", dproj, wg_in, comm=_chips_comm([_pair_sum(big_in, sib_in, core)]))
    grad_x, dg_pre_mix = _prenorm_bwd(dx1, dh, xs, g_pre_mix)

    small = [dg_pre_mix, db_in, dw_dw.reshape(1, -1), db_dw, dg_ln, db_ln, dg_post_mix, dg_pre_mlp, dg_post_mlp]
    sizes = [a.shape[1] for a in small]
    packed = jnp.concatenate(small, axis=1).reshape(-1, LANES)
    total = _sum_small(_all_gather_small(packed)).reshape(1, -1)
    offs = [0]
    for n in sizes:
        offs.append(offs[-1] + n)
    (g_g_pre_mix, g_b_in, g_w_dw_flat, g_b_dw, g_g_conv_ln, g_b_conv_ln, g_g_post_mix, g_g_pre_mlp,
     g_g_post_mlp) = [total[:, offs[k]:offs[k + 1]] for k in range(len(sizes))]
    ch = w_dw.shape[2]
    g_w_dw = lax.dynamic_slice_in_dim(g_w_dw_flat.reshape(CONV_PAD, d_half), dev * ch, ch, axis=1)[:CONV_WIDTH]

    loss = lax.psum(loss_part[0, 0], ("x", "y", "c"))

    res = {}
    res["w_up"] = _adamw_big("adamw_w_up", w_up[0], m_w_up[0], v_w_up[0], r_up)
    res["w_down"] = _adamw_big("adamw_w_down", w_down[0], m_w_down[0], v_w_down[0], r_down)
    res["g_pre_mix"] = _adamw_small("adamw_g_pre_mix", g_pre_mix, m_g_pre_mix, v_g_pre_mix, g_g_pre_mix)
    res["w_in"] = _adamw_big("adamw_w_in", w_in[0], m_w_in[0], v_w_in[0], r_in)
    res["b_in"] = _adamw_small("adamw_b_in", b_in, m_b_in, v_b_in, g_b_in)
    res["w_dw"] = _adamw_small("adamw_w_dw", w_dw[0], m_w_dw[0], v_w_dw[0], g_w_dw)
    res["b_dw"] = _adamw_small("adamw_b_dw", b_dw, m_b_dw, v_b_dw, g_b_dw)
    res["g_conv_ln"] = _adamw_small("adamw_g_conv_ln", g_conv_ln, m_g_conv_ln, v_g_conv_ln, g_g_conv_ln)
    res["b_conv_ln"] = _adamw_small("adamw_b_conv_ln", b_conv_ln, m_b_conv_ln, v_b_conv_ln, g_b_conv_ln)
    res["w_sb_out"] = _adamw_big("adamw_w_sb_out", w_sb_out[0], m_w_sb_out[0], v_w_sb_out[0], r_sb)
    res["w_conv_out"] = _adamw_big("adamw_w_conv_out", w_conv_out[0], m_w_conv_out[0], v_w_conv_out[0], r_cv)
    res["w_o"] = _adamw_big("adamw_w_o", w_o[0], m_w_o[0], v_w_o[0], r_o)
    res["g_post_mix"] = _adamw_small("adamw_g_post_mix", g_post_mix, m_g_post_mix, v_g_post_mix, g_g_post_mix)
    res["g_pre_mlp"] = _adamw_small("adamw_g_pre_mlp", g_pre_mlp, m_g_pre_mlp, v_g_pre_mlp, g_g_pre_mlp)
    res["g_post_mlp"] = _adamw_small("adamw_g_post_mlp", g_post_mlp, m_g_post_mlp, v_g_post_mlp, g_g_post_mlp)

    names = ["g_pre_mix", "w_in", "b_in", "w_dw", "b_dw", "g_conv_ln", "b_conv_ln", "w_sb_out", "w_conv_out", "w_o",
             "g_post_mix", "g_pre_mlp", "w_up", "w_down", "g_post_mlp"]
    three_d = {"w_in", "w_dw", "w_sb_out", "w_conv_out", "w_o", "w_up", "w_down"}

    def shaped(nm, arr):
        return arr[None] if nm in three_d else arr

    out = [loss, grad_x[None]]
    for k in range(4):
        out += [shaped(nm, res[nm][k]) for nm in names]
    return tuple(out)
```

```python
import functools
import math

import jax
import jax.numpy as jnp
from jax import lax
from jax.experimental import pallas as pl
from jax.experimental.pallas import tpu as pltpu

F32 = jnp.float32
BF16 = jnp.bfloat16
NDEV = 8
LANES = 128
EPS = 1e-6
CONV_WIDTH = 31
CONV_PAD = 32
HEAD_DIM = 128
ADAM_LR = 0.001
ADAM_B1 = 0.9
ADAM_B2 = 0.999
ADAM_EPS = 1e-08
ADAM_WD = 0.01
ADAM_STEP = 10
VMEM_LIMIT = 56 * 1024 * 1024
MESH = pl.DeviceIdType.MESH
ANY = pl.BlockSpec(memory_space=pl.ANY)


def _tile(n, pref):
    t = min(n, pref)
    assert n % t == 0, (n, t)
    return t


def _sigmoid(v):
    return 1.0 / (1.0 + jnp.exp(-v))


def _position():
    return lax.axis_index("x"), lax.axis_index("y"), lax.axis_index("c")


class _Comm:
    def __init__(self, ins, outs, scratch, start, finish, aliases=None, mid=None):
        self.ins, self.outs, self.scratch = list(ins), list(outs), list(scratch)
        self.start, self.finish, self.aliases = start, finish, dict(aliases or {})
        self.mid = mid


_NO_COMM = _Comm([], [], [], None, None)


def _call(name, body, grid, in_specs, out_specs, out_shape, scratch_shapes, sem, args, comm=None):
    comm = comm or _NO_COMM
    n_in, n_out, n_scr = len(in_specs), len(out_specs), len(scratch_shapes)
    n_cin, n_cout = len(comm.ins), len(comm.outs)

    n_steps = math.prod(grid)
    mid_step = n_steps - max(1, n_steps // 8) if (comm.mid is not None and n_steps >= 4) else None

    def edge(c_ins, c_outs, c_scr, at_step, actions):
        linear = 0
        for ax, g in enumerate(grid):
            linear = linear * g + pl.program_id(ax)

        @pl.when(linear == at_step)
        def _():
            for act in actions:
                act(c_ins, c_outs, c_scr)

    def wrapped(*refs):
        ins, c_ins = refs[:n_in], refs[n_in:n_in + n_cin]
        pos = n_in + n_cin
        outs, c_outs = refs[pos:pos + n_out], refs[pos + n_out:pos + n_out + n_cout]
        pos += n_out + n_cout
        scr, c_scr = refs[pos:pos + n_scr], refs[pos + n_scr:]
        if n_cin:
            edge(c_ins, c_outs, c_scr, 0, [comm.start])
            if mid_step is not None:
                edge(c_ins, c_outs, c_scr, mid_step, [comm.mid])
        body(*ins, *outs, *scr)
        if n_cin:
            late = [comm.finish] if (mid_step is not None or comm.mid is None) else [comm.mid, comm.finish]
            edge(c_ins, c_outs, c_scr, n_steps - 1, late)

    if n_cin:
        sem = ("arbitrary",) * len(grid)
    return pl.pallas_call(
        wrapped, name=name, grid=grid,
        in_specs=[*in_specs, *[ANY] * n_cin], out_specs=[*out_specs, *[ANY] * n_cout],
        out_shape=[*out_shape, *comm.outs], scratch_shapes=[*scratch_shapes, *comm.scratch],
        input_output_aliases={n_in + ci: n_out + co for ci, co in comm.aliases.items()},
        compiler_params=pltpu.CompilerParams(dimension_semantics=sem, vmem_limit_bytes=VMEM_LIMIT),
    )(*args, *comm.ins)


def _run_comm(name, comm):
    n_in, n_out = len(comm.ins), len(comm.outs)

    def body(*refs):
        ins, outs, scr = refs[:n_in], refs[n_in:n_in + n_out], refs[n_in + n_out:]
        comm.start(ins, outs, scr)
        if comm.mid is not None:
            comm.mid(ins, outs, scr)
        comm.finish(ins, outs, scr)

    return pl.pallas_call(body, name=name, in_specs=[ANY] * n_in, out_specs=[ANY] * n_out, out_shape=comm.outs,
                          scratch_shapes=comm.scratch,
                          input_output_aliases=comm.aliases)(*comm.ins)


def _rows_of(ref, rows):
    return ref if rows is None else ref.at[pl.ds(rows[0], rows[1])]


def _ag_comm(shards, rows=None, into=None):
    n = len(shards)

    def parts(ins, outs, scr):
        send_sems, recv_sems, local_sems = scr
        x, y, c = _position()
        chips = [(1 - x, y), (x, 1 - y), (1 - x, 1 - y)]

        def copy(a, k, block, to, own=False):
            px, py, pc = block
            dst = _rows_of(outs[a].at[4 * px + 2 * py + pc], rows)
            return pltpu.make_async_remote_copy(src_ref=_rows_of(ins[a], rows) if own else dst, dst_ref=dst,
                                                send_sem=send_sems.at[a, k], recv_sem=recv_sems.at[a, k],
                                                device_id=to, device_id_type=MESH)

        mine = [pltpu.make_async_copy(_rows_of(ins[a], rows), _rows_of(outs[a].at[4 * x + 2 * y + c], rows),
                                      local_sems.at[a]) for a in range(n)]
        first = []
        for a in range(n):
            first.append(copy(a, 0, (x, y, c), (x, y, 1 - c), own=True))
            first += [copy(a, 1 + j, (x, y, c), (*chip, c), own=True) for j, chip in enumerate(chips)]
        return copy, mine, first, chips, (x, y, c), (x, y, 1 - c)

    def start(ins, outs, scr):
        _, mine, first, _, _, _ = parts(ins, outs, scr)
        for cp in mine + first:
            cp.start()

    def mid(ins, outs, scr):
        copy, _, _, chips, me, sibling = parts(ins, outs, scr)
        for a in range(n):
            for j, chip in enumerate(chips):
                copy(a, 1 + j, (*chip, me[2]), me).wait_recv()
                copy(a, 4 + j, (*chip, me[2]), sibling).start()

    def finish(ins, outs, scr):
        copy, mine, first, chips, me, sibling = parts(ins, outs, scr)
        c = me[2]
        passed = [copy(a, 4 + j, (*chip, c), sibling) for a in range(n) for j, chip in enumerate(chips)]
        for a in range(n):
            copy(a, 0, sibling, me).wait_recv()
            for j, chip in enumerate(chips):
                copy(a, 4 + j, (*chip, 1 - c), me).wait_recv()
        for cp in first + passed:
            cp.wait_send()
        for cp in mine:
            cp.wait()

    return _Comm([*shards, *(into or [])], [jax.ShapeDtypeStruct((NDEV, *sh.shape), sh.dtype) for sh in shards],
                 [pltpu.SemaphoreType.DMA((n, 7)), pltpu.SemaphoreType.DMA((n, 7)), pltpu.SemaphoreType.DMA((n,))],
                 start, finish, {n + a: a for a in range(n)} if into else None, mid)


def _sibling_comm(grads):
    n = len(grads)

    def copies(ins, outs, scr):
        send_sems, recv_sems = scr
        x, y, c = _position()
        return [pltpu.make_async_remote_copy(src_ref=ins[a].at[k, 1 - c], dst_ref=outs[a].at[k],
                                             send_sem=send_sems.at[a, k], recv_sem=recv_sems.at[a, k],
                                             device_id=(x, y, 1 - c), device_id_type=MESH)
                for a in range(n) for k in range(4)]

    def start(ins, outs, scr):
        for cp in copies(ins, outs, scr):
            cp.start()

    def finish(ins, outs, scr):
        for cp in copies(ins, outs, scr):
            cp.wait()

    return _Comm(grads, [jax.ShapeDtypeStruct((4, *g.shape[2:]), g.dtype) for g in grads],
                 [pltpu.SemaphoreType.DMA((n, 4)), pltpu.SemaphoreType.DMA((n, 4))], start, finish)


def _chips_comm(sums, rows=None, into=None):
    n = len(sums)

    def copies(ins, outs, scr):
        send_sems, recv_sems, local_sems = scr
        x, y, c = _position()
        chips = [(1 - x, y), (x, 1 - y), (1 - x, 1 - y)]
        own = [pltpu.make_async_copy(_rows_of(ins[a].at[2 * x + y], rows), _rows_of(outs[a].at[3], rows), local_sems.at[a])
               for a in range(n)]
        remote = [pltpu.make_async_remote_copy(src_ref=_rows_of(ins[a].at[2 * px + py], rows),
                                               dst_ref=_rows_of(outs[a].at[j], rows),
                                               send_sem=send_sems.at[a, j], recv_sem=recv_sems.at[a, j],
                                               device_id=(px, py, c), device_id_type=MESH)
                  for a in range(n) for j, (px, py) in enumerate(chips)]
        return own + remote

    def start(ins, outs, scr):
        for cp in copies(ins, outs, scr):
            cp.start()

    def finish(ins, outs, scr):
        for cp in copies(ins, outs, scr):
            cp.wait()

    return _Comm([*sums, *(into or [])], [jax.ShapeDtypeStruct(sm.shape, sm.dtype) for sm in sums],
                 [pltpu.SemaphoreType.DMA((n, 3)), pltpu.SemaphoreType.DMA((n, 3)), pltpu.SemaphoreType.DMA((n,))],
                 start, finish, {n + a: a for a in range(n)} if into else None)


def _join(c1, c2):
    n_in, n_out, n_scr = len(c1.ins), len(c1.outs), len(c1.scratch)
    aliases = dict(c1.aliases)
    aliases.update({n_in + ci: n_out + co for ci, co in c2.aliases.items()})

    def start(ins, outs, scr):
        c1.start(ins[:n_in], outs[:n_out], scr[:n_scr])
        c2.start(ins[n_in:], outs[n_out:], scr[n_scr:])

    def mid(ins, outs, scr):
        if c1.mid is not None:
            c1.mid(ins[:n_in], outs[:n_out], scr[:n_scr])
        if c2.mid is not None:
            c2.mid(ins[n_in:], outs[n_out:], scr[n_scr:])

    def finish(ins, outs, scr):
        c1.finish(ins[:n_in], outs[:n_out], scr[:n_scr])
        c2.finish(ins[n_in:], outs[n_out:], scr[n_scr:])

    return _Comm(c1.ins + c2.ins, c1.outs + c2.outs, c1.scratch + c2.scratch, start, finish, aliases,
                 mid if (c1.mid is not None or c2.mid is not None) else None)


class _Chunked:
    def __init__(self, make, arrays, n_rows, n_chunks):
        self.make, self.arrays, self.into = make, arrays, None
        step = n_rows // n_chunks
        assert step * n_chunks == n_rows
        self.todo = [(k * step, step) for k in range(n_chunks)]

    def take(self, count=1):
        r0, nr = self.todo[0][0], sum(t[1] for t in self.todo[:count])
        self.todo = self.todo[count:]
        return self.make(self.arrays, (r0, nr), self.into)

    def done(self, outs):
        self.into = list(outs)
        return self.into


def _all_gather_small(part):
    def body(in_ref, out_ref, send_sems, recv_sems, local_sem):
        x, y, c = _position()
        me = 4 * x + 2 * y + c
        mine = pltpu.make_async_copy(in_ref, out_ref.at[me], local_sem)
        mine.start()
        flips = [(fx, fy, fc) for fx in (0, 1) for fy in (0, 1) for fc in (0, 1)][1:]
        copies = []
        for k, (fx, fy, fc) in enumerate(flips):
            cp = pltpu.make_async_remote_copy(src_ref=in_ref, dst_ref=out_ref.at[me], send_sem=send_sems.at[k],
                                              recv_sem=recv_sems.at[k],
                                              device_id=(x ^ fx, y ^ fy, c ^ fc), device_id_type=MESH)
            cp.start()
            copies.append(cp)
        for k, (fx, fy, fc) in enumerate(flips):
            peer = 4 * (x ^ fx) + 2 * (y ^ fy) + (c ^ fc)
            pltpu.make_async_remote_copy(src_ref=in_ref, dst_ref=out_ref.at[peer], send_sem=send_sems.at[k],
                                         recv_sem=recv_sems.at[k], device_id=(x, y, c), device_id_type=MESH).wait_recv()
        for cp in copies:
            cp.wait_send()
        mine.wait()

    return pl.pallas_call(
        body, name="all_gather_small", in_specs=[ANY], out_specs=ANY,
        out_shape=jax.ShapeDtypeStruct((NDEV, *part.shape), part.dtype),
        scratch_shapes=[pltpu.SemaphoreType.DMA((7,)), pltpu.SemaphoreType.DMA((7,)), pltpu.SemaphoreType.DMA],
    )(part)


def _pair_sum(g, recv, core):
    _, _, r, c = g.shape
    tr = _tile(r, 512)

    def body(core_ref, g_ref, r_ref, o_ref):
        o_ref[...] = (g_ref[...].astype(F32) + r_ref[...].astype(F32)).astype(o_ref.dtype)

    return pl.pallas_call(
        body, name="pair_sum",
        grid_spec=pltpu.PrefetchScalarGridSpec(
            num_scalar_prefetch=1, grid=(4, r // tr),
            in_specs=[pl.BlockSpec((None, None, tr, c), lambda k, i, core_ref: (k, core_ref[0], i, 0)),
                      pl.BlockSpec((None, tr, c), lambda k, i, core_ref: (k, i, 0))],
            out_specs=pl.BlockSpec((None, tr, c), lambda k, i, core_ref: (k, i, 0))),
        out_shape=jax.ShapeDtypeStruct((4, r, c), g.dtype),
        compiler_params=pltpu.CompilerParams(dimension_semantics=("parallel", "parallel"), vmem_limit_bytes=VMEM_LIMIT),
    )(core, g, recv)


def _sum_small(gathered):
    _, r, l = gathered.shape

    def body(g_ref, o_ref):
        acc = g_ref[0]
        for d in range(1, NDEV):
            acc = acc + g_ref[d]
        o_ref[...] = acc

    return pl.pallas_call(
        body, name="sum_small", in_specs=[pl.BlockSpec((NDEV, r, l), lambda: (0, 0, 0))],
        out_specs=pl.BlockSpec((r, l), lambda: (0, 0)), out_shape=jax.ShapeDtypeStruct((r, l), F32),
    )(gathered)


def _matmul(name, a, b, outs, grid, a_spec, b_spec, out_specs, n_red, nt, acc_shape,
            extra=(), extra_specs=(), epilogue=None, comm=None):
    n_extra, n_out = len(extra), len(outs)
    red_axes = tuple(range(len(grid) - n_red, len(grid)))
    red_sizes = tuple(grid[ax] for ax in red_axes)
    single = all(sz == 1 for sz in red_sizes)
    dims = (((1,), (1,)), ((), ())) if nt else (((1,), (0,)), ((), ()))

    def body(*refs):
        a_ref, b_ref = refs[0], refs[1]
        ex_refs = refs[2:2 + n_extra]
        o_refs = refs[2 + n_extra:2 + n_extra + n_out]
        acc_ref = refs[-1]

        def write(acc):
            vals = (acc,) if epilogue is None else epilogue(acc, *[r[...] for r in ex_refs])
            for o_ref, val in zip(o_refs, vals):
                o_ref[...] = val.astype(o_ref.dtype)

        if len(b_ref.shape) == 3:
            w = b_ref.shape[2]
            part = sum(lax.dot_general(a_ref[:, p * w:(p + 1) * w].astype(BF16), b_ref[p].astype(BF16), dims,
                                       preferred_element_type=F32) for p in range(b_ref.shape[0]))
        else:
            part = lax.dot_general(a_ref[...].astype(BF16), b_ref[...].astype(BF16), dims, preferred_element_type=F32)
        if single:
            write(part)
        else:
            ks = [pl.program_id(ax) for ax in red_axes]
            first = functools.reduce(jnp.logical_and, [k == 0 for k in ks])
            last = functools.reduce(jnp.logical_and, [k == sz - 1 for k, sz in zip(ks, red_sizes)])

            @pl.when(first)
            def _():
                acc_ref[...] = part

            @pl.when(jnp.logical_not(first))
            def _():
                acc_ref[...] += part

            @pl.when(last)
            def _():
                write(acc_ref[...])

    sem = ("parallel",) * (len(grid) - n_red) + ("arbitrary",) * n_red
    return _call(name, body, grid, [a_spec, b_spec, *extra_specs], list(out_specs), list(outs),
                 [pltpu.VMEM((8, LANES) if single else acc_shape, F32)], sem, (a, b, *extra), comm)


def _mm_cols(name, a, wg, bias=None, out_dtype=F32, comm=None):
    m, k = a.shape
    _, _, ns = wg.shape
    tm, tk = _tile(m, 1024), _tile(k, 2048)
    grid = (m // tm, NDEV, k // tk)
    extra, extra_specs, epi = (), (), None
    if bias is not None:
        extra, extra_specs = (bias,), (pl.BlockSpec((1, ns), lambda i, d, kk: (0, d)),)
        epi = lambda acc, bv: (acc + bv,)
    return _matmul(name, a, wg, [jax.ShapeDtypeStruct((m, NDEV * ns), out_dtype)], grid,
                   pl.BlockSpec((tm, tk), lambda i, d, kk: (i, kk)),
                   pl.BlockSpec((None, tk, ns), lambda i, d, kk: (d, kk, 0)),
                   [pl.BlockSpec((tm, ns), lambda i, d, kk: (i, d))], 1, False, (tm, ns),
                   extra, extra_specs, epi, comm)


def _mm_cols_t(name, a, wg, comm=None):
    m, _ = a.shape
    _, n, ns = wg.shape
    tm, tn = _tile(m, 1024), _tile(n, 1024)
    per = 2
    grid = (m // tm, n // tn, NDEV // per)
    return _matmul(name, a, wg, [jax.ShapeDtypeStruct((m, n), F32)], grid,
                   pl.BlockSpec((tm, per * ns), lambda i, j, d: (i, d)),
                   pl.BlockSpec((per, tn, ns), lambda i, j, d: (d, j, 0)),
                   [pl.BlockSpec((tm, tn), lambda i, j, d: (i, j))], 1, True, (tm, tn), comm=comm)


def _mm_dw_cols(name, at, g, comm=None):
    m, t = at.shape
    ns = g.shape[1] // NDEV
    tm, tk = _tile(m, 1024), _tile(t, 2048)
    grid = (m // tm, NDEV, t // tk)
    return _matmul(name, at, g, [jax.ShapeDtypeStruct((NDEV, m, ns), BF16)], grid,
                   pl.BlockSpec((tm, tk), lambda i, d, kk: (i, kk)),
                   pl.BlockSpec((tk, ns), lambda i, d, kk: (kk, d)),
                   [pl.BlockSpec((None, tm, ns), lambda i, d, kk: (d, i, 0))], 1, False, (tm, ns), comm=comm)


def _mm_plain(name, a, b, nt, out_dtype, extra=(), extra_specs=(), epilogue=None, outs=None, out_specs=None, comm=None):
    m, k = a.shape
    n = b.shape[0] if nt else b.shape[1]
    tm, tn, tk = _tile(m, 1024), _tile(n, 1024), _tile(k, 2048)
    grid = (m // tm, n // tn, k // tk)
    b_spec = (pl.BlockSpec((tn, tk), lambda i, j, kk: (j, kk)) if nt
              else pl.BlockSpec((tk, tn), lambda i, j, kk: (kk, j)))
    if outs is None:
        outs = [jax.ShapeDtypeStruct((m, n), out_dtype)]
        out_specs = [pl.BlockSpec((tm, tn), lambda i, j, kk: (i, j))]
    return _matmul(name, a, b, outs, grid, pl.BlockSpec((tm, tk), lambda i, j, kk: (i, kk)), b_spec,
                   out_specs, 1, nt, (tm, tn), extra, extra_specs, epilogue, comm)


def _rms_rows(v):
    return lax.rsqrt(jnp.mean(v * v, axis=-1, keepdims=True) + EPS)


def _prenorm(x, g):
    s, d = x.shape
    ts = _tile(s, 256)

    def body(x_ref, g_ref, h_ref, ht_ref):
        xv = x_ref[...]
        h = xv * _rms_rows(xv) * g_ref[...]
        h_ref[...] = h.astype(BF16)
        ht_ref[...] = h.T.astype(BF16)

    return _call("prenorm", body, (s // ts,),
                 [pl.BlockSpec((ts, d), lambda i: (i, 0)), pl.BlockSpec((1, d), lambda i: (0, 0))],
                 [pl.BlockSpec((ts, d), lambda i: (i, 0)), pl.BlockSpec((d, ts), lambda i: (0, i))],
                 [jax.ShapeDtypeStruct((s, d), BF16), jax.ShapeDtypeStruct((d, s), BF16)], [], ("parallel",), (x, g))


def _ln_silu(u1, g, b):
    s, c = u1.shape
    ts = _tile(s, 256)

    def body(u_ref, g_ref, b_ref, o_ref, ot_ref):
        u = u_ref[...]
        mu = jnp.mean(u, axis=-1, keepdims=True)
        var = jnp.mean(jnp.square(u - mu), axis=-1, keepdims=True)
        u2 = (u - mu) * lax.rsqrt(var + EPS) * g_ref[...] + b_ref[...]
        u3 = u2 * _sigmoid(u2)
        o_ref[...] = u3.astype(BF16)
        ot_ref[...] = u3.T.astype(BF16)

    vec = pl.BlockSpec((1, c), lambda i: (0, 0))
    return _call("ln_silu", body, (s // ts,), [pl.BlockSpec((ts, c), lambda i: (i, 0)), vec, vec],
                 [pl.BlockSpec((ts, c), lambda i: (i, 0)), pl.BlockSpec((c, ts), lambda i: (0, i))],
                 [jax.ShapeDtypeStruct((s, c), BF16), jax.ShapeDtypeStruct((c, s), BF16)], [], ("parallel",), (u1, g, b))


def _ln_silu_bwd(du3, u1, g, b, comm=None):
    s, c = u1.shape
    ts = _tile(s, 256)

    def body(d_ref, u_ref, g_ref, b_ref, du1_ref, dg_ref, db_ref):
        @pl.when(pl.program_id(0) == 0)
        def _():
            dg_ref[...] = jnp.zeros_like(dg_ref)
            db_ref[...] = jnp.zeros_like(db_ref)

        u = u_ref[...]
        mu = jnp.mean(u, axis=-1, keepdims=True)
        var = jnp.mean(jnp.square(u - mu), axis=-1, keepdims=True)
        rstd = lax.rsqrt(var + EPS)
        uhat = (u - mu) * rstd
        u2 = uhat * g_ref[...] + b_ref[...]
        sg = _sigmoid(u2)
        du2 = d_ref[...] * (sg * (1.0 + u2 * (1.0 - sg)))
        dg_ref[...] += jnp.sum(du2 * uhat, axis=0, keepdims=True)
        db_ref[...] += jnp.sum(du2, axis=0, keepdims=True)
        duh = du2 * g_ref[...]
        du1_ref[...] = rstd * (duh - jnp.mean(duh, axis=-1, keepdims=True)
                               - uhat * jnp.mean(duh * uhat, axis=-1, keepdims=True))

    row = pl.BlockSpec((ts, c), lambda i: (i, 0))
    vec = pl.BlockSpec((1, c), lambda i: (0, 0))
    return _call("ln_silu_bwd", body, (s // ts,), [row, row, vec, vec], [row, vec, vec],
                 [jax.ShapeDtypeStruct((s, c), F32), jax.ShapeDtypeStruct((1, c), F32), jax.ShapeDtypeStruct((1, c), F32)],
                 [], ("arbitrary",), (du3, u1, g, b), comm)


def _merge(proj, o_sbp, o_cv, d, comm=None):
    s = proj.shape[0]
    w = d // 2
    ts = _tile(s, 256)

    def body(gs_ref, gc_ref, a_ref, b_ref, m_ref, mt_ref):
        mg = _sigmoid(gs_ref[...]) * a_ref[...] + _sigmoid(gc_ref[...]) * b_ref[...]
        m_ref[...] = mg.astype(BF16)
        mt_ref[...] = mg.T.astype(BF16)

    blk = pl.BlockSpec((ts, w), lambda i, j: (i, j))
    return _call("merge", body, (s // ts, 2),
                 [pl.BlockSpec((ts, w), lambda i, j: (i, 5 + j)), pl.BlockSpec((ts, w), lambda i, j: (i, 7 + j)), blk, blk],
                 [blk, pl.BlockSpec((w, ts), lambda i, j: (j, i))],
                 [jax.ShapeDtypeStruct((s, d), BF16), jax.ShapeDtypeStruct((d, s), BF16)], [],
                 ("parallel", "parallel"), (proj, proj, o_sbp, o_cv), comm)


def _merge_bwd(dmerged, proj, o_sbp, o_cv, d, comm=None):
    s = proj.shape[0]
    w = d // 2
    ts = _tile(s, 256)

    def body(dm_ref, gs_ref, gc_ref, a_ref, b_ref, da_ref, db_ref, dgs_ref, dgc_ref):
        dm = dm_ref[...]
        ss = _sigmoid(gs_ref[...])
        sc = _sigmoid(gc_ref[...])
        da_ref[...] = (dm * ss).astype(BF16)
        db_ref[...] = (dm * sc).astype(BF16)
        dgs_ref[...] = (dm * a_ref[...] * ss * (1.0 - ss)).astype(BF16)
        dgc_ref[...] = (dm * b_ref[...] * sc * (1.0 - sc)).astype(BF16)

    blk = pl.BlockSpec((ts, w), lambda i, j: (i, j))
    sds = jax.ShapeDtypeStruct((s, d), BF16)
    return _call("merge_bwd", body, (s // ts, 2),
                 [blk, pl.BlockSpec((ts, w), lambda i, j: (i, 5 + j)), pl.BlockSpec((ts, w), lambda i, j: (i, 7 + j)),
                  blk, blk],
                 [blk, blk, blk, blk], [sds, sds, sds, sds], [], ("parallel", "parallel"),
                 (dmerged, proj, proj, o_sbp, o_cv), comm)


def _postnorm_mix(x, y, g_post, g_pre, comm=None):
    s, d = x.shape
    ts = _tile(s, 256)

    def body(x_ref, y_ref, gp_ref, gn_ref, x1_ref, h_ref, ht_ref):
        yv = y_ref[...]
        x1 = x_ref[...] + yv * _rms_rows(yv) * gp_ref[...]
        x1_ref[...] = x1
        h = x1 * _rms_rows(x1) * gn_ref[...]
        h_ref[...] = h.astype(BF16)
        ht_ref[...] = h.T.astype(BF16)

    row = pl.BlockSpec((ts, d), lambda i: (i, 0))
    vec = pl.BlockSpec((1, d), lambda i: (0, 0))
    return _call("postnorm_mix", body, (s // ts,), [row, row, vec, vec],
                 [row, row, pl.BlockSpec((d, ts), lambda i: (0, i))],
                 [jax.ShapeDtypeStruct((s, d), F32), jax.ShapeDtypeStruct((s, d), BF16), jax.ShapeDtypeStruct((d, s), BF16)],
                 [], ("parallel",), (x, y, g_post, g_pre), comm)


def _rms_bwd(dout, vin, g):
    r = _rms_rows(vin)
    vhat = vin * r
    dyh = dout * g
    dvin = r * (dyh - vhat * jnp.mean(dyh * vhat, axis=-1, keepdims=True))
    return dvin, jnp.sum(dout * vhat, axis=0, keepdims=True)


def _loss_head(x1, f2, tgt, g):
    s, d = x1.shape
    ts = _tile(s, 256)

    def body(x1_ref, f_ref, t_ref, g_ref, dx2_ref, df2_ref, dg_ref, loss_ref):
        @pl.when(pl.program_id(0) == 0)
        def _():
            dg_ref[...] = jnp.zeros_like(dg_ref)
            loss_ref[...] = jnp.zeros_like(loss_ref)

        fv = f_ref[...]
        x2 = x1_ref[...] + fv * _rms_rows(fv) * g_ref[...]
        err = x2 - t_ref[...]
        loss_ref[...] += 0.5 * jnp.sum(jnp.mean(err * err, axis=-1, keepdims=True), axis=0, keepdims=True)
        dx2 = err * (1.0 / d)
        dx2_ref[...] = dx2
        df2, dg = _rms_bwd(dx2, fv, g_ref[...])
        df2_ref[...] = df2.astype(BF16)
        dg_ref[...] += dg

    row = pl.BlockSpec((ts, d), lambda i: (i, 0))
    vec = pl.BlockSpec((1, d), lambda i: (0, 0))
    return _call("loss_head", body, (s // ts,), [row, row, row, vec],
                 [row, row, vec, pl.BlockSpec((1, LANES), lambda i: (0, 0))],
                 [jax.ShapeDtypeStruct((s, d), F32), jax.ShapeDtypeStruct((s, d), BF16),
                  jax.ShapeDtypeStruct((1, d), F32), jax.ShapeDtypeStruct((1, LANES), F32)],
                 [], ("arbitrary",), (x1, f2, tgt, g))


def _midnorm_bwd(dx2, dh2, x1, y, g_post, g_pre, comm=None):
    s, d = x1.shape
    ts = _tile(s, 256)

    def body(dx2_ref, dh_ref, x1_ref, y_ref, gp_ref, gn_ref, dx1_ref, dy_ref, dgn_ref, dgp_ref):
        @pl.when(pl.program_id(0) == 0)
        def _():
            dgn_ref[...] = jnp.zeros_like(dgn_ref)
            dgp_ref[...] = jnp.zeros_like(dgp_ref)

        dxa, dgn = _rms_bwd(dh_ref[...], x1_ref[...], gn_ref[...])
        dx1 = dx2_ref[...] + dxa
        dx1_ref[...] = dx1
        dy, dgp = _rms_bwd(dx1, y_ref[...], gp_ref[...])
        dy_ref[...] = dy.astype(BF16)
        dgn_ref[...] += dgn
        dgp_ref[...] += dgp

    row = pl.BlockSpec((ts, d), lambda i: (i, 0))
    vec = pl.BlockSpec((1, d), lambda i: (0, 0))
    return _call("midnorm_bwd", body, (s // ts,), [row, row, row, row, vec, vec], [row, row, vec, vec],
                 [jax.ShapeDtypeStruct((s, d), F32), jax.ShapeDtypeStruct((s, d), BF16),
                  jax.ShapeDtypeStruct((1, d), F32), jax.ShapeDtypeStruct((1, d), F32)],
                 [], ("arbitrary",), (dx2, dh2, x1, y, g_post, g_pre), comm)


def _prenorm_bwd(dx1, dh, x, g, comm=None):
    s, d = x.shape
    ts = _tile(s, 256)

    def body(dx1_ref, dh_ref, x_ref, g_ref, dx_ref, dg_ref):
        @pl.when(pl.program_id(0) == 0)
        def _():
            dg_ref[...] = jnp.zeros_like(dg_ref)

        dxa, dg = _rms_bwd(dh_ref[...], x_ref[...], g_ref[...])
        dx_ref[...] = dx1_ref[...] + dxa
        dg_ref[...] += dg

    row = pl.BlockSpec((ts, d), lambda i: (i, 0))
    vec = pl.BlockSpec((1, d), lambda i: (0, 0))
    return _call("prenorm_bwd", body, (s // ts,), [row, row, row, vec], [row, vec],
                 [jax.ShapeDtypeStruct((s, d), F32), jax.ShapeDtypeStruct((1, d), F32)],
                 [], ("arbitrary",), (dx1, dh, x, g), comm)


def _colsum(a):
    s, n = a.shape
    ts = _tile(s, 256)

    def body(a_ref, o_ref):
        @pl.when(pl.program_id(0) == 0)
        def _():
            o_ref[...] = jnp.zeros_like(o_ref)

        o_ref[...] += jnp.sum(a_ref[...].astype(F32), axis=0, keepdims=True)

    return _call("colsum", body, (s // ts,), [pl.BlockSpec((ts, n), lambda i: (i, 0))],
                 [pl.BlockSpec((1, n), lambda i: (0, 0))], [jax.ShapeDtypeStruct((1, n), F32)], [], ("arbitrary",), (a,))[0]


def _shift_rows(win, off, t):
    n = win.shape[0]
    if off == 0:
        return win[:t]
    return pltpu.roll(win, n - off, axis=0)[:t]


def _conv_fwd(proj, w_pad, b_dw, c_total, comm=None):
    s = proj.shape[0]
    nct = c_total // LANES
    t = _tile(s, 256)

    def body(ga_ref, gb_ref, w_ref, b_ref, o_ref, u0_ref):
        u0_ref[pl.ds(0, CONV_PAD), :] = jnp.zeros((CONV_PAD, LANES), F32)
        u0_ref[pl.ds(CONV_PAD, s), :] = ga_ref[...] * _sigmoid(gb_ref[...])
        wv = w_ref[...]

        def chunk(r, carry):
            r0 = pl.multiple_of(r * t, t)
            win = u0_ref[pl.ds(r0, t + CONV_PAD), :]
            acc = jnp.broadcast_to(b_ref[...], (t, LANES))
            for j in range(CONV_WIDTH):
                acc = acc + wv[j:j + 1, :] * _shift_rows(win, j + CONV_PAD - (CONV_WIDTH - 1), t)
            o_ref[pl.ds(r0, t), :] = acc
            return carry

        lax.fori_loop(0, s // t, chunk, 0)

    return _call("conv_fwd", body, (nct,),
                 [pl.BlockSpec((s, LANES), lambda c: (0, 3 * nct + c)), pl.BlockSpec((s, LANES), lambda c: (0, 4 * nct + c)),
                  pl.BlockSpec((CONV_PAD, LANES), lambda c: (0, c)), pl.BlockSpec((1, LANES), lambda c: (0, c))],
                 [pl.BlockSpec((s, LANES), lambda c: (0, c))], [jax.ShapeDtypeStruct((s, c_total), F32)],
                 [pltpu.VMEM((s + CONV_PAD, LANES), F32)], ("parallel",), (proj, proj, w_pad, b_dw), comm)


def _conv_bwd(du1, proj, w_pad, c_total, comm=None):
    s = proj.shape[0]
    nct = c_total // LANES
    t = _tile(s, 256)

    def body(d_ref, ga_ref, gb_ref, w_ref, dga_ref, dgb_ref, dw_ref, db_ref, u0_ref, dp_ref):
        sg = _sigmoid(gb_ref[...])
        u0_ref[pl.ds(0, CONV_PAD), :] = jnp.zeros((CONV_PAD, LANES), F32)
        u0_ref[pl.ds(CONV_PAD, s), :] = ga_ref[...] * sg
        dp_ref[pl.ds(0, s), :] = d_ref[...]
        dp_ref[pl.ds(s, CONV_PAD), :] = jnp.zeros((CONV_PAD, LANES), F32)
        dw_ref[...] = jnp.zeros_like(dw_ref)
        db_ref[...] = jnp.sum(d_ref[...], axis=0, keepdims=True)
        wv = w_ref[...]

        def chunk(r, carry):
            r0 = pl.multiple_of(r * t, t)
            win = u0_ref[pl.ds(r0, t + CONV_PAD), :]
            dwin = dp_ref[pl.ds(r0, t + CONV_PAD), :]
            dcur = dwin[:t]
            du0 = jnp.zeros((t, LANES), F32)
            for j in range(CONV_WIDTH):
                du0 = du0 + wv[j:j + 1, :] * _shift_rows(dwin, CONV_WIDTH - 1 - j, t)
                sh = _shift_rows(win, j + CONV_PAD - (CONV_WIDTH - 1), t)
                dw_ref[j:j + 1, :] += jnp.sum(dcur * sh, axis=0, keepdims=True)
            gav = ga_ref[pl.ds(r0, t), :]
            sgv = _sigmoid(gb_ref[pl.ds(r0, t), :])
            dga_ref[pl.ds(r0, t), :] = (du0 * sgv).astype(BF16)
            dgb_ref[pl.ds(r0, t), :] = (du0 * gav * sgv * (1.0 - sgv)).astype(BF16)
            return carry

        lax.fori_loop(0, s // t, chunk, 0)

    col = pl.BlockSpec((s, LANES), lambda c: (0, c))
    return _call("conv_bwd", body, (nct,),
                 [col, pl.BlockSpec((s, LANES), lambda c: (0, 3 * nct + c)),
                  pl.BlockSpec((s, LANES), lambda c: (0, 4 * nct + c)), pl.BlockSpec((CONV_PAD, LANES), lambda c: (0, c))],
                 [col, col, pl.BlockSpec((CONV_PAD, LANES), lambda c: (0, c)), pl.BlockSpec((1, LANES), lambda c: (0, c))],
                 [jax.ShapeDtypeStruct((s, c_total), BF16), jax.ShapeDtypeStruct((s, c_total), BF16),
                  jax.ShapeDtypeStruct((CONV_PAD, c_total), F32), jax.ShapeDtypeStruct((1, c_total), F32)],
                 [pltpu.VMEM((s + CONV_PAD, LANES), F32), pltpu.VMEM((s + CONV_PAD, LANES), F32)],
                 ("parallel",), (du1, proj, proj, w_pad), comm)


TQ_PREF = 256
NU = 4
NU_BWD = 2
TK = 256


def _split_dot(v, tri):
    hi = v.astype(BF16)
    lo = (v - hi.astype(F32)).astype(BF16)
    return (jnp.dot(hi, tri, preferred_element_type=F32) + jnp.dot(lo, tri, preferred_element_type=F32))


def _causal_mask(i, j, tq):
    tpos = i * tq + lax.broadcasted_iota(jnp.int32, (tq, TK), 0)
    spos = j * TK + lax.broadcasted_iota(jnp.int32, (tq, TK), 1)
    return spos < tpos


def _log_terms(z, mask):
    sp = jnp.log(1.0 + jnp.exp(-jnp.abs(z)))
    return jnp.minimum(z, 0.0) - sp, jnp.where(mask, -jnp.maximum(z, 0.0) - sp, 0.0)


def _tri(after):
    r = lax.broadcasted_iota(jnp.int32, (TK, TK), 0)
    c = lax.broadcasted_iota(jnp.int32, (TK, TK), 1)
    return (r > c).astype(BF16) if after else (r < c).astype(BF16)


def _attn_fwd(proj, n_heads, comm=None):
    s = proj.shape[0]
    tq = _tile(s, TQ_PREF)
    scale = 1.0 / math.sqrt(HEAD_DIM)
    ratio = tq // TK

    def body(q_ref, k_ref, v_ref, o_ref, ot_ref, acc_ref, *clms):
        i = pl.program_id(1)
        heads = [slice(u * HEAD_DIM, (u + 1) * HEAD_DIM) for u in range(NU)]
        qs = [q_ref[:, hs].astype(BF16) for hs in heads]
        tri_after = _tri(True)
        acc_ref[...] = jnp.zeros_like(acc_ref)
        for cr in clms:
            cr[...] = jnp.zeros_like(cr)
        nkb = (i + 1) * ratio

        def step(jj, carry):
            j = nkb - 1 - jj
            rows = pl.ds(pl.multiple_of(j * TK, TK), TK)
            mask = _causal_mask(i, j, tq)
            zs = [lax.dot_general(qs[u], k_ref[rows, hs].astype(BF16), (((1,), (1,)), ((), ())),
                                  preferred_element_type=F32) * scale for u, hs in enumerate(heads)]
            lls = [_log_terms(z, mask) for z in zs]
            sufs = [clms[u][...] + _split_dot(lls[u][1], tri_after) for u in range(NU)]
            for u, hs in enumerate(heads):
                a = jnp.where(mask, jnp.exp(lls[u][0] + sufs[u]), 0.0)
                acc_ref[:, hs] += jnp.dot(a.astype(BF16), v_ref[rows, hs].astype(BF16), preferred_element_type=F32)
                clms[u][...] += jnp.sum(lls[u][1], axis=1, keepdims=True)
            return carry

        lax.fori_loop(0, nkb, step, 0)
        o = acc_ref[...]
        o_ref[...] = o
        ot_ref[...] = o.T.astype(BF16)

    w = NU * HEAD_DIM
    ng = n_heads // NU
    return _call("attn_fwd", body, (ng, s // tq),
                 [pl.BlockSpec((tq, w), lambda h, i: (i, h)),
                  pl.BlockSpec((s, w), lambda h, i: (0, ng + h)),
                  pl.BlockSpec((s, w), lambda h, i: (0, 2 * ng + h))],
                 [pl.BlockSpec((tq, w), lambda h, i: (i, h)), pl.BlockSpec((w, tq), lambda h, i: (h, i))],
                 [jax.ShapeDtypeStruct((s, n_heads * HEAD_DIM), F32), jax.ShapeDtypeStruct((n_heads * HEAD_DIM, s), BF16)],
                 [pltpu.VMEM((tq, w), F32), *[pltpu.VMEM((tq, 1), F32)] * NU],
                 ("parallel", "arbitrary"), (proj, proj, proj), comm)


def _attn_bwd(proj, do_sb, n_heads, comm=None):
    s = proj.shape[0]
    tq = _tile(s, TQ_PREF)
    scale = 1.0 / math.sqrt(HEAD_DIM)
    ratio = tq // TK
    n_kb = s // TK
    n_qb = s // tq
    nu = NU_BWD
    heads = [slice(u * HEAD_DIM, (u + 1) * HEAD_DIM) for u in range(nu)]
    nt_dims = (((1,), (1,)), ((), ()))

    def body(q_ref, k_ref, v_ref, do_ref, dq_ref, dk_ref, dv_ref, dka_ref, dva_ref, dl_ref, be_ref, dqa_ref, *c_refs):
        i = pl.program_id(1)

        @pl.when(i == 0)
        def _():
            dka_ref[...] = jnp.zeros_like(dka_ref)
            dva_ref[...] = jnp.zeros_like(dva_ref)

        qs = [q_ref[:, hs].astype(BF16) for hs in heads]
        dobs = [do_ref[:, hs].astype(BF16) for hs in heads]
        tri_after = _tri(True)
        tri_before = _tri(False)
        nkb = (i + 1) * ratio

        for cr in c_refs:
            cr[...] = jnp.zeros_like(cr)

        def sweep_a(jj, carry):
            j = nkb - 1 - jj
            rows = pl.ds(pl.multiple_of(j * TK, TK), TK)
            mask = _causal_mask(i, j, tq)
            zs = [lax.dot_general(qs[u], k_ref[rows, hs].astype(BF16), nt_dims, preferred_element_type=F32) * scale
                  for u, hs in enumerate(heads)]
            das = [lax.dot_general(dobs[u], v_ref[rows, hs].astype(BF16), nt_dims, preferred_element_type=F32)
                   for u, hs in enumerate(heads)]
            lls = [_log_terms(z, mask) for z in zs]
            sufs = [c_refs[u][...] + _split_dot(lls[u][1], tri_after) for u in range(nu)]
            for u, hs in enumerate(heads):
                a = jnp.where(mask, jnp.exp(lls[u][0] + sufs[u]), 0.0)
                dl_ref[u, j] = das[u] * a
                be_ref[u, j] = jnp.exp(lls[u][0])
                dva_ref[rows, hs] += jnp.dot(a.T.astype(BF16), dobs[u], preferred_element_type=F32)
                c_refs[u][...] += jnp.sum(lls[u][1], axis=1, keepdims=True)
            return carry

        lax.fori_loop(0, nkb, sweep_a, 0)

        for cr in c_refs:
            cr[...] = jnp.zeros_like(cr)
        dqa_ref[...] = jnp.zeros_like(dqa_ref)

        def sweep_b(j, carry):
            rows = pl.ds(pl.multiple_of(j * TK, TK), TK)
            mask = _causal_mask(i, j, tq)
            dls = [dl_ref[u, j] for u in range(nu)]
            ps = [c_refs[u][...] + _split_dot(dls[u], tri_before) for u in range(nu)]
            for u, hs in enumerate(heads):
                beta = be_ref[u, j]
                dz = jnp.where(mask, (dls[u] * (1.0 - beta) - beta * ps[u]) * scale, 0.0)
                dqa_ref[:, hs] += jnp.dot(dz.astype(BF16), k_ref[rows, hs].astype(BF16), preferred_element_type=F32)
                dka_ref[rows, hs] += jnp.dot(dz.T.astype(BF16), qs[u], preferred_element_type=F32)
                c_refs[u][...] += jnp.sum(dls[u], axis=1, keepdims=True)
            return carry

        lax.fori_loop(0, nkb, sweep_b, 0)
        dq_ref[...] = dqa_ref[...].astype(BF16)

        @pl.when(i == n_qb - 1)
        def _():
            dk_ref[...] = dka_ref[...].astype(BF16)
            dv_ref[...] = dva_ref[...].astype(BF16)

    w = nu * HEAD_DIM
    ng = n_heads // nu
    qblk = pl.BlockSpec((tq, w), lambda h, i: (i, h))
    full = pl.BlockSpec((s, w), lambda h, i: (0, h))
    sds = jax.ShapeDtypeStruct((s, n_heads * HEAD_DIM), BF16)
    return _call("attn_bwd", body, (ng, n_qb),
                 [qblk, pl.BlockSpec((s, w), lambda h, i: (0, ng + h)), pl.BlockSpec((s, w), lambda h, i: (0, 2 * ng + h)),
                  qblk],
                 [qblk, full, full], [sds, sds, sds],
                 [pltpu.VMEM((s, w), F32), pltpu.VMEM((s, w), F32),
                  pltpu.VMEM((nu, n_kb, tq, TK), F32), pltpu.VMEM((nu, n_kb, tq, TK), F32),
                  pltpu.VMEM((tq, w), F32), *[pltpu.VMEM((tq, 1), F32)] * nu],
                 ("parallel", "arbitrary"), (proj, proj, proj, do_sb), comm)


def _adamw(name, w, m, v, parts, part_specs, tr, comm=None):
    r, c = w.shape
    n_parts = len(parts)

    def body(*refs):
        w_ref, m_ref, v_ref = refs[:3]
        p_refs = refs[3:3 + n_parts]
        g_ref, d_ref, nm_ref, nv_ref = refs[3 + n_parts:]
        g = p_refs[0][...].astype(F32)
        for p in p_refs[1:]:
            g = g + p[...].astype(F32)
        nm = ADAM_B1 * m_ref[...] + (1.0 - ADAM_B1) * g
        nv = ADAM_B2 * v_ref[...] + (1.0 - ADAM_B2) * jnp.square(g)
        m_hat = nm / (1.0 - ADAM_B1 ** ADAM_STEP)
        v_hat = nv / (1.0 - ADAM_B2 ** ADAM_STEP)
        g_ref[...] = g
        d_ref[...] = -ADAM_LR * (m_hat / (jnp.sqrt(v_hat) + ADAM_EPS) + ADAM_WD * w_ref[...])
        nm_ref[...] = nm
        nv_ref[...] = nv

    blk = pl.BlockSpec((tr, c), lambda i: (i, 0))
    sds = jax.ShapeDtypeStruct((r, c), F32)
    return _call(name, body, (r // tr,), [blk, blk, blk, *part_specs], [blk] * 4, [sds] * 4, [], ("parallel",),
                 (w, m, v, *parts), comm)


def _adamw_big(name, w, m, v, recv, comm=None):
    r, c = w.shape
    tr = _tile(r, 128)
    order = (3, 0, 1, 2)
    specs = [pl.BlockSpec((None, tr, c), functools.partial(lambda i, slot: (slot, i, 0), slot=sl)) for sl in order]
    return _adamw(name, w, m, v, [recv] * 4, specs, tr, comm)


def _adamw_small(name, w, m, v, g):
    r, c = w.shape
    return _adamw(name, w, m, v, [g], [pl.BlockSpec((r, c), lambda i: (0, 0))], r)


def kernel(x, g_pre_mix, w_in, b_in, w_dw, b_dw, g_conv_ln, b_conv_ln, w_sb_out, w_conv_out, w_o, g_post_mix, g_pre_mlp, w_up, w_down, g_post_mlp, loss_target, m_g_pre_mix, m_w_in, m_b_in, m_w_dw, m_b_dw, m_g_conv_ln, m_b_conv_ln, m_w_sb_out, m_w_conv_out, m_w_o, m_g_post_mix, m_g_pre_mlp, m_w_up, m_w_down, m_g_post_mlp, v_g_pre_mix, v_w_in, v_b_in, v_w_dw, v_b_dw, v_g_conv_ln, v_b_conv_ln, v_w_sb_out, v_w_conv_out, v_w_o, v_g_post_mix, v_g_pre_mlp, v_w_up, v_w_down, v_g_post_mlp):
    xs, tgt = x[0], loss_target[0]
    s, d = xs.shape
    d_half = d // 2
    n_heads = d_half // HEAD_DIM
    d_ff = NDEV * w_up.shape[2]
    core = lax.axis_index("c").astype(jnp.int32).reshape(1)
    dev = 4 * lax.axis_index("x") + 2 * lax.axis_index("y") + lax.axis_index("c")

    w_dw_pad = jnp.pad(w_dw[0], ((0, CONV_PAD - CONV_WIDTH), (0, 0)))
    sh_in, sh_sb, sh_cv, sh_o, sh_up, sh_down = [w[0].astype(BF16) for w in (w_in, w_sb_out, w_conv_out, w_o, w_up, w_down)]
    wg_in, = _run_comm("all_gather_first", _ag_comm([sh_in]))

    ag_down = _Chunked(_ag_comm, [sh_down], sh_down.shape[0], 8)
    h, h_t = _prenorm(xs, g_pre_mix)
    proj, wg_sb, wg_cv, wg_o, wg_dw = _mm_cols("proj", h, wg_in, bias=b_in,
                                               comm=_ag_comm([sh_sb, sh_cv, sh_o, w_dw_pad]))
    o_sb, o_sb_t, wg_up = _attn_fwd(proj, n_heads, _ag_comm([sh_up]))
    wf_dw = wg_dw.transpose(1, 0, 2).reshape(CONV_PAD, d_half)
    wf_o = wg_o.reshape(d, d)
    u1, *part = _conv_fwd(proj, wf_dw, b_dw, d_half, ag_down.take())
    ag_down.done(part)
    u3, u3_t = _ln_silu(u1, g_conv_ln, b_conv_ln)
    o_sbp = _mm_cols("sb_out", o_sb, wg_sb)[0]
    o_cv = _mm_cols("conv_out", u3, wg_cv)[0]
    merged, merged_t, *part = _merge(proj, o_sbp, o_cv, d, ag_down.take())
    ag_down.done(part)
    y, *part = _mm_plain("w_o", merged, wf_o, False, F32, comm=ag_down.take())
    ag_down.done(part)
    x1, h2, h2_t, *part = _postnorm_mix(xs, y, g_post_mix, g_pre_mlp, ag_down.take())
    ag_down.done(part)

    tm_up = _tile(s, 1024)
    ns_up = wg_up.shape[2]
    tk_up = _tile(d, 2048)

    def up_epilogue(acc):
        f = jnp.square(jnp.maximum(acc, 0.0))
        return acc, f, f.T

    a_act, f, f_t, wg_down = _matmul(
        "w_up", h2, wg_up,
        [jax.ShapeDtypeStruct((s, d_ff), BF16), jax.ShapeDtypeStruct((s, d_ff), BF16), jax.ShapeDtypeStruct((d_ff, s), BF16)],
        (s // tm_up, NDEV, d // tk_up),
        pl.BlockSpec((tm_up, tk_up), lambda i, dd, kk: (i, kk)),
        pl.BlockSpec((None, tk_up, ns_up), lambda i, dd, kk: (dd, kk, 0)),
        [pl.BlockSpec((tm_up, ns_up), lambda i, dd, kk: (i, dd)), pl.BlockSpec((tm_up, ns_up), lambda i, dd, kk: (i, dd)),
         pl.BlockSpec((ns_up, tm_up), lambda i, dd, kk: (dd, i))],
        1, False, (tm_up, ns_up), epilogue=up_epilogue, comm=ag_down.take(4))
    wf_down = wg_down.reshape(d_ff, d)
    f2 = _mm_plain("w_down", f, wf_down, False, F32)[0]
    dx2, df2, dg_post_mlp, loss_part = _loss_head(x1, f2, tgt, g_post_mlp)

    tm_b, tn_b = _tile(s, 1024), _tile(d_ff, 1024)
    da = _mm_plain("w_down_bwd", df2, wf_down, True, BF16,
                   extra=(a_act,), extra_specs=(pl.BlockSpec((tm_b, tn_b), lambda i, j, kk: (i, j)),),
                   epilogue=lambda acc, av: (acc * (2.0 * jnp.maximum(av.astype(F32), 0.0)),),
                   outs=[jax.ShapeDtypeStruct((s, d_ff), BF16)],
                   out_specs=[pl.BlockSpec((tm_b, tn_b), lambda i, j, kk: (i, j))])[0]
    gw_down = _mm_plain("w_down_grad", f_t, df2, False, BF16)[0]
    big_down = gw_down.reshape(4, 2, d_ff // NDEV, d)
    gw_up, sib_down = _mm_dw_cols("w_up_grad", h2_t, da, comm=_sibling_comm([big_down]))
    big_up = gw_up.reshape(4, 2, d, d_ff // NDEV)
    dh2, sib_up = _mm_cols_t("w_up_bwd", da, wg_up, comm=_sibling_comm([big_up]))
    rs_down = _Chunked(_chips_comm, [_pair_sum(big_down, sib_down, core)], d_ff // NDEV, 8)
    rs_up = _Chunked(_chips_comm, [_pair_sum(big_up, sib_up, core)], d, 8)
    dx1, dy, dg_pre_mlp, dg_post_mix, *part = _midnorm_bwd(dx2, dh2, x1, y, g_post_mix, g_pre_mlp, rs_down.take())
    rs_down.done(part)
    gw_o, *part = _mm_plain("w_o_grad", merged_t, dy, False, BF16, comm=rs_down.take())
    rs_down.done(part)
    dmerged, *part = _mm_plain("w_o_bwd", dy, wf_o, True, F32, comm=rs_down.take())
    rs_down.done(part)
    do_sbp, do_cv, dgate_sb, dgate_cv, *part = _merge_bwd(dmerged, proj, o_sbp, o_cv, d, rs_down.take())
    rs_down.done(part)
    gw_cv = _mm_dw_cols("conv_out_grad", u3_t, do_cv)[0]
    gw_sb = _mm_dw_cols("sb_out_grad", o_sb_t, do_sbp)[0]
    du3 = _mm_cols_t("conv_out_bwd", do_cv, wg_cv)[0]
    do_sb = _mm_cols_t("sb_out_bwd", do_sbp, wg_sb)[0]
    du1, dg_ln, db_ln = _ln_silu_bwd(du3, u1, g_conv_ln, b_conv_ln)
    dglu_a, dglu_b, dw_dw, db_dw, *part = _conv_bwd(du1, proj, wf_dw, d_half, rs_up.take(2))
    rs_up.done(part)
    big_mid = [gw_sb.reshape(4, 2, d_half, d // NDEV), gw_cv.reshape(4, 2, d_half, d // NDEV),
               gw_o.reshape(4, 2, d // NDEV, d)]
    sib_mid = _run_comm("exchange_sibling_mid", _sibling_comm(big_mid))
    sums_mid = [_pair_sum(g, r, core) for g, r in zip(big_mid, sib_mid)]
    dq, dk, dv, r_up, r_sb, r_cv, r_o = _attn_bwd(proj, do_sb, n_heads, _join(rs_up.take(6), _chips_comm(sums_mid)))
    dproj = jnp.concatenate([dq, dk, dv, dglu_a, dglu_b, dgate_sb, dgate_cv], axis=1)
    db_in = _colsum(dproj)
    gw_in, r_down = _mm_dw_cols("w_in_grad", h_t, dproj, comm=rs_down.take(4))
    big_in = gw_in.reshape(4, 2, d, gw_in.shape[2])
    sib_in, = _run_comm("exchange_sibling_in", _sibling_comm([big_in]))
    dh, r_in = _mm_cols_t("w_in_bwd", dproj, wg_in, comm=_chips_comm([_pair_sum(big_in, sib_in, core)]))
    grad_x, dg_pre_mix = _prenorm_bwd(dx1, dh, xs, g_pre_mix)

    small = [dg_pre_mix, db_in, dw_dw.reshape(1, -1), db_dw, dg_ln, db_ln, dg_post_mix, dg_pre_mlp, dg_post_mlp]
    sizes = [a.shape[1] for a in small]
    packed = jnp.concatenate(small, axis=1).reshape(-1, LANES)
    total = _sum_small(_all_gather_small(packed)).reshape(1, -1)
    offs = [0]
    for n in sizes:
        offs.append(offs[-1] + n)
    (g_g_pre_mix, g_b_in, g_w_dw_flat, g_b_dw, g_g_conv_ln, g_b_conv_ln, g_g_post_mix, g_g_pre_mlp,
     g_g_post_mlp) = [total[:, offs[k]:offs[k + 1]] for k in range(len(sizes))]
    ch = w_dw.shape[2]
    g_w_dw = lax.dynamic_slice_in_dim(g_w_dw_flat.reshape(CONV_PAD, d_half), dev * ch, ch, axis=1)[:CONV_WIDTH]

    loss = lax.psum(loss_part[0, 0], ("x", "y", "c"))

    res = {}
    res["w_up"] = _adamw_big("adamw_w_up", w_up[0], m_w_up[0], v_w_up[0], r_up)
    res["w_down"] = _adamw_big("adamw_w_down", w_down[0], m_w_down[0], v_w_down[0], r_down)
    res["g_pre_mix"] = _adamw_small("adamw_g_pre_mix", g_pre_mix, m_g_pre_mix, v_g_pre_mix, g_g_pre_mix)
    res["w_in"] = _adamw_big("adamw_w_in", w_in[0], m_w_in[0], v_w_in[0], r_in)
    res["b_in"] = _adamw_small("adamw_b_in", b_in, m_b_in, v_b_in, g_b_in)
    res["w_dw"] = _adamw_small("adamw_w_dw", w_dw[0], m_w_dw[0], v_w_dw[0], g_w_dw)
    res["b_dw"] = _adamw_small("adamw_b_dw", b_dw, m_b_dw, v_b_dw, g_b_dw)
    res["g_conv_ln"] = _adamw_small("adamw_g_conv_ln", g_conv_ln, m_g_conv_ln, v_g_conv_ln, g_g_conv_ln)
    res["b_conv_ln"] = _adamw_small("adamw_b_conv_ln", b_conv_ln, m_b_conv_ln, v_b_conv_ln, g_b_conv_ln)
    res["w_sb_out"] = _adamw_big("adamw_w_sb_out", w_sb_out[0], m_w_sb_out[0], v_w_sb_out[0], r_sb)
    res["w_conv_out"] = _adamw_big("adamw_w_conv_out", w_conv_out[0], m_w_conv_out[0], v_w_conv_out[0], r_cv)
    res["w_o"] = _adamw_big("adamw_w_o", w_o[0], m_w_o[0], v_w_o[0], r_o)
    res["g_post_mix"] = _adamw_small("adamw_g_post_mix", g_post_mix, m_g_post_mix, v_g_post_mix, g_g_post_mix)
    res["g_pre_mlp"] = _adamw_small("adamw_g_pre_mlp", g_pre_mlp, m_g_pre_mlp, v_g_pre_mlp, g_g_pre_mlp)
    res["g_post_mlp"] = _adamw_small("adamw_g_post_mlp", g_post_mlp, m_g_post_mlp, v_g_post_mlp, g_g_post_mlp)

    names = ["g_pre_mix", "w_in", "b_in", "w_dw", "b_dw", "g_conv_ln", "b_conv_ln", "w_sb_out", "w_conv_out", "w_o",
             "g_post_mix", "g_pre_mlp", "w_up", "w_down", "g_post_mlp"]
    three_d = {"w_in", "w_dw", "w_sb_out", "w_conv_out", "w_o", "w_up", "w_down"}

    def shaped(nm, arr):
        return arr[None] if nm in three_d else arr

    out = [loss, grad_x[None]]
    for k in range(4):
        out += [shaped(nm, res[nm][k]) for nm in names]
    return tuple(out)
```

```python
import functools
import math

import jax
import jax.numpy as jnp
from jax import lax
from jax.experimental import pallas as pl
from jax.experimental.pallas import tpu as pltpu

F32 = jnp.float32
BF16 = jnp.bfloat16
NDEV = 8
LANES = 128
EPS = 1e-6
CONV_WIDTH = 31
CONV_PAD = 32
HEAD_DIM = 128
ADAM_LR = 0.001
ADAM_B1 = 0.9
ADAM_B2 = 0.999
ADAM_EPS = 1e-08
ADAM_WD = 0.01
ADAM_STEP = 10
VMEM_LIMIT = 56 * 1024 * 1024
MESH = pl.DeviceIdType.MESH
ANY = pl.BlockSpec(memory_space=pl.ANY)


def _tile(n, pref):
    t = min(n, pref)
    assert n % t == 0, (n, t)
    return t


def _sigmoid(v):
    return 1.0 / (1.0 + jnp.exp(-v))


def _position():
    return lax.axis_index("x"), lax.axis_index("y"), lax.axis_index("c")


class _Comm:
    def __init__(self, ins, outs, scratch, start, finish, aliases=None, mid=None):
        self.ins, self.outs, self.scratch = list(ins), list(outs), list(scratch)
        self.start, self.finish, self.aliases = start, finish, dict(aliases or {})
        self.mid = mid


_NO_COMM = _Comm([], [], [], None, None)


def _call(name, body, grid, in_specs, out_specs, out_shape, scratch_shapes, sem, args, comm=None):
    comm = comm or _NO_COMM
    n_in, n_out, n_scr = len(in_specs), len(out_specs), len(scratch_shapes)
    n_cin, n_cout = len(comm.ins), len(comm.outs)

    n_steps = math.prod(grid)
    mid_step = n_steps - max(1, n_steps // 8) if (comm.mid is not None and n_steps >= 4) else None

    def edge(c_ins, c_outs, c_scr, at_step, actions):
        linear = 0
        for ax, g in enumerate(grid):
            linear = linear * g + pl.program_id(ax)

        @pl.when(linear == at_step)
        def _():
            for act in actions:
                act(c_ins, c_outs, c_scr)

    def wrapped(*refs):
        ins, c_ins = refs[:n_in], refs[n_in:n_in + n_cin]
        pos = n_in + n_cin
        outs, c_outs = refs[pos:pos + n_out], refs[pos + n_out:pos + n_out + n_cout]
        pos += n_out + n_cout
        scr, c_scr = refs[pos:pos + n_scr], refs[pos + n_scr:]
        if n_cin:
            edge(c_ins, c_outs, c_scr, 0, [comm.start])
            if mid_step is not None:
                edge(c_ins, c_outs, c_scr, mid_step, [comm.mid])
        body(*ins, *outs, *scr)
        if n_cin:
            late = [comm.finish] if (mid_step is not None or comm.mid is None) else [comm.mid, comm.finish]
            edge(c_ins, c_outs, c_scr, n_steps - 1, late)

    if n_cin:
        sem = ("arbitrary",) * len(grid)
    return pl.pallas_call(
        wrapped, name=name, grid=grid,
        in_specs=[*in_specs, *[ANY] * n_cin], out_specs=[*out_specs, *[ANY] * n_cout],
        out_shape=[*out_shape, *comm.outs], scratch_shapes=[*scratch_shapes, *comm.scratch],
        input_output_aliases={n_in + ci: n_out + co for ci, co in comm.aliases.items()},
        compiler_params=pltpu.CompilerParams(dimension_semantics=sem, vmem_limit_bytes=VMEM_LIMIT),
    )(*args, *comm.ins)


def _run_comm(name, comm):
    n_in, n_out = len(comm.ins), len(comm.outs)

    def body(*refs):
        ins, outs, scr = refs[:n_in], refs[n_in:n_in + n_out], refs[n_in + n_out:]
        comm.start(ins, outs, scr)
        if comm.mid is not None:
            comm.mid(ins, outs, scr)
        comm.finish(ins, outs, scr)

    return pl.pallas_call(body, name=name, in_specs=[ANY] * n_in, out_specs=[ANY] * n_out, out_shape=comm.outs,
                          scratch_shapes=comm.scratch,
                          input_output_aliases=comm.aliases)(*comm.ins)


def _rows_of(ref, rows):
    return ref if rows is None else ref.at[pl.ds(rows[0], rows[1])]


def _ag_comm(shards, rows=None, into=None):
    n = len(shards)

    def parts(ins, outs, scr):
        send_sems, recv_sems, local_sems = scr
        x, y, c = _position()
        chips = [(1 - x, y), (x, 1 - y), (1 - x, 1 - y)]

        def copy(a, k, block, to, own=False):
            px, py, pc = block
            dst = _rows_of(outs[a].at[4 * px + 2 * py + pc], rows)
            return pltpu.make_async_remote_copy(src_ref=_rows_of(ins[a], rows) if own else dst, dst_ref=dst,
                                                send_sem=send_sems.at[a, k], recv_sem=recv_sems.at[a, k],
                                                device_id=to, device_id_type=MESH)

        mine = [pltpu.make_async_copy(_rows_of(ins[a], rows), _rows_of(outs[a].at[4 * x + 2 * y + c], rows),
                                      local_sems.at[a]) for a in range(n)]
        first = []
        for a in range(n):
            first.append(copy(a, 0, (x, y, c), (x, y, 1 - c), own=True))
            first += [copy(a, 1 + j, (x, y, c), (*chip, c), own=True) for j, chip in enumerate(chips)]
        return copy, mine, first, chips, (x, y, c), (x, y, 1 - c)

    def start(ins, outs, scr):
        _, mine, first, _, _, _ = parts(ins, outs, scr)
        for cp in mine + first:
            cp.start()

    def mid(ins, outs, scr):
        copy, _, _, chips, me, sibling = parts(ins, outs, scr)
        for a in range(n):
            for j, chip in enumerate(chips):
                copy(a, 1 + j, (*chip, me[2]), me).wait_recv()
                copy(a, 4 + j, (*chip, me[2]), sibling).start()

    def finish(ins, outs, scr):
        copy, mine, first, chips, me, sibling = parts(ins, outs, scr)
        c = me[2]
        passed = [copy(a, 4 + j, (*chip, c), sibling) for a in range(n) for j, chip in enumerate(chips)]
        for a in range(n):
            copy(a, 0, sibling, me).wait_recv()
            for j, chip in enumerate(chips):
                copy(a, 4 + j, (*chip, 1 - c), me).wait_recv()
        for cp in first + passed:
            cp.wait_send()
        for cp in mine:
            cp.wait()

    return _Comm([*shards, *(into or [])], [jax.ShapeDtypeStruct((NDEV, *sh.shape), sh.dtype) for sh in shards],
                 [pltpu.SemaphoreType.DMA((n, 7)), pltpu.SemaphoreType.DMA((n, 7)), pltpu.SemaphoreType.DMA((n,))],
                 start, finish, {n + a: a for a in range(n)} if into else None, mid)


def _sibling_comm(grads):
    n = len(grads)

    def copies(ins, outs, scr):
        send_sems, recv_sems = scr
        x, y, c = _position()
        return [pltpu.make_async_remote_copy(src_ref=ins[a].at[k, 1 - c], dst_ref=outs[a].at[k],
                                             send_sem=send_sems.at[a, k], recv_sem=recv_sems.at[a, k],
                                             device_id=(x, y, 1 - c), device_id_type=MESH)
                for a in range(n) for k in range(4)]

    def start(ins, outs, scr):
        for cp in copies(ins, outs, scr):
            cp.start()

    def finish(ins, outs, scr):
        for cp in copies(ins, outs, scr):
            cp.wait()

    return _Comm(grads, [jax.ShapeDtypeStruct((4, *g.shape[2:]), g.dtype) for g in grads],
                 [pltpu.SemaphoreType.DMA((n, 4)), pltpu.SemaphoreType.DMA((n, 4))], start, finish)


def _chips_comm(sums, rows=None, into=None):
    n = len(sums)

    def copies(ins, outs, scr):
        send_sems, recv_sems, local_sems = scr
        x, y, c = _position()
        chips = [(1 - x, y), (x, 1 - y), (1 - x, 1 - y)]
        own = [pltpu.make_async_copy(_rows_of(ins[a].at[2 * x + y], rows), _rows_of(outs[a].at[3], rows), local_sems.at[a])
               for a in range(n)]
        remote = [pltpu.make_async_remote_copy(src_ref=_rows_of(ins[a].at[2 * px + py], rows),
                                               dst_ref=_rows_of(outs[a].at[j], rows),
                                               send_sem=send_sems.at[a, j], recv_sem=recv_sems.at[a, j],
                                               device_id=(px, py, c), device_id_type=MESH)
                  for a in range(n) for j, (px, py) in enumerate(chips)]
        return own + remote

    def start(ins, outs, scr):
        for cp in copies(ins, outs, scr):
            cp.start()

    def finish(ins, outs, scr):
        for cp in copies(ins, outs, scr):
            cp.wait()

    return _Comm([*sums, *(into or [])], [jax.ShapeDtypeStruct(sm.shape, sm.dtype) for sm in sums],
                 [pltpu.SemaphoreType.DMA((n, 3)), pltpu.SemaphoreType.DMA((n, 3)), pltpu.SemaphoreType.DMA((n,))],
                 start, finish, {n + a: a for a in range(n)} if into else None)


def _join(c1, c2):
    n_in, n_out, n_scr = len(c1.ins), len(c1.outs), len(c1.scratch)
    aliases = dict(c1.aliases)
    aliases.update({n_in + ci: n_out + co for ci, co in c2.aliases.items()})

    def start(ins, outs, scr):
        c1.start(ins[:n_in], outs[:n_out], scr[:n_scr])
        c2.start(ins[n_in:], outs[n_out:], scr[n_scr:])

    def mid(ins, outs, scr):
        if c1.mid is not None:
            c1.mid(ins[:n_in], outs[:n_out], scr[:n_scr])
        if c2.mid is not None:
            c2.mid(ins[n_in:], outs[n_out:], scr[n_scr:])

    def finish(ins, outs, scr):
        c1.finish(ins[:n_in], outs[:n_out], scr[:n_scr])
        c2.finish(ins[n_in:], outs[n_out:], scr[n_scr:])

    return _Comm(c1.ins + c2.ins, c1.outs + c2.outs, c1.scratch + c2.scratch, start, finish, aliases,
                 mid if (c1.mid is not None or c2.mid is not None) else None)


class _Chunked:
    def __init__(self, make, arrays, n_rows, n_chunks):
        self.make, self.arrays, self.into = make, arrays, None
        step = n_rows // n_chunks
        assert step * n_chunks == n_rows
        self.todo = [(k * step, step) for k in range(n_chunks)]

    def take(self, count=1):
        r0, nr = self.todo[0][0], sum(t[1] for t in self.todo[:count])
        self.todo = self.todo[count:]
        return self.make(self.arrays, (r0, nr), self.into)

    def done(self, outs):
        self.into = list(outs)
        return self.into


def _all_gather_small(part):
    def body(in_ref, out_ref, send_sems, recv_sems, local_sem):
        x, y, c = _position()
        me = 4 * x + 2 * y + c
        mine = pltpu.make_async_copy(in_ref, out_ref.at[me], local_sem)
        mine.start()
        flips = [(fx, fy, fc) for fx in (0, 1) for fy in (0, 1) for fc in (0, 1)][1:]
        copies = []
        for k, (fx, fy, fc) in enumerate(flips):
            cp = pltpu.make_async_remote_copy(src_ref=in_ref, dst_ref=out_ref.at[me], send_sem=send_sems.at[k],
                                              recv_sem=recv_sems.at[k],
                                              device_id=(x ^ fx, y ^ fy, c ^ fc), device_id_type=MESH)
            cp.start()
            copies.append(cp)
        for k, (fx, fy, fc) in enumerate(flips):
            peer = 4 * (x ^ fx) + 2 * (y ^ fy) + (c ^ fc)
            pltpu.make_async_remote_copy(src_ref=in_ref, dst_ref=out_ref.at[peer], send_sem=send_sems.at[k],
                                         recv_sem=recv_sems.at[k], device_id=(x, y, c), device_id_type=MESH).wait_recv()
        for cp in copies:
            cp.wait_send()
        mine.wait()

    return pl.pallas_call(
        body, name="all_gather_small", in_specs=[ANY], out_specs=ANY,
        out_shape=jax.ShapeDtypeStruct((NDEV, *part.shape), part.dtype),
        scratch_shapes=[pltpu.SemaphoreType.DMA((7,)), pltpu.SemaphoreType.DMA((7,)), pltpu.SemaphoreType.DMA],
    )(part)


def _pair_sum(g, recv, core):
    _, _, r, c = g.shape
    tr = _tile(r, 512)

    def body(core_ref, g_ref, r_ref, o_ref):
        o_ref[...] = (g_ref[...].astype(F32) + r_ref[...].astype(F32)).astype(o_ref.dtype)

    return pl.pallas_call(
        body, name="pair_sum",
        grid_spec=pltpu.PrefetchScalarGridSpec(
            num_scalar_prefetch=1, grid=(4, r // tr),
            in_specs=[pl.BlockSpec((None, None, tr, c), lambda k, i, core_ref: (k, core_ref[0], i, 0)),
                      pl.BlockSpec((None, tr, c), lambda k, i, core_ref: (k, i, 0))],
            out_specs=pl.BlockSpec((None, tr, c), lambda k, i, core_ref: (k, i, 0))),
        out_shape=jax.ShapeDtypeStruct((4, r, c), g.dtype),
        compiler_params=pltpu.CompilerParams(dimension_semantics=("parallel", "parallel"), vmem_limit_bytes=VMEM_LIMIT),
    )(core, g, recv)


def _sum_small(gathered):
    _, r, l = gathered.shape

    def body(g_ref, o_ref):
        acc = g_ref[0]
        for d in range(1, NDEV):
            acc = acc + g_ref[d]
        o_ref[...] = acc

    return pl.pallas_call(
        body, name="sum_small", in_specs=[pl.BlockSpec((NDEV, r, l), lambda: (0, 0, 0))],
        out_specs=pl.BlockSpec((r, l), lambda: (0, 0)), out_shape=jax.ShapeDtypeStruct((r, l), F32),
    )(gathered)


def _matmul(name, a, b, outs, grid, a_spec, b_spec, out_specs, n_red, nt, acc_shape,
            extra=(), extra_specs=(), epilogue=None, comm=None):
    n_extra, n_out = len(extra), len(outs)
    red_axes = tuple(range(len(grid) - n_red, len(grid)))
    red_sizes = tuple(grid[ax] for ax in red_axes)
    single = all(sz == 1 for sz in red_sizes)
    dims = (((1,), (1,)), ((), ())) if nt else (((1,), (0,)), ((), ()))

    def body(*refs):
        a_ref, b_ref = refs[0], refs[1]
        ex_refs = refs[2:2 + n_extra]
        o_refs = refs[2 + n_extra:2 + n_extra + n_out]
        acc_ref = refs[-1]

        def write(acc):
            vals = (acc,) if epilogue is None else epilogue(acc, *[r[...] for r in ex_refs])
            for o_ref, val in zip(o_refs, vals):
                o_ref[...] = val.astype(o_ref.dtype)

        if len(b_ref.shape) == 3:
            w = b_ref.shape[2]
            part = sum(lax.dot_general(a_ref[:, p * w:(p + 1) * w].astype(BF16), b_ref[p].astype(BF16), dims,
                                       preferred_element_type=F32) for p in range(b_ref.shape[0]))
        else:
            part = lax.dot_general(a_ref[...].astype(BF16), b_ref[...].astype(BF16), dims, preferred_element_type=F32)
        if single:
            write(part)
        else:
            ks = [pl.program_id(ax) for ax in red_axes]
            first = functools.reduce(jnp.logical_and, [k == 0 for k in ks])
            last = functools.reduce(jnp.logical_and, [k == sz - 1 for k, sz in zip(ks, red_sizes)])

            @pl.when(first)
            def _():
                acc_ref[...] = part

            @pl.when(jnp.logical_not(first))
            def _():
                acc_ref[...] += part

            @pl.when(last)
            def _():
                write(acc_ref[...])

    sem = ("parallel",) * (len(grid) - n_red) + ("arbitrary",) * n_red
    return _call(name, body, grid, [a_spec, b_spec, *extra_specs], list(out_specs), list(outs),
                 [pltpu.VMEM((8, LANES) if single else acc_shape, F32)], sem, (a, b, *extra), comm)


def _mm_cols(name, a, wg, bias=None, out_dtype=F32, comm=None):
    m, k = a.shape
    _, _, ns = wg.shape
    tm, tk = _tile(m, 1024), _tile(k, 2048)
    grid = (m // tm, NDEV, k // tk)
    extra, extra_specs, epi = (), (), None
    if bias is not None:
        extra, extra_specs = (bias,), (pl.BlockSpec((1, ns), lambda i, d, kk: (0, d)),)
        epi = lambda acc, bv: (acc + bv,)
    return _matmul(name, a, wg, [jax.ShapeDtypeStruct((m, NDEV * ns), out_dtype)], grid,
                   pl.BlockSpec((tm, tk), lambda i, d, kk: (i, kk)),
                   pl.BlockSpec((None, tk, ns), lambda i, d, kk: (d, kk, 0)),
                   [pl.BlockSpec((tm, ns), lambda i, d, kk: (i, d))], 1, False, (tm, ns),
                   extra, extra_specs, epi, comm)


def _mm_cols_t(name, a, wg, comm=None):
    m, _ = a.shape
    _, n, ns = wg.shape
    tm, tn = _tile(m, 1024), _tile(n, 1024)
    per = 2
    grid = (m // tm, n // tn, NDEV // per)
    return _matmul(name, a, wg, [jax.ShapeDtypeStruct((m, n), F32)], grid,
                   pl.BlockSpec((tm, per * ns), lambda i, j, d: (i, d)),
                   pl.BlockSpec((per, tn, ns), lambda i, j, d: (d, j, 0)),
                   [pl.BlockSpec((tm, tn), lambda i, j, d: (i, j))], 1, True, (tm, tn), comm=comm)


def _mm_dw_cols(name, at, g, comm=None):
    m, t = at.shape
    ns = g.shape[1] // NDEV
    tm, tk = _tile(m, 1024), _tile(t, 2048)
    grid = (m // tm, NDEV, t // tk)
    return _matmul(name, at, g, [jax.ShapeDtypeStruct((NDEV, m, ns), BF16)], grid,
                   pl.BlockSpec((tm, tk), lambda i, d, kk: (i, kk)),
                   pl.BlockSpec((tk, ns), lambda i, d, kk: (kk, d)),
                   [pl.BlockSpec((None, tm, ns), lambda i, d, kk: (d, i, 0))], 1, False, (tm, ns), comm=comm)


def _mm_plain(name, a, b, nt, out_dtype, extra=(), extra_specs=(), epilogue=None, outs=None, out_specs=None, comm=None):
    m, k = a.shape
    n = b.shape[0] if nt else b.shape[1]
    tm, tn, tk = _tile(m, 1024), _tile(n, 1024), _tile(k, 2048)
    grid = (m // tm, n // tn, k // tk)
    b_spec = (pl.BlockSpec((tn, tk), lambda i, j, kk: (j, kk)) if nt
              else pl.BlockSpec((tk, tn), lambda i, j, kk: (kk, j)))
    if outs is None:
        outs = [jax.ShapeDtypeStruct((m, n), out_dtype)]
        out_specs = [pl.BlockSpec((tm, tn), lambda i, j, kk: (i, j))]
    return _matmul(name, a, b, outs, grid, pl.BlockSpec((tm, tk), lambda i, j, kk: (i, kk)), b_spec,
                   out_specs, 1, nt, (tm, tn), extra, extra_specs, epilogue, comm)


def _rms_rows(v):
    return lax.rsqrt(jnp.mean(v * v, axis=-1, keepdims=True) + EPS)


def _prenorm(x, g):
    s, d = x.shape
    ts = _tile(s, 256)

    def body(x_ref, g_ref, h_ref, ht_ref):
        xv = x_ref[...]
        h = xv * _rms_rows(xv) * g_ref[...]
        h_ref[...] = h.astype(BF16)
        ht_ref[...] = h.T.astype(BF16)

    return _call("prenorm", body, (s // ts,),
                 [pl.BlockSpec((ts, d), lambda i: (i, 0)), pl.BlockSpec((1, d), lambda i: (0, 0))],
                 [pl.BlockSpec((ts, d), lambda i: (i, 0)), pl.BlockSpec((d, ts), lambda i: (0, i))],
                 [jax.ShapeDtypeStruct((s, d), BF16), jax.ShapeDtypeStruct((d, s), BF16)], [], ("parallel",), (x, g))


def _ln_silu(u1, g, b):
    s, c = u1.shape
    ts = _tile(s, 256)

    def body(u_ref, g_ref, b_ref, o_ref, ot_ref):
        u = u_ref[...]
        mu = jnp.mean(u, axis=-1, keepdims=True)
        var = jnp.mean(jnp.square(u - mu), axis=-1, keepdims=True)
        u2 = (u - mu) * lax.rsqrt(var + EPS) * g_ref[...] + b_ref[...]
        u3 = u2 * _sigmoid(u2)
        o_ref[...] = u3.astype(BF16)
        ot_ref[...] = u3.T.astype(BF16)

    vec = pl.BlockSpec((1, c), lambda i: (0, 0))
    return _call("ln_silu", body, (s // ts,), [pl.BlockSpec((ts, c), lambda i: (i, 0)), vec, vec],
                 [pl.BlockSpec((ts, c), lambda i: (i, 0)), pl.BlockSpec((c, ts), lambda i: (0, i))],
                 [jax.ShapeDtypeStruct((s, c), BF16), jax.ShapeDtypeStruct((c, s), BF16)], [], ("parallel",), (u1, g, b))


def _ln_silu_bwd(du3, u1, g, b, comm=None):
    s, c = u1.shape
    ts = _tile(s, 256)

    def body(d_ref, u_ref, g_ref, b_ref, du1_ref, dg_ref, db_ref):
        @pl.when(pl.program_id(0) == 0)
        def _():
            dg_ref[...] = jnp.zeros_like(dg_ref)
            db_ref[...] = jnp.zeros_like(db_ref)

        u = u_ref[...]
        mu = jnp.mean(u, axis=-1, keepdims=True)
        var = jnp.mean(jnp.square(u - mu), axis=-1, keepdims=True)
        rstd = lax.rsqrt(var + EPS)
        uhat = (u - mu) * rstd
        u2 = uhat * g_ref[...] + b_ref[...]
        sg = _sigmoid(u2)
        du2 = d_ref[...] * (sg * (1.0 + u2 * (1.0 - sg)))
        dg_ref[...] += jnp.sum(du2 * uhat, axis=0, keepdims=True)
        db_ref[...] += jnp.sum(du2, axis=0, keepdims=True)
        duh = du2 * g_ref[...]
        du1_ref[...] = rstd * (duh - jnp.mean(duh, axis=-1, keepdims=True)
                               - uhat * jnp.mean(duh * uhat, axis=-1, keepdims=True))

    row = pl.BlockSpec((ts, c), lambda i: (i, 0))
    vec = pl.BlockSpec((1, c), lambda i: (0, 0))
    return _call("ln_silu_bwd", body, (s // ts,), [row, row, vec, vec], [row, vec, vec],
                 [jax.ShapeDtypeStruct((s, c), F32), jax.ShapeDtypeStruct((1, c), F32), jax.ShapeDtypeStruct((1, c), F32)],
                 [], ("arbitrary",), (du3, u1, g, b), comm)


def _merge(proj, o_sbp, o_cv, d, comm=None):
    s = proj.shape[0]
    w = d // 2
    ts = _tile(s, 256)

    def body(gs_ref, gc_ref, a_ref, b_ref, m_ref, mt_ref):
        mg = _sigmoid(gs_ref[...]) * a_ref[...] + _sigmoid(gc_ref[...]) * b_ref[...]
        m_ref[...] = mg.astype(BF16)
        mt_ref[...] = mg.T.astype(BF16)

    blk = pl.BlockSpec((ts, w), lambda i, j: (i, j))
    return _call("merge", body, (s // ts, 2),
                 [pl.BlockSpec((ts, w), lambda i, j: (i, 5 + j)), pl.BlockSpec((ts, w), lambda i, j: (i, 7 + j)), blk, blk],
                 [blk, pl.BlockSpec((w, ts), lambda i, j: (j, i))],
                 [jax.ShapeDtypeStruct((s, d), BF16), jax.ShapeDtypeStruct((d, s), BF16)], [],
                 ("parallel", "parallel"), (proj, proj, o_sbp, o_cv), comm)


def _merge_bwd(dmerged, proj, o_sbp, o_cv, d, comm=None):
    s = proj.shape[0]
    w = d // 2
    ts = _tile(s, 256)

    def body(dm_ref, gs_ref, gc_ref, a_ref, b_ref, da_ref, db_ref, dgs_ref, dgc_ref):
        dm = dm_ref[...]
        ss = _sigmoid(gs_ref[...])
        sc = _sigmoid(gc_ref[...])
        da_ref[...] = (dm * ss).astype(BF16)
        db_ref[...] = (dm * sc).astype(BF16)
        dgs_ref[...] = (dm * a_ref[...] * ss * (1.0 - ss)).astype(BF16)
        dgc_ref[...] = (dm * b_ref[...] * sc * (1.0 - sc)).astype(BF16)

    blk = pl.BlockSpec((ts, w), lambda i, j: (i, j))
    sds = jax.ShapeDtypeStruct((s, d), BF16)
    return _call("merge_bwd", body, (s // ts, 2),
                 [blk, pl.BlockSpec((ts, w), lambda i, j: (i, 5 + j)), pl.BlockSpec((ts, w), lambda i, j: (i, 7 + j)),
                  blk, blk],
                 [blk, blk, blk, blk], [sds, sds, sds, sds], [], ("parallel", "parallel"),
                 (dmerged, proj, proj, o_sbp, o_cv), comm)


def _postnorm_mix(x, y, g_post, g_pre, comm=None):
    s, d = x.shape
    ts = _tile(s, 256)

    def body(x_ref, y_ref, gp_ref, gn_ref, x1_ref, h_ref, ht_ref):
        yv = y_ref[...]
        x1 = x_ref[...] + yv * _rms_rows(yv) * gp_ref[...]
        x1_ref[...] = x1
        h = x1 * _rms_rows(x1) * gn_ref[...]
        h_ref[...] = h.astype(BF16)
        ht_ref[...] = h.T.astype(BF16)

    row = pl.BlockSpec((ts, d), lambda i: (i, 0))
    vec = pl.BlockSpec((1, d), lambda i: (0, 0))
    return _call("postnorm_mix", body, (s // ts,), [row, row, vec, vec],
                 [row, row, pl.BlockSpec((d, ts), lambda i: (0, i))],
                 [jax.ShapeDtypeStruct((s, d), F32), jax.ShapeDtypeStruct((s, d), BF16), jax.ShapeDtypeStruct((d, s), BF16)],
                 [], ("parallel",), (x, y, g_post, g_pre), comm)


def _rms_bwd(dout, vin, g):
    r = _rms_rows(vin)
    vhat = vin * r
    dyh = dout * g
    dvin = r * (dyh - vhat * jnp.mean(dyh * vhat, axis=-1, keepdims=True))
    return dvin, jnp.sum(dout * vhat, axis=0, keepdims=True)


def _loss_head(x1, f2, tgt, g):
    s, d = x1.shape
    ts = _tile(s, 256)

    def body(x1_ref, f_ref, t_ref, g_ref, dx2_ref, df2_ref, dg_ref, loss_ref):
        @pl.when(pl.program_id(0) == 0)
        def _():
            dg_ref[...] = jnp.zeros_like(dg_ref)
            loss_ref[...] = jnp.zeros_like(loss_ref)

        fv = f_ref[...]
        x2 = x1_ref[...] + fv * _rms_rows(fv) * g_ref[...]
        err = x2 - t_ref[...]
        loss_ref[...] += 0.5 * jnp.sum(jnp.mean(err * err, axis=-1, keepdims=True), axis=0, keepdims=True)
        dx2 = err * (1.0 / d)
        dx2_ref[...] = dx2
        df2, dg = _rms_bwd(dx2, fv, g_ref[...])
        df2_ref[...] = df2.astype(BF16)
        dg_ref[...] += dg

    row = pl.BlockSpec((ts, d), lambda i: (i, 0))
    vec = pl.BlockSpec((1, d), lambda i: (0, 0))
    return _call("loss_head", body, (s // ts,), [row, row, row, vec],
                 [row, row, vec, pl.BlockSpec((1, LANES), lambda i: (0, 0))],
                 [jax.ShapeDtypeStruct((s, d), F32), jax.ShapeDtypeStruct((s, d), BF16),
                  jax.ShapeDtypeStruct((1, d), F32), jax.ShapeDtypeStruct((1, LANES), F32)],
                 [], ("arbitrary",), (x1, f2, tgt, g))


def _midnorm_bwd(dx2, dh2, x1, y, g_post, g_pre, comm=None):
    s, d = x1.shape
    ts = _tile(s, 256)

    def body(dx2_ref, dh_ref, x1_ref, y_ref, gp_ref, gn_ref, dx1_ref, dy_ref, dgn_ref, dgp_ref):
        @pl.when(pl.program_id(0) == 0)
        def _():
            dgn_ref[...] = jnp.zeros_like(dgn_ref)
            dgp_ref[...] = jnp.zeros_like(dgp_ref)

        dxa, dgn = _rms_bwd(dh_ref[...], x1_ref[...], gn_ref[...])
        dx1 = dx2_ref[...] + dxa
        dx1_ref[...] = dx1
        dy, dgp = _rms_bwd(dx1, y_ref[...], gp_ref[...])
        dy_ref[...] = dy.astype(BF16)
        dgn_ref[...] += dgn
        dgp_ref[...] += dgp

    row = pl.BlockSpec((ts, d), lambda i: (i, 0))
    vec = pl.BlockSpec((1, d), lambda i: (0, 0))
    return _call("midnorm_bwd", body, (s // ts,), [row, row, row, row, vec, vec], [row, row, vec, vec],
                 [jax.ShapeDtypeStruct((s, d), F32), jax.ShapeDtypeStruct((s, d), BF16),
                  jax.ShapeDtypeStruct((1, d), F32), jax.ShapeDtypeStruct((1, d), F32)],
                 [], ("arbitrary",), (dx2, dh2, x1, y, g_post, g_pre), comm)


def _prenorm_bwd(dx1, dh, x, g, comm=None):
    s, d = x.shape
    ts = _tile(s, 256)

    def body(dx1_ref, dh_ref, x_ref, g_ref, dx_ref, dg_ref):
        @pl.when(pl.program_id(0) == 0)
        def _():
            dg_ref[...] = jnp.zeros_like(dg_ref)

        dxa, dg = _rms_bwd(dh_ref[...], x_ref[...], g_ref[...])
        dx_ref[...] = dx1_ref[...] + dxa
        dg_ref[...] += dg

    row = pl.BlockSpec((ts, d), lambda i: (i, 0))
    vec = pl.BlockSpec((1, d), lambda i: (0, 0))
    return _call("prenorm_bwd", body, (s // ts,), [row, row, row, vec], [row, vec],
                 [jax.ShapeDtypeStruct((s, d), F32), jax.ShapeDtypeStruct((1, d), F32)],
                 [], ("arbitrary",), (dx1, dh, x, g), comm)


def _colsum(a, comm=None):
    s, n = a.shape
    ts = _tile(s, 256)

    def body(a_ref, o_ref):
        @pl.when(pl.program_id(0) == 0)
        def _():
            o_ref[...] = jnp.zeros_like(o_ref)

        o_ref[...] += jnp.sum(a_ref[...].astype(F32), axis=0, keepdims=True)

    return _call("colsum", body, (s // ts,), [pl.BlockSpec((ts, n), lambda i: (i, 0))],
                 [pl.BlockSpec((1, n), lambda i: (0, 0))], [jax.ShapeDtypeStruct((1, n), F32)], [], ("arbitrary",), (a,),
                 comm)


def _shift_rows(win, off, t):
    n = win.shape[0]
    if off == 0:
        return win[:t]
    return pltpu.roll(win, n - off, axis=0)[:t]


def _conv_fwd(proj, w_pad, b_dw, c_total, comm=None):
    s = proj.shape[0]
    nct = c_total // LANES
    t = _tile(s, 256)

    def body(ga_ref, gb_ref, w_ref, b_ref, o_ref, u0_ref):
        u0_ref[pl.ds(0, CONV_PAD), :] = jnp.zeros((CONV_PAD, LANES), F32)
        u0_ref[pl.ds(CONV_PAD, s), :] = ga_ref[...] * _sigmoid(gb_ref[...])
        wv = w_ref[...]

        def chunk(r, carry):
            r0 = pl.multiple_of(r * t, t)
            win = u0_ref[pl.ds(r0, t + CONV_PAD), :]
            acc = jnp.broadcast_to(b_ref[...], (t, LANES))
            for j in range(CONV_WIDTH):
                acc = acc + wv[j:j + 1, :] * _shift_rows(win, j + CONV_PAD - (CONV_WIDTH - 1), t)
            o_ref[pl.ds(r0, t), :] = acc
            return carry

        lax.fori_loop(0, s // t, chunk, 0)

    return _call("conv_fwd", body, (nct,),
                 [pl.BlockSpec((s, LANES), lambda c: (0, 3 * nct + c)), pl.BlockSpec((s, LANES), lambda c: (0, 4 * nct + c)),
                  pl.BlockSpec((CONV_PAD, LANES), lambda c: (0, c)), pl.BlockSpec((1, LANES), lambda c: (0, c))],
                 [pl.BlockSpec((s, LANES), lambda c: (0, c))], [jax.ShapeDtypeStruct((s, c_total), F32)],
                 [pltpu.VMEM((s + CONV_PAD, LANES), F32)], ("parallel",), (proj, proj, w_pad, b_dw), comm)


def _conv_bwd(du1, proj, w_pad, c_total, comm=None):
    s = proj.shape[0]
    nct = c_total // LANES
    t = _tile(s, 256)

    def body(d_ref, ga_ref, gb_ref, w_ref, dga_ref, dgb_ref, dw_ref, db_ref, u0_ref, dp_ref):
        sg = _sigmoid(gb_ref[...])
        u0_ref[pl.ds(0, CONV_PAD), :] = jnp.zeros((CONV_PAD, LANES), F32)
        u0_ref[pl.ds(CONV_PAD, s), :] = ga_ref[...] * sg
        dp_ref[pl.ds(0, s), :] = d_ref[...]
        dp_ref[pl.ds(s, CONV_PAD), :] = jnp.zeros((CONV_PAD, LANES), F32)
        dw_ref[...] = jnp.zeros_like(dw_ref)
        db_ref[...] = jnp.sum(d_ref[...], axis=0, keepdims=True)
        wv = w_ref[...]

        def chunk(r, carry):
            r0 = pl.multiple_of(r * t, t)
            win = u0_ref[pl.ds(r0, t + CONV_PAD), :]
            dwin = dp_ref[pl.ds(r0, t + CONV_PAD), :]
            dcur = dwin[:t]
            du0 = jnp.zeros((t, LANES), F32)
            for j in range(CONV_WIDTH):
                du0 = du0 + wv[j:j + 1, :] * _shift_rows(dwin, CONV_WIDTH - 1 - j, t)
                sh = _shift_rows(win, j + CONV_PAD - (CONV_WIDTH - 1), t)
                dw_ref[j:j + 1, :] += jnp.sum(dcur * sh, axis=0, keepdims=True)
            gav = ga_ref[pl.ds(r0, t), :]
            sgv = _sigmoid(gb_ref[pl.ds(r0, t), :])
            dga_ref[pl.ds(r0, t), :] = (du0 * sgv).astype(BF16)
            dgb_ref[pl.ds(r0, t), :] = (du0 * gav * sgv * (1.0 - sgv)).astype(BF16)
            return carry

        lax.fori_loop(0, s // t, chunk, 0)

    col = pl.BlockSpec((s, LANES), lambda c: (0, c))
    return _call("conv_bwd", body, (nct,),
                 [col, pl.BlockSpec((s, LANES), lambda c: (0, 3 * nct + c)),
                  pl.BlockSpec((s, LANES), lambda c: (0, 4 * nct + c)), pl.BlockSpec((CONV_PAD, LANES), lambda c: (0, c))],
                 [col, col, pl.BlockSpec((CONV_PAD, LANES), lambda c: (0, c)), pl.BlockSpec((1, LANES), lambda c: (0, c))],
                 [jax.ShapeDtypeStruct((s, c_total), BF16), jax.ShapeDtypeStruct((s, c_total), BF16),
                  jax.ShapeDtypeStruct((CONV_PAD, c_total), F32), jax.ShapeDtypeStruct((1, c_total), F32)],
                 [pltpu.VMEM((s + CONV_PAD, LANES), F32), pltpu.VMEM((s + CONV_PAD, LANES), F32)],
                 ("parallel",), (du1, proj, proj, w_pad), comm)


TQ_PREF = 256
NU = 4
NU_BWD = 2
TK = 256


def _split_dot(v, tri):
    hi = v.astype(BF16)
    lo = (v - hi.astype(F32)).astype(BF16)
    return (jnp.dot(hi, tri, preferred_element_type=F32) + jnp.dot(lo, tri, preferred_element_type=F32))


def _causal_mask(i, j, tq):
    tpos = i * tq + lax.broadcasted_iota(jnp.int32, (tq, TK), 0)
    spos = j * TK + lax.broadcasted_iota(jnp.int32, (tq, TK), 1)
    return spos < tpos


def _log_terms(z, mask):
    sp = jnp.log(1.0 + jnp.exp(-jnp.abs(z)))
    return jnp.minimum(z, 0.0) - sp, jnp.where(mask, -jnp.maximum(z, 0.0) - sp, 0.0)


def _tri(after):
    r = lax.broadcasted_iota(jnp.int32, (TK, TK), 0)
    c = lax.broadcasted_iota(jnp.int32, (TK, TK), 1)
    return (r > c).astype(BF16) if after else (r < c).astype(BF16)


def _attn_fwd(proj, n_heads, comm=None):
    s = proj.shape[0]
    tq = _tile(s, TQ_PREF)
    scale = 1.0 / math.sqrt(HEAD_DIM)
    ratio = tq // TK

    def body(q_ref, k_ref, v_ref, o_ref, ot_ref, acc_ref, *clms):
        i = pl.program_id(1)
        heads = [slice(u * HEAD_DIM, (u + 1) * HEAD_DIM) for u in range(NU)]
        qs = [q_ref[:, hs].astype(BF16) for hs in heads]
        tri_after = _tri(True)
        acc_ref[...] = jnp.zeros_like(acc_ref)
        for cr in clms:
            cr[...] = jnp.zeros_like(cr)
        nkb = (i + 1) * ratio

        def step(jj, carry):
            j = nkb - 1 - jj
            rows = pl.ds(pl.multiple_of(j * TK, TK), TK)
            mask = _causal_mask(i, j, tq)
            zs = [lax.dot_general(qs[u], k_ref[rows, hs].astype(BF16), (((1,), (1,)), ((), ())),
                                  preferred_element_type=F32) * scale for u, hs in enumerate(heads)]
            lls = [_log_terms(z, mask) for z in zs]
            sufs = [clms[u][...] + _split_dot(lls[u][1], tri_after) for u in range(NU)]
            for u, hs in enumerate(heads):
                a = jnp.where(mask, jnp.exp(lls[u][0] + sufs[u]), 0.0)
                acc_ref[:, hs] += jnp.dot(a.astype(BF16), v_ref[rows, hs].astype(BF16), preferred_element_type=F32)
                clms[u][...] += jnp.sum(lls[u][1], axis=1, keepdims=True)
            return carry

        lax.fori_loop(0, nkb, step, 0)
        o = acc_ref[...]
        o_ref[...] = o
        ot_ref[...] = o.T.astype(BF16)

    w = NU * HEAD_DIM
    ng = n_heads // NU
    return _call("attn_fwd", body, (ng, s // tq),
                 [pl.BlockSpec((tq, w), lambda h, i: (i, h)),
                  pl.BlockSpec((s, w), lambda h, i: (0, ng + h)),
                  pl.BlockSpec((s, w), lambda h, i: (0, 2 * ng + h))],
                 [pl.BlockSpec((tq, w), lambda h, i: (i, h)), pl.BlockSpec((w, tq), lambda h, i: (h, i))],
                 [jax.ShapeDtypeStruct((s, n_heads * HEAD_DIM), F32), jax.ShapeDtypeStruct((n_heads * HEAD_DIM, s), BF16)],
                 [pltpu.VMEM((tq, w), F32), *[pltpu.VMEM((tq, 1), F32)] * NU],
                 ("parallel", "arbitrary"), (proj, proj, proj), comm)


def _attn_bwd(proj, do_sb, n_heads, comm=None):
    s = proj.shape[0]
    tq = _tile(s, TQ_PREF)
    scale = 1.0 / math.sqrt(HEAD_DIM)
    ratio = tq // TK
    n_kb = s // TK
    n_qb = s // tq
    nu = NU_BWD
    heads = [slice(u * HEAD_DIM, (u + 1) * HEAD_DIM) for u in range(nu)]
    nt_dims = (((1,), (1,)), ((), ()))

    def body(q_ref, k_ref, v_ref, do_ref, dq_ref, dk_ref, dv_ref, dka_ref, dva_ref, dl_ref, be_ref, dqa_ref, *c_refs):
        i = pl.program_id(1)

        @pl.when(i == 0)
        def _():
            dka_ref[...] = jnp.zeros_like(dka_ref)
            dva_ref[...] = jnp.zeros_like(dva_ref)

        qs = [q_ref[:, hs].astype(BF16) for hs in heads]
        dobs = [do_ref[:, hs].astype(BF16) for hs in heads]
        tri_after = _tri(True)
        tri_before = _tri(False)
        nkb = (i + 1) * ratio

        for cr in c_refs:
            cr[...] = jnp.zeros_like(cr)

        def sweep_a(jj, carry):
            j = nkb - 1 - jj
            rows = pl.ds(pl.multiple_of(j * TK, TK), TK)
            mask = _causal_mask(i, j, tq)
            zs = [lax.dot_general(qs[u], k_ref[rows, hs].astype(BF16), nt_dims, preferred_element_type=F32) * scale
                  for u, hs in enumerate(heads)]
            das = [lax.dot_general(dobs[u], v_ref[rows, hs].astype(BF16), nt_dims, preferred_element_type=F32)
                   for u, hs in enumerate(heads)]
            lls = [_log_terms(z, mask) for z in zs]
            sufs = [c_refs[u][...] + _split_dot(lls[u][1], tri_after) for u in range(nu)]
            for u, hs in enumerate(heads):
                a = jnp.where(mask, jnp.exp(lls[u][0] + sufs[u]), 0.0)
                dl_ref[u, j] = das[u] * a
                be_ref[u, j] = jnp.exp(lls[u][0])
                dva_ref[rows, hs] += jnp.dot(a.T.astype(BF16), dobs[u], preferred_element_type=F32)
                c_refs[u][...] += jnp.sum(lls[u][1], axis=1, keepdims=True)
            return carry

        lax.fori_loop(0, nkb, sweep_a, 0)

        for cr in c_refs:
            cr[...] = jnp.zeros_like(cr)
        dqa_ref[...] = jnp.zeros_like(dqa_ref)

        def sweep_b(j, carry):
            rows = pl.ds(pl.multiple_of(j * TK, TK), TK)
            mask = _causal_mask(i, j, tq)
            dls = [dl_ref[u, j] for u in range(nu)]
            ps = [c_refs[u][...] + _split_dot(dls[u], tri_before) for u in range(nu)]
            for u, hs in enumerate(heads):
                beta = be_ref[u, j]
                dz = jnp.where(mask, (dls[u] * (1.0 - beta) - beta * ps[u]) * scale, 0.0)
                dqa_ref[:, hs] += jnp.dot(dz.astype(BF16), k_ref[rows, hs].astype(BF16), preferred_element_type=F32)
                dka_ref[rows, hs] += jnp.dot(dz.T.astype(BF16), qs[u], preferred_element_type=F32)
                c_refs[u][...] += jnp.sum(dls[u], axis=1, keepdims=True)
            return carry

        lax.fori_loop(0, nkb, sweep_b, 0)
        dq_ref[...] = dqa_ref[...].astype(BF16)

        @pl.when(i == n_qb - 1)
        def _():
            dk_ref[...] = dka_ref[...].astype(BF16)
            dv_ref[...] = dva_ref[...].astype(BF16)

    w = nu * HEAD_DIM
    ng = n_heads // nu
    qblk = pl.BlockSpec((tq, w), lambda h, i: (i, h))
    full = pl.BlockSpec((s, w), lambda h, i: (0, h))
    sds = jax.ShapeDtypeStruct((s, n_heads * HEAD_DIM), BF16)
    return _call("attn_bwd", body, (ng, n_qb),
                 [qblk, pl.BlockSpec((s, w), lambda h, i: (0, ng + h)), pl.BlockSpec((s, w), lambda h, i: (0, 2 * ng + h)),
                  qblk],
                 [qblk, full, full], [sds, sds, sds],
                 [pltpu.VMEM((s, w), F32), pltpu.VMEM((s, w), F32),
                  pltpu.VMEM((nu, n_kb, tq, TK), F32), pltpu.VMEM((nu, n_kb, tq, TK), F32),
                  pltpu.VMEM((tq, w), F32), *[pltpu.VMEM((tq, 1), F32)] * nu],
                 ("parallel", "arbitrary"), (proj, proj, proj, do_sb), comm)


def _adamw(name, w, m, v, parts, part_specs, tr, comm=None):
    r, c = w.shape
    n_parts = len(parts)

    def body(*refs):
        w_ref, m_ref, v_ref = refs[:3]
        p_refs = refs[3:3 + n_parts]
        g_ref, d_ref, nm_ref, nv_ref = refs[3 + n_parts:]
        g = p_refs[0][...].astype(F32)
        for p in p_refs[1:]:
            g = g + p[...].astype(F32)
        nm = ADAM_B1 * m_ref[...] + (1.0 - ADAM_B1) * g
        nv = ADAM_B2 * v_ref[...] + (1.0 - ADAM_B2) * jnp.square(g)
        m_hat = nm / (1.0 - ADAM_B1 ** ADAM_STEP)
        v_hat = nv / (1.0 - ADAM_B2 ** ADAM_STEP)
        g_ref[...] = g
        d_ref[...] = -ADAM_LR * (m_hat / (jnp.sqrt(v_hat) + ADAM_EPS) + ADAM_WD * w_ref[...])
        nm_ref[...] = nm
        nv_ref[...] = nv

    blk = pl.BlockSpec((tr, c), lambda i: (i, 0))
    sds = jax.ShapeDtypeStruct((r, c), F32)
    return _call(name, body, (r // tr,), [blk, blk, blk, *part_specs], [blk] * 4, [sds] * 4, [], ("parallel",),
                 (w, m, v, *parts), comm)


def _adamw_big(name, w, m, v, recv, comm=None):
    r, c = w.shape
    tr = _tile(r, 128)
    order = (3, 0, 1, 2)
    specs = [pl.BlockSpec((None, tr, c), functools.partial(lambda i, slot: (slot, i, 0), slot=sl)) for sl in order]
    return _adamw(name, w, m, v, [recv] * 4, specs, tr, comm)


def _adamw_small(name, w, m, v, g):
    r, c = w.shape
    return _adamw(name, w, m, v, [g], [pl.BlockSpec((r, c), lambda i: (0, 0))], r)


def kernel(x, g_pre_mix, w_in, b_in, w_dw, b_dw, g_conv_ln, b_conv_ln, w_sb_out, w_conv_out, w_o, g_post_mix, g_pre_mlp, w_up, w_down, g_post_mlp, loss_target, m_g_pre_mix, m_w_in, m_b_in, m_w_dw, m_b_dw, m_g_conv_ln, m_b_conv_ln, m_w_sb_out, m_w_conv_out, m_w_o, m_g_post_mix, m_g_pre_mlp, m_w_up, m_w_down, m_g_post_mlp, v_g_pre_mix, v_w_in, v_b_in, v_w_dw, v_b_dw, v_g_conv_ln, v_b_conv_ln, v_w_sb_out, v_w_conv_out, v_w_o, v_g_post_mix, v_g_pre_mlp, v_w_up, v_w_down, v_g_post_mlp):
    xs, tgt = x[0], loss_target[0]
    s, d = xs.shape
    d_half = d // 2
    n_heads = d_half // HEAD_DIM
    d_ff = NDEV * w_up.shape[2]
    core = lax.axis_index("c").astype(jnp.int32).reshape(1)
    dev = 4 * lax.axis_index("x") + 2 * lax.axis_index("y") + lax.axis_index("c")

    w_dw_pad = jnp.pad(w_dw[0], ((0, CONV_PAD - CONV_WIDTH), (0, 0)))
    sh_in, sh_sb, sh_cv, sh_o, sh_up, sh_down = [w[0].astype(BF16) for w in (w_in, w_sb_out, w_conv_out, w_o, w_up, w_down)]
    wg_in, = _run_comm("all_gather_first", _ag_comm([sh_in]))

    ag_down = _Chunked(_ag_comm, [sh_down], sh_down.shape[0], 8)
    h, h_t = _prenorm(xs, g_pre_mix)
    proj, wg_sb, wg_cv, wg_o, wg_dw = _mm_cols("proj", h, wg_in, bias=b_in,
                                               comm=_ag_comm([sh_sb, sh_cv, sh_o, w_dw_pad]))
    o_sb, o_sb_t, wg_up = _attn_fwd(proj, n_heads, _ag_comm([sh_up]))
    wf_dw = wg_dw.transpose(1, 0, 2).reshape(CONV_PAD, d_half)
    wf_o = wg_o.reshape(d, d)
    u1, *part = _conv_fwd(proj, wf_dw, b_dw, d_half, ag_down.take())
    ag_down.done(part)
    u3, u3_t = _ln_silu(u1, g_conv_ln, b_conv_ln)
    o_sbp = _mm_cols("sb_out", o_sb, wg_sb)[0]
    o_cv = _mm_cols("conv_out", u3, wg_cv)[0]
    merged, merged_t, *part = _merge(proj, o_sbp, o_cv, d, ag_down.take())
    ag_down.done(part)
    y, *part = _mm_plain("w_o", merged, wf_o, False, F32, comm=ag_down.take())
    ag_down.done(part)
    x1, h2, h2_t, *part = _postnorm_mix(xs, y, g_post_mix, g_pre_mlp, ag_down.take())
    ag_down.done(part)

    tm_up = _tile(s, 1024)
    ns_up = wg_up.shape[2]
    tk_up = _tile(d, 2048)

    def up_epilogue(acc):
        f = jnp.square(jnp.maximum(acc, 0.0))
        return acc, f, f.T

    a_act, f, f_t, wg_down = _matmul(
        "w_up", h2, wg_up,
        [jax.ShapeDtypeStruct((s, d_ff), BF16), jax.ShapeDtypeStruct((s, d_ff), BF16), jax.ShapeDtypeStruct((d_ff, s), BF16)],
        (s // tm_up, NDEV, d // tk_up),
        pl.BlockSpec((tm_up, tk_up), lambda i, dd, kk: (i, kk)),
        pl.BlockSpec((None, tk_up, ns_up), lambda i, dd, kk: (dd, kk, 0)),
        [pl.BlockSpec((tm_up, ns_up), lambda i, dd, kk: (i, dd)), pl.BlockSpec((tm_up, ns_up), lambda i, dd, kk: (i, dd)),
         pl.BlockSpec((ns_up, tm_up), lambda i, dd, kk: (dd, i))],
        1, False, (tm_up, ns_up), epilogue=up_epilogue, comm=ag_down.take(4))
    wf_down = wg_down.reshape(d_ff, d)
    f2 = _mm_plain("w_down", f, wf_down, False, F32)[0]
    dx2, df2, dg_post_mlp, loss_part = _loss_head(x1, f2, tgt, g_post_mlp)

    tm_b, tn_b = _tile(s, 1024), _tile(d_ff, 1024)
    da = _mm_plain("w_down_bwd", df2, wf_down, True, BF16,
                   extra=(a_act,), extra_specs=(pl.BlockSpec((tm_b, tn_b), lambda i, j, kk: (i, j)),),
                   epilogue=lambda acc, av: (acc * (2.0 * jnp.maximum(av.astype(F32), 0.0)),),
                   outs=[jax.ShapeDtypeStruct((s, d_ff), BF16)],
                   out_specs=[pl.BlockSpec((tm_b, tn_b), lambda i, j, kk: (i, j))])[0]
    gw_down = _mm_plain("w_down_grad", f_t, df2, False, BF16)[0]
    big_down = gw_down.reshape(4, 2, d_ff // NDEV, d)
    gw_up, sib_down = _mm_dw_cols("w_up_grad", h2_t, da, comm=_sibling_comm([big_down]))
    big_up = gw_up.reshape(4, 2, d, d_ff // NDEV)
    dh2, sib_up = _mm_cols_t("w_up_bwd", da, wg_up, comm=_sibling_comm([big_up]))
    rs_down = _Chunked(_chips_comm, [_pair_sum(big_down, sib_down, core)], d_ff // NDEV, 8)
    rs_up = _Chunked(_chips_comm, [_pair_sum(big_up, sib_up, core)], d, 8)
    dx1, dy, dg_pre_mlp, dg_post_mix, *part = _midnorm_bwd(dx2, dh2, x1, y, g_post_mix, g_pre_mlp, rs_down.take())
    rs_down.done(part)
    gw_o, *part = _mm_plain("w_o_grad", merged_t, dy, False, BF16, comm=rs_down.take())
    rs_down.done(part)
    dmerged, *part = _mm_plain("w_o_bwd", dy, wf_o, True, F32, comm=rs_down.take())
    rs_down.done(part)
    do_sbp, do_cv, dgate_sb, dgate_cv, *part = _merge_bwd(dmerged, proj, o_sbp, o_cv, d, rs_down.take())
    rs_down.done(part)
    gw_cv = _mm_dw_cols("conv_out_grad", u3_t, do_cv)[0]
    gw_sb = _mm_dw_cols("sb_out_grad", o_sb_t, do_sbp)[0]
    du3 = _mm_cols_t("conv_out_bwd", do_cv, wg_cv)[0]
    do_sb = _mm_cols_t("sb_out_bwd", do_sbp, wg_sb)[0]
    du1, dg_ln, db_ln = _ln_silu_bwd(du3, u1, g_conv_ln, b_conv_ln)
    big_mid = [gw_sb.reshape(4, 2, d_half, d // NDEV), gw_cv.reshape(4, 2, d_half, d // NDEV),
               gw_o.reshape(4, 2, d // NDEV, d)]
    dglu_a, dglu_b, dw_dw, db_dw, part, *sib_mid = _conv_bwd(du1, proj, wf_dw, d_half,
                                                             _join(rs_up.take(2), _sibling_comm(big_mid)))
    rs_up.done([part])
    sums_mid = [_pair_sum(g, r, core) for g, r in zip(big_mid, sib_mid)]
    dq, dk, dv, r_up, r_sb, r_cv, r_o = _attn_bwd(proj, do_sb, n_heads, _join(rs_up.take(6), _chips_comm(sums_mid)))
    dproj = jnp.concatenate([dq, dk, dv, dglu_a, dglu_b, dgate_sb, dgate_cv], axis=1)
    gw_in, r_down = _mm_dw_cols("w_in_grad", h_t, dproj, comm=rs_down.take(4))
    big_in = gw_in.reshape(4, 2, d, gw_in.shape[2])
    db_in, sib_in = _colsum(dproj, _sibling_comm([big_in]))
    dh, r_in = _mm_cols_t("w_in_bwd", dproj, wg_in, comm=_chips_comm([_pair_sum(big_in, sib_in, core)]))
    grad_x, dg_pre_mix = _prenorm_bwd(dx1, dh, xs, g_pre_mix)

    small = [dg_pre_mix, db_in, dw_dw.reshape(1, -1), db_dw, dg_ln, db_ln, dg_post_mix, dg_pre_mlp, dg_post_mlp]
    sizes = [a.shape[1] for a in small]
    packed = jnp.concatenate(small, axis=1).reshape(-1, LANES)
    total = _sum_small(_all_gather_small(packed)).reshape(1, -1)
    offs = [0]
    for n in sizes:
        offs.append(offs[-1] + n)
    (g_g_pre_mix, g_b_in, g_w_dw_flat, g_b_dw, g_g_conv_ln, g_b_conv_ln, g_g_post_mix, g_g_pre_mlp,
     g_g_post_mlp) = [total[:, offs[k]:offs[k + 1]] for k in range(len(sizes))]
    ch = w_dw.shape[2]
    g_w_dw = lax.dynamic_slice_in_dim(g_w_dw_flat.reshape(CONV_PAD, d_half), dev * ch, ch, axis=1)[:CONV_WIDTH]

    loss = lax.psum(loss_part[0, 0], ("x", "y", "c"))

    res = {}
    res["w_up"] = _adamw_big("adamw_w_up", w_up[0], m_w_up[0], v_w_up[0], r_up)
    res["w_down"] = _adamw_big("adamw_w_down", w_down[0], m_w_down[0], v_w_down[0], r_down)
    res["g_pre_mix"] = _adamw_small("adamw_g_pre_mix", g_pre_mix, m_g_pre_mix, v_g_pre_mix, g_g_pre_mix)
    res["w_in"] = _adamw_big("adamw_w_in", w_in[0], m_w_in[0], v_w_in[0], r_in)
    res["b_in"] = _adamw_small("adamw_b_in", b_in, m_b_in, v_b_in, g_b_in)
    res["w_dw"] = _adamw_small("adamw_w_dw", w_dw[0], m_w_dw[0], v_w_dw[0], g_w_dw)
    res["b_dw"] = _adamw_small("adamw_b_dw", b_dw, m_b_dw, v_b_dw, g_b_dw)
    res["g_conv_ln"] = _adamw_small("adamw_g_conv_ln", g_conv_ln, m_g_conv_ln, v_g_conv_ln, g_g_conv_ln)
    res["b_conv_ln"] = _adamw_small("adamw_b_conv_ln", b_conv_ln, m_b_conv_ln, v_b_conv_ln, g_b_conv_ln)
    res["w_sb_out"] = _adamw_big("adamw_w_sb_out", w_sb_out[0], m_w_sb_out[0], v_w_sb_out[0], r_sb)
    res["w_conv_out"] = _adamw_big("adamw_w_conv_out", w_conv_out[0], m_w_conv_out[0], v_w_conv_out[0], r_cv)
    res["w_o"] = _adamw_big("adamw_w_o", w_o[0], m_w_o[0], v_w_o[0], r_o)
    res["g_post_mix"] = _adamw_small("adamw_g_post_mix", g_post_mix, m_g_post_mix, v_g_post_mix, g_g_post_mix)
    res["g_pre_mlp"] = _adamw_small("adamw_g_pre_mlp", g_pre_mlp, m_g_pre_mlp, v_g_pre_mlp, g_g_pre_mlp)
    res["g_post_mlp"] = _adamw_small("adamw_g_post_mlp", g_post_mlp, m_g_post_mlp, v_g_post_mlp, g_g_post_mlp)

    names = ["g_pre_mix", "w_in", "b_in", "w_dw", "b_dw", "g_conv_ln", "b_conv_ln", "w_sb_out", "w_conv_out", "w_o",
             "g_post_mix", "g_pre_mlp", "w_up", "w_down", "g_post_mlp"]
    three_d = {"w_in", "w_dw", "w_sb_out", "w_conv_out", "w_o", "w_up", "w_down"}

    def shaped(nm, arr):
        return arr[None] if nm in three_d else arr

    out = [loss, grad_x[None]]
    for k in range(4):
        out += [shaped(nm, res[nm][k]) for nm in names]
    return tuple(out)
```

```python
import functools
import math

import jax
import jax.numpy as jnp
from jax import lax
from jax.experimental import pallas as pl
from jax.experimental.pallas import tpu as pltpu

F32 = jnp.float32
BF16 = jnp.bfloat16
NDEV = 8
LANES = 128
EPS = 1e-6
CONV_WIDTH = 31
CONV_PAD = 32
HEAD_DIM = 128
ADAM_LR = 0.001
ADAM_B1 = 0.9
ADAM_B2 = 0.999
ADAM_EPS = 1e-08
ADAM_WD = 0.01
ADAM_STEP = 10
VMEM_LIMIT = 56 * 1024 * 1024
MESH = pl.DeviceIdType.MESH
ANY = pl.BlockSpec(memory_space=pl.ANY)


def _tile(n, pref):
    t = min(n, pref)
    assert n % t == 0, (n, t)
    return t


def _sigmoid(v):
    return 1.0 / (1.0 + jnp.exp(-v))


def _position():
    return lax.axis_index("x"), lax.axis_index("y"), lax.axis_index("c")


class _Comm:
    def __init__(self, ins, outs, scratch, start, finish, aliases=None, mid=None):
        self.ins, self.outs, self.scratch = list(ins), list(outs), list(scratch)
        self.start, self.finish, self.aliases = start, finish, dict(aliases or {})
        self.mid = mid


_NO_COMM = _Comm([], [], [], None, None)


def _call(name, body, grid, in_specs, out_specs, out_shape, scratch_shapes, sem, args, comm=None):
    comm = comm or _NO_COMM
    n_in, n_out, n_scr = len(in_specs), len(out_specs), len(scratch_shapes)
    n_cin, n_cout = len(comm.ins), len(comm.outs)

    n_steps = math.prod(grid)
    mid_step = n_steps - max(1, n_steps // 8) if (comm.mid is not None and n_steps >= 4) else None

    def edge(c_ins, c_outs, c_scr, at_step, actions):
        linear = 0
        for ax, g in enumerate(grid):
            linear = linear * g + pl.program_id(ax)

        @pl.when(linear == at_step)
        def _():
            for act in actions:
                act(c_ins, c_outs, c_scr)

    def wrapped(*refs):
        ins, c_ins = refs[:n_in], refs[n_in:n_in + n_cin]
        pos = n_in + n_cin
        outs, c_outs = refs[pos:pos + n_out], refs[pos + n_out:pos + n_out + n_cout]
        pos += n_out + n_cout
        scr, c_scr = refs[pos:pos + n_scr], refs[pos + n_scr:]
        if n_cin:
            edge(c_ins, c_outs, c_scr, 0, [comm.start])
            if mid_step is not None:
                edge(c_ins, c_outs, c_scr, mid_step, [comm.mid])
        body(*ins, *outs, *scr)
        if n_cin:
            late = [comm.finish] if (mid_step is not None or comm.mid is None) else [comm.mid, comm.finish]
            edge(c_ins, c_outs, c_scr, n_steps - 1, late)

    if n_cin:
        sem = ("arbitrary",) * len(grid)
    return pl.pallas_call(
        wrapped, name=name, grid=grid,
        in_specs=[*in_specs, *[ANY] * n_cin], out_specs=[*out_specs, *[ANY] * n_cout],
        out_shape=[*out_shape, *comm.outs], scratch_shapes=[*scratch_shapes, *comm.scratch],
        input_output_aliases={n_in + ci: n_out + co for ci, co in comm.aliases.items()},
        compiler_params=pltpu.CompilerParams(dimension_semantics=sem, vmem_limit_bytes=VMEM_LIMIT),
    )(*args, *comm.ins)


def _run_comm(name, comm):
    n_in, n_out = len(comm.ins), len(comm.outs)

    def body(*refs):
        ins, outs, scr = refs[:n_in], refs[n_in:n_in + n_out], refs[n_in + n_out:]
        comm.start(ins, outs, scr)
        if comm.mid is not None:
            comm.mid(ins, outs, scr)
        comm.finish(ins, outs, scr)

    return pl.pallas_call(body, name=name, in_specs=[ANY] * n_in, out_specs=[ANY] * n_out, out_shape=comm.outs,
                          scratch_shapes=comm.scratch,
                          input_output_aliases=comm.aliases)(*comm.ins)


def _rows_of(ref, rows):
    return ref if rows is None else ref.at[pl.ds(rows[0], rows[1])]


def _ag_comm(shards, rows=None, into=None):
    n = len(shards)

    def parts(ins, outs, scr):
        send_sems, recv_sems, local_sems = scr
        x, y, c = _position()
        chips = [(1 - x, y), (x, 1 - y), (1 - x, 1 - y)]

        def copy(a, k, block, to, own=False):
            px, py, pc = block
            dst = _rows_of(outs[a].at[4 * px + 2 * py + pc], rows)
            return pltpu.make_async_remote_copy(src_ref=_rows_of(ins[a], rows) if own else dst, dst_ref=dst,
                                                send_sem=send_sems.at[a, k], recv_sem=recv_sems.at[a, k],
                                                device_id=to, device_id_type=MESH)

        mine = [pltpu.make_async_copy(_rows_of(ins[a], rows), _rows_of(outs[a].at[4 * x + 2 * y + c], rows),
                                      local_sems.at[a]) for a in range(n)]
        first = []
        for a in range(n):
            first.append(copy(a, 0, (x, y, c), (x, y, 1 - c), own=True))
            first += [copy(a, 1 + j, (x, y, c), (*chip, c), own=True) for j, chip in enumerate(chips)]
        return copy, mine, first, chips, (x, y, c), (x, y, 1 - c)

    def start(ins, outs, scr):
        _, mine, first, _, _, _ = parts(ins, outs, scr)
        for cp in mine + first:
            cp.start()

    def mid(ins, outs, scr):
        copy, _, _, chips, me, sibling = parts(ins, outs, scr)
        for a in range(n):
            for j, chip in enumerate(chips):
                copy(a, 1 + j, (*chip, me[2]), me).wait_recv()
                copy(a, 4 + j, (*chip, me[2]), sibling).start()

    def finish(ins, outs, scr):
        copy, mine, first, chips, me, sibling = parts(ins, outs, scr)
        c = me[2]
        passed = [copy(a, 4 + j, (*chip, c), sibling) for a in range(n) for j, chip in enumerate(chips)]
        for a in range(n):
            copy(a, 0, sibling, me).wait_recv()
            for j, chip in enumerate(chips):
                copy(a, 4 + j, (*chip, 1 - c), me).wait_recv()
        for cp in first + passed:
            cp.wait_send()
        for cp in mine:
            cp.wait()

    return _Comm([*shards, *(into or [])], [jax.ShapeDtypeStruct((NDEV, *sh.shape), sh.dtype) for sh in shards],
                 [pltpu.SemaphoreType.DMA((n, 7)), pltpu.SemaphoreType.DMA((n, 7)), pltpu.SemaphoreType.DMA((n,))],
                 start, finish, {n + a: a for a in range(n)} if into else None, mid)


def _sibling_comm(grads):
    n = len(grads)

    def copies(ins, outs, scr):
        send_sems, recv_sems = scr
        x, y, c = _position()
        return [pltpu.make_async_remote_copy(src_ref=ins[a].at[k, 1 - c], dst_ref=outs[a].at[k],
                                             send_sem=send_sems.at[a, k], recv_sem=recv_sems.at[a, k],
                                             device_id=(x, y, 1 - c), device_id_type=MESH)
                for a in range(n) for k in range(4)]

    def start(ins, outs, scr):
        for cp in copies(ins, outs, scr):
            cp.start()

    def finish(ins, outs, scr):
        for cp in copies(ins, outs, scr):
            cp.wait()

    return _Comm(grads, [jax.ShapeDtypeStruct((4, *g.shape[2:]), g.dtype) for g in grads],
                 [pltpu.SemaphoreType.DMA((n, 4)), pltpu.SemaphoreType.DMA((n, 4))], start, finish)


def _chips_comm(sums, rows=None, into=None):
    n = len(sums)

    def copies(ins, outs, scr):
        send_sems, recv_sems, local_sems = scr
        x, y, c = _position()
        chips = [(1 - x, y), (x, 1 - y), (1 - x, 1 - y)]
        own = [pltpu.make_async_copy(_rows_of(ins[a].at[2 * x + y], rows), _rows_of(outs[a].at[3], rows), local_sems.at[a])
               for a in range(n)]
        remote = [pltpu.make_async_remote_copy(src_ref=_rows_of(ins[a].at[2 * px + py], rows),
                                               dst_ref=_rows_of(outs[a].at[j], rows),
                                               send_sem=send_sems.at[a, j], recv_sem=recv_sems.at[a, j],
                                               device_id=(px, py, c), device_id_type=MESH)
                  for a in range(n) for j, (px, py) in enumerate(chips)]
        return own + remote

    def start(ins, outs, scr):
        for cp in copies(ins, outs, scr):
            cp.start()

    def finish(ins, outs, scr):
        for cp in copies(ins, outs, scr):
            cp.wait()

    return _Comm([*sums, *(into or [])], [jax.ShapeDtypeStruct(sm.shape, sm.dtype) for sm in sums],
                 [pltpu.SemaphoreType.DMA((n, 3)), pltpu.SemaphoreType.DMA((n, 3)), pltpu.SemaphoreType.DMA((n,))],
                 start, finish, {n + a: a for a in range(n)} if into else None)


def _join(c1, c2):
    n_in, n_out, n_scr = len(c1.ins), len(c1.outs), len(c1.scratch)
    aliases = dict(c1.aliases)
    aliases.update({n_in + ci: n_out + co for ci, co in c2.aliases.items()})

    def start(ins, outs, scr):
        c1.start(ins[:n_in], outs[:n_out], scr[:n_scr])
        c2.start(ins[n_in:], outs[n_out:], scr[n_scr:])

    def mid(ins, outs, scr):
        if c1.mid is not None:
            c1.mid(ins[:n_in], outs[:n_out], scr[:n_scr])
        if c2.mid is not None:
            c2.mid(ins[n_in:], outs[n_out:], scr[n_scr:])

    def finish(ins, outs, scr):
        c1.finish(ins[:n_in], outs[:n_out], scr[:n_scr])
        c2.finish(ins[n_in:], outs[n_out:], scr[n_scr:])

    return _Comm(c1.ins + c2.ins, c1.outs + c2.outs, c1.scratch + c2.scratch, start, finish, aliases,
                 mid if (c1.mid is not None or c2.mid is not None) else None)


class _Chunked:
    def __init__(self, make, arrays, n_rows, n_chunks):
        self.make, self.arrays, self.into = make, arrays, None
        step = n_rows // n_chunks
        assert step * n_chunks == n_rows
        self.todo = [(k * step, step) for k in range(n_chunks)]

    def take(self, count=1):
        r0, nr = self.todo[0][0], sum(t[1] for t in self.todo[:count])
        self.todo = self.todo[count:]
        return self.make(self.arrays, (r0, nr), self.into)

    def done(self, outs):
        self.into = list(outs)
        return self.into


def _all_gather_small(part):
    def body(in_ref, out_ref, send_sems, recv_sems, local_sem):
        x, y, c = _position()
        me = 4 * x + 2 * y + c
        mine = pltpu.make_async_copy(in_ref, out_ref.at[me], local_sem)
        mine.start()
        flips = [(fx, fy, fc) for fx in (0, 1) for fy in (0, 1) for fc in (0, 1)][1:]
        copies = []
        for k, (fx, fy, fc) in enumerate(flips):
            cp = pltpu.make_async_remote_copy(src_ref=in_ref, dst_ref=out_ref.at[me], send_sem=send_sems.at[k],
                                              recv_sem=recv_sems.at[k],
                                              device_id=(x ^ fx, y ^ fy, c ^ fc), device_id_type=MESH)
            cp.start()
            copies.append(cp)
        for k, (fx, fy, fc) in enumerate(flips):
            peer = 4 * (x ^ fx) + 2 * (y ^ fy) + (c ^ fc)
            pltpu.make_async_remote_copy(src_ref=in_ref, dst_ref=out_ref.at[peer], send_sem=send_sems.at[k],
                                         recv_sem=recv_sems.at[k], device_id=(x, y, c), device_id_type=MESH).wait_recv()
        for cp in copies:
            cp.wait_send()
        mine.wait()

    return pl.pallas_call(
        body, name="all_gather_small", in_specs=[ANY], out_specs=ANY,
        out_shape=jax.ShapeDtypeStruct((NDEV, *part.shape), part.dtype),
        scratch_shapes=[pltpu.SemaphoreType.DMA((7,)), pltpu.SemaphoreType.DMA((7,)), pltpu.SemaphoreType.DMA],
    )(part)


def _pair_sum(g, recv, core):
    _, _, r, c = g.shape
    tr = _tile(r, 512)

    def body(core_ref, g_ref, r_ref, o_ref):
        o_ref[...] = (g_ref[...].astype(F32) + r_ref[...].astype(F32)).astype(o_ref.dtype)

    return pl.pallas_call(
        body, name="pair_sum",
        grid_spec=pltpu.PrefetchScalarGridSpec(
            num_scalar_prefetch=1, grid=(4, r // tr),
            in_specs=[pl.BlockSpec((None, None, tr, c), lambda k, i, core_ref: (k, core_ref[0], i, 0)),
                      pl.BlockSpec((None, tr, c), lambda k, i, core_ref: (k, i, 0))],
            out_specs=pl.BlockSpec((None, tr, c), lambda k, i, core_ref: (k, i, 0))),
        out_shape=jax.ShapeDtypeStruct((4, r, c), g.dtype),
        compiler_params=pltpu.CompilerParams(dimension_semantics=("parallel", "parallel"), vmem_limit_bytes=VMEM_LIMIT),
    )(core, g, recv)


def _sum_small(gathered):
    _, r, l = gathered.shape

    def body(g_ref, o_ref):
        acc = g_ref[0]
        for d in range(1, NDEV):
            acc = acc + g_ref[d]
        o_ref[...] = acc

    return pl.pallas_call(
        body, name="sum_small", in_specs=[pl.BlockSpec((NDEV, r, l), lambda: (0, 0, 0))],
        out_specs=pl.BlockSpec((r, l), lambda: (0, 0)), out_shape=jax.ShapeDtypeStruct((r, l), F32),
    )(gathered)


def _matmul(name, a, b, outs, grid, a_spec, b_spec, out_specs, n_red, nt, acc_shape,
            extra=(), extra_specs=(), epilogue=None, comm=None):
    n_extra, n_out = len(extra), len(outs)
    red_axes = tuple(range(len(grid) - n_red, len(grid)))
    red_sizes = tuple(grid[ax] for ax in red_axes)
    single = all(sz == 1 for sz in red_sizes)
    dims = (((1,), (1,)), ((), ())) if nt else (((1,), (0,)), ((), ()))

    def body(*refs):
        a_ref, b_ref = refs[0], refs[1]
        ex_refs = refs[2:2 + n_extra]
        o_refs = refs[2 + n_extra:2 + n_extra + n_out]
        acc_ref = refs[-1]

        def write(acc):
            vals = (acc,) if epilogue is None else epilogue(acc, *[r[...] for r in ex_refs])
            for o_ref, val in zip(o_refs, vals):
                o_ref[...] = val.astype(o_ref.dtype)

        if len(b_ref.shape) == 3:
            w = b_ref.shape[2]
            part = sum(lax.dot_general(a_ref[:, p * w:(p + 1) * w].astype(BF16), b_ref[p].astype(BF16), dims,
                                       preferred_element_type=F32) for p in range(b_ref.shape[0]))
        else:
            part = lax.dot_general(a_ref[...].astype(BF16), b_ref[...].astype(BF16), dims, preferred_element_type=F32)
        if single:
            write(part)
        else:
            ks = [pl.program_id(ax) for ax in red_axes]
            first = functools.reduce(jnp.logical_and, [k == 0 for k in ks])
            last = functools.reduce(jnp.logical_and, [k == sz - 1 for k, sz in zip(ks, red_sizes)])

            @pl.when(first)
            def _():
                acc_ref[...] = part

            @pl.when(jnp.logical_not(first))
            def _():
                acc_ref[...] += part

            @pl.when(last)
            def _():
                write(acc_ref[...])

    sem = ("parallel",) * (len(grid) - n_red) + ("arbitrary",) * n_red
    return _call(name, body, grid, [a_spec, b_spec, *extra_specs], list(out_specs), list(outs),
                 [pltpu.VMEM((8, LANES) if single else acc_shape, F32)], sem, (a, b, *extra), comm)


def _mm_cols(name, a, wg, bias=None, out_dtype=F32, comm=None):
    m, k = a.shape
    _, _, ns = wg.shape
    tm, tk = _tile(m, 1024), _tile(k, 2048)
    grid = (m // tm, NDEV, k // tk)
    extra, extra_specs, epi = (), (), None
    if bias is not None:
        extra, extra_specs = (bias,), (pl.BlockSpec((1, ns), lambda i, d, kk: (0, d)),)
        epi = lambda acc, bv: (acc + bv,)
    return _matmul(name, a, wg, [jax.ShapeDtypeStruct((m, NDEV * ns), out_dtype)], grid,
                   pl.BlockSpec((tm, tk), lambda i, d, kk: (i, kk)),
                   pl.BlockSpec((None, tk, ns), lambda i, d, kk: (d, kk, 0)),
                   [pl.BlockSpec((tm, ns), lambda i, d, kk: (i, d))], 1, False, (tm, ns),
                   extra, extra_specs, epi, comm)


def _mm_cols_t(name, a, wg, comm=None):
    m, _ = a.shape
    _, n, ns = wg.shape
    tm, tn = _tile(m, 1024), _tile(n, 1024)
    per = 2
    grid = (m // tm, n // tn, NDEV // per)
    return _matmul(name, a, wg, [jax.ShapeDtypeStruct((m, n), F32)], grid,
                   pl.BlockSpec((tm, per * ns), lambda i, j, d: (i, d)),
                   pl.BlockSpec((per, tn, ns), lambda i, j, d: (d, j, 0)),
                   [pl.BlockSpec((tm, tn), lambda i, j, d: (i, j))], 1, True, (tm, tn), comm=comm)


def _mm_dw_cols(name, at, g, comm=None):
    m, t = at.shape
    ns = g.shape[1] // NDEV
    tm, tk = _tile(m, 1024), _tile(t, 2048)
    grid = (m // tm, NDEV, t // tk)
    return _matmul(name, at, g, [jax.ShapeDtypeStruct((NDEV, m, ns), BF16)], grid,
                   pl.BlockSpec((tm, tk), lambda i, d, kk: (i, kk)),
                   pl.BlockSpec((tk, ns), lambda i, d, kk: (kk, d)),
                   [pl.BlockSpec((None, tm, ns), lambda i, d, kk: (d, i, 0))], 1, False, (tm, ns), comm=comm)


def _mm_plain(name, a, b, nt, out_dtype, extra=(), extra_specs=(), epilogue=None, outs=None, out_specs=None, comm=None):
    m, k = a.shape
    n = b.shape[0] if nt else b.shape[1]
    tm, tn, tk = _tile(m, 1024), _tile(n, 1024), _tile(k, 2048)
    grid = (m // tm, n // tn, k // tk)
    b_spec = (pl.BlockSpec((tn, tk), lambda i, j, kk: (j, kk)) if nt
              else pl.BlockSpec((tk, tn), lambda i, j, kk: (kk, j)))
    if outs is None:
        outs = [jax.ShapeDtypeStruct((m, n), out_dtype)]
        out_specs = [pl.BlockSpec((tm, tn), lambda i, j, kk: (i, j))]
    return _matmul(name, a, b, outs, grid, pl.BlockSpec((tm, tk), lambda i, j, kk: (i, kk)), b_spec,
                   out_specs, 1, nt, (tm, tn), extra, extra_specs, epilogue, comm)


def _rms_rows(v):
    return lax.rsqrt(jnp.mean(v * v, axis=-1, keepdims=True) + EPS)


def _prenorm(x, g):
    s, d = x.shape
    ts = _tile(s, 256)

    def body(x_ref, g_ref, h_ref, ht_ref):
        xv = x_ref[...]
        h = xv * _rms_rows(xv) * g_ref[...]
        h_ref[...] = h.astype(BF16)
        ht_ref[...] = h.T.astype(BF16)

    return _call("prenorm", body, (s // ts,),
                 [pl.BlockSpec((ts, d), lambda i: (i, 0)), pl.BlockSpec((1, d), lambda i: (0, 0))],
                 [pl.BlockSpec((ts, d), lambda i: (i, 0)), pl.BlockSpec((d, ts), lambda i: (0, i))],
                 [jax.ShapeDtypeStruct((s, d), BF16), jax.ShapeDtypeStruct((d, s), BF16)], [], ("parallel",), (x, g))


def _ln_silu(u1, g, b):
    s, c = u1.shape
    ts = _tile(s, 256)

    def body(u_ref, g_ref, b_ref, o_ref, ot_ref):
        u = u_ref[...]
        mu = jnp.mean(u, axis=-1, keepdims=True)
        var = jnp.mean(jnp.square(u - mu), axis=-1, keepdims=True)
        u2 = (u - mu) * lax.rsqrt(var + EPS) * g_ref[...] + b_ref[...]
        u3 = u2 * _sigmoid(u2)
        o_ref[...] = u3.astype(BF16)
        ot_ref[...] = u3.T.astype(BF16)

    vec = pl.BlockSpec((1, c), lambda i: (0, 0))
    return _call("ln_silu", body, (s // ts,), [pl.BlockSpec((ts, c), lambda i: (i, 0)), vec, vec],
                 [pl.BlockSpec((ts, c), lambda i: (i, 0)), pl.BlockSpec((c, ts), lambda i: (0, i))],
                 [jax.ShapeDtypeStruct((s, c), BF16), jax.ShapeDtypeStruct((c, s), BF16)], [], ("parallel",), (u1, g, b))


def _ln_silu_bwd(du3, u1, g, b, comm=None):
    s, c = u1.shape
    ts = _tile(s, 256)

    def body(d_ref, u_ref, g_ref, b_ref, du1_ref, dg_ref, db_ref):
        @pl.when(pl.program_id(0) == 0)
        def _():
            dg_ref[...] = jnp.zeros_like(dg_ref)
            db_ref[...] = jnp.zeros_like(db_ref)

        u = u_ref[...]
        mu = jnp.mean(u, axis=-1, keepdims=True)
        var = jnp.mean(jnp.square(u - mu), axis=-1, keepdims=True)
        rstd = lax.rsqrt(var + EPS)
        uhat = (u - mu) * rstd
        u2 = uhat * g_ref[...] + b_ref[...]
        sg = _sigmoid(u2)
        du2 = d_ref[...] * (sg * (1.0 + u2 * (1.0 - sg)))
        dg_ref[...] += jnp.sum(du2 * uhat, axis=0, keepdims=True)
        db_ref[...] += jnp.sum(du2, axis=0, keepdims=True)
        duh = du2 * g_ref[...]
        du1_ref[...] = rstd * (duh - jnp.mean(duh, axis=-1, keepdims=True)
                               - uhat * jnp.mean(duh * uhat, axis=-1, keepdims=True))

    row = pl.BlockSpec((ts, c), lambda i: (i, 0))
    vec = pl.BlockSpec((1, c), lambda i: (0, 0))
    return _call("ln_silu_bwd", body, (s // ts,), [row, row, vec, vec], [row, vec, vec],
                 [jax.ShapeDtypeStruct((s, c), F32), jax.ShapeDtypeStruct((1, c), F32), jax.ShapeDtypeStruct((1, c), F32)],
                 [], ("arbitrary",), (du3, u1, g, b), comm)


def _merge(proj, o_sbp, o_cv, d, comm=None):
    s = proj.shape[0]
    w = d // 2
    ts = _tile(s, 256)

    def body(gs_ref, gc_ref, a_ref, b_ref, m_ref, mt_ref):
        mg = _sigmoid(gs_ref[...]) * a_ref[...] + _sigmoid(gc_ref[...]) * b_ref[...]
        m_ref[...] = mg.astype(BF16)
        mt_ref[...] = mg.T.astype(BF16)

    blk = pl.BlockSpec((ts, w), lambda i, j: (i, j))
    return _call("merge", body, (s // ts, 2),
                 [pl.BlockSpec((ts, w), lambda i, j: (i, 5 + j)), pl.BlockSpec((ts, w), lambda i, j: (i, 7 + j)), blk, blk],
                 [blk, pl.BlockSpec((w, ts), lambda i, j: (j, i))],
                 [jax.ShapeDtypeStruct((s, d), BF16), jax.ShapeDtypeStruct((d, s), BF16)], [],
                 ("parallel", "parallel"), (proj, proj, o_sbp, o_cv), comm)


def _merge_bwd(dmerged, proj, o_sbp, o_cv, d, comm=None):
    s = proj.shape[0]
    w = d // 2
    ts = _tile(s, 256)

    def body(dm_ref, gs_ref, gc_ref, a_ref, b_ref, da_ref, db_ref, dgs_ref, dgc_ref):
        dm = dm_ref[...]
        ss = _sigmoid(gs_ref[...])
        sc = _sigmoid(gc_ref[...])
        da_ref[...] = (dm * ss).astype(BF16)
        db_ref[...] = (dm * sc).astype(BF16)
        dgs_ref[...] = (dm * a_ref[...] * ss * (1.0 - ss)).astype(BF16)
        dgc_ref[...] = (dm * b_ref[...] * sc * (1.0 - sc)).astype(BF16)

    blk = pl.BlockSpec((ts, w), lambda i, j: (i, j))
    sds = jax.ShapeDtypeStruct((s, d), BF16)
    return _call("merge_bwd", body, (s // ts, 2),
                 [blk, pl.BlockSpec((ts, w), lambda i, j: (i, 5 + j)), pl.BlockSpec((ts, w), lambda i, j: (i, 7 + j)),
                  blk, blk],
                 [blk, blk, blk, blk], [sds, sds, sds, sds], [], ("parallel", "parallel"),
                 (dmerged, proj, proj, o_sbp, o_cv), comm)


def _postnorm_mix(x, y, g_post, g_pre, comm=None):
    s, d = x.shape
    ts = _tile(s, 256)

    def body(x_ref, y_ref, gp_ref, gn_ref, x1_ref, h_ref, ht_ref):
        yv = y_ref[...]
        x1 = x_ref[...] + yv * _rms_rows(yv) * gp_ref[...]
        x1_ref[...] = x1
        h = x1 * _rms_rows(x1) * gn_ref[...]
        h_ref[...] = h.astype(BF16)
        ht_ref[...] = h.T.astype(BF16)

    row = pl.BlockSpec((ts, d), lambda i: (i, 0))
    vec = pl.BlockSpec((1, d), lambda i: (0, 0))
    return _call("postnorm_mix", body, (s // ts,), [row, row, vec, vec],
                 [row, row, pl.BlockSpec((d, ts), lambda i: (0, i))],
                 [jax.ShapeDtypeStruct((s, d), F32), jax.ShapeDtypeStruct((s, d), BF16), jax.ShapeDtypeStruct((d, s), BF16)],
                 [], ("parallel",), (x, y, g_post, g_pre), comm)


def _rms_bwd(dout, vin, g):
    r = _rms_rows(vin)
    vhat = vin * r
    dyh = dout * g
    dvin = r * (dyh - vhat * jnp.mean(dyh * vhat, axis=-1, keepdims=True))
    return dvin, jnp.sum(dout * vhat, axis=0, keepdims=True)


def _loss_head(x1, f2, tgt, g):
    s, d = x1.shape
    ts = _tile(s, 256)

    def body(x1_ref, f_ref, t_ref, g_ref, dx2_ref, df2_ref, dg_ref, loss_ref):
        @pl.when(pl.program_id(0) == 0)
        def _():
            dg_ref[...] = jnp.zeros_like(dg_ref)
            loss_ref[...] = jnp.zeros_like(loss_ref)

        fv = f_ref[...]
        x2 = x1_ref[...] + fv * _rms_rows(fv) * g_ref[...]
        err = x2 - t_ref[...]
        loss_ref[...] += 0.5 * jnp.sum(jnp.mean(err * err, axis=-1, keepdims=True), axis=0, keepdims=True)
        dx2 = err * (1.0 / d)
        dx2_ref[...] = dx2
        df2, dg = _rms_bwd(dx2, fv, g_ref[...])
        df2_ref[...] = df2.astype(BF16)
        dg_ref[...] += dg

    row = pl.BlockSpec((ts, d), lambda i: (i, 0))
    vec = pl.BlockSpec((1, d), lambda i: (0, 0))
    return _call("loss_head", body, (s // ts,), [row, row, row, vec],
                 [row, row, vec, pl.BlockSpec((1, LANES), lambda i: (0, 0))],
                 [jax.ShapeDtypeStruct((s, d), F32), jax.ShapeDtypeStruct((s, d), BF16),
                  jax.ShapeDtypeStruct((1, d), F32), jax.ShapeDtypeStruct((1, LANES), F32)],
                 [], ("arbitrary",), (x1, f2, tgt, g))


def _midnorm_bwd(dx2, dh2, x1, y, g_post, g_pre, comm=None):
    s, d = x1.shape
    ts = _tile(s, 256)

    def body(dx2_ref, dh_ref, x1_ref, y_ref, gp_ref, gn_ref, dx1_ref, dy_ref, dgn_ref, dgp_ref):
        @pl.when(pl.program_id(0) == 0)
        def _():
            dgn_ref[...] = jnp.zeros_like(dgn_ref)
            dgp_ref[...] = jnp.zeros_like(dgp_ref)

        dxa, dgn = _rms_bwd(dh_ref[...], x1_ref[...], gn_ref[...])
        dx1 = dx2_ref[...] + dxa
        dx1_ref[...] = dx1
        dy, dgp = _rms_bwd(dx1, y_ref[...], gp_ref[...])
        dy_ref[...] = dy.astype(BF16)
        dgn_ref[...] += dgn
        dgp_ref[...] += dgp

    row = pl.BlockSpec((ts, d), lambda i: (i, 0))
    vec = pl.BlockSpec((1, d), lambda i: (0, 0))
    return _call("midnorm_bwd", body, (s // ts,), [row, row, row, row, vec, vec], [row, row, vec, vec],
                 [jax.ShapeDtypeStruct((s, d), F32), jax.ShapeDtypeStruct((s, d), BF16),
                  jax.ShapeDtypeStruct((1, d), F32), jax.ShapeDtypeStruct((1, d), F32)],
                 [], ("arbitrary",), (dx2, dh2, x1, y, g_post, g_pre), comm)


def _prenorm_bwd(dx1, dh, x, g, comm=None):
    s, d = x.shape
    ts = _tile(s, 256)

    def body(dx1_ref, dh_ref, x_ref, g_ref, dx_ref, dg_ref):
        @pl.when(pl.program_id(0) == 0)
        def _():
            dg_ref[...] = jnp.zeros_like(dg_ref)

        dxa, dg = _rms_bwd(dh_ref[...], x_ref[...], g_ref[...])
        dx_ref[...] = dx1_ref[...] + dxa
        dg_ref[...] += dg

    row = pl.BlockSpec((ts, d), lambda i: (i, 0))
    vec = pl.BlockSpec((1, d), lambda i: (0, 0))
    return _call("prenorm_bwd", body, (s // ts,), [row, row, row, vec], [row, vec],
                 [jax.ShapeDtypeStruct((s, d), F32), jax.ShapeDtypeStruct((1, d), F32)],
                 [], ("arbitrary",), (dx1, dh, x, g), comm)


def _colsum(a, comm=None):
    s, n = a.shape
    ts = _tile(s, 256)

    def body(a_ref, o_ref):
        @pl.when(pl.program_id(0) == 0)
        def _():
            o_ref[...] = jnp.zeros_like(o_ref)

        o_ref[...] += jnp.sum(a_ref[...].astype(F32), axis=0, keepdims=True)

    return _call("colsum", body, (s // ts,), [pl.BlockSpec((ts, n), lambda i: (i, 0))],
                 [pl.BlockSpec((1, n), lambda i: (0, 0))], [jax.ShapeDtypeStruct((1, n), F32)], [], ("arbitrary",), (a,),
                 comm)


def _shift_rows(win, off, t):
    n = win.shape[0]
    if off == 0:
        return win[:t]
    return pltpu.roll(win, n - off, axis=0)[:t]


def _conv_fwd(proj, w_pad, b_dw, c_total, comm=None):
    s = proj.shape[0]
    nct = c_total // LANES
    t = _tile(s, 256)

    def body(ga_ref, gb_ref, w_ref, b_ref, o_ref, u0_ref):
        u0_ref[pl.ds(0, CONV_PAD), :] = jnp.zeros((CONV_PAD, LANES), F32)
        u0_ref[pl.ds(CONV_PAD, s), :] = ga_ref[...] * _sigmoid(gb_ref[...])
        wv = w_ref[...]

        def chunk(r, carry):
            r0 = pl.multiple_of(r * t, t)
            win = u0_ref[pl.ds(r0, t + CONV_PAD), :]
            acc = jnp.broadcast_to(b_ref[...], (t, LANES))
            for j in range(CONV_WIDTH):
                acc = acc + wv[j:j + 1, :] * _shift_rows(win, j + CONV_PAD - (CONV_WIDTH - 1), t)
            o_ref[pl.ds(r0, t), :] = acc
            return carry

        lax.fori_loop(0, s // t, chunk, 0)

    return _call("conv_fwd", body, (nct,),
                 [pl.BlockSpec((s, LANES), lambda c: (0, 3 * nct + c)), pl.BlockSpec((s, LANES), lambda c: (0, 4 * nct + c)),
                  pl.BlockSpec((CONV_PAD, LANES), lambda c: (0, c)), pl.BlockSpec((1, LANES), lambda c: (0, c))],
                 [pl.BlockSpec((s, LANES), lambda c: (0, c))], [jax.ShapeDtypeStruct((s, c_total), F32)],
                 [pltpu.VMEM((s + CONV_PAD, LANES), F32)], ("parallel",), (proj, proj, w_pad, b_dw), comm)


def _conv_bwd(du1, proj, w_pad, c_total, comm=None):
    s = proj.shape[0]
    nct = c_total // LANES
    t = _tile(s, 256)

    def body(d_ref, ga_ref, gb_ref, w_ref, dga_ref, dgb_ref, dw_ref, db_ref, u0_ref, dp_ref):
        sg = _sigmoid(gb_ref[...])
        u0_ref[pl.ds(0, CONV_PAD), :] = jnp.zeros((CONV_PAD, LANES), F32)
        u0_ref[pl.ds(CONV_PAD, s), :] = ga_ref[...] * sg
        dp_ref[pl.ds(0, s), :] = d_ref[...]
        dp_ref[pl.ds(s, CONV_PAD), :] = jnp.zeros((CONV_PAD, LANES), F32)
        dw_ref[...] = jnp.zeros_like(dw_ref)
        db_ref[...] = jnp.sum(d_ref[...], axis=0, keepdims=True)
        wv = w_ref[...]

        def chunk(r, carry):
            r0 = pl.multiple_of(r * t, t)
            win = u0_ref[pl.ds(r0, t + CONV_PAD), :]
            dwin = dp_ref[pl.ds(r0, t + CONV_PAD), :]
            dcur = dwin[:t]
            du0 = jnp.zeros((t, LANES), F32)
            for j in range(CONV_WIDTH):
                du0 = du0 + wv[j:j + 1, :] * _shift_rows(dwin, CONV_WIDTH - 1 - j, t)
                sh = _shift_rows(win, j + CONV_PAD - (CONV_WIDTH - 1), t)
                dw_ref[j:j + 1, :] += jnp.sum(dcur * sh, axis=0, keepdims=True)
            gav = ga_ref[pl.ds(r0, t), :]
            sgv = _sigmoid(gb_ref[pl.ds(r0, t), :])
            dga_ref[pl.ds(r0, t), :] = (du0 * sgv).astype(BF16)
            dgb_ref[pl.ds(r0, t), :] = (du0 * gav * sgv * (1.0 - sgv)).astype(BF16)
            return carry

        lax.fori_loop(0, s // t, chunk, 0)

    col = pl.BlockSpec((s, LANES), lambda c: (0, c))
    return _call("conv_bwd", body, (nct,),
                 [col, pl.BlockSpec((s, LANES), lambda c: (0, 3 * nct + c)),
                  pl.BlockSpec((s, LANES), lambda c: (0, 4 * nct + c)), pl.BlockSpec((CONV_PAD, LANES), lambda c: (0, c))],
                 [col, col, pl.BlockSpec((CONV_PAD, LANES), lambda c: (0, c)), pl.BlockSpec((1, LANES), lambda c: (0, c))],
                 [jax.ShapeDtypeStruct((s, c_total), BF16), jax.ShapeDtypeStruct((s, c_total), BF16),
                  jax.ShapeDtypeStruct((CONV_PAD, c_total), F32), jax.ShapeDtypeStruct((1, c_total), F32)],
                 [pltpu.VMEM((s + CONV_PAD, LANES), F32), pltpu.VMEM((s + CONV_PAD, LANES), F32)],
                 ("parallel",), (du1, proj, proj, w_pad), comm)


TQ_PREF = 256
NU = 4
NU_BWD = 2
TK = 256


def _split_dot(v, tri):
    hi = v.astype(BF16)
    lo = (v - hi.astype(F32)).astype(BF16)
    return (jnp.dot(hi, tri, preferred_element_type=F32) + jnp.dot(lo, tri, preferred_element_type=F32))


def _causal_mask(i, j, tq):
    tpos = i * tq + lax.broadcasted_iota(jnp.int32, (tq, TK), 0)
    spos = j * TK + lax.broadcasted_iota(jnp.int32, (tq, TK), 1)
    return spos < tpos


def _log_terms(z, mask):
    sp = jnp.log(1.0 + jnp.exp(-jnp.abs(z)))
    return jnp.minimum(z, 0.0) - sp, jnp.where(mask, -jnp.maximum(z, 0.0) - sp, 0.0)


def _tri(after):
    r = lax.broadcasted_iota(jnp.int32, (TK, TK), 0)
    c = lax.broadcasted_iota(jnp.int32, (TK, TK), 1)
    return (r > c).astype(BF16) if after else (r < c).astype(BF16)


def _attn_fwd(proj, n_heads, comm=None):
    s = proj.shape[0]
    tq = _tile(s, TQ_PREF)
    scale = 1.0 / math.sqrt(HEAD_DIM)
    ratio = tq // TK

    def body(q_ref, k_ref, v_ref, o_ref, ot_ref, acc_ref, *clms):
        i = pl.program_id(1)
        heads = [slice(u * HEAD_DIM, (u + 1) * HEAD_DIM) for u in range(NU)]
        qs = [q_ref[:, hs].astype(BF16) for hs in heads]
        tri_after = _tri(True)
        acc_ref[...] = jnp.zeros_like(acc_ref)
        for cr in clms:
            cr[...] = jnp.zeros_like(cr)
        nkb = (i + 1) * ratio

        def step(jj, carry):
            j = nkb - 1 - jj
            rows = pl.ds(pl.multiple_of(j * TK, TK), TK)
            mask = _causal_mask(i, j, tq)
            zs = [lax.dot_general(qs[u], k_ref[rows, hs].astype(BF16), (((1,), (1,)), ((), ())),
                                  preferred_element_type=F32) * scale for u, hs in enumerate(heads)]
            lls = [_log_terms(z, mask) for z in zs]
            sufs = [clms[u][...] + _split_dot(lls[u][1], tri_after) for u in range(NU)]
            for u, hs in enumerate(heads):
                a = jnp.where(mask, jnp.exp(lls[u][0] + sufs[u]), 0.0)
                acc_ref[:, hs] += jnp.dot(a.astype(BF16), v_ref[rows, hs].astype(BF16), preferred_element_type=F32)
                clms[u][...] += jnp.sum(lls[u][1], axis=1, keepdims=True)
            return carry

        lax.fori_loop(0, nkb, step, 0)
        o = acc_ref[...]
        o_ref[...] = o
        ot_ref[...] = o.T.astype(BF16)

    w = NU * HEAD_DIM
    ng = n_heads // NU
    return _call("attn_fwd", body, (ng, s // tq),
                 [pl.BlockSpec((tq, w), lambda h, i: (i, h)),
                  pl.BlockSpec((s, w), lambda h, i: (0, ng + h)),
                  pl.BlockSpec((s, w), lambda h, i: (0, 2 * ng + h))],
                 [pl.BlockSpec((tq, w), lambda h, i: (i, h)), pl.BlockSpec((w, tq), lambda h, i: (h, i))],
                 [jax.ShapeDtypeStruct((s, n_heads * HEAD_DIM), F32), jax.ShapeDtypeStruct((n_heads * HEAD_DIM, s), BF16)],
                 [pltpu.VMEM((tq, w), F32), *[pltpu.VMEM((tq, 1), F32)] * NU],
                 ("parallel", "arbitrary"), (proj, proj, proj), comm)


def _attn_bwd(proj, do_sb, n_heads, comm=None):
    s = proj.shape[0]
    tq = _tile(s, TQ_PREF)
    scale = 1.0 / math.sqrt(HEAD_DIM)
    ratio = tq // TK
    n_kb = s // TK
    n_qb = s // tq
    nu = NU_BWD
    heads = [slice(u * HEAD_DIM, (u + 1) * HEAD_DIM) for u in range(nu)]
    nt_dims = (((1,), (1,)), ((), ()))

    def body(q_ref, k_ref, v_ref, do_ref, dq_ref, dk_ref, dv_ref, dka_ref, dva_ref, dl_ref, be_ref, dqa_ref, *c_refs):
        i = pl.program_id(1)

        @pl.when(i == 0)
        def _():
            dka_ref[...] = jnp.zeros_like(dka_ref)
            dva_ref[...] = jnp.zeros_like(dva_ref)

        qs = [q_ref[:, hs].astype(BF16) for hs in heads]
        dobs = [do_ref[:, hs].astype(BF16) for hs in heads]
        tri_after = _tri(True)
        tri_before = _tri(False)
        nkb = (i + 1) * ratio

        for cr in c_refs:
            cr[...] = jnp.zeros_like(cr)

        def sweep_a(jj, carry):
            j = nkb - 1 - jj
            rows = pl.ds(pl.multiple_of(j * TK, TK), TK)
            mask = _causal_mask(i, j, tq)
            zs = [lax.dot_general(qs[u], k_ref[rows, hs].astype(BF16), nt_dims, preferred_element_type=F32) * scale
                  for u, hs in enumerate(heads)]
            das = [lax.dot_general(dobs[u], v_ref[rows, hs].astype(BF16), nt_dims, preferred_element_type=F32)
                   for u, hs in enumerate(heads)]
            lls = [_log_terms(z, mask) for z in zs]
            sufs = [c_refs[u][...] + _split_dot(lls[u][1], tri_after) for u in range(nu)]
            for u, hs in enumerate(heads):
                a = jnp.where(mask, jnp.exp(lls[u][0] + sufs[u]), 0.0)
                dl_ref[u, j] = das[u] * a
                be_ref[u, j] = jnp.exp(lls[u][0])
                dva_ref[rows, hs] += jnp.dot(a.T.astype(BF16), dobs[u], preferred_element_type=F32)
                c_refs[u][...] += jnp.sum(lls[u][1], axis=1, keepdims=True)
            return carry

        lax.fori_loop(0, nkb, sweep_a, 0)

        for cr in c_refs:
            cr[...] = jnp.zeros_like(cr)
        dqa_ref[...] = jnp.zeros_like(dqa_ref)

        def sweep_b(j, carry):
            rows = pl.ds(pl.multiple_of(j * TK, TK), TK)
            mask = _causal_mask(i, j, tq)
            dls = [dl_ref[u, j] for u in range(nu)]
            ps = [c_refs[u][...] + _split_dot(dls[u], tri_before) for u in range(nu)]
            for u, hs in enumerate(heads):
                beta = be_ref[u, j]
                dz = jnp.where(mask, (dls[u] * (1.0 - beta) - beta * ps[u]) * scale, 0.0)
                dqa_ref[:, hs] += jnp.dot(dz.astype(BF16), k_ref[rows, hs].astype(BF16), preferred_element_type=F32)
                dka_ref[rows, hs] += jnp.dot(dz.T.astype(BF16), qs[u], preferred_element_type=F32)
                c_refs[u][...] += jnp.sum(dls[u], axis=1, keepdims=True)
            return carry

        lax.fori_loop(0, nkb, sweep_b, 0)
        dq_ref[...] = dqa_ref[...].astype(BF16)

        @pl.when(i == n_qb - 1)
        def _():
            dk_ref[...] = dka_ref[...].astype(BF16)
            dv_ref[...] = dva_ref[...].astype(BF16)

    w = nu * HEAD_DIM
    ng = n_heads // nu
    qblk = pl.BlockSpec((tq, w), lambda h, i: (i, h))
    full = pl.BlockSpec((s, w), lambda h, i: (0, h))
    sds = jax.ShapeDtypeStruct((s, n_heads * HEAD_DIM), BF16)
    return _call("attn_bwd", body, (ng, n_qb),
                 [qblk, pl.BlockSpec((s, w), lambda h, i: (0, ng + h)), pl.BlockSpec((s, w), lambda h, i: (0, 2 * ng + h)),
                  qblk],
                 [qblk, full, full], [sds, sds, sds],
                 [pltpu.VMEM((s, w), F32), pltpu.VMEM((s, w), F32),
                  pltpu.VMEM((nu, n_kb, tq, TK), F32), pltpu.VMEM((nu, n_kb, tq, TK), F32),
                  pltpu.VMEM((tq, w), F32), *[pltpu.VMEM((tq, 1), F32)] * nu],
                 ("parallel", "arbitrary"), (proj, proj, proj, do_sb), comm)


def _adamw(name, w, m, v, parts, part_specs, tr, comm=None):
    r, c = w.shape
    n_parts = len(parts)

    def body(*refs):
        w_ref, m_ref, v_ref = refs[:3]
        p_refs = refs[3:3 + n_parts]
        g_ref, d_ref, nm_ref, nv_ref = refs[3 + n_parts:]
        g = p_refs[0][...].astype(F32)
        for p in p_refs[1:]:
            g = g + p[...].astype(F32)
        nm = ADAM_B1 * m_ref[...] + (1.0 - ADAM_B1) * g
        nv = ADAM_B2 * v_ref[...] + (1.0 - ADAM_B2) * jnp.square(g)
        m_hat = nm / (1.0 - ADAM_B1 ** ADAM_STEP)
        v_hat = nv / (1.0 - ADAM_B2 ** ADAM_STEP)
        g_ref[...] = g
        d_ref[...] = -ADAM_LR * (m_hat / (jnp.sqrt(v_hat) + ADAM_EPS) + ADAM_WD * w_ref[...])
        nm_ref[...] = nm
        nv_ref[...] = nv

    blk = pl.BlockSpec((tr, c), lambda i: (i, 0))
    sds = jax.ShapeDtypeStruct((r, c), F32)
    return _call(name, body, (r // tr,), [blk, blk, blk, *part_specs], [blk] * 4, [sds] * 4, [], ("parallel",),
                 (w, m, v, *parts), comm)


def _adamw_big(name, w, m, v, recv, comm=None):
    r, c = w.shape
    tr = _tile(r, 128)
    order = (3, 0, 1, 2)
    specs = [pl.BlockSpec((None, tr, c), functools.partial(lambda i, slot: (slot, i, 0), slot=sl)) for sl in order]
    return _adamw(name, w, m, v, [recv] * 4, specs, tr, comm)


def _adamw_small(name, w, m, v, g):
    r, c = w.shape
    return _adamw(name, w, m, v, [g], [pl.BlockSpec((r, c), lambda i: (0, 0))], r)


def kernel(x, g_pre_mix, w_in, b_in, w_dw, b_dw, g_conv_ln, b_conv_ln, w_sb_out, w_conv_out, w_o, g_post_mix, g_pre_mlp, w_up, w_down, g_post_mlp, loss_target, m_g_pre_mix, m_w_in, m_b_in, m_w_dw, m_b_dw, m_g_conv_ln, m_b_conv_ln, m_w_sb_out, m_w_conv_out, m_w_o, m_g_post_mix, m_g_pre_mlp, m_w_up, m_w_down, m_g_post_mlp, v_g_pre_mix, v_w_in, v_b_in, v_w_dw, v_b_dw, v_g_conv_ln, v_b_conv_ln, v_w_sb_out, v_w_conv_out, v_w_o, v_g_post_mix, v_g_pre_mlp, v_w_up, v_w_down, v_g_post_mlp):
    xs, tgt = x[0], loss_target[0]
    s, d = xs.shape
    d_half = d // 2
    n_heads = d_half // HEAD_DIM
    d_ff = NDEV * w_up.shape[2]
    core = lax.axis_index("c").astype(jnp.int32).reshape(1)
    dev = 4 * lax.axis_index("x") + 2 * lax.axis_index("y") + lax.axis_index("c")

    w_dw_pad = jnp.pad(w_dw[0], ((0, CONV_PAD - CONV_WIDTH), (0, 0)))
    sh_in, sh_sb, sh_cv, sh_o, sh_up, sh_down = [w[0].astype(BF16) for w in (w_in, w_sb_out, w_conv_out, w_o, w_up, w_down)]
    wg_in, = _run_comm("all_gather_first", _ag_comm([sh_in]))

    ag_down = _Chunked(_ag_comm, [sh_down], sh_down.shape[0], 8)
    h, h_t = _prenorm(xs, g_pre_mix)
    ag_up = _Chunked(_ag_comm, [sh_up], sh_up.shape[0], 8)
    proj, wg_sb, wg_cv, wg_o, wg_dw, *part = _mm_cols(
        "proj", h, wg_in, bias=b_in, comm=_join(_ag_comm([sh_sb, sh_cv, sh_o, w_dw_pad]), ag_up.take(1)))
    ag_up.done(part)
    o_sb, o_sb_t, wg_up = _attn_fwd(proj, n_heads, ag_up.take(7))
    wf_dw = wg_dw.transpose(1, 0, 2).reshape(CONV_PAD, d_half)
    wf_o = wg_o.reshape(d, d)
    u1, *part = _conv_fwd(proj, wf_dw, b_dw, d_half, ag_down.take())
    ag_down.done(part)
    u3, u3_t = _ln_silu(u1, g_conv_ln, b_conv_ln)
    o_sbp = _mm_cols("sb_out", o_sb, wg_sb)[0]
    o_cv = _mm_cols("conv_out", u3, wg_cv)[0]
    merged, merged_t, *part = _merge(proj, o_sbp, o_cv, d, ag_down.take())
    ag_down.done(part)
    y, *part = _mm_plain("w_o", merged, wf_o, False, F32, comm=ag_down.take())
    ag_down.done(part)
    x1, h2, h2_t, *part = _postnorm_mix(xs, y, g_post_mix, g_pre_mlp, ag_down.take())
    ag_down.done(part)

    tm_up = _tile(s, 1024)
    ns_up = wg_up.shape[2]
    tk_up = _tile(d, 2048)

    def up_epilogue(acc):
        f = jnp.square(jnp.maximum(acc, 0.0))
        return acc, f, f.T

    a_act, f, f_t, wg_down = _matmul(
        "w_up", h2, wg_up,
        [jax.ShapeDtypeStruct((s, d_ff), BF16), jax.ShapeDtypeStruct((s, d_ff), BF16), jax.ShapeDtypeStruct((d_ff, s), BF16)],
        (s // tm_up, NDEV, d // tk_up),
        pl.BlockSpec((tm_up, tk_up), lambda i, dd, kk: (i, kk)),
        pl.BlockSpec((None, tk_up, ns_up), lambda i, dd, kk: (dd, kk, 0)),
        [pl.BlockSpec((tm_up, ns_up), lambda i, dd, kk: (i, dd)), pl.BlockSpec((tm_up, ns_up), lambda i, dd, kk: (i, dd)),
         pl.BlockSpec((ns_up, tm_up), lambda i, dd, kk: (dd, i))],
        1, False, (tm_up, ns_up), epilogue=up_epilogue, comm=ag_down.take(4))
    wf_down = wg_down.reshape(d_ff, d)
    f2 = _mm_plain("w_down", f, wf_down, False, F32)[0]
    dx2, df2, dg_post_mlp, loss_part = _loss_head(x1, f2, tgt, g_post_mlp)

    tm_b, tn_b = _tile(s, 1024), _tile(d_ff, 1024)
    da = _mm_plain("w_down_bwd", df2, wf_down, True, BF16,
                   extra=(a_act,), extra_specs=(pl.BlockSpec((tm_b, tn_b), lambda i, j, kk: (i, j)),),
                   epilogue=lambda acc, av: (acc * (2.0 * jnp.maximum(av.astype(F32), 0.0)),),
                   outs=[jax.ShapeDtypeStruct((s, d_ff), BF16)],
                   out_specs=[pl.BlockSpec((tm_b, tn_b), lambda i, j, kk: (i, j))])[0]
    gw_down = _mm_plain("w_down_grad", f_t, df2, False, BF16)[0]
    big_down = gw_down.reshape(4, 2, d_ff // NDEV, d)
    gw_up, sib_down = _mm_dw_cols("w_up_grad", h2_t, da, comm=_sibling_comm([big_down]))
    big_up = gw_up.reshape(4, 2, d, d_ff // NDEV)
    dh2, sib_up = _mm_cols_t("w_up_bwd", da, wg_up, comm=_sibling_comm([big_up]))
    rs_down = _Chunked(_chips_comm, [_pair_sum(big_down, sib_down, core)], d_ff // NDEV, 8)
    rs_up = _Chunked(_chips_comm, [_pair_sum(big_up, sib_up, core)], d, 8)
    dx1, dy, dg_pre_mlp, dg_post_mix, *part = _midnorm_bwd(dx2, dh2, x1, y, g_post_mix, g_pre_mlp, rs_down.take())
    rs_down.done(part)
    gw_o, *part = _mm_plain("w_o_grad", merged_t, dy, False, BF16, comm=rs_down.take())
    rs_down.done(part)
    dmerged, *part = _mm_plain("w_o_bwd", dy, wf_o, True, F32, comm=rs_down.take())
    rs_down.done(part)
    do_sbp, do_cv, dgate_sb, dgate_cv, *part = _merge_bwd(dmerged, proj, o_sbp, o_cv, d, rs_down.take())
    rs_down.done(part)
    gw_cv = _mm_dw_cols("conv_out_grad", u3_t, do_cv)[0]
    gw_sb = _mm_dw_cols("sb_out_grad", o_sb_t, do_sbp)[0]
    du3 = _mm_cols_t("conv_out_bwd", do_cv, wg_cv)[0]
    do_sb = _mm_cols_t("sb_out_bwd", do_sbp, wg_sb)[0]
    du1, dg_ln, db_ln = _ln_silu_bwd(du3, u1, g_conv_ln, b_conv_ln)
    big_mid = [gw_sb.reshape(4, 2, d_half, d // NDEV), gw_cv.reshape(4, 2, d_half, d // NDEV),
               gw_o.reshape(4, 2, d // NDEV, d)]
    dglu_a, dglu_b, dw_dw, db_dw, part, *sib_mid = _conv_bwd(du1, proj, wf_dw, d_half,
                                                             _join(rs_up.take(2), _sibling_comm(big_mid)))
    rs_up.done([part])
    sums_mid = [_pair_sum(g, r, core) for g, r in zip(big_mid, sib_mid)]
    dq, dk, dv, r_up, r_sb, r_cv, r_o = _attn_bwd(proj, do_sb, n_heads, _join(rs_up.take(6), _chips_comm(sums_mid)))
    dproj = jnp.concatenate([dq, dk, dv, dglu_a, dglu_b, dgate_sb, dgate_cv], axis=1)
    gw_in, r_down = _mm_dw_cols("w_in_grad", h_t, dproj, comm=rs_down.take(4))
    big_in = gw_in.reshape(4, 2, d, gw_in.shape[2])
    db_in, sib_in = _colsum(dproj, _sibling_comm([big_in]))
    dh, r_in = _mm_cols_t("w_in_bwd", dproj, wg_in, comm=_chips_comm([_pair_sum(big_in, sib_in, core)]))
    grad_x, dg_pre_mix = _prenorm_bwd(dx1, dh, xs, g_pre_mix)

    small = [dg_pre_mix, db_in, dw_dw.reshape(1, -1), db_dw, dg_ln, db_ln, dg_post_mix, dg_pre_mlp, dg_post_mlp]
    sizes = [a.shape[1] for a in small]
    packed = jnp.concatenate(small, axis=1).reshape(-1, LANES)
    total = _sum_small(_all_gather_small(packed)).reshape(1, -1)
    offs = [0]
    for n in sizes:
        offs.append(offs[-1] + n)
    (g_g_pre_mix, g_b_in, g_w_dw_flat, g_b_dw, g_g_conv_ln, g_b_conv_ln, g_g_post_mix, g_g_pre_mlp,
     g_g_post_mlp) = [total[:, offs[k]:offs[k + 1]] for k in range(len(sizes))]
    ch = w_dw.shape[2]
    g_w_dw = lax.dynamic_slice_in_dim(g_w_dw_flat.reshape(CONV_PAD, d_half), dev * ch, ch, axis=1)[:CONV_WIDTH]

    loss = lax.psum(loss_part[0, 0], ("x", "y", "c"))

    res = {}
    res["w_up"] = _adamw_big("adamw_w_up", w_up[0], m_w_up[0], v_w_up[0], r_up)
    res["w_down"] = _adamw_big("adamw_w_down", w_down[0], m_w_down[0], v_w_down[0], r_down)
    res["g_pre_mix"] = _adamw_small("adamw_g_pre_mix", g_pre_mix, m_g_pre_mix, v_g_pre_mix, g_g_pre_mix)
    res["w_in"] = _adamw_big("adamw_w_in", w_in[0], m_w_in[0], v_w_in[0], r_in)
    res["b_in"] = _adamw_small("adamw_b_in", b_in, m_b_in, v_b_in, g_b_in)
    res["w_dw"] = _adamw_small("adamw_w_dw", w_dw[0], m_w_dw[0], v_w_dw[0], g_w_dw)
    res["b_dw"] = _adamw_small("adamw_b_dw", b_dw, m_b_dw, v_b_dw, g_b_dw)
    res["g_conv_ln"] = _adamw_small("adamw_g_conv_ln", g_conv_ln, m_g_conv_ln, v_g_conv_ln, g_g_conv_ln)
    res["b_conv_ln"] = _adamw_small("adamw_b_conv_ln", b_conv_ln, m_b_conv_ln, v_b_conv_ln, g_b_conv_ln)
    res["w_sb_out"] = _adamw_big("adamw_w_sb_out", w_sb_out[0], m_w_sb_out[0], v_w_sb_out[0], r_sb)
    res["w_conv_out"] = _adamw_big("adamw_w_conv_out", w_conv_out[0], m_w_conv_out[0], v_w_conv_out[0], r_cv)
    res["w_o"] = _adamw_big("adamw_w_o", w_o[0], m_w_o[0], v_w_o[0], r_o)
    res["g_post_mix"] = _adamw_small("adamw_g_post_mix", g_post_mix, m_g_post_mix, v_g_post_mix, g_g_post_mix)
    res["g_pre_mlp"] = _adamw_small("adamw_g_pre_mlp", g_pre_mlp, m_g_pre_mlp, v_g_pre_mlp, g_g_pre_mlp)
    res["g_post_mlp"] = _adamw_small("adamw_g_post_mlp", g_post_mlp, m_g_post_mlp, v_g_post_mlp, g_g_post_mlp)

    names = ["g_pre_mix", "w_in", "b_in", "w_dw", "b_dw", "g_conv_ln", "b_conv_ln", "w_sb_out", "w_conv_out", "w_o",
             "g_post_mix", "g_pre_mlp", "w_up", "w_down", "g_post_mlp"]
    three_d = {"w_in", "w_dw", "w_sb_out", "w_conv_out", "w_o", "w_up", "w_down"}

    def shaped(nm, arr):
        return arr[None] if nm in three_d else arr

    out = [loss, grad_x[None]]
    for k in range(4):
        out += [shaped(nm, res[nm][k]) for nm in names]
    return tuple(out)
```

```python
import functools
import math

import jax
import jax.numpy as jnp
from jax import lax
from jax.experimental import pallas as pl
from jax.experimental.pallas import tpu as pltpu

F32 = jnp.float32
BF16 = jnp.bfloat16
NDEV = 8
LANES = 128
EPS = 1e-6
CONV_WIDTH = 31
CONV_PAD = 32
HEAD_DIM = 128
ADAM_LR = 0.001
ADAM_B1 = 0.9
ADAM_B2 = 0.999
ADAM_EPS = 1e-08
ADAM_WD = 0.01
ADAM_STEP = 10
VMEM_LIMIT = 56 * 1024 * 1024
MESH = pl.DeviceIdType.MESH
ANY = pl.BlockSpec(memory_space=pl.ANY)


def _tile(n, pref):
    t = min(n, pref)
    assert n % t == 0, (n, t)
    return t


def _sigmoid(v):
    return 1.0 / (1.0 + jnp.exp(-v))


def _position():
    return lax.axis_index("x"), lax.axis_index("y"), lax.axis_index("c")


class _Comm:
    def __init__(self, ins, outs, scratch, start, finish, aliases=None, mid=None):
        self.ins, self.outs, self.scratch = list(ins), list(outs), list(scratch)
        self.start, self.finish, self.aliases = start, finish, dict(aliases or {})
        self.mid = mid


_NO_COMM = _Comm([], [], [], None, None)


def _call(name, body, grid, in_specs, out_specs, out_shape, scratch_shapes, sem, args, comm=None):
    comm = comm or _NO_COMM
    n_in, n_out, n_scr = len(in_specs), len(out_specs), len(scratch_shapes)
    n_cin, n_cout = len(comm.ins), len(comm.outs)

    n_steps = math.prod(grid)
    mid_step = n_steps - max(1, n_steps // 8) if (comm.mid is not None and n_steps >= 4) else None

    def edge(c_ins, c_outs, c_scr, at_step, actions):
        linear = 0
        for ax, g in enumerate(grid):
            linear = linear * g + pl.program_id(ax)

        @pl.when(linear == at_step)
        def _():
            for act in actions:
                act(c_ins, c_outs, c_scr)

    def wrapped(*refs):
        ins, c_ins = refs[:n_in], refs[n_in:n_in + n_cin]
        pos = n_in + n_cin
        outs, c_outs = refs[pos:pos + n_out], refs[pos + n_out:pos + n_out + n_cout]
        pos += n_out + n_cout
        scr, c_scr = refs[pos:pos + n_scr], refs[pos + n_scr:]
        if n_cin:
            edge(c_ins, c_outs, c_scr, 0, [comm.start])
            if mid_step is not None:
                edge(c_ins, c_outs, c_scr, mid_step, [comm.mid])
        body(*ins, *outs, *scr)
        if n_cin:
            late = [comm.finish] if (mid_step is not None or comm.mid is None) else [comm.mid, comm.finish]
            edge(c_ins, c_outs, c_scr, n_steps - 1, late)

    if n_cin:
        sem = ("arbitrary",) * len(grid)
    return pl.pallas_call(
        wrapped, name=name, grid=grid,
        in_specs=[*in_specs, *[ANY] * n_cin], out_specs=[*out_specs, *[ANY] * n_cout],
        out_shape=[*out_shape, *comm.outs], scratch_shapes=[*scratch_shapes, *comm.scratch],
        input_output_aliases={n_in + ci: n_out + co for ci, co in comm.aliases.items()},
        compiler_params=pltpu.CompilerParams(dimension_semantics=sem, vmem_limit_bytes=VMEM_LIMIT),
    )(*args, *comm.ins)


def _run_comm(name, comm):
    n_in, n_out = len(comm.ins), len(comm.outs)

    def body(*refs):
        ins, outs, scr = refs[:n_in], refs[n_in:n_in + n_out], refs[n_in + n_out:]
        comm.start(ins, outs, scr)
        if comm.mid is not None:
            comm.mid(ins, outs, scr)
        comm.finish(ins, outs, scr)

    return pl.pallas_call(body, name=name, in_specs=[ANY] * n_in, out_specs=[ANY] * n_out, out_shape=comm.outs,
                          scratch_shapes=comm.scratch,
                          input_output_aliases=comm.aliases)(*comm.ins)


def _rows_of(ref, rows):
    return ref if rows is None else ref.at[pl.ds(rows[0], rows[1])]


def _ag_comm(shards, rows=None, into=None):
    n = len(shards)

    def parts(ins, outs, scr):
        send_sems, recv_sems, local_sems = scr
        x, y, c = _position()
        chips = [(1 - x, y), (x, 1 - y), (1 - x, 1 - y)]

        def copy(a, k, block, to, own=False):
            px, py, pc = block
            dst = _rows_of(outs[a].at[4 * px + 2 * py + pc], rows)
            return pltpu.make_async_remote_copy(src_ref=_rows_of(ins[a], rows) if own else dst, dst_ref=dst,
                                                send_sem=send_sems.at[a, k], recv_sem=recv_sems.at[a, k],
                                                device_id=to, device_id_type=MESH)

        mine = [pltpu.make_async_copy(_rows_of(ins[a], rows), _rows_of(outs[a].at[4 * x + 2 * y + c], rows),
                                      local_sems.at[a]) for a in range(n)]
        first = []
        for a in range(n):
            first.append(copy(a, 0, (x, y, c), (x, y, 1 - c), own=True))
            first += [copy(a, 1 + j, (x, y, c), (*chip, c), own=True) for j, chip in enumerate(chips)]
        return copy, mine, first, chips, (x, y, c), (x, y, 1 - c)

    def start(ins, outs, scr):
        _, mine, first, _, _, _ = parts(ins, outs, scr)
        for cp in mine + first:
            cp.start()

    def mid(ins, outs, scr):
        copy, _, _, chips, me, sibling = parts(ins, outs, scr)
        for a in range(n):
            for j, chip in enumerate(chips):
                copy(a, 1 + j, (*chip, me[2]), me).wait_recv()
                copy(a, 4 + j, (*chip, me[2]), sibling).start()

    def finish(ins, outs, scr):
        copy, mine, first, chips, me, sibling = parts(ins, outs, scr)
        c = me[2]
        passed = [copy(a, 4 + j, (*chip, c), sibling) for a in range(n) for j, chip in enumerate(chips)]
        for a in range(n):
            copy(a, 0, sibling, me).wait_recv()
            for j, chip in enumerate(chips):
                copy(a, 4 + j, (*chip, 1 - c), me).wait_recv()
        for cp in first + passed:
            cp.wait_send()
        for cp in mine:
            cp.wait()

    return _Comm([*shards, *(into or [])], [jax.ShapeDtypeStruct((NDEV, *sh.shape), sh.dtype) for sh in shards],
                 [pltpu.SemaphoreType.DMA((n, 7)), pltpu.SemaphoreType.DMA((n, 7)), pltpu.SemaphoreType.DMA((n,))],
                 start, finish, {n + a: a for a in range(n)} if into else None, mid)


def _sibling_comm(grads):
    n = len(grads)

    def copies(ins, outs, scr):
        send_sems, recv_sems = scr
        x, y, c = _position()
        return [pltpu.make_async_remote_copy(src_ref=ins[a].at[k, 1 - c], dst_ref=outs[a].at[k],
                                             send_sem=send_sems.at[a, k], recv_sem=recv_sems.at[a, k],
                                             device_id=(x, y, 1 - c), device_id_type=MESH)
                for a in range(n) for k in range(4)]

    def start(ins, outs, scr):
        for cp in copies(ins, outs, scr):
            cp.start()

    def finish(ins, outs, scr):
        for cp in copies(ins, outs, scr):
            cp.wait()

    return _Comm(grads, [jax.ShapeDtypeStruct((4, *g.shape[2:]), g.dtype) for g in grads],
                 [pltpu.SemaphoreType.DMA((n, 4)), pltpu.SemaphoreType.DMA((n, 4))], start, finish)


def _chips_comm(sums, rows=None, into=None):
    n = len(sums)

    def copies(ins, outs, scr):
        send_sems, recv_sems, local_sems = scr
        x, y, c = _position()
        chips = [(1 - x, y), (x, 1 - y), (1 - x, 1 - y)]
        own = [pltpu.make_async_copy(_rows_of(ins[a].at[2 * x + y], rows), _rows_of(outs[a].at[3], rows), local_sems.at[a])
               for a in range(n)]
        remote = [pltpu.make_async_remote_copy(src_ref=_rows_of(ins[a].at[2 * px + py], rows),
                                               dst_ref=_rows_of(outs[a].at[j], rows),
                                               send_sem=send_sems.at[a, j], recv_sem=recv_sems.at[a, j],
                                               device_id=(px, py, c), device_id_type=MESH)
                  for a in range(n) for j, (px, py) in enumerate(chips)]
        return own + remote

    def start(ins, outs, scr):
        for cp in copies(ins, outs, scr):
            cp.start()

    def finish(ins, outs, scr):
        for cp in copies(ins, outs, scr):
            cp.wait()

    return _Comm([*sums, *(into or [])], [jax.ShapeDtypeStruct(sm.shape, sm.dtype) for sm in sums],
                 [pltpu.SemaphoreType.DMA((n, 3)), pltpu.SemaphoreType.DMA((n, 3)), pltpu.SemaphoreType.DMA((n,))],
                 start, finish, {n + a: a for a in range(n)} if into else None)


def _join(c1, c2):
    n_in, n_out, n_scr = len(c1.ins), len(c1.outs), len(c1.scratch)
    aliases = dict(c1.aliases)
    aliases.update({n_in + ci: n_out + co for ci, co in c2.aliases.items()})

    def start(ins, outs, scr):
        c1.start(ins[:n_in], outs[:n_out], scr[:n_scr])
        c2.start(ins[n_in:], outs[n_out:], scr[n_scr:])

    def mid(ins, outs, scr):
        if c1.mid is not None:
            c1.mid(ins[:n_in], outs[:n_out], scr[:n_scr])
        if c2.mid is not None:
            c2.mid(ins[n_in:], outs[n_out:], scr[n_scr:])

    def finish(ins, outs, scr):
        c1.finish(ins[:n_in], outs[:n_out], scr[:n_scr])
        c2.finish(ins[n_in:], outs[n_out:], scr[n_scr:])

    return _Comm(c1.ins + c2.ins, c1.outs + c2.outs, c1.scratch + c2.scratch, start, finish, aliases,
                 mid if (c1.mid is not None or c2.mid is not None) else None)


class _Chunked:
    def __init__(self, make, arrays, n_rows, n_chunks):
        self.make, self.arrays, self.into = make, arrays, None
        step = n_rows // n_chunks
        assert step * n_chunks == n_rows
        self.todo = [(k * step, step) for k in range(n_chunks)]

    def take(self, count=1):
        r0, nr = self.todo[0][0], sum(t[1] for t in self.todo[:count])
        self.todo = self.todo[count:]
        return self.make(self.arrays, (r0, nr), self.into)

    def done(self, outs):
        self.into = list(outs)
        return self.into


def _all_gather_small(part):
    def body(in_ref, out_ref, send_sems, recv_sems, local_sem):
        x, y, c = _position()
        me = 4 * x + 2 * y + c
        mine = pltpu.make_async_copy(in_ref, out_ref.at[me], local_sem)
        mine.start()
        flips = [(fx, fy, fc) for fx in (0, 1) for fy in (0, 1) for fc in (0, 1)][1:]
        copies = []
        for k, (fx, fy, fc) in enumerate(flips):
            cp = pltpu.make_async_remote_copy(src_ref=in_ref, dst_ref=out_ref.at[me], send_sem=send_sems.at[k],
                                              recv_sem=recv_sems.at[k],
                                              device_id=(x ^ fx, y ^ fy, c ^ fc), device_id_type=MESH)
            cp.start()
            copies.append(cp)
        for k, (fx, fy, fc) in enumerate(flips):
            peer = 4 * (x ^ fx) + 2 * (y ^ fy) + (c ^ fc)
            pltpu.make_async_remote_copy(src_ref=in_ref, dst_ref=out_ref.at[peer], send_sem=send_sems.at[k],
                                         recv_sem=recv_sems.at[k], device_id=(x, y, c), device_id_type=MESH).wait_recv()
        for cp in copies:
            cp.wait_send()
        mine.wait()

    return pl.pallas_call(
        body, name="all_gather_small", in_specs=[ANY], out_specs=ANY,
        out_shape=jax.ShapeDtypeStruct((NDEV, *part.shape), part.dtype),
        scratch_shapes=[pltpu.SemaphoreType.DMA((7,)), pltpu.SemaphoreType.DMA((7,)), pltpu.SemaphoreType.DMA],
    )(part)


def _pair_sum(g, recv, core):
    _, _, r, c = g.shape
    tr = _tile(r, 512)

    def body(core_ref, g_ref, r_ref, o_ref):
        o_ref[...] = (g_ref[...].astype(F32) + r_ref[...].astype(F32)).astype(o_ref.dtype)

    return pl.pallas_call(
        body, name="pair_sum",
        grid_spec=pltpu.PrefetchScalarGridSpec(
            num_scalar_prefetch=1, grid=(4, r // tr),
            in_specs=[pl.BlockSpec((None, None, tr, c), lambda k, i, core_ref: (k, core_ref[0], i, 0)),
                      pl.BlockSpec((None, tr, c), lambda k, i, core_ref: (k, i, 0))],
            out_specs=pl.BlockSpec((None, tr, c), lambda k, i, core_ref: (k, i, 0))),
        out_shape=jax.ShapeDtypeStruct((4, r, c), g.dtype),
        compiler_params=pltpu.CompilerParams(dimension_semantics=("parallel", "parallel"), vmem_limit_bytes=VMEM_LIMIT),
    )(core, g, recv)


def _sum_small(gathered):
    _, r, l = gathered.shape

    def body(g_ref, o_ref):
        acc = g_ref[0]
        for d in range(1, NDEV):
            acc = acc + g_ref[d]
        o_ref[...] = acc

    return pl.pallas_call(
        body, name="sum_small", in_specs=[pl.BlockSpec((NDEV, r, l), lambda: (0, 0, 0))],
        out_specs=pl.BlockSpec((r, l), lambda: (0, 0)), out_shape=jax.ShapeDtypeStruct((r, l), F32),
    )(gathered)


def _matmul(name, a, b, outs, grid, a_spec, b_spec, out_specs, n_red, nt, acc_shape,
            extra=(), extra_specs=(), epilogue=None, comm=None):
    n_extra, n_out = len(extra), len(outs)
    red_axes = tuple(range(len(grid) - n_red, len(grid)))
    red_sizes = tuple(grid[ax] for ax in red_axes)
    single = all(sz == 1 for sz in red_sizes)
    dims = (((1,), (1,)), ((), ())) if nt else (((1,), (0,)), ((), ()))

    def body(*refs):
        a_ref, b_ref = refs[0], refs[1]
        ex_refs = refs[2:2 + n_extra]
        o_refs = refs[2 + n_extra:2 + n_extra + n_out]
        acc_ref = refs[-1]

        def write(acc):
            vals = (acc,) if epilogue is None else epilogue(acc, *[r[...] for r in ex_refs])
            for o_ref, val in zip(o_refs, vals):
                o_ref[...] = val.astype(o_ref.dtype)

        if len(b_ref.shape) == 3:
            w = b_ref.shape[2]
            part = sum(lax.dot_general(a_ref[:, p * w:(p + 1) * w].astype(BF16), b_ref[p].astype(BF16), dims,
                                       preferred_element_type=F32) for p in range(b_ref.shape[0]))
        else:
            part = lax.dot_general(a_ref[...].astype(BF16), b_ref[...].astype(BF16), dims, preferred_element_type=F32)
        if single:
            write(part)
        else:
            ks = [pl.program_id(ax) for ax in red_axes]
            first = functools.reduce(jnp.logical_and, [k == 0 for k in ks])
            last = functools.reduce(jnp.logical_and, [k == sz - 1 for k, sz in zip(ks, red_sizes)])

            @pl.when(first)
            def _():
                acc_ref[...] = part

            @pl.when(jnp.logical_not(first))
            def _():
                acc_ref[...] += part

            @pl.when(last)
            def _():
                write(acc_ref[...])

    sem = ("parallel",) * (len(grid) - n_red) + ("arbitrary",) * n_red
    return _call(name, body, grid, [a_spec, b_spec, *extra_specs], list(out_specs), list(outs),
                 [pltpu.VMEM((8, LANES) if single else acc_shape, F32)], sem, (a, b, *extra), comm)


def _mm_cols(name, a, wg, bias=None, out_dtype=F32, comm=None):
    m, k = a.shape
    _, _, ns = wg.shape
    tm, tk = _tile(m, 1024), _tile(k, 2048)
    grid = (m // tm, NDEV, k // tk)
    extra, extra_specs, epi = (), (), None
    if bias is not None:
        extra, extra_specs = (bias,), (pl.BlockSpec((1, ns), lambda i, d, kk: (0, d)),)
        epi = lambda acc, bv: (acc + bv,)
    return _matmul(name, a, wg, [jax.ShapeDtypeStruct((m, NDEV * ns), out_dtype)], grid,
                   pl.BlockSpec((tm, tk), lambda i, d, kk: (i, kk)),
                   pl.BlockSpec((None, tk, ns), lambda i, d, kk: (d, kk, 0)),
                   [pl.BlockSpec((tm, ns), lambda i, d, kk: (i, d))], 1, False, (tm, ns),
                   extra, extra_specs, epi, comm)


def _mm_cols_t(name, a, wg, comm=None):
    m, _ = a.shape
    _, n, ns = wg.shape
    tm, tn = _tile(m, 1024), _tile(n, 1024)
    per = 2
    grid = (m // tm, n // tn, NDEV // per)
    return _matmul(name, a, wg, [jax.ShapeDtypeStruct((m, n), F32)], grid,
                   pl.BlockSpec((tm, per * ns), lambda i, j, d: (i, d)),
                   pl.BlockSpec((per, tn, ns), lambda i, j, d: (d, j, 0)),
                   [pl.BlockSpec((tm, tn), lambda i, j, d: (i, j))], 1, True, (tm, tn), comm=comm)


def _mm_dw_cols(name, at, g, comm=None):
    m, t = at.shape
    ns = g.shape[1] // NDEV
    tm, tk = _tile(m, 1024), _tile(t, 2048)
    grid = (m // tm, NDEV, t // tk)
    return _matmul(name, at, g, [jax.ShapeDtypeStruct((NDEV, m, ns), BF16)], grid,
                   pl.BlockSpec((tm, tk), lambda i, d, kk: (i, kk)),
                   pl.BlockSpec((tk, ns), lambda i, d, kk: (kk, d)),
                   [pl.BlockSpec((None, tm, ns), lambda i, d, kk: (d, i, 0))], 1, False, (tm, ns), comm=comm)


def _mm_plain(name, a, b, nt, out_dtype, extra=(), extra_specs=(), epilogue=None, outs=None, out_specs=None, comm=None):
    m, k = a.shape
    n = b.shape[0] if nt else b.shape[1]
    tm, tn, tk = _tile(m, 1024), _tile(n, 1024), _tile(k, 2048)
    grid = (m // tm, n // tn, k // tk)
    b_spec = (pl.BlockSpec((tn, tk), lambda i, j, kk: (j, kk)) if nt
              else pl.BlockSpec((tk, tn), lambda i, j, kk: (kk, j)))
    if outs is None:
        outs = [jax.ShapeDtypeStruct((m, n), out_dtype)]
        out_specs = [pl.BlockSpec((tm, tn), lambda i, j, kk: (i, j))]
    return _matmul(name, a, b, outs, grid, pl.BlockSpec((tm, tk), lambda i, j, kk: (i, kk)), b_spec,
                   out_specs, 1, nt, (tm, tn), extra, extra_specs, epilogue, comm)


def _rms_rows(v):
    return lax.rsqrt(jnp.mean(v * v, axis=-1, keepdims=True) + EPS)


def _prenorm(x, g, comm=None):
    s, d = x.shape
    ts = _tile(s, 256)

    def body(x_ref, g_ref, h_ref, ht_ref):
        xv = x_ref[...]
        h = xv * _rms_rows(xv) * g_ref[...]
        h_ref[...] = h.astype(BF16)
        ht_ref[...] = h.T.astype(BF16)

    return _call("prenorm", body, (s // ts,),
                 [pl.BlockSpec((ts, d), lambda i: (i, 0)), pl.BlockSpec((1, d), lambda i: (0, 0))],
                 [pl.BlockSpec((ts, d), lambda i: (i, 0)), pl.BlockSpec((d, ts), lambda i: (0, i))],
                 [jax.ShapeDtypeStruct((s, d), BF16), jax.ShapeDtypeStruct((d, s), BF16)], [], ("parallel",), (x, g), comm)


def _ln_silu(u1, g, b):
    s, c = u1.shape
    ts = _tile(s, 256)

    def body(u_ref, g_ref, b_ref, o_ref, ot_ref):
        u = u_ref[...]
        mu = jnp.mean(u, axis=-1, keepdims=True)
        var = jnp.mean(jnp.square(u - mu), axis=-1, keepdims=True)
        u2 = (u - mu) * lax.rsqrt(var + EPS) * g_ref[...] + b_ref[...]
        u3 = u2 * _sigmoid(u2)
        o_ref[...] = u3.astype(BF16)
        ot_ref[...] = u3.T.astype(BF16)

    vec = pl.BlockSpec((1, c), lambda i: (0, 0))
    return _call("ln_silu", body, (s // ts,), [pl.BlockSpec((ts, c), lambda i: (i, 0)), vec, vec],
                 [pl.BlockSpec((ts, c), lambda i: (i, 0)), pl.BlockSpec((c, ts), lambda i: (0, i))],
                 [jax.ShapeDtypeStruct((s, c), BF16), jax.ShapeDtypeStruct((c, s), BF16)], [], ("parallel",), (u1, g, b))


def _ln_silu_bwd(du3, u1, g, b, comm=None):
    s, c = u1.shape
    ts = _tile(s, 256)

    def body(d_ref, u_ref, g_ref, b_ref, du1_ref, dg_ref, db_ref):
        @pl.when(pl.program_id(0) == 0)
        def _():
            dg_ref[...] = jnp.zeros_like(dg_ref)
            db_ref[...] = jnp.zeros_like(db_ref)

        u = u_ref[...]
        mu = jnp.mean(u, axis=-1, keepdims=True)
        var = jnp.mean(jnp.square(u - mu), axis=-1, keepdims=True)
        rstd = lax.rsqrt(var + EPS)
        uhat = (u - mu) * rstd
        u2 = uhat * g_ref[...] + b_ref[...]
        sg = _sigmoid(u2)
        du2 = d_ref[...] * (sg * (1.0 + u2 * (1.0 - sg)))
        dg_ref[...] += jnp.sum(du2 * uhat, axis=0, keepdims=True)
        db_ref[...] += jnp.sum(du2, axis=0, keepdims=True)
        duh = du2 * g_ref[...]
        du1_ref[...] = rstd * (duh - jnp.mean(duh, axis=-1, keepdims=True)
                               - uhat * jnp.mean(duh * uhat, axis=-1, keepdims=True))

    row = pl.BlockSpec((ts, c), lambda i: (i, 0))
    vec = pl.BlockSpec((1, c), lambda i: (0, 0))
    return _call("ln_silu_bwd", body, (s // ts,), [row, row, vec, vec], [row, vec, vec],
                 [jax.ShapeDtypeStruct((s, c), F32), jax.ShapeDtypeStruct((1, c), F32), jax.ShapeDtypeStruct((1, c), F32)],
                 [], ("arbitrary",), (du3, u1, g, b), comm)


def _merge(proj, o_sbp, o_cv, d, comm=None):
    s = proj.shape[0]
    w = d // 2
    ts = _tile(s, 256)

    def body(gs_ref, gc_ref, a_ref, b_ref, m_ref, mt_ref):
        mg = _sigmoid(gs_ref[...]) * a_ref[...] + _sigmoid(gc_ref[...]) * b_ref[...]
        m_ref[...] = mg.astype(BF16)
        mt_ref[...] = mg.T.astype(BF16)

    blk = pl.BlockSpec((ts, w), lambda i, j: (i, j))
    return _call("merge", body, (s // ts, 2),
                 [pl.BlockSpec((ts, w), lambda i, j: (i, 5 + j)), pl.BlockSpec((ts, w), lambda i, j: (i, 7 + j)), blk, blk],
                 [blk, pl.BlockSpec((w, ts), lambda i, j: (j, i))],
                 [jax.ShapeDtypeStruct((s, d), BF16), jax.ShapeDtypeStruct((d, s), BF16)], [],
                 ("parallel", "parallel"), (proj, proj, o_sbp, o_cv), comm)


def _merge_bwd(dmerged, proj, o_sbp, o_cv, d, comm=None):
    s = proj.shape[0]
    w = d // 2
    ts = _tile(s, 256)

    def body(dm_ref, gs_ref, gc_ref, a_ref, b_ref, da_ref, db_ref, dgs_ref, dgc_ref):
        dm = dm_ref[...]
        ss = _sigmoid(gs_ref[...])
        sc = _sigmoid(gc_ref[...])
        da_ref[...] = (dm * ss).astype(BF16)
        db_ref[...] = (dm * sc).astype(BF16)
        dgs_ref[...] = (dm * a_ref[...] * ss * (1.0 - ss)).astype(BF16)
        dgc_ref[...] = (dm * b_ref[...] * sc * (1.0 - sc)).astype(BF16)

    blk = pl.BlockSpec((ts, w), lambda i, j: (i, j))
    sds = jax.ShapeDtypeStruct((s, d), BF16)
    return _call("merge_bwd", body, (s // ts, 2),
                 [blk, pl.BlockSpec((ts, w), lambda i, j: (i, 5 + j)), pl.BlockSpec((ts, w), lambda i, j: (i, 7 + j)),
                  blk, blk],
                 [blk, blk, blk, blk], [sds, sds, sds, sds], [], ("parallel", "parallel"),
                 (dmerged, proj, proj, o_sbp, o_cv), comm)


def _postnorm_mix(x, y, g_post, g_pre, comm=None):
    s, d = x.shape
    ts = _tile(s, 256)

    def body(x_ref, y_ref, gp_ref, gn_ref, x1_ref, h_ref, ht_ref):
        yv = y_ref[...]
        x1 = x_ref[...] + yv * _rms_rows(yv) * gp_ref[...]
        x1_ref[...] = x1
        h = x1 * _rms_rows(x1) * gn_ref[...]
        h_ref[...] = h.astype(BF16)
        ht_ref[...] = h.T.astype(BF16)

    row = pl.BlockSpec((ts, d), lambda i: (i, 0))
    vec = pl.BlockSpec((1, d), lambda i: (0, 0))
    return _call("postnorm_mix", body, (s // ts,), [row, row, vec, vec],
                 [row, row, pl.BlockSpec((d, ts), lambda i: (0, i))],
                 [jax.ShapeDtypeStruct((s, d), F32), jax.ShapeDtypeStruct((s, d), BF16), jax.ShapeDtypeStruct((d, s), BF16)],
                 [], ("parallel",), (x, y, g_post, g_pre), comm)


def _rms_bwd(dout, vin, g):
    r = _rms_rows(vin)
    vhat = vin * r
    dyh = dout * g
    dvin = r * (dyh - vhat * jnp.mean(dyh * vhat, axis=-1, keepdims=True))
    return dvin, jnp.sum(dout * vhat, axis=0, keepdims=True)


def _loss_head(x1, f2, tgt, g):
    s, d = x1.shape
    ts = _tile(s, 256)

    def body(x1_ref, f_ref, t_ref, g_ref, dx2_ref, df2_ref, dg_ref, loss_ref):
        @pl.when(pl.program_id(0) == 0)
        def _():
            dg_ref[...] = jnp.zeros_like(dg_ref)
            loss_ref[...] = jnp.zeros_like(loss_ref)

        fv = f_ref[...]
        x2 = x1_ref[...] + fv * _rms_rows(fv) * g_ref[...]
        err = x2 - t_ref[...]
        loss_ref[...] += 0.5 * jnp.sum(jnp.mean(err * err, axis=-1, keepdims=True), axis=0, keepdims=True)
        dx2 = err * (1.0 / d)
        dx2_ref[...] = dx2
        df2, dg = _rms_bwd(dx2, fv, g_ref[...])
        df2_ref[...] = df2.astype(BF16)
        dg_ref[...] += dg

    row = pl.BlockSpec((ts, d), lambda i: (i, 0))
    vec = pl.BlockSpec((1, d), lambda i: (0, 0))
    return _call("loss_head", body, (s // ts,), [row, row, row, vec],
                 [row, row, vec, pl.BlockSpec((1, LANES), lambda i: (0, 0))],
                 [jax.ShapeDtypeStruct((s, d), F32), jax.ShapeDtypeStruct((s, d), BF16),
                  jax.ShapeDtypeStruct((1, d), F32), jax.ShapeDtypeStruct((1, LANES), F32)],
                 [], ("arbitrary",), (x1, f2, tgt, g))


def _midnorm_bwd(dx2, dh2, x1, y, g_post, g_pre, comm=None):
    s, d = x1.shape
    ts = _tile(s, 256)

    def body(dx2_ref, dh_ref, x1_ref, y_ref, gp_ref, gn_ref, dx1_ref, dy_ref, dgn_ref, dgp_ref):
        @pl.when(pl.program_id(0) == 0)
        def _():
            dgn_ref[...] = jnp.zeros_like(dgn_ref)
            dgp_ref[...] = jnp.zeros_like(dgp_ref)

        dxa, dgn = _rms_bwd(dh_ref[...], x1_ref[...], gn_ref[...])
        dx1 = dx2_ref[...] + dxa
        dx1_ref[...] = dx1
        dy, dgp = _rms_bwd(dx1, y_ref[...], gp_ref[...])
        dy_ref[...] = dy.astype(BF16)
        dgn_ref[...] += dgn
        dgp_ref[...] += dgp

    row = pl.BlockSpec((ts, d), lambda i: (i, 0))
    vec = pl.BlockSpec((1, d), lambda i: (0, 0))
    return _call("midnorm_bwd", body, (s // ts,), [row, row, row, row, vec, vec], [row, row, vec, vec],
                 [jax.ShapeDtypeStruct((s, d), F32), jax.ShapeDtypeStruct((s, d), BF16),
                  jax.ShapeDtypeStruct((1, d), F32), jax.ShapeDtypeStruct((1, d), F32)],
                 [], ("arbitrary",), (dx2, dh2, x1, y, g_post, g_pre), comm)


def _prenorm_bwd(dx1, dh, x, g, comm=None):
    s, d = x.shape
    ts = _tile(s, 256)

    def body(dx1_ref, dh_ref, x_ref, g_ref, dx_ref, dg_ref):
        @pl.when(pl.program_id(0) == 0)
        def _():
            dg_ref[...] = jnp.zeros_like(dg_ref)

        dxa, dg = _rms_bwd(dh_ref[...], x_ref[...], g_ref[...])
        dx_ref[...] = dx1_ref[...] + dxa
        dg_ref[...] += dg

    row = pl.BlockSpec((ts, d), lambda i: (i, 0))
    vec = pl.BlockSpec((1, d), lambda i: (0, 0))
    return _call("prenorm_bwd", body, (s // ts,), [row, row, row, vec], [row, vec],
                 [jax.ShapeDtypeStruct((s, d), F32), jax.ShapeDtypeStruct((1, d), F32)],
                 [], ("arbitrary",), (dx1, dh, x, g), comm)


def _colsum(a, comm=None):
    s, n = a.shape
    ts = _tile(s, 256)

    def body(a_ref, o_ref):
        @pl.when(pl.program_id(0) == 0)
        def _():
            o_ref[...] = jnp.zeros_like(o_ref)

        o_ref[...] += jnp.sum(a_ref[...].astype(F32), axis=0, keepdims=True)

    return _call("colsum", body, (s // ts,), [pl.BlockSpec((ts, n), lambda i: (i, 0))],
                 [pl.BlockSpec((1, n), lambda i: (0, 0))], [jax.ShapeDtypeStruct((1, n), F32)], [], ("arbitrary",), (a,),
                 comm)


def _shift_rows(win, off, t):
    n = win.shape[0]
    if off == 0:
        return win[:t]
    return pltpu.roll(win, n - off, axis=0)[:t]


def _conv_fwd(proj, w_pad, b_dw, c_total, comm=None):
    s = proj.shape[0]
    nct = c_total // LANES
    t = _tile(s, 256)

    def body(ga_ref, gb_ref, w_ref, b_ref, o_ref, u0_ref):
        u0_ref[pl.ds(0, CONV_PAD), :] = jnp.zeros((CONV_PAD, LANES), F32)
        u0_ref[pl.ds(CONV_PAD, s), :] = ga_ref[...] * _sigmoid(gb_ref[...])
        wv = w_ref[...]

        def chunk(r, carry):
            r0 = pl.multiple_of(r * t, t)
            win = u0_ref[pl.ds(r0, t + CONV_PAD), :]
            acc = jnp.broadcast_to(b_ref[...], (t, LANES))
            for j in range(CONV_WIDTH):
                acc = acc + wv[j:j + 1, :] * _shift_rows(win, j + CONV_PAD - (CONV_WIDTH - 1), t)
            o_ref[pl.ds(r0, t), :] = acc
            return carry

        lax.fori_loop(0, s // t, chunk, 0)

    return _call("conv_fwd", body, (nct,),
                 [pl.BlockSpec((s, LANES), lambda c: (0, 3 * nct + c)), pl.BlockSpec((s, LANES), lambda c: (0, 4 * nct + c)),
                  pl.BlockSpec((CONV_PAD, LANES), lambda c: (0, c)), pl.BlockSpec((1, LANES), lambda c: (0, c))],
                 [pl.BlockSpec((s, LANES), lambda c: (0, c))], [jax.ShapeDtypeStruct((s, c_total), F32)],
                 [pltpu.VMEM((s + CONV_PAD, LANES), F32)], ("parallel",), (proj, proj, w_pad, b_dw), comm)


def _conv_bwd(du1, proj, w_pad, c_total, comm=None):
    s = proj.shape[0]
    nct = c_total // LANES
    t = _tile(s, 256)

    def body(d_ref, ga_ref, gb_ref, w_ref, dga_ref, dgb_ref, dw_ref, db_ref, u0_ref, dp_ref):
        sg = _sigmoid(gb_ref[...])
        u0_ref[pl.ds(0, CONV_PAD), :] = jnp.zeros((CONV_PAD, LANES), F32)
        u0_ref[pl.ds(CONV_PAD, s), :] = ga_ref[...] * sg
        dp_ref[pl.ds(0, s), :] = d_ref[...]
        dp_ref[pl.ds(s, CONV_PAD), :] = jnp.zeros((CONV_PAD, LANES), F32)
        dw_ref[...] = jnp.zeros_like(dw_ref)
        db_ref[...] = jnp.sum(d_ref[...], axis=0, keepdims=True)
        wv = w_ref[...]

        def chunk(r, carry):
            r0 = pl.multiple_of(r * t, t)
            win = u0_ref[pl.ds(r0, t + CONV_PAD), :]
            dwin = dp_ref[pl.ds(r0, t + CONV_PAD), :]
            dcur = dwin[:t]
            du0 = jnp.zeros((t, LANES), F32)
            for j in range(CONV_WIDTH):
                du0 = du0 + wv[j:j + 1, :] * _shift_rows(dwin, CONV_WIDTH - 1 - j, t)
                sh = _shift_rows(win, j + CONV_PAD - (CONV_WIDTH - 1), t)
                dw_ref[j:j + 1, :] += jnp.sum(dcur * sh, axis=0, keepdims=True)
            gav = ga_ref[pl.ds(r0, t), :]
            sgv = _sigmoid(gb_ref[pl.ds(r0, t), :])
            dga_ref[pl.ds(r0, t), :] = (du0 * sgv).astype(BF16)
            dgb_ref[pl.ds(r0, t), :] = (du0 * gav * sgv * (1.0 - sgv)).astype(BF16)
            return carry

        lax.fori_loop(0, s // t, chunk, 0)

    col = pl.BlockSpec((s, LANES), lambda c: (0, c))
    return _call("conv_bwd", body, (nct,),
                 [col, pl.BlockSpec((s, LANES), lambda c: (0, 3 * nct + c)),
                  pl.BlockSpec((s, LANES), lambda c: (0, 4 * nct + c)), pl.BlockSpec((CONV_PAD, LANES), lambda c: (0, c))],
                 [col, col, pl.BlockSpec((CONV_PAD, LANES), lambda c: (0, c)), pl.BlockSpec((1, LANES), lambda c: (0, c))],
                 [jax.ShapeDtypeStruct((s, c_total), BF16), jax.ShapeDtypeStruct((s, c_total), BF16),
                  jax.ShapeDtypeStruct((CONV_PAD, c_total), F32), jax.ShapeDtypeStruct((1, c_total), F32)],
                 [pltpu.VMEM((s + CONV_PAD, LANES), F32), pltpu.VMEM((s + CONV_PAD, LANES), F32)],
                 ("parallel",), (du1, proj, proj, w_pad), comm)


TQ_PREF = 256
NU = 4
NU_BWD = 2
TK = 256


def _split_dot(v, tri):
    hi = v.astype(BF16)
    lo = (v - hi.astype(F32)).astype(BF16)
    return (jnp.dot(hi, tri, preferred_element_type=F32) + jnp.dot(lo, tri, preferred_element_type=F32))


def _causal_mask(i, j, tq):
    tpos = i * tq + lax.broadcasted_iota(jnp.int32, (tq, TK), 0)
    spos = j * TK + lax.broadcasted_iota(jnp.int32, (tq, TK), 1)
    return spos < tpos


def _log_terms(z, mask):
    sp = jnp.log(1.0 + jnp.exp(-jnp.abs(z)))
    return jnp.minimum(z, 0.0) - sp, jnp.where(mask, -jnp.maximum(z, 0.0) - sp, 0.0)


def _tri(after):
    r = lax.broadcasted_iota(jnp.int32, (TK, TK), 0)
    c = lax.broadcasted_iota(jnp.int32, (TK, TK), 1)
    return (r > c).astype(BF16) if after else (r < c).astype(BF16)


def _attn_fwd(proj, n_heads, comm=None):
    s = proj.shape[0]
    tq = _tile(s, TQ_PREF)
    scale = 1.0 / math.sqrt(HEAD_DIM)
    ratio = tq // TK

    def body(q_ref, k_ref, v_ref, o_ref, ot_ref, acc_ref, *clms):
        i = pl.program_id(1)
        heads = [slice(u * HEAD_DIM, (u + 1) * HEAD_DIM) for u in range(NU)]
        qs = [q_ref[:, hs].astype(BF16) for hs in heads]
        tri_after = _tri(True)
        acc_ref[...] = jnp.zeros_like(acc_ref)
        for cr in clms:
            cr[...] = jnp.zeros_like(cr)
        nkb = (i + 1) * ratio

        def step(jj, carry):
            j = nkb - 1 - jj
            rows = pl.ds(pl.multiple_of(j * TK, TK), TK)
            mask = _causal_mask(i, j, tq)
            zs = [lax.dot_general(qs[u], k_ref[rows, hs].astype(BF16), (((1,), (1,)), ((), ())),
                                  preferred_element_type=F32) * scale for u, hs in enumerate(heads)]
            lls = [_log_terms(z, mask) for z in zs]
            sufs = [clms[u][...] + _split_dot(lls[u][1], tri_after) for u in range(NU)]
            for u, hs in enumerate(heads):
                a = jnp.where(mask, jnp.exp(lls[u][0] + sufs[u]), 0.0)
                acc_ref[:, hs] += jnp.dot(a.astype(BF16), v_ref[rows, hs].astype(BF16), preferred_element_type=F32)
                clms[u][...] += jnp.sum(lls[u][1], axis=1, keepdims=True)
            return carry

        lax.fori_loop(0, nkb, step, 0)
        o = acc_ref[...]
        o_ref[...] = o
        ot_ref[...] = o.T.astype(BF16)

    w = NU * HEAD_DIM
    ng = n_heads // NU
    return _call("attn_fwd", body, (ng, s // tq),
                 [pl.BlockSpec((tq, w), lambda h, i: (i, h)),
                  pl.BlockSpec((s, w), lambda h, i: (0, ng + h)),
                  pl.BlockSpec((s, w), lambda h, i: (0, 2 * ng + h))],
                 [pl.BlockSpec((tq, w), lambda h, i: (i, h)), pl.BlockSpec((w, tq), lambda h, i: (h, i))],
                 [jax.ShapeDtypeStruct((s, n_heads * HEAD_DIM), F32), jax.ShapeDtypeStruct((n_heads * HEAD_DIM, s), BF16)],
                 [pltpu.VMEM((tq, w), F32), *[pltpu.VMEM((tq, 1), F32)] * NU],
                 ("parallel", "arbitrary"), (proj, proj, proj), comm)


def _attn_bwd(proj, do_sb, n_heads, comm=None):
    s = proj.shape[0]
    tq = _tile(s, TQ_PREF)
    scale = 1.0 / math.sqrt(HEAD_DIM)
    ratio = tq // TK
    n_kb = s // TK
    n_qb = s // tq
    nu = NU_BWD
    heads = [slice(u * HEAD_DIM, (u + 1) * HEAD_DIM) for u in range(nu)]
    nt_dims = (((1,), (1,)), ((), ()))

    def body(q_ref, k_ref, v_ref, do_ref, dq_ref, dk_ref, dv_ref, dka_ref, dva_ref, dl_ref, be_ref, dqa_ref, *c_refs):
        i = pl.program_id(1)

        @pl.when(i == 0)
        def _():
            dka_ref[...] = jnp.zeros_like(dka_ref)
            dva_ref[...] = jnp.zeros_like(dva_ref)

        qs = [q_ref[:, hs].astype(BF16) for hs in heads]
        dobs = [do_ref[:, hs].astype(BF16) for hs in heads]
        tri_after = _tri(True)
        tri_before = _tri(False)
        nkb = (i + 1) * ratio

        for cr in c_refs:
            cr[...] = jnp.zeros_like(cr)

        def sweep_a(jj, carry):
            j = nkb - 1 - jj
            rows = pl.ds(pl.multiple_of(j * TK, TK), TK)
            mask = _causal_mask(i, j, tq)
            zs = [lax.dot_general(qs[u], k_ref[rows, hs].astype(BF16), nt_dims, preferred_element_type=F32) * scale
                  for u, hs in enumerate(heads)]
            das = [lax.dot_general(dobs[u], v_ref[rows, hs].astype(BF16), nt_dims, preferred_element_type=F32)
                   for u, hs in enumerate(heads)]
            lls = [_log_terms(z, mask) for z in zs]
            sufs = [c_refs[u][...] + _split_dot(lls[u][1], tri_after) for u in range(nu)]
            for u, hs in enumerate(heads):
                a = jnp.where(mask, jnp.exp(lls[u][0] + sufs[u]), 0.0)
                dl_ref[u, j] = das[u] * a
                be_ref[u, j] = jnp.exp(lls[u][0])
                dva_ref[rows, hs] += jnp.dot(a.T.astype(BF16), dobs[u], preferred_element_type=F32)
                c_refs[u][...] += jnp.sum(lls[u][1], axis=1, keepdims=True)
            return carry

        lax.fori_loop(0, nkb, sweep_a, 0)

        for cr in c_refs:
            cr[...] = jnp.zeros_like(cr)
        dqa_ref[...] = jnp.zeros_like(dqa_ref)

        def sweep_b(j, carry):
            rows = pl.ds(pl.multiple_of(j * TK, TK), TK)
            mask = _causal_mask(i, j, tq)
            dls = [dl_ref[u, j] for u in range(nu)]
            ps = [c_refs[u][...] + _split_dot(dls[u], tri_before) for u in range(nu)]
            for u, hs in enumerate(heads):
                beta = be_ref[u, j]
                dz = jnp.where(mask, (dls[u] * (1.0 - beta) - beta * ps[u]) * scale, 0.0)
                dqa_ref[:, hs] += jnp.dot(dz.astype(BF16), k_ref[rows, hs].astype(BF16), preferred_element_type=F32)
                dka_ref[rows, hs] += jnp.dot(dz.T.astype(BF16), qs[u], preferred_element_type=F32)
                c_refs[u][...] += jnp.sum(dls[u], axis=1, keepdims=True)
            return carry

        lax.fori_loop(0, nkb, sweep_b, 0)
        dq_ref[...] = dqa_ref[...].astype(BF16)

        @pl.when(i == n_qb - 1)
        def _():
            dk_ref[...] = dka_ref[...].astype(BF16)
            dv_ref[...] = dva_ref[...].astype(BF16)

    w = nu * HEAD_DIM
    ng = n_heads // nu
    qblk = pl.BlockSpec((tq, w), lambda h, i: (i, h))
    full = pl.BlockSpec((s, w), lambda h, i: (0, h))
    sds = jax.ShapeDtypeStruct((s, n_heads * HEAD_DIM), BF16)
    return _call("attn_bwd", body, (ng, n_qb),
                 [qblk, pl.BlockSpec((s, w), lambda h, i: (0, ng + h)), pl.BlockSpec((s, w), lambda h, i: (0, 2 * ng + h)),
                  qblk],
                 [qblk, full, full], [sds, sds, sds],
                 [pltpu.VMEM((s, w), F32), pltpu.VMEM((s, w), F32),
                  pltpu.VMEM((nu, n_kb, tq, TK), F32), pltpu.VMEM((nu, n_kb, tq, TK), F32),
                  pltpu.VMEM((tq, w), F32), *[pltpu.VMEM((tq, 1), F32)] * nu],
                 ("parallel", "arbitrary"), (proj, proj, proj, do_sb), comm)


def _adamw(name, w, m, v, parts, part_specs, tr, comm=None):
    r, c = w.shape
    n_parts = len(parts)

    def body(*refs):
        w_ref, m_ref, v_ref = refs[:3]
        p_refs = refs[3:3 + n_parts]
        g_ref, d_ref, nm_ref, nv_ref = refs[3 + n_parts:]
        g = p_refs[0][...].astype(F32)
        for p in p_refs[1:]:
            g = g + p[...].astype(F32)
        nm = ADAM_B1 * m_ref[...] + (1.0 - ADAM_B1) * g
        nv = ADAM_B2 * v_ref[...] + (1.0 - ADAM_B2) * jnp.square(g)
        m_hat = nm / (1.0 - ADAM_B1 ** ADAM_STEP)
        v_hat = nv / (1.0 - ADAM_B2 ** ADAM_STEP)
        g_ref[...] = g
        d_ref[...] = -ADAM_LR * (m_hat / (jnp.sqrt(v_hat) + ADAM_EPS) + ADAM_WD * w_ref[...])
        nm_ref[...] = nm
        nv_ref[...] = nv

    blk = pl.BlockSpec((tr, c), lambda i: (i, 0))
    sds = jax.ShapeDtypeStruct((r, c), F32)
    return _call(name, body, (r // tr,), [blk, blk, blk, *part_specs], [blk] * 4, [sds] * 4, [], ("parallel",),
                 (w, m, v, *parts), comm)


def _adamw_big(name, w, m, v, recv, comm=None):
    r, c = w.shape
    tr = _tile(r, 128)
    order = (3, 0, 1, 2)
    specs = [pl.BlockSpec((None, tr, c), functools.partial(lambda i, slot: (slot, i, 0), slot=sl)) for sl in order]
    return _adamw(name, w, m, v, [recv] * 4, specs, tr, comm)


def _adamw_small(name, w, m, v, g):
    r, c = w.shape
    return _adamw(name, w, m, v, [g], [pl.BlockSpec((r, c), lambda i: (0, 0))], r)


def kernel(x, g_pre_mix, w_in, b_in, w_dw, b_dw, g_conv_ln, b_conv_ln, w_sb_out, w_conv_out, w_o, g_post_mix, g_pre_mlp, w_up, w_down, g_post_mlp, loss_target, m_g_pre_mix, m_w_in, m_b_in, m_w_dw, m_b_dw, m_g_conv_ln, m_b_conv_ln, m_w_sb_out, m_w_conv_out, m_w_o, m_g_post_mix, m_g_pre_mlp, m_w_up, m_w_down, m_g_post_mlp, v_g_pre_mix, v_w_in, v_b_in, v_w_dw, v_b_dw, v_g_conv_ln, v_b_conv_ln, v_w_sb_out, v_w_conv_out, v_w_o, v_g_post_mix, v_g_pre_mlp, v_w_up, v_w_down, v_g_post_mlp):
    xs, tgt = x[0], loss_target[0]
    s, d = xs.shape
    d_half = d // 2
    n_heads = d_half // HEAD_DIM
    d_ff = NDEV * w_up.shape[2]
    core = lax.axis_index("c").astype(jnp.int32).reshape(1)
    dev = 4 * lax.axis_index("x") + 2 * lax.axis_index("y") + lax.axis_index("c")

    w_dw_pad = jnp.pad(w_dw[0], ((0, CONV_PAD - CONV_WIDTH), (0, 0)))
    sh_in, sh_sb, sh_cv, sh_o, sh_up, sh_down = [w[0].astype(BF16) for w in (w_in, w_sb_out, w_conv_out, w_o, w_up, w_down)]

    ag_down = _Chunked(_ag_comm, [sh_down], sh_down.shape[0], 8)
    h, h_t, wg_in = _prenorm(xs, g_pre_mix, _ag_comm([sh_in]))
    ag_up = _Chunked(_ag_comm, [sh_up], sh_up.shape[0], 8)
    proj, wg_sb, wg_cv, wg_o, wg_dw, *part = _mm_cols(
        "proj", h, wg_in, bias=b_in, comm=_join(_ag_comm([sh_sb, sh_cv, sh_o, w_dw_pad]), ag_up.take(1)))
    ag_up.done(part)
    o_sb, o_sb_t, wg_up = _attn_fwd(proj, n_heads, ag_up.take(7))
    wf_dw = wg_dw.transpose(1, 0, 2).reshape(CONV_PAD, d_half)
    wf_o = wg_o.reshape(d, d)
    u1, *part = _conv_fwd(proj, wf_dw, b_dw, d_half, ag_down.take())
    ag_down.done(part)
    u3, u3_t = _ln_silu(u1, g_conv_ln, b_conv_ln)
    o_sbp = _mm_cols("sb_out", o_sb, wg_sb)[0]
    o_cv = _mm_cols("conv_out", u3, wg_cv)[0]
    merged, merged_t, *part = _merge(proj, o_sbp, o_cv, d, ag_down.take())
    ag_down.done(part)
    y, *part = _mm_plain("w_o", merged, wf_o, False, F32, comm=ag_down.take())
    ag_down.done(part)
    x1, h2, h2_t, *part = _postnorm_mix(xs, y, g_post_mix, g_pre_mlp, ag_down.take())
    ag_down.done(part)

    tm_up = _tile(s, 1024)
    ns_up = wg_up.shape[2]
    tk_up = _tile(d, 2048)

    def up_epilogue(acc):
        f = jnp.square(jnp.maximum(acc, 0.0))
        return acc, f, f.T

    a_act, f, f_t, wg_down = _matmul(
        "w_up", h2, wg_up,
        [jax.ShapeDtypeStruct((s, d_ff), BF16), jax.ShapeDtypeStruct((s, d_ff), BF16), jax.ShapeDtypeStruct((d_ff, s), BF16)],
        (s // tm_up, NDEV, d // tk_up),
        pl.BlockSpec((tm_up, tk_up), lambda i, dd, kk: (i, kk)),
        pl.BlockSpec((None, tk_up, ns_up), lambda i, dd, kk: (dd, kk, 0)),
        [pl.BlockSpec((tm_up, ns_up), lambda i, dd, kk: (i, dd)), pl.BlockSpec((tm_up, ns_up), lambda i, dd, kk: (i, dd)),
         pl.BlockSpec((ns_up, tm_up), lambda i, dd, kk: (dd, i))],
        1, False, (tm_up, ns_up), epilogue=up_epilogue, comm=ag_down.take(4))
    wf_down = wg_down.reshape(d_ff, d)
    f2 = _mm_plain("w_down", f, wf_down, False, F32)[0]
    dx2, df2, dg_post_mlp, loss_part = _loss_head(x1, f2, tgt, g_post_mlp)

    tm_b, tn_b = _tile(s, 1024), _tile(d_ff, 1024)
    da = _mm_plain("w_down_bwd", df2, wf_down, True, BF16,
                   extra=(a_act,), extra_specs=(pl.BlockSpec((tm_b, tn_b), lambda i, j, kk: (i, j)),),
                   epilogue=lambda acc, av: (acc * (2.0 * jnp.maximum(av.astype(F32), 0.0)),),
                   outs=[jax.ShapeDtypeStruct((s, d_ff), BF16)],
                   out_specs=[pl.BlockSpec((tm_b, tn_b), lambda i, j, kk: (i, j))])[0]
    gw_down = _mm_plain("w_down_grad", f_t, df2, False, BF16)[0]
    big_down = gw_down.reshape(4, 2, d_ff // NDEV, d)
    gw_up, sib_down = _mm_dw_cols("w_up_grad", h2_t, da, comm=_sibling_comm([big_down]))
    big_up = gw_up.reshape(4, 2, d, d_ff // NDEV)
    dh2, sib_up = _mm_cols_t("w_up_bwd", da, wg_up, comm=_sibling_comm([big_up]))
    rs_down = _Chunked(_chips_comm, [_pair_sum(big_down, sib_down, core)], d_ff // NDEV, 8)
    rs_up = _Chunked(_chips_comm, [_pair_sum(big_up, sib_up, core)], d, 8)
    dx1, dy, dg_pre_mlp, dg_post_mix, *part = _midnorm_bwd(dx2, dh2, x1, y, g_post_mix, g_pre_mlp, rs_down.take())
    rs_down.done(part)
    gw_o, *part = _mm_plain("w_o_grad", merged_t, dy, False, BF16, comm=rs_down.take())
    rs_down.done(part)
    dmerged, *part = _mm_plain("w_o_bwd", dy, wf_o, True, F32, comm=rs_down.take())
    rs_down.done(part)
    do_sbp, do_cv, dgate_sb, dgate_cv, *part = _merge_bwd(dmerged, proj, o_sbp, o_cv, d, rs_down.take())
    rs_down.done(part)
    gw_cv = _mm_dw_cols("conv_out_grad", u3_t, do_cv)[0]
    gw_sb = _mm_dw_cols("sb_out_grad", o_sb_t, do_sbp)[0]
    du3 = _mm_cols_t("conv_out_bwd", do_cv, wg_cv)[0]
    do_sb = _mm_cols_t("sb_out_bwd", do_sbp, wg_sb)[0]
    du1, dg_ln, db_ln = _ln_silu_bwd(du3, u1, g_conv_ln, b_conv_ln)
    big_mid = [gw_sb.reshape(4, 2, d_half, d // NDEV), gw_cv.reshape(4, 2, d_half, d // NDEV),
               gw_o.reshape(4, 2, d // NDEV, d)]
    dglu_a, dglu_b, dw_dw, db_dw, part, *sib_mid = _conv_bwd(du1, proj, wf_dw, d_half,
                                                             _join(rs_up.take(2), _sibling_comm(big_mid)))
    rs_up.done([part])
    sums_mid = [_pair_sum(g, r, core) for g, r in zip(big_mid, sib_mid)]
    dq, dk, dv, r_up, r_sb, r_cv, r_o = _attn_bwd(proj, do_sb, n_heads, _join(rs_up.take(6), _chips_comm(sums_mid)))
    dproj = jnp.concatenate([dq, dk, dv, dglu_a, dglu_b, dgate_sb, dgate_cv], axis=1)
    gw_in, r_down = _mm_dw_cols("w_in_grad", h_t, dproj, comm=rs_down.take(4))
    big_in = gw_in.reshape(4, 2, d, gw_in.shape[2])
    db_in, sib_in = _colsum(dproj, _sibling_comm([big_in]))
    dh, r_in = _mm_cols_t("w_in_bwd", dproj, wg_in, comm=_chips_comm([_pair_sum(big_in, sib_in, core)]))
    grad_x, dg_pre_mix = _prenorm_bwd(dx1, dh, xs, g_pre_mix)

    small = [dg_pre_mix, db_in, dw_dw.reshape(1, -1), db_dw, dg_ln, db_ln, dg_post_mix, dg_pre_mlp, dg_post_mlp]
    sizes = [a.shape[1] for a in small]
    packed = jnp.concatenate(small, axis=1).reshape(-1, LANES)
    total = _sum_small(_all_gather_small(packed)).reshape(1, -1)
    offs = [0]
    for n in sizes:
        offs.append(offs[-1] + n)
    (g_g_pre_mix, g_b_in, g_w_dw_flat, g_b_dw, g_g_conv_ln, g_b_conv_ln, g_g_post_mix, g_g_pre_mlp,
     g_g_post_mlp) = [total[:, offs[k]:offs[k + 1]] for k in range(len(sizes))]
    ch = w_dw.shape[2]
    g_w_dw = lax.dynamic_slice_in_dim(g_w_dw_flat.reshape(CONV_PAD, d_half), dev * ch, ch, axis=1)[:CONV_WIDTH]

    loss = lax.psum(loss_part[0, 0], ("x", "y", "c"))

    res = {}
    res["w_up"] = _adamw_big("adamw_w_up", w_up[0], m_w_up[0], v_w_up[0], r_up)
    res["w_down"] = _adamw_big("adamw_w_down", w_down[0], m_w_down[0], v_w_down[0], r_down)
    res["g_pre_mix"] = _adamw_small("adamw_g_pre_mix", g_pre_mix, m_g_pre_mix, v_g_pre_mix, g_g_pre_mix)
    res["w_in"] = _adamw_big("adamw_w_in", w_in[0], m_w_in[0], v_w_in[0], r_in)
    res["b_in"] = _adamw_small("adamw_b_in", b_in, m_b_in, v_b_in, g_b_in)
    res["w_dw"] = _adamw_small("adamw_w_dw", w_dw[0], m_w_dw[0], v_w_dw[0], g_w_dw)
    res["b_dw"] = _adamw_small("adamw_b_dw", b_dw, m_b_dw, v_b_dw, g_b_dw)
    res["g_conv_ln"] = _adamw_small("adamw_g_conv_ln", g_conv_ln, m_g_conv_ln, v_g_conv_ln, g_g_conv_ln)
    res["b_conv_ln"] = _adamw_small("adamw_b_conv_ln", b_conv_ln, m_b_conv_ln, v_b_conv_ln, g_b_conv_ln)
    res["w_sb_out"] = _adamw_big("adamw_w_sb_out", w_sb_out[0], m_w_sb_out[0], v_w_sb_out[0], r_sb)
    res["w_conv_out"] = _adamw_big("adamw_w_conv_out", w_conv_out[0], m_w_conv_out[0], v_w_conv_out[0], r_cv)
    res["w_o"] = _adamw_big("adamw_w_o", w_o[0], m_w_o[0], v_w_o[0], r_o)
    res["g_post_mix"] = _adamw_small("adamw_g_post_mix", g_post_mix, m_g_post_mix, v_g_post_mix, g_g_post_mix)
    res["g_pre_mlp"] = _adamw_small("adamw_g_pre_mlp", g_pre_mlp, m_g_pre_mlp, v_g_pre_mlp, g_g_pre_mlp)
    res["g_post_mlp"] = _adamw_small("adamw_g_post_mlp", g_post_mlp, m_g_post_mlp, v_g_post_mlp, g_g_post_mlp)

    names = ["g_pre_mix", "w_in", "b_in", "w_dw", "b_dw", "g_conv_ln", "b_conv_ln", "w_sb_out", "w_conv_out", "w_o",
             "g_post_mix", "g_pre_mlp", "w_up", "w_down", "g_post_mlp"]
    three_d = {"w_in", "w_dw", "w_sb_out", "w_conv_out", "w_o", "w_up", "w_down"}

    def shaped(nm, arr):
        return arr[None] if nm in three_d else arr

    out = [loss, grad_x[None]]
    for k in range(4):
        out += [shaped(nm, res[nm][k]) for nm in names]
    return tuple(out)
```

```python
import functools
import math

import jax
import jax.numpy as jnp
from jax import lax
from jax.experimental import pallas as pl
from jax.experimental.pallas import tpu as pltpu

F32 = jnp.float32
BF16 = jnp.bfloat16
NDEV = 8
LANES = 128
EPS = 1e-6
CONV_WIDTH = 31
CONV_PAD = 32
HEAD_DIM = 128
ADAM_LR = 0.001
ADAM_B1 = 0.9
ADAM_B2 = 0.999
ADAM_EPS = 1e-08
ADAM_WD = 0.01
ADAM_STEP = 10
VMEM_LIMIT = 56 * 1024 * 1024
MESH = pl.DeviceIdType.MESH
ANY = pl.BlockSpec(memory_space=pl.ANY)


def _tile(n, pref):
    t = min(n, pref)
    assert n % t == 0, (n, t)
    return t


def _sigmoid(v):
    return 1.0 / (1.0 + jnp.exp(-v))


def _position():
    return lax.axis_index("x"), lax.axis_index("y"), lax.axis_index("c")


class _Comm:
    def __init__(self, ins, outs, scratch, start, finish, aliases=None, mid=None):
        self.ins, self.outs, self.scratch = list(ins), list(outs), list(scratch)
        self.start, self.finish, self.aliases = start, finish, dict(aliases or {})
        self.mid = mid


_NO_COMM = _Comm([], [], [], None, None)


def _call(name, body, grid, in_specs, out_specs, out_shape, scratch_shapes, sem, args, comm=None):
    comm = comm or _NO_COMM
    n_in, n_out, n_scr = len(in_specs), len(out_specs), len(scratch_shapes)
    n_cin, n_cout = len(comm.ins), len(comm.outs)

    n_steps = math.prod(grid)
    mid_step = n_steps - max(1, n_steps // 8) if (comm.mid is not None and n_steps >= 4) else None

    def edge(c_ins, c_outs, c_scr, at_step, actions):
        linear = 0
        for ax, g in enumerate(grid):
            linear = linear * g + pl.program_id(ax)

        @pl.when(linear == at_step)
        def _():
            for act in actions:
                act(c_ins, c_outs, c_scr)

    def wrapped(*refs):
        ins, c_ins = refs[:n_in], refs[n_in:n_in + n_cin]
        pos = n_in + n_cin
        outs, c_outs = refs[pos:pos + n_out], refs[pos + n_out:pos + n_out + n_cout]
        pos += n_out + n_cout
        scr, c_scr = refs[pos:pos + n_scr], refs[pos + n_scr:]
        if n_cin:
            edge(c_ins, c_outs, c_scr, 0, [comm.start])
            if mid_step is not None:
                edge(c_ins, c_outs, c_scr, mid_step, [comm.mid])
        body(*ins, *outs, *scr)
        if n_cin:
            late = [comm.finish] if (mid_step is not None or comm.mid is None) else [comm.mid, comm.finish]
            edge(c_ins, c_outs, c_scr, n_steps - 1, late)

    if n_cin:
        sem = ("arbitrary",) * len(grid)
    return pl.pallas_call(
        wrapped, name=name, grid=grid,
        in_specs=[*in_specs, *[ANY] * n_cin], out_specs=[*out_specs, *[ANY] * n_cout],
        out_shape=[*out_shape, *comm.outs], scratch_shapes=[*scratch_shapes, *comm.scratch],
        input_output_aliases={n_in + ci: n_out + co for ci, co in comm.aliases.items()},
        compiler_params=pltpu.CompilerParams(dimension_semantics=sem, vmem_limit_bytes=VMEM_LIMIT),
    )(*args, *comm.ins)


def _run_comm(name, comm):
    n_in, n_out = len(comm.ins), len(comm.outs)

    def body(*refs):
        ins, outs, scr = refs[:n_in], refs[n_in:n_in + n_out], refs[n_in + n_out:]
        comm.start(ins, outs, scr)
        if comm.mid is not None:
            comm.mid(ins, outs, scr)
        comm.finish(ins, outs, scr)

    return pl.pallas_call(body, name=name, in_specs=[ANY] * n_in, out_specs=[ANY] * n_out, out_shape=comm.outs,
                          scratch_shapes=comm.scratch,
                          input_output_aliases=comm.aliases)(*comm.ins)


def _rows_of(ref, rows):
    return ref if rows is None else ref.at[pl.ds(rows[0], rows[1])]


def _ag_comm(shards, rows=None, into=None):
    n = len(shards)

    def parts(ins, outs, scr):
        send_sems, recv_sems, local_sems = scr
        x, y, c = _position()
        chips = [(1 - x, y), (x, 1 - y), (1 - x, 1 - y)]

        def copy(a, k, block, to, own=False):
            px, py, pc = block
            dst = _rows_of(outs[a].at[4 * px + 2 * py + pc], rows)
            return pltpu.make_async_remote_copy(src_ref=_rows_of(ins[a], rows) if own else dst, dst_ref=dst,
                                                send_sem=send_sems.at[a, k], recv_sem=recv_sems.at[a, k],
                                                device_id=to, device_id_type=MESH)

        mine = [pltpu.make_async_copy(_rows_of(ins[a], rows), _rows_of(outs[a].at[4 * x + 2 * y + c], rows),
                                      local_sems.at[a]) for a in range(n)]
        first = []
        for a in range(n):
            first.append(copy(a, 0, (x, y, c), (x, y, 1 - c), own=True))
            first += [copy(a, 1 + j, (x, y, c), (*chip, c), own=True) for j, chip in enumerate(chips)]
        return copy, mine, first, chips, (x, y, c), (x, y, 1 - c)

    def start(ins, outs, scr):
        _, mine, first, _, _, _ = parts(ins, outs, scr)
        for cp in mine + first:
            cp.start()

    def mid(ins, outs, scr):
        copy, _, _, chips, me, sibling = parts(ins, outs, scr)
        for a in range(n):
            for j, chip in enumerate(chips):
                copy(a, 1 + j, (*chip, me[2]), me).wait_recv()
                copy(a, 4 + j, (*chip, me[2]), sibling).start()

    def finish(ins, outs, scr):
        copy, mine, first, chips, me, sibling = parts(ins, outs, scr)
        c = me[2]
        passed = [copy(a, 4 + j, (*chip, c), sibling) for a in range(n) for j, chip in enumerate(chips)]
        for a in range(n):
            copy(a, 0, sibling, me).wait_recv()
            for j, chip in enumerate(chips):
                copy(a, 4 + j, (*chip, 1 - c), me).wait_recv()
        for cp in first + passed:
            cp.wait_send()
        for cp in mine:
            cp.wait()

    return _Comm([*shards, *(into or [])], [jax.ShapeDtypeStruct((NDEV, *sh.shape), sh.dtype) for sh in shards],
                 [pltpu.SemaphoreType.DMA((n, 7)), pltpu.SemaphoreType.DMA((n, 7)), pltpu.SemaphoreType.DMA((n,))],
                 start, finish, {n + a: a for a in range(n)} if into else None, mid)


def _sibling_comm(grads):
    n = len(grads)

    def copies(ins, outs, scr):
        send_sems, recv_sems = scr
        x, y, c = _position()
        return [pltpu.make_async_remote_copy(src_ref=ins[a].at[k, 1 - c], dst_ref=outs[a].at[k],
                                             send_sem=send_sems.at[a, k], recv_sem=recv_sems.at[a, k],
                                             device_id=(x, y, 1 - c), device_id_type=MESH)
                for a in range(n) for k in range(4)]

    def start(ins, outs, scr):
        for cp in copies(ins, outs, scr):
            cp.start()

    def finish(ins, outs, scr):
        for cp in copies(ins, outs, scr):
            cp.wait()

    return _Comm(grads, [jax.ShapeDtypeStruct((4, *g.shape[2:]), g.dtype) for g in grads],
                 [pltpu.SemaphoreType.DMA((n, 4)), pltpu.SemaphoreType.DMA((n, 4))], start, finish)


def _chips_comm(sums, rows=None, into=None):
    n = len(sums)

    def copies(ins, outs, scr):
        send_sems, recv_sems, local_sems = scr
        x, y, c = _position()
        chips = [(1 - x, y), (x, 1 - y), (1 - x, 1 - y)]
        own = [pltpu.make_async_copy(_rows_of(ins[a].at[2 * x + y], rows), _rows_of(outs[a].at[3], rows), local_sems.at[a])
               for a in range(n)]
        remote = [pltpu.make_async_remote_copy(src_ref=_rows_of(ins[a].at[2 * px + py], rows),
                                               dst_ref=_rows_of(outs[a].at[j], rows),
                                               send_sem=send_sems.at[a, j], recv_sem=recv_sems.at[a, j],
                                               device_id=(px, py, c), device_id_type=MESH)
                  for a in range(n) for j, (px, py) in enumerate(chips)]
        return own + remote

    def start(ins, outs, scr):
        for cp in copies(ins, outs, scr):
            cp.start()

    def finish(ins, outs, scr):
        for cp in copies(ins, outs, scr):
            cp.wait()

    return _Comm([*sums, *(into or [])], [jax.ShapeDtypeStruct(sm.shape, sm.dtype) for sm in sums],
                 [pltpu.SemaphoreType.DMA((n, 3)), pltpu.SemaphoreType.DMA((n, 3)), pltpu.SemaphoreType.DMA((n,))],
                 start, finish, {n + a: a for a in range(n)} if into else None)


def _join(c1, c2):
    n_in, n_out, n_scr = len(c1.ins), len(c1.outs), len(c1.scratch)
    aliases = dict(c1.aliases)
    aliases.update({n_in + ci: n_out + co for ci, co in c2.aliases.items()})

    def start(ins, outs, scr):
        c1.start(ins[:n_in], outs[:n_out], scr[:n_scr])
        c2.start(ins[n_in:], outs[n_out:], scr[n_scr:])

    def mid(ins, outs, scr):
        if c1.mid is not None:
            c1.mid(ins[:n_in], outs[:n_out], scr[:n_scr])
        if c2.mid is not None:
            c2.mid(ins[n_in:], outs[n_out:], scr[n_scr:])

    def finish(ins, outs, scr):
        c1.finish(ins[:n_in], outs[:n_out], scr[:n_scr])
        c2.finish(ins[n_in:], outs[n_out:], scr[n_scr:])

    return _Comm(c1.ins + c2.ins, c1.outs + c2.outs, c1.scratch + c2.scratch, start, finish, aliases,
                 mid if (c1.mid is not None or c2.mid is not None) else None)


class _Chunked:
    def __init__(self, make, arrays, n_rows, n_chunks):
        self.make, self.arrays, self.into = make, arrays, None
        step = n_rows // n_chunks
        assert step * n_chunks == n_rows
        self.todo = [(k * step, step) for k in range(n_chunks)]

    def take(self, count=1):
        r0, nr = self.todo[0][0], sum(t[1] for t in self.todo[:count])
        self.todo = self.todo[count:]
        return self.make(self.arrays, (r0, nr), self.into)

    def done(self, outs):
        self.into = list(outs)
        return self.into


def _all_gather_small(part):
    def body(in_ref, out_ref, send_sems, recv_sems, local_sem):
        x, y, c = _position()
        me = 4 * x + 2 * y + c
        mine = pltpu.make_async_copy(in_ref, out_ref.at[me], local_sem)
        mine.start()
        flips = [(fx, fy, fc) for fx in (0, 1) for fy in (0, 1) for fc in (0, 1)][1:]
        copies = []
        for k, (fx, fy, fc) in enumerate(flips):
            cp = pltpu.make_async_remote_copy(src_ref=in_ref, dst_ref=out_ref.at[me], send_sem=send_sems.at[k],
                                              recv_sem=recv_sems.at[k],
                                              device_id=(x ^ fx, y ^ fy, c ^ fc), device_id_type=MESH)
            cp.start()
            copies.append(cp)
        for k, (fx, fy, fc) in enumerate(flips):
            peer = 4 * (x ^ fx) + 2 * (y ^ fy) + (c ^ fc)
            pltpu.make_async_remote_copy(src_ref=in_ref, dst_ref=out_ref.at[peer], send_sem=send_sems.at[k],
                                         recv_sem=recv_sems.at[k], device_id=(x, y, c), device_id_type=MESH).wait_recv()
        for cp in copies:
            cp.wait_send()
        mine.wait()

    return pl.pallas_call(
        body, name="all_gather_small", in_specs=[ANY], out_specs=ANY,
        out_shape=jax.ShapeDtypeStruct((NDEV, *part.shape), part.dtype),
        scratch_shapes=[pltpu.SemaphoreType.DMA((7,)), pltpu.SemaphoreType.DMA((7,)), pltpu.SemaphoreType.DMA],
    )(part)


def _pair_sum(g, recv, core):
    _, _, r, c = g.shape
    tr = _tile(r, 512)

    def body(core_ref, g_ref, r_ref, o_ref):
        o_ref[...] = (g_ref[...].astype(F32) + r_ref[...].astype(F32)).astype(o_ref.dtype)

    return pl.pallas_call(
        body, name="pair_sum",
        grid_spec=pltpu.PrefetchScalarGridSpec(
            num_scalar_prefetch=1, grid=(4, r // tr),
            in_specs=[pl.BlockSpec((None, None, tr, c), lambda k, i, core_ref: (k, core_ref[0], i, 0)),
                      pl.BlockSpec((None, tr, c), lambda k, i, core_ref: (k, i, 0))],
            out_specs=pl.BlockSpec((None, tr, c), lambda k, i, core_ref: (k, i, 0))),
        out_shape=jax.ShapeDtypeStruct((4, r, c), g.dtype),
        compiler_params=pltpu.CompilerParams(dimension_semantics=("parallel", "parallel"), vmem_limit_bytes=VMEM_LIMIT),
    )(core, g, recv)


def _sum_small(gathered):
    _, r, l = gathered.shape

    def body(g_ref, o_ref):
        acc = g_ref[0]
        for d in range(1, NDEV):
            acc = acc + g_ref[d]
        o_ref[...] = acc

    return pl.pallas_call(
        body, name="sum_small", in_specs=[pl.BlockSpec((NDEV, r, l), lambda: (0, 0, 0))],
        out_specs=pl.BlockSpec((r, l), lambda: (0, 0)), out_shape=jax.ShapeDtypeStruct((r, l), F32),
    )(gathered)


def _matmul(name, a, b, outs, grid, a_spec, b_spec, out_specs, n_red, nt, acc_shape,
            extra=(), extra_specs=(), epilogue=None, comm=None):
    n_extra, n_out = len(extra), len(outs)
    red_axes = tuple(range(len(grid) - n_red, len(grid)))
    red_sizes = tuple(grid[ax] for ax in red_axes)
    single = all(sz == 1 for sz in red_sizes)
    dims = (((1,), (1,)), ((), ())) if nt else (((1,), (0,)), ((), ()))

    def body(*refs):
        a_ref, b_ref = refs[0], refs[1]
        ex_refs = refs[2:2 + n_extra]
        o_refs = refs[2 + n_extra:2 + n_extra + n_out]
        acc_ref = refs[-1]

        def write(acc):
            vals = (acc,) if epilogue is None else epilogue(acc, *[r[...] for r in ex_refs])
            for o_ref, val in zip(o_refs, vals):
                o_ref[...] = val.astype(o_ref.dtype)

        if len(b_ref.shape) == 3:
            w = b_ref.shape[2]
            part = sum(lax.dot_general(a_ref[:, p * w:(p + 1) * w].astype(BF16), b_ref[p].astype(BF16), dims,
                                       preferred_element_type=F32) for p in range(b_ref.shape[0]))
        else:
            part = lax.dot_general(a_ref[...].astype(BF16), b_ref[...].astype(BF16), dims, preferred_element_type=F32)
        if single:
            write(part)
        else:
            ks = [pl.program_id(ax) for ax in red_axes]
            first = functools.reduce(jnp.logical_and, [k == 0 for k in ks])
            last = functools.reduce(jnp.logical_and, [k == sz - 1 for k, sz in zip(ks, red_sizes)])

            @pl.when(first)
            def _():
                acc_ref[...] = part

            @pl.when(jnp.logical_not(first))
            def _():
                acc_ref[...] += part

            @pl.when(last)
            def _():
                write(acc_ref[...])

    sem = ("parallel",) * (len(grid) - n_red) + ("arbitrary",) * n_red
    return _call(name, body, grid, [a_spec, b_spec, *extra_specs], list(out_specs), list(outs),
                 [pltpu.VMEM((8, LANES) if single else acc_shape, F32)], sem, (a, b, *extra), comm)


def _mm_cols(name, a, wg, bias=None, out_dtype=F32, comm=None, bf16_copy=False):
    m, k = a.shape
    _, _, ns = wg.shape
    tm, tk = _tile(m, 1024), _tile(k, 2048)
    grid = (m // tm, NDEV, k // tk)
    n_copies = 2 if bf16_copy else 1
    extra, extra_specs = (), ()
    epi = lambda acc: (acc,) * n_copies
    if bias is not None:
        extra, extra_specs = (bias,), (pl.BlockSpec((1, ns), lambda i, d, kk: (0, d)),)
        epi = lambda acc, bv: (acc + bv,) * n_copies
    outs = [jax.ShapeDtypeStruct((m, NDEV * ns), dt) for dt in (out_dtype, BF16)[:n_copies]]
    return _matmul(name, a, wg, outs, grid,
                   pl.BlockSpec((tm, tk), lambda i, d, kk: (i, kk)),
                   pl.BlockSpec((None, tk, ns), lambda i, d, kk: (d, kk, 0)),
                   [pl.BlockSpec((tm, ns), lambda i, d, kk: (i, d))] * n_copies, 1, False, (tm, ns),
                   extra, extra_specs, epi, comm)


def _mm_cols_t(name, a, wg, comm=None):
    m, _ = a.shape
    _, n, ns = wg.shape
    tm, tn = _tile(m, 1024), _tile(n, 1024)
    per = 2
    grid = (m // tm, n // tn, NDEV // per)
    return _matmul(name, a, wg, [jax.ShapeDtypeStruct((m, n), F32)], grid,
                   pl.BlockSpec((tm, per * ns), lambda i, j, d: (i, d)),
                   pl.BlockSpec((per, tn, ns), lambda i, j, d: (d, j, 0)),
                   [pl.BlockSpec((tm, tn), lambda i, j, d: (i, j))], 1, True, (tm, tn), comm=comm)


def _mm_dw_cols(name, at, g, comm=None):
    m, t = at.shape
    ns = g.shape[1] // NDEV
    tm, tk = _tile(m, 1024), _tile(t, 2048)
    grid = (m // tm, NDEV, t // tk)
    return _matmul(name, at, g, [jax.ShapeDtypeStruct((NDEV, m, ns), BF16)], grid,
                   pl.BlockSpec((tm, tk), lambda i, d, kk: (i, kk)),
                   pl.BlockSpec((tk, ns), lambda i, d, kk: (kk, d)),
                   [pl.BlockSpec((None, tm, ns), lambda i, d, kk: (d, i, 0))], 1, False, (tm, ns), comm=comm)


def _mm_plain(name, a, b, nt, out_dtype, extra=(), extra_specs=(), epilogue=None, outs=None, out_specs=None, comm=None):
    m, k = a.shape
    n = b.shape[0] if nt else b.shape[1]
    tm, tn, tk = _tile(m, 1024), _tile(n, 1024), _tile(k, 2048)
    grid = (m // tm, n // tn, k // tk)
    b_spec = (pl.BlockSpec((tn, tk), lambda i, j, kk: (j, kk)) if nt
              else pl.BlockSpec((tk, tn), lambda i, j, kk: (kk, j)))
    if outs is None:
        outs = [jax.ShapeDtypeStruct((m, n), out_dtype)]
        out_specs = [pl.BlockSpec((tm, tn), lambda i, j, kk: (i, j))]
    return _matmul(name, a, b, outs, grid, pl.BlockSpec((tm, tk), lambda i, j, kk: (i, kk)), b_spec,
                   out_specs, 1, nt, (tm, tn), extra, extra_specs, epilogue, comm)


def _rms_rows(v):
    return lax.rsqrt(jnp.mean(v * v, axis=-1, keepdims=True) + EPS)


def _prenorm(x, g, comm=None):
    s, d = x.shape
    ts = _tile(s, 256)

    def body(x_ref, g_ref, h_ref, ht_ref):
        xv = x_ref[...]
        h = xv * _rms_rows(xv) * g_ref[...]
        h_ref[...] = h.astype(BF16)
        ht_ref[...] = h.T.astype(BF16)

    return _call("prenorm", body, (s // ts,),
                 [pl.BlockSpec((ts, d), lambda i: (i, 0)), pl.BlockSpec((1, d), lambda i: (0, 0))],
                 [pl.BlockSpec((ts, d), lambda i: (i, 0)), pl.BlockSpec((d, ts), lambda i: (0, i))],
                 [jax.ShapeDtypeStruct((s, d), BF16), jax.ShapeDtypeStruct((d, s), BF16)], [], ("parallel",), (x, g), comm)


def _ln_silu(u1, g, b):
    s, c = u1.shape
    ts = _tile(s, 256)

    def body(u_ref, g_ref, b_ref, o_ref, ot_ref):
        u = u_ref[...]
        mu = jnp.mean(u, axis=-1, keepdims=True)
        var = jnp.mean(jnp.square(u - mu), axis=-1, keepdims=True)
        u2 = (u - mu) * lax.rsqrt(var + EPS) * g_ref[...] + b_ref[...]
        u3 = u2 * _sigmoid(u2)
        o_ref[...] = u3.astype(BF16)
        ot_ref[...] = u3.T.astype(BF16)

    vec = pl.BlockSpec((1, c), lambda i: (0, 0))
    return _call("ln_silu", body, (s // ts,), [pl.BlockSpec((ts, c), lambda i: (i, 0)), vec, vec],
                 [pl.BlockSpec((ts, c), lambda i: (i, 0)), pl.BlockSpec((c, ts), lambda i: (0, i))],
                 [jax.ShapeDtypeStruct((s, c), BF16), jax.ShapeDtypeStruct((c, s), BF16)], [], ("parallel",), (u1, g, b))


def _ln_silu_bwd(du3, u1, g, b, comm=None):
    s, c = u1.shape
    ts = _tile(s, 256)

    def body(d_ref, u_ref, g_ref, b_ref, du1_ref, dg_ref, db_ref):
        @pl.when(pl.program_id(0) == 0)
        def _():
            dg_ref[...] = jnp.zeros_like(dg_ref)
            db_ref[...] = jnp.zeros_like(db_ref)

        u = u_ref[...]
        mu = jnp.mean(u, axis=-1, keepdims=True)
        var = jnp.mean(jnp.square(u - mu), axis=-1, keepdims=True)
        rstd = lax.rsqrt(var + EPS)
        uhat = (u - mu) * rstd
        u2 = uhat * g_ref[...] + b_ref[...]
        sg = _sigmoid(u2)
        du2 = d_ref[...] * (sg * (1.0 + u2 * (1.0 - sg)))
        dg_ref[...] += jnp.sum(du2 * uhat, axis=0, keepdims=True)
        db_ref[...] += jnp.sum(du2, axis=0, keepdims=True)
        duh = du2 * g_ref[...]
        du1_ref[...] = rstd * (duh - jnp.mean(duh, axis=-1, keepdims=True)
                               - uhat * jnp.mean(duh * uhat, axis=-1, keepdims=True))

    row = pl.BlockSpec((ts, c), lambda i: (i, 0))
    vec = pl.BlockSpec((1, c), lambda i: (0, 0))
    return _call("ln_silu_bwd", body, (s // ts,), [row, row, vec, vec], [row, vec, vec],
                 [jax.ShapeDtypeStruct((s, c), F32), jax.ShapeDtypeStruct((1, c), F32), jax.ShapeDtypeStruct((1, c), F32)],
                 [], ("arbitrary",), (du3, u1, g, b), comm)


def _merge(proj, o_sbp, o_cv, d, comm=None):
    s = proj.shape[0]
    w = d // 2
    ts = _tile(s, 256)

    def body(gs_ref, gc_ref, a_ref, b_ref, m_ref, mt_ref):
        mg = _sigmoid(gs_ref[...]) * a_ref[...] + _sigmoid(gc_ref[...]) * b_ref[...]
        m_ref[...] = mg.astype(BF16)
        mt_ref[...] = mg.T.astype(BF16)

    blk = pl.BlockSpec((ts, w), lambda i, j: (i, j))
    return _call("merge", body, (s // ts, 2),
                 [pl.BlockSpec((ts, w), lambda i, j: (i, 5 + j)), pl.BlockSpec((ts, w), lambda i, j: (i, 7 + j)), blk, blk],
                 [blk, pl.BlockSpec((w, ts), lambda i, j: (j, i))],
                 [jax.ShapeDtypeStruct((s, d), BF16), jax.ShapeDtypeStruct((d, s), BF16)], [],
                 ("parallel", "parallel"), (proj, proj, o_sbp, o_cv), comm)


def _merge_bwd(dmerged, proj, o_sbp, o_cv, d, comm=None):
    s = proj.shape[0]
    w = d // 2
    ts = _tile(s, 256)

    def body(dm_ref, gs_ref, gc_ref, a_ref, b_ref, da_ref, db_ref, dgs_ref, dgc_ref):
        dm = dm_ref[...]
        ss = _sigmoid(gs_ref[...])
        sc = _sigmoid(gc_ref[...])
        da_ref[...] = (dm * ss).astype(BF16)
        db_ref[...] = (dm * sc).astype(BF16)
        dgs_ref[...] = (dm * a_ref[...] * ss * (1.0 - ss)).astype(BF16)
        dgc_ref[...] = (dm * b_ref[...] * sc * (1.0 - sc)).astype(BF16)

    blk = pl.BlockSpec((ts, w), lambda i, j: (i, j))
    sds = jax.ShapeDtypeStruct((s, d), BF16)
    return _call("merge_bwd", body, (s // ts, 2),
                 [blk, pl.BlockSpec((ts, w), lambda i, j: (i, 5 + j)), pl.BlockSpec((ts, w), lambda i, j: (i, 7 + j)),
                  blk, blk],
                 [blk, blk, blk, blk], [sds, sds, sds, sds], [], ("parallel", "parallel"),
                 (dmerged, proj, proj, o_sbp, o_cv), comm)


def _postnorm_mix(x, y, g_post, g_pre, comm=None):
    s, d = x.shape
    ts = _tile(s, 256)

    def body(x_ref, y_ref, gp_ref, gn_ref, x1_ref, h_ref, ht_ref):
        yv = y_ref[...]
        x1 = x_ref[...] + yv * _rms_rows(yv) * gp_ref[...]
        x1_ref[...] = x1
        h = x1 * _rms_rows(x1) * gn_ref[...]
        h_ref[...] = h.astype(BF16)
        ht_ref[...] = h.T.astype(BF16)

    row = pl.BlockSpec((ts, d), lambda i: (i, 0))
    vec = pl.BlockSpec((1, d), lambda i: (0, 0))
    return _call("postnorm_mix", body, (s // ts,), [row, row, vec, vec],
                 [row, row, pl.BlockSpec((d, ts), lambda i: (0, i))],
                 [jax.ShapeDtypeStruct((s, d), F32), jax.ShapeDtypeStruct((s, d), BF16), jax.ShapeDtypeStruct((d, s), BF16)],
                 [], ("parallel",), (x, y, g_post, g_pre), comm)


def _rms_bwd(dout, vin, g):
    r = _rms_rows(vin)
    vhat = vin * r
    dyh = dout * g
    dvin = r * (dyh - vhat * jnp.mean(dyh * vhat, axis=-1, keepdims=True))
    return dvin, jnp.sum(dout * vhat, axis=0, keepdims=True)


def _loss_head(x1, f2, tgt, g):
    s, d = x1.shape
    ts = _tile(s, 256)

    def body(x1_ref, f_ref, t_ref, g_ref, dx2_ref, df2_ref, dg_ref, loss_ref):
        @pl.when(pl.program_id(0) == 0)
        def _():
            dg_ref[...] = jnp.zeros_like(dg_ref)
            loss_ref[...] = jnp.zeros_like(loss_ref)

        fv = f_ref[...]
        x2 = x1_ref[...] + fv * _rms_rows(fv) * g_ref[...]
        err = x2 - t_ref[...]
        loss_ref[...] += 0.5 * jnp.sum(jnp.mean(err * err, axis=-1, keepdims=True), axis=0, keepdims=True)
        dx2 = err * (1.0 / d)
        dx2_ref[...] = dx2
        df2, dg = _rms_bwd(dx2, fv, g_ref[...])
        df2_ref[...] = df2.astype(BF16)
        dg_ref[...] += dg

    row = pl.BlockSpec((ts, d), lambda i: (i, 0))
    vec = pl.BlockSpec((1, d), lambda i: (0, 0))
    return _call("loss_head", body, (s // ts,), [row, row, row, vec],
                 [row, row, vec, pl.BlockSpec((1, LANES), lambda i: (0, 0))],
                 [jax.ShapeDtypeStruct((s, d), F32), jax.ShapeDtypeStruct((s, d), BF16),
                  jax.ShapeDtypeStruct((1, d), F32), jax.ShapeDtypeStruct((1, LANES), F32)],
                 [], ("arbitrary",), (x1, f2, tgt, g))


def _midnorm_bwd(dx2, dh2, x1, y, g_post, g_pre, comm=None):
    s, d = x1.shape
    ts = _tile(s, 256)

    def body(dx2_ref, dh_ref, x1_ref, y_ref, gp_ref, gn_ref, dx1_ref, dy_ref, dgn_ref, dgp_ref):
        @pl.when(pl.program_id(0) == 0)
        def _():
            dgn_ref[...] = jnp.zeros_like(dgn_ref)
            dgp_ref[...] = jnp.zeros_like(dgp_ref)

        dxa, dgn = _rms_bwd(dh_ref[...], x1_ref[...], gn_ref[...])
        dx1 = dx2_ref[...] + dxa
        dx1_ref[...] = dx1
        dy, dgp = _rms_bwd(dx1, y_ref[...], gp_ref[...])
        dy_ref[...] = dy.astype(BF16)
        dgn_ref[...] += dgn
        dgp_ref[...] += dgp

    row = pl.BlockSpec((ts, d), lambda i: (i, 0))
    vec = pl.BlockSpec((1, d), lambda i: (0, 0))
    return _call("midnorm_bwd", body, (s // ts,), [row, row, row, row, vec, vec], [row, row, vec, vec],
                 [jax.ShapeDtypeStruct((s, d), F32), jax.ShapeDtypeStruct((s, d), BF16),
                  jax.ShapeDtypeStruct((1, d), F32), jax.ShapeDtypeStruct((1, d), F32)],
                 [], ("arbitrary",), (dx2, dh2, x1, y, g_post, g_pre), comm)


def _prenorm_bwd(dx1, dh, x, g, comm=None):
    s, d = x.shape
    ts = _tile(s, 256)

    def body(dx1_ref, dh_ref, x_ref, g_ref, dx_ref, dg_ref):
        @pl.when(pl.program_id(0) == 0)
        def _():
            dg_ref[...] = jnp.zeros_like(dg_ref)

        dxa, dg = _rms_bwd(dh_ref[...], x_ref[...], g_ref[...])
        dx_ref[...] = dx1_ref[...] + dxa
        dg_ref[...] += dg

    row = pl.BlockSpec((ts, d), lambda i: (i, 0))
    vec = pl.BlockSpec((1, d), lambda i: (0, 0))
    return _call("prenorm_bwd", body, (s // ts,), [row, row, row, vec], [row, vec],
                 [jax.ShapeDtypeStruct((s, d), F32), jax.ShapeDtypeStruct((1, d), F32)],
                 [], ("arbitrary",), (dx1, dh, x, g), comm)


def _colsum(a, comm=None):
    s, n = a.shape
    ts = _tile(s, 256)

    def body(a_ref, o_ref):
        @pl.when(pl.program_id(0) == 0)
        def _():
            o_ref[...] = jnp.zeros_like(o_ref)

        o_ref[...] += jnp.sum(a_ref[...].astype(F32), axis=0, keepdims=True)

    return _call("colsum", body, (s // ts,), [pl.BlockSpec((ts, n), lambda i: (i, 0))],
                 [pl.BlockSpec((1, n), lambda i: (0, 0))], [jax.ShapeDtypeStruct((1, n), F32)], [], ("arbitrary",), (a,),
                 comm)


def _shift_rows(win, off, t):
    n = win.shape[0]
    if off == 0:
        return win[:t]
    return pltpu.roll(win, n - off, axis=0)[:t]


def _conv_fwd(proj, w_pad, b_dw, c_total, comm=None):
    s = proj.shape[0]
    nct = c_total // LANES
    t = _tile(s, 256)

    def body(ga_ref, gb_ref, w_ref, b_ref, o_ref, u0_ref):
        u0_ref[pl.ds(0, CONV_PAD), :] = jnp.zeros((CONV_PAD, LANES), F32)
        u0_ref[pl.ds(CONV_PAD, s), :] = ga_ref[...] * _sigmoid(gb_ref[...])
        wv = w_ref[...]

        def chunk(r, carry):
            r0 = pl.multiple_of(r * t, t)
            win = u0_ref[pl.ds(r0, t + CONV_PAD), :]
            acc = jnp.broadcast_to(b_ref[...], (t, LANES))
            for j in range(CONV_WIDTH):
                acc = acc + wv[j:j + 1, :] * _shift_rows(win, j + CONV_PAD - (CONV_WIDTH - 1), t)
            o_ref[pl.ds(r0, t), :] = acc
            return carry

        lax.fori_loop(0, s // t, chunk, 0)

    return _call("conv_fwd", body, (nct,),
                 [pl.BlockSpec((s, LANES), lambda c: (0, 3 * nct + c)), pl.BlockSpec((s, LANES), lambda c: (0, 4 * nct + c)),
                  pl.BlockSpec((CONV_PAD, LANES), lambda c: (0, c)), pl.BlockSpec((1, LANES), lambda c: (0, c))],
                 [pl.BlockSpec((s, LANES), lambda c: (0, c))], [jax.ShapeDtypeStruct((s, c_total), F32)],
                 [pltpu.VMEM((s + CONV_PAD, LANES), F32)], ("parallel",), (proj, proj, w_pad, b_dw), comm)


def _conv_bwd(du1, proj, w_pad, c_total, comm=None):
    s = proj.shape[0]
    nct = c_total // LANES
    t = _tile(s, 256)

    def body(d_ref, ga_ref, gb_ref, w_ref, dga_ref, dgb_ref, dw_ref, db_ref, u0_ref, dp_ref):
        sg = _sigmoid(gb_ref[...])
        u0_ref[pl.ds(0, CONV_PAD), :] = jnp.zeros((CONV_PAD, LANES), F32)
        u0_ref[pl.ds(CONV_PAD, s), :] = ga_ref[...] * sg
        dp_ref[pl.ds(0, s), :] = d_ref[...]
        dp_ref[pl.ds(s, CONV_PAD), :] = jnp.zeros((CONV_PAD, LANES), F32)
        dw_ref[...] = jnp.zeros_like(dw_ref)
        db_ref[...] = jnp.sum(d_ref[...], axis=0, keepdims=True)
        wv = w_ref[...]

        def chunk(r, carry):
            r0 = pl.multiple_of(r * t, t)
            win = u0_ref[pl.ds(r0, t + CONV_PAD), :]
            dwin = dp_ref[pl.ds(r0, t + CONV_PAD), :]
            dcur = dwin[:t]
            du0 = jnp.zeros((t, LANES), F32)
            for j in range(CONV_WIDTH):
                du0 = du0 + wv[j:j + 1, :] * _shift_rows(dwin, CONV_WIDTH - 1 - j, t)
                sh = _shift_rows(win, j + CONV_PAD - (CONV_WIDTH - 1), t)
                dw_ref[j:j + 1, :] += jnp.sum(dcur * sh, axis=0, keepdims=True)
            gav = ga_ref[pl.ds(r0, t), :]
            sgv = _sigmoid(gb_ref[pl.ds(r0, t), :])
            dga_ref[pl.ds(r0, t), :] = (du0 * sgv).astype(BF16)
            dgb_ref[pl.ds(r0, t), :] = (du0 * gav * sgv * (1.0 - sgv)).astype(BF16)
            return carry

        lax.fori_loop(0, s // t, chunk, 0)

    col = pl.BlockSpec((s, LANES), lambda c: (0, c))
    return _call("conv_bwd", body, (nct,),
                 [col, pl.BlockSpec((s, LANES), lambda c: (0, 3 * nct + c)),
                  pl.BlockSpec((s, LANES), lambda c: (0, 4 * nct + c)), pl.BlockSpec((CONV_PAD, LANES), lambda c: (0, c))],
                 [col, col, pl.BlockSpec((CONV_PAD, LANES), lambda c: (0, c)), pl.BlockSpec((1, LANES), lambda c: (0, c))],
                 [jax.ShapeDtypeStruct((s, c_total), BF16), jax.ShapeDtypeStruct((s, c_total), BF16),
                  jax.ShapeDtypeStruct((CONV_PAD, c_total), F32), jax.ShapeDtypeStruct((1, c_total), F32)],
                 [pltpu.VMEM((s + CONV_PAD, LANES), F32), pltpu.VMEM((s + CONV_PAD, LANES), F32)],
                 ("parallel",), (du1, proj, proj, w_pad), comm)


TQ_PREF = 256
NU = 4
NU_BWD = 4
TK = 256


def _split_dot(v, tri):
    hi = v.astype(BF16)
    lo = (v - hi.astype(F32)).astype(BF16)
    return (jnp.dot(hi, tri, preferred_element_type=F32) + jnp.dot(lo, tri, preferred_element_type=F32))


def _causal_mask(i, j, tq):
    tpos = i * tq + lax.broadcasted_iota(jnp.int32, (tq, TK), 0)
    spos = j * TK + lax.broadcasted_iota(jnp.int32, (tq, TK), 1)
    return spos < tpos


def _log_terms(z, mask):
    sp = jnp.log(1.0 + jnp.exp(-jnp.abs(z)))
    return jnp.minimum(z, 0.0) - sp, jnp.where(mask, -jnp.maximum(z, 0.0) - sp, 0.0)


def _tri(after):
    r = lax.broadcasted_iota(jnp.int32, (TK, TK), 0)
    c = lax.broadcasted_iota(jnp.int32, (TK, TK), 1)
    return (r > c).astype(BF16) if after else (r < c).astype(BF16)


def _attn_fwd(proj, n_heads, comm=None):
    s = proj.shape[0]
    tq = _tile(s, TQ_PREF)
    scale = 1.0 / math.sqrt(HEAD_DIM)
    ratio = tq // TK

    def body(q_ref, k_ref, v_ref, o_ref, ot_ref, acc_ref, *clms):
        i = pl.program_id(1)
        heads = [slice(u * HEAD_DIM, (u + 1) * HEAD_DIM) for u in range(NU)]
        qs = [q_ref[:, hs].astype(BF16) for hs in heads]
        tri_after = _tri(True)
        acc_ref[...] = jnp.zeros_like(acc_ref)
        for cr in clms:
            cr[...] = jnp.zeros_like(cr)
        nkb = (i + 1) * ratio

        def step(jj, carry):
            j = nkb - 1 - jj
            rows = pl.ds(pl.multiple_of(j * TK, TK), TK)
            mask = _causal_mask(i, j, tq)
            zs = [lax.dot_general(qs[u], k_ref[rows, hs].astype(BF16), (((1,), (1,)), ((), ())),
                                  preferred_element_type=F32) * scale for u, hs in enumerate(heads)]
            lls = [_log_terms(z, mask) for z in zs]
            sufs = [clms[u][...] + _split_dot(lls[u][1], tri_after) for u in range(NU)]
            for u, hs in enumerate(heads):
                a = jnp.where(mask, jnp.exp(lls[u][0] + sufs[u]), 0.0)
                acc_ref[:, hs] += jnp.dot(a.astype(BF16), v_ref[rows, hs].astype(BF16), preferred_element_type=F32)
                clms[u][...] += jnp.sum(lls[u][1], axis=1, keepdims=True)
            return carry

        lax.fori_loop(0, nkb, step, 0)
        o = acc_ref[...]
        o_ref[...] = o
        ot_ref[...] = o.T.astype(BF16)

    w = NU * HEAD_DIM
    ng = n_heads // NU
    return _call("attn_fwd", body, (ng, s // tq),
                 [pl.BlockSpec((tq, w), lambda h, i: (i, h)),
                  pl.BlockSpec((s, w), lambda h, i: (0, ng + h)),
                  pl.BlockSpec((s, w), lambda h, i: (0, 2 * ng + h))],
                 [pl.BlockSpec((tq, w), lambda h, i: (i, h)), pl.BlockSpec((w, tq), lambda h, i: (h, i))],
                 [jax.ShapeDtypeStruct((s, n_heads * HEAD_DIM), F32), jax.ShapeDtypeStruct((n_heads * HEAD_DIM, s), BF16)],
                 [pltpu.VMEM((tq, w), F32), *[pltpu.VMEM((tq, 1), F32)] * NU],
                 ("parallel", "arbitrary"), (proj, proj, proj), comm)


def _attn_bwd(proj, do_sb, n_heads, comm=None):
    s = proj.shape[0]
    tq = _tile(s, TQ_PREF)
    scale = 1.0 / math.sqrt(HEAD_DIM)
    ratio = tq // TK
    n_kb = s // TK
    n_qb = s // tq
    nu = NU_BWD
    heads = [slice(u * HEAD_DIM, (u + 1) * HEAD_DIM) for u in range(nu)]
    nt_dims = (((1,), (1,)), ((), ()))

    def body(q_ref, k_ref, v_ref, do_ref, dq_ref, dk_ref, dv_ref, dka_ref, dva_ref, dl_ref, be_ref, dqa_ref, *c_refs):
        i = pl.program_id(1)

        @pl.when(i == 0)
        def _():
            dka_ref[...] = jnp.zeros_like(dka_ref)
            dva_ref[...] = jnp.zeros_like(dva_ref)

        qs = [q_ref[:, hs].astype(BF16) for hs in heads]
        dobs = [do_ref[:, hs].astype(BF16) for hs in heads]
        tri_after = _tri(True)
        tri_before = _tri(False)
        nkb = (i + 1) * ratio

        for cr in c_refs:
            cr[...] = jnp.zeros_like(cr)

        def sweep_a(jj, carry):
            j = nkb - 1 - jj
            rows = pl.ds(pl.multiple_of(j * TK, TK), TK)
            mask = _causal_mask(i, j, tq)
            zs = [lax.dot_general(qs[u], k_ref[rows, hs].astype(BF16), nt_dims, preferred_element_type=F32) * scale
                  for u, hs in enumerate(heads)]
            das = [lax.dot_general(dobs[u], v_ref[rows, hs].astype(BF16), nt_dims, preferred_element_type=F32)
                   for u, hs in enumerate(heads)]
            lls = [_log_terms(z, mask) for z in zs]
            sufs = [c_refs[u][...] + _split_dot(lls[u][1], tri_after) for u in range(nu)]
            for u, hs in enumerate(heads):
                a = jnp.where(mask, jnp.exp(lls[u][0] + sufs[u]), 0.0)
                dl_ref[u, j] = das[u] * a
                be_ref[u, j] = jnp.exp(lls[u][0])
                dva_ref[rows, hs] += jnp.dot(a.T.astype(BF16), dobs[u], preferred_element_type=F32)
                c_refs[u][...] += jnp.sum(lls[u][1], axis=1, keepdims=True)
            return carry

        lax.fori_loop(0, nkb, sweep_a, 0)

        for cr in c_refs:
            cr[...] = jnp.zeros_like(cr)
        dqa_ref[...] = jnp.zeros_like(dqa_ref)

        def sweep_b(j, carry):
            rows = pl.ds(pl.multiple_of(j * TK, TK), TK)
            mask = _causal_mask(i, j, tq)
            dls = [dl_ref[u, j] for u in range(nu)]
            ps = [c_refs[u][...] + _split_dot(dls[u], tri_before) for u in range(nu)]
            for u, hs in enumerate(heads):
                beta = be_ref[u, j]
                dz = jnp.where(mask, (dls[u] * (1.0 - beta) - beta * ps[u]) * scale, 0.0)
                dqa_ref[:, hs] += jnp.dot(dz.astype(BF16), k_ref[rows, hs].astype(BF16), preferred_element_type=F32)
                dka_ref[rows, hs] += jnp.dot(dz.T.astype(BF16), qs[u], preferred_element_type=F32)
                c_refs[u][...] += jnp.sum(dls[u], axis=1, keepdims=True)
            return carry

        lax.fori_loop(0, nkb, sweep_b, 0)
        dq_ref[...] = dqa_ref[...].astype(BF16)

        @pl.when(i == n_qb - 1)
        def _():
            dk_ref[...] = dka_ref[...].astype(BF16)
            dv_ref[...] = dva_ref[...].astype(BF16)

    w = nu * HEAD_DIM
    ng = n_heads // nu
    qblk = pl.BlockSpec((tq, w), lambda h, i: (i, h))
    full = pl.BlockSpec((s, w), lambda h, i: (0, h))
    sds = jax.ShapeDtypeStruct((s, n_heads * HEAD_DIM), BF16)
    return _call("attn_bwd", body, (ng, n_qb),
                 [qblk, pl.BlockSpec((s, w), lambda h, i: (0, ng + h)), pl.BlockSpec((s, w), lambda h, i: (0, 2 * ng + h)),
                  qblk],
                 [qblk, full, full], [sds, sds, sds],
                 [pltpu.VMEM((s, w), F32), pltpu.VMEM((s, w), F32),
                  pltpu.VMEM((nu, n_kb, tq, TK), F32), pltpu.VMEM((nu, n_kb, tq, TK), F32),
                  pltpu.VMEM((tq, w), F32), *[pltpu.VMEM((tq, 1), F32)] * nu],
                 ("parallel", "arbitrary"), (proj, proj, proj, do_sb), comm)


def _adamw(name, w, m, v, parts, part_specs, tr, comm=None):
    r, c = w.shape
    n_parts = len(parts)

    def body(*refs):
        w_ref, m_ref, v_ref = refs[:3]
        p_refs = refs[3:3 + n_parts]
        g_ref, d_ref, nm_ref, nv_ref = refs[3 + n_parts:]
        g = p_refs[0][...].astype(F32)
        for p in p_refs[1:]:
            g = g + p[...].astype(F32)
        nm = ADAM_B1 * m_ref[...] + (1.0 - ADAM_B1) * g
        nv = ADAM_B2 * v_ref[...] + (1.0 - ADAM_B2) * jnp.square(g)
        m_hat = nm / (1.0 - ADAM_B1 ** ADAM_STEP)
        v_hat = nv / (1.0 - ADAM_B2 ** ADAM_STEP)
        g_ref[...] = g
        d_ref[...] = -ADAM_LR * (m_hat / (jnp.sqrt(v_hat) + ADAM_EPS) + ADAM_WD * w_ref[...])
        nm_ref[...] = nm
        nv_ref[...] = nv

    blk = pl.BlockSpec((tr, c), lambda i: (i, 0))
    sds = jax.ShapeDtypeStruct((r, c), F32)
    return _call(name, body, (r // tr,), [blk, blk, blk, *part_specs], [blk] * 4, [sds] * 4, [], ("parallel",),
                 (w, m, v, *parts), comm)


def _adamw_big(name, w, m, v, recv, comm=None):
    r, c = w.shape
    tr = _tile(r, 128)
    order = (3, 0, 1, 2)
    specs = [pl.BlockSpec((None, tr, c), functools.partial(lambda i, slot: (slot, i, 0), slot=sl)) for sl in order]
    return _adamw(name, w, m, v, [recv] * 4, specs, tr, comm)


def _adamw_small(name, w, m, v, g):
    r, c = w.shape
    return _adamw(name, w, m, v, [g], [pl.BlockSpec((r, c), lambda i: (0, 0))], r)


def kernel(x, g_pre_mix, w_in, b_in, w_dw, b_dw, g_conv_ln, b_conv_ln, w_sb_out, w_conv_out, w_o, g_post_mix, g_pre_mlp, w_up, w_down, g_post_mlp, loss_target, m_g_pre_mix, m_w_in, m_b_in, m_w_dw, m_b_dw, m_g_conv_ln, m_b_conv_ln, m_w_sb_out, m_w_conv_out, m_w_o, m_g_post_mix, m_g_pre_mlp, m_w_up, m_w_down, m_g_post_mlp, v_g_pre_mix, v_w_in, v_b_in, v_w_dw, v_b_dw, v_g_conv_ln, v_b_conv_ln, v_w_sb_out, v_w_conv_out, v_w_o, v_g_post_mix, v_g_pre_mlp, v_w_up, v_w_down, v_g_post_mlp):
    xs, tgt = x[0], loss_target[0]
    s, d = xs.shape
    d_half = d // 2
    n_heads = d_half // HEAD_DIM
    d_ff = NDEV * w_up.shape[2]
    core = lax.axis_index("c").astype(jnp.int32).reshape(1)
    dev = 4 * lax.axis_index("x") + 2 * lax.axis_index("y") + lax.axis_index("c")

    w_dw_pad = jnp.pad(w_dw[0], ((0, CONV_PAD - CONV_WIDTH), (0, 0)))
    sh_in, sh_sb, sh_cv, sh_o, sh_up, sh_down = [w[0].astype(BF16) for w in (w_in, w_sb_out, w_conv_out, w_o, w_up, w_down)]

    ag_down = _Chunked(_ag_comm, [sh_down], sh_down.shape[0], 8)
    h, h_t, wg_in = _prenorm(xs, g_pre_mix, _ag_comm([sh_in]))
    ag_up = _Chunked(_ag_comm, [sh_up], sh_up.shape[0], 8)
    proj, proj_bf, wg_sb, wg_cv, wg_o, wg_dw, *part = _mm_cols(
        "proj", h, wg_in, bias=b_in, bf16_copy=True,
        comm=_join(_ag_comm([sh_sb, sh_cv, sh_o, w_dw_pad]), ag_up.take(1)))
    ag_up.done(part)
    o_sb, o_sb_t, wg_up = _attn_fwd(proj_bf, n_heads, ag_up.take(7))
    wf_dw = wg_dw.transpose(1, 0, 2).reshape(CONV_PAD, d_half)
    wf_o = wg_o.reshape(d, d)
    u1, *part = _conv_fwd(proj, wf_dw, b_dw, d_half, ag_down.take())
    ag_down.done(part)
    u3, u3_t = _ln_silu(u1, g_conv_ln, b_conv_ln)
    o_sbp = _mm_cols("sb_out", o_sb, wg_sb)[0]
    o_cv = _mm_cols("conv_out", u3, wg_cv)[0]
    merged, merged_t, *part = _merge(proj, o_sbp, o_cv, d, ag_down.take())
    ag_down.done(part)
    y, *part = _mm_plain("w_o", merged, wf_o, False, F32, comm=ag_down.take())
    ag_down.done(part)
    x1, h2, h2_t, *part = _postnorm_mix(xs, y, g_post_mix, g_pre_mlp, ag_down.take())
    ag_down.done(part)

    tm_up = _tile(s, 1024)
    ns_up = wg_up.shape[2]
    tk_up = _tile(d, 2048)

    def up_epilogue(acc):
        f = jnp.square(jnp.maximum(acc, 0.0))
        return acc, f, f.T

    a_act, f, f_t, wg_down = _matmul(
        "w_up", h2, wg_up,
        [jax.ShapeDtypeStruct((s, d_ff), BF16), jax.ShapeDtypeStruct((s, d_ff), BF16), jax.ShapeDtypeStruct((d_ff, s), BF16)],
        (s // tm_up, NDEV, d // tk_up),
        pl.BlockSpec((tm_up, tk_up), lambda i, dd, kk: (i, kk)),
        pl.BlockSpec((None, tk_up, ns_up), lambda i, dd, kk: (dd, kk, 0)),
        [pl.BlockSpec((tm_up, ns_up), lambda i, dd, kk: (i, dd)), pl.BlockSpec((tm_up, ns_up), lambda i, dd, kk: (i, dd)),
         pl.BlockSpec((ns_up, tm_up), lambda i, dd, kk: (dd, i))],
        1, False, (tm_up, ns_up), epilogue=up_epilogue, comm=ag_down.take(4))
    wf_down = wg_down.reshape(d_ff, d)
    f2 = _mm_plain("w_down", f, wf_down, False, F32)[0]
    dx2, df2, dg_post_mlp, loss_part = _loss_head(x1, f2, tgt, g_post_mlp)

    tm_b, tn_b = _tile(s, 1024), _tile(d_ff, 1024)
    da = _mm_plain("w_down_bwd", df2, wf_down, True, BF16,
                   extra=(a_act,), extra_specs=(pl.BlockSpec((tm_b, tn_b), lambda i, j, kk: (i, j)),),
                   epilogue=lambda acc, av: (acc * (2.0 * jnp.maximum(av.astype(F32), 0.0)),),
                   outs=[jax.ShapeDtypeStruct((s, d_ff), BF16)],
                   out_specs=[pl.BlockSpec((tm_b, tn_b), lambda i, j, kk: (i, j))])[0]
    gw_down = _mm_plain("w_down_grad", f_t, df2, False, BF16)[0]
    big_down = gw_down.reshape(4, 2, d_ff // NDEV, d)
    gw_up, sib_down = _mm_dw_cols("w_up_grad", h2_t, da, comm=_sibling_comm([big_down]))
    big_up = gw_up.reshape(4, 2, d, d_ff // NDEV)
    dh2, sib_up = _mm_cols_t("w_up_bwd", da, wg_up, comm=_sibling_comm([big_up]))
    rs_down = _Chunked(_chips_comm, [_pair_sum(big_down, sib_down, core)], d_ff // NDEV, 8)
    rs_up = _Chunked(_chips_comm, [_pair_sum(big_up, sib_up, core)], d, 8)
    dx1, dy, dg_pre_mlp, dg_post_mix, *part = _midnorm_bwd(dx2, dh2, x1, y, g_post_mix, g_pre_mlp, rs_down.take())
    rs_down.done(part)
    gw_o, *part = _mm_plain("w_o_grad", merged_t, dy, False, BF16, comm=rs_down.take())
    rs_down.done(part)
    dmerged, *part = _mm_plain("w_o_bwd", dy, wf_o, True, F32, comm=rs_down.take())
    rs_down.done(part)
    do_sbp, do_cv, dgate_sb, dgate_cv, *part = _merge_bwd(dmerged, proj, o_sbp, o_cv, d, rs_down.take())
    rs_down.done(part)
    gw_cv = _mm_dw_cols("conv_out_grad", u3_t, do_cv)[0]
    gw_sb = _mm_dw_cols("sb_out_grad", o_sb_t, do_sbp)[0]
    du3 = _mm_cols_t("conv_out_bwd", do_cv, wg_cv)[0]
    do_sb = _mm_cols_t("sb_out_bwd", do_sbp, wg_sb)[0]
    du1, dg_ln, db_ln = _ln_silu_bwd(du3, u1, g_conv_ln, b_conv_ln)
    big_mid = [gw_sb.reshape(4, 2, d_half, d // NDEV), gw_cv.reshape(4, 2, d_half, d // NDEV),
               gw_o.reshape(4, 2, d // NDEV, d)]
    dglu_a, dglu_b, dw_dw, db_dw, part, *sib_mid = _conv_bwd(du1, proj, wf_dw, d_half,
                                                             _join(rs_up.take(2), _sibling_comm(big_mid)))
    rs_up.done([part])
    sums_mid = [_pair_sum(g, r, core) for g, r in zip(big_mid, sib_mid)]
    dq, dk, dv, r_up, r_sb, r_cv, r_o = _attn_bwd(proj_bf, do_sb, n_heads, _join(rs_up.take(6), _chips_comm(sums_mid)))
    dproj = jnp.concatenate([dq, dk, dv, dglu_a, dglu_b, dgate_sb, dgate_cv], axis=1)
    gw_in, r_down = _mm_dw_cols("w_in_grad", h_t, dproj, comm=rs_down.take(4))
    big_in = gw_in.reshape(4, 2, d, gw_in.shape[2])
    db_in, sib_in = _colsum(dproj, _sibling_comm([big_in]))
    dh, r_in = _mm_cols_t("w_in_bwd", dproj, wg_in, comm=_chips_comm([_pair_sum(big_in, sib_in, core)]))
    grad_x, dg_pre_mix = _prenorm_bwd(dx1, dh, xs, g_pre_mix)

    small = [dg_pre_mix, db_in, dw_dw.reshape(1, -1), db_dw, dg_ln, db_ln, dg_post_mix, dg_pre_mlp, dg_post_mlp]
    sizes = [a.shape[1] for a in small]
    packed = jnp.concatenate(small, axis=1).reshape(-1, LANES)
    total = _sum_small(_all_gather_small(packed)).reshape(1, -1)
    offs = [0]
    for n in sizes:
        offs.append(offs[-1] + n)
    (g_g_pre_mix, g_b_in, g_w_dw_flat, g_b_dw, g_g_conv_ln, g_b_conv_ln, g_g_post_mix, g_g_pre_mlp,
     g_g_post_mlp) = [total[:, offs[k]:offs[k + 1]] for k in range(len(sizes))]
    ch = w_dw.shape[2]
    g_w_dw = lax.dynamic_slice_in_dim(g_w_dw_flat.reshape(CONV_PAD, d_half), dev * ch, ch, axis=1)[:CONV_WIDTH]

    loss = lax.psum(loss_part[0, 0], ("x", "y", "c"))

    res = {}
    res["w_up"] = _adamw_big("adamw_w_up", w_up[0], m_w_up[0], v_w_up[0], r_up)
    res["w_down"] = _adamw_big("adamw_w_down", w_down[0], m_w_down[0], v_w_down[0], r_down)
    res["g_pre_mix"] = _adamw_small("adamw_g_pre_mix", g_pre_mix, m_g_pre_mix, v_g_pre_mix, g_g_pre_mix)
    res["w_in"] = _adamw_big("adamw_w_in", w_in[0], m_w_in[0], v_w_in[0], r_in)
    res["b_in"] = _adamw_small("adamw_b_in", b_in, m_b_in, v_b_in, g_b_in)
    res["w_dw"] = _adamw_small("adamw_w_dw", w_dw[0], m_w_dw[0], v_w_dw[0], g_w_dw)
    res["b_dw"] = _adamw_small("adamw_b_dw", b_dw, m_b_dw, v_b_dw, g_b_dw)
    res["g_conv_ln"] = _adamw_small("adamw_g_conv_ln", g_conv_ln, m_g_conv_ln, v_g_conv_ln, g_g_conv_ln)
    res["b_conv_ln"] = _adamw_small("adamw_b_conv_ln", b_conv_ln, m_b_conv_ln, v_b_conv_ln, g_b_conv_ln)
    res["w_sb_out"] = _adamw_big("adamw_w_sb_out", w_sb_out[0], m_w_sb_out[0], v_w_sb_out[0], r_sb)
    res["w_conv_out"] = _adamw_big("adamw_w_conv_out", w_conv_out[0], m_w_conv_out[0], v_w_conv_out[0], r_cv)
    res["w_o"] = _adamw_big("adamw_w_o", w_o[0], m_w_o[0], v_w_o[0], r_o)
    res["g_post_mix"] = _adamw_small("adamw_g_post_mix", g_post_mix, m_g_post_mix, v_g_post_mix, g_g_post_mix)
    res["g_pre_mlp"] = _adamw_small("adamw_g_pre_mlp", g_pre_mlp, m_g_pre_mlp, v_g_pre_mlp, g_g_pre_mlp)
    res["g_post_mlp"] = _adamw_small("adamw_g_post_mlp", g_post_mlp, m_g_post_mlp, v_g_post_mlp, g_g_post_mlp)

    names = ["g_pre_mix", "w_in", "b_in", "w_dw", "b_dw", "g_conv_ln", "b_conv_ln", "w_sb_out", "w_conv_out", "w_o",
             "g_post_mix", "g_pre_mlp", "w_up", "w_down", "g_post_mlp"]
    three_d = {"w_in", "w_dw", "w_sb_out", "w_conv_out", "w_o", "w_up", "w_down"}

    def shaped(nm, arr):
        return arr[None] if nm in three_d else arr

    out = [loss, grad_x[None]]
    for k in range(4):
        out += [shaped(nm, res[nm][k]) for nm in names]
    return tuple(out)
```

```python
import functools
import math

import jax
import jax.numpy as jnp
from jax import lax
from jax.experimental import pallas as pl
from jax.experimental.pallas import tpu as pltpu

F32 = jnp.float32
BF16 = jnp.bfloat16
NDEV = 8
LANES = 128
EPS = 1e-6
CONV_WIDTH = 31
CONV_PAD = 32
HEAD_DIM = 128
ADAM_LR = 0.001
ADAM_B1 = 0.9
ADAM_B2 = 0.999
ADAM_EPS = 1e-08
ADAM_WD = 0.01
ADAM_STEP = 10
VMEM_LIMIT = 56 * 1024 * 1024
MESH = pl.DeviceIdType.MESH
ANY = pl.BlockSpec(memory_space=pl.ANY)


def _tile(n, pref):
    t = min(n, pref)
    assert n % t == 0, (n, t)
    return t


def _sigmoid(v):
    return 1.0 / (1.0 + jnp.exp(-v))


def _position():
    return lax.axis_index("x"), lax.axis_index("y"), lax.axis_index("c")


class _Comm:
    def __init__(self, ins, outs, scratch, start, finish, aliases=None, mid=None):
        self.ins, self.outs, self.scratch = list(ins), list(outs), list(scratch)
        self.start, self.finish, self.aliases = start, finish, dict(aliases or {})
        self.mid = mid


_NO_COMM = _Comm([], [], [], None, None)


def _call(name, body, grid, in_specs, out_specs, out_shape, scratch_shapes, sem, args, comm=None):
    comm = comm or _NO_COMM
    n_in, n_out, n_scr = len(in_specs), len(out_specs), len(scratch_shapes)
    n_cin, n_cout = len(comm.ins), len(comm.outs)

    n_steps = math.prod(grid)
    mid_step = n_steps - max(1, n_steps // 8) if (comm.mid is not None and n_steps >= 4) else None

    def edge(c_ins, c_outs, c_scr, at_step, actions):
        linear = 0
        for ax, g in enumerate(grid):
            linear = linear * g + pl.program_id(ax)

        @pl.when(linear == at_step)
        def _():
            for act in actions:
                act(c_ins, c_outs, c_scr)

    def wrapped(*refs):
        ins, c_ins = refs[:n_in], refs[n_in:n_in + n_cin]
        pos = n_in + n_cin
        outs, c_outs = refs[pos:pos + n_out], refs[pos + n_out:pos + n_out + n_cout]
        pos += n_out + n_cout
        scr, c_scr = refs[pos:pos + n_scr], refs[pos + n_scr:]
        if n_cin:
            edge(c_ins, c_outs, c_scr, 0, [comm.start])
            if mid_step is not None:
                edge(c_ins, c_outs, c_scr, mid_step, [comm.mid])
        body(*ins, *outs, *scr)
        if n_cin:
            late = [comm.finish] if (mid_step is not None or comm.mid is None) else [comm.mid, comm.finish]
            edge(c_ins, c_outs, c_scr, n_steps - 1, late)

    if n_cin:
        sem = ("arbitrary",) * len(grid)
    return pl.pallas_call(
        wrapped, name=name, grid=grid,
        in_specs=[*in_specs, *[ANY] * n_cin], out_specs=[*out_specs, *[ANY] * n_cout],
        out_shape=[*out_shape, *comm.outs], scratch_shapes=[*scratch_shapes, *comm.scratch],
        input_output_aliases={n_in + ci: n_out + co for ci, co in comm.aliases.items()},
        compiler_params=pltpu.CompilerParams(dimension_semantics=sem, vmem_limit_bytes=VMEM_LIMIT),
    )(*args, *comm.ins)


def _run_comm(name, comm):
    n_in, n_out = len(comm.ins), len(comm.outs)

    def body(*refs):
        ins, outs, scr = refs[:n_in], refs[n_in:n_in + n_out], refs[n_in + n_out:]
        comm.start(ins, outs, scr)
        if comm.mid is not None:
            comm.mid(ins, outs, scr)
        comm.finish(ins, outs, scr)

    return pl.pallas_call(body, name=name, in_specs=[ANY] * n_in, out_specs=[ANY] * n_out, out_shape=comm.outs,
                          scratch_shapes=comm.scratch,
                          input_output_aliases=comm.aliases)(*comm.ins)


def _rows_of(ref, rows):
    return ref if rows is None else ref.at[pl.ds(rows[0], rows[1])]


def _ag_comm(shards, rows=None, into=None):
    n = len(shards)

    def parts(ins, outs, scr):
        send_sems, recv_sems, local_sems = scr
        x, y, c = _position()
        chips = [(1 - x, y), (x, 1 - y), (1 - x, 1 - y)]

        def copy(a, k, block, to, own=False):
            px, py, pc = block
            dst = _rows_of(outs[a].at[4 * px + 2 * py + pc], rows)
            return pltpu.make_async_remote_copy(src_ref=_rows_of(ins[a], rows) if own else dst, dst_ref=dst,
                                                send_sem=send_sems.at[a, k], recv_sem=recv_sems.at[a, k],
                                                device_id=to, device_id_type=MESH)

        mine = [pltpu.make_async_copy(_rows_of(ins[a], rows), _rows_of(outs[a].at[4 * x + 2 * y + c], rows),
                                      local_sems.at[a]) for a in range(n)]
        first = []
        for a in range(n):
            first.append(copy(a, 0, (x, y, c), (x, y, 1 - c), own=True))
            first += [copy(a, 1 + j, (x, y, c), (*chip, c), own=True) for j, chip in enumerate(chips)]
        return copy, mine, first, chips, (x, y, c), (x, y, 1 - c)

    def start(ins, outs, scr):
        _, mine, first, _, _, _ = parts(ins, outs, scr)
        for cp in mine + first:
            cp.start()

    def mid(ins, outs, scr):
        copy, _, _, chips, me, sibling = parts(ins, outs, scr)
        for a in range(n):
            for j, chip in enumerate(chips):
                copy(a, 1 + j, (*chip, me[2]), me).wait_recv()
                copy(a, 4 + j, (*chip, me[2]), sibling).start()

    def finish(ins, outs, scr):
        copy, mine, first, chips, me, sibling = parts(ins, outs, scr)
        c = me[2]
        passed = [copy(a, 4 + j, (*chip, c), sibling) for a in range(n) for j, chip in enumerate(chips)]
        for a in range(n):
            copy(a, 0, sibling, me).wait_recv()
            for j, chip in enumerate(chips):
                copy(a, 4 + j, (*chip, 1 - c), me).wait_recv()
        for cp in first + passed:
            cp.wait_send()
        for cp in mine:
            cp.wait()

    return _Comm([*shards, *(into or [])], [jax.ShapeDtypeStruct((NDEV, *sh.shape), sh.dtype) for sh in shards],
                 [pltpu.SemaphoreType.DMA((n, 7)), pltpu.SemaphoreType.DMA((n, 7)), pltpu.SemaphoreType.DMA((n,))],
                 start, finish, {n + a: a for a in range(n)} if into else None, mid)


def _sibling_comm(grads):
    n = len(grads)

    def copies(ins, outs, scr):
        send_sems, recv_sems = scr
        x, y, c = _position()
        return [pltpu.make_async_remote_copy(src_ref=ins[a].at[k, 1 - c], dst_ref=outs[a].at[k],
                                             send_sem=send_sems.at[a, k], recv_sem=recv_sems.at[a, k],
                                             device_id=(x, y, 1 - c), device_id_type=MESH)
                for a in range(n) for k in range(4)]

    def start(ins, outs, scr):
        for cp in copies(ins, outs, scr):
            cp.start()

    def finish(ins, outs, scr):
        for cp in copies(ins, outs, scr):
            cp.wait()

    return _Comm(grads, [jax.ShapeDtypeStruct((4, *g.shape[2:]), g.dtype) for g in grads],
                 [pltpu.SemaphoreType.DMA((n, 4)), pltpu.SemaphoreType.DMA((n, 4))], start, finish)


def _chips_comm(sums, rows=None, into=None):
    n = len(sums)

    def copies(ins, outs, scr):
        send_sems, recv_sems, local_sems = scr
        x, y, c = _position()
        chips = [(1 - x, y), (x, 1 - y), (1 - x, 1 - y)]
        own = [pltpu.make_async_copy(_rows_of(ins[a].at[2 * x + y], rows), _rows_of(outs[a].at[3], rows), local_sems.at[a])
               for a in range(n)]
        remote = [pltpu.make_async_remote_copy(src_ref=_rows_of(ins[a].at[2 * px + py], rows),
                                               dst_ref=_rows_of(outs[a].at[j], rows),
                                               send_sem=send_sems.at[a, j], recv_sem=recv_sems.at[a, j],
                                               device_id=(px, py, c), device_id_type=MESH)
                  for a in range(n) for j, (px, py) in enumerate(chips)]
        return own + remote

    def start(ins, outs, scr):
        for cp in copies(ins, outs, scr):
            cp.start()

    def finish(ins, outs, scr):
        for cp in copies(ins, outs, scr):
            cp.wait()

    return _Comm([*sums, *(into or [])], [jax.ShapeDtypeStruct(sm.shape, sm.dtype) for sm in sums],
                 [pltpu.SemaphoreType.DMA((n, 3)), pltpu.SemaphoreType.DMA((n, 3)), pltpu.SemaphoreType.DMA((n,))],
                 start, finish, {n + a: a for a in range(n)} if into else None)


def _join(c1, c2):
    n_in, n_out, n_scr = len(c1.ins), len(c1.outs), len(c1.scratch)
    aliases = dict(c1.aliases)
    aliases.update({n_in + ci: n_out + co for ci, co in c2.aliases.items()})

    def start(ins, outs, scr):
        c1.start(ins[:n_in], outs[:n_out], scr[:n_scr])
        c2.start(ins[n_in:], outs[n_out:], scr[n_scr:])

    def mid(ins, outs, scr):
        if c1.mid is not None:
            c1.mid(ins[:n_in], outs[:n_out], scr[:n_scr])
        if c2.mid is not None:
            c2.mid(ins[n_in:], outs[n_out:], scr[n_scr:])

    def finish(ins, outs, scr):
        c1.finish(ins[:n_in], outs[:n_out], scr[:n_scr])
        c2.finish(ins[n_in:], outs[n_out:], scr[n_scr:])

    return _Comm(c1.ins + c2.ins, c1.outs + c2.outs, c1.scratch + c2.scratch, start, finish, aliases,
                 mid if (c1.mid is not None or c2.mid is not None) else None)


class _Chunked:
    def __init__(self, make, arrays, n_rows, n_chunks):
        self.make, self.arrays, self.into = make, arrays, None
        step = n_rows // n_chunks
        assert step * n_chunks == n_rows
        self.todo = [(k * step, step) for k in range(n_chunks)]

    def take(self, count=1):
        r0, nr = self.todo[0][0], sum(t[1] for t in self.todo[:count])
        self.todo = self.todo[count:]
        return self.make(self.arrays, (r0, nr), self.into)

    def done(self, outs):
        self.into = list(outs)
        return self.into


def _all_gather_small(part):
    def body(in_ref, out_ref, send_sems, recv_sems, local_sem):
        x, y, c = _position()
        me = 4 * x + 2 * y + c
        mine = pltpu.make_async_copy(in_ref, out_ref.at[me], local_sem)
        mine.start()
        flips = [(fx, fy, fc) for fx in (0, 1) for fy in (0, 1) for fc in (0, 1)][1:]
        copies = []
        for k, (fx, fy, fc) in enumerate(flips):
            cp = pltpu.make_async_remote_copy(src_ref=in_ref, dst_ref=out_ref.at[me], send_sem=send_sems.at[k],
                                              recv_sem=recv_sems.at[k],
                                              device_id=(x ^ fx, y ^ fy, c ^ fc), device_id_type=MESH)
            cp.start()
            copies.append(cp)
        for k, (fx, fy, fc) in enumerate(flips):
            peer = 4 * (x ^ fx) + 2 * (y ^ fy) + (c ^ fc)
            pltpu.make_async_remote_copy(src_ref=in_ref, dst_ref=out_ref.at[peer], send_sem=send_sems.at[k],
                                         recv_sem=recv_sems.at[k], device_id=(x, y, c), device_id_type=MESH).wait_recv()
        for cp in copies:
            cp.wait_send()
        mine.wait()

    return pl.pallas_call(
        body, name="all_gather_small", in_specs=[ANY], out_specs=ANY,
        out_shape=jax.ShapeDtypeStruct((NDEV, *part.shape), part.dtype),
        scratch_shapes=[pltpu.SemaphoreType.DMA((7,)), pltpu.SemaphoreType.DMA((7,)), pltpu.SemaphoreType.DMA],
    )(part)


def _pair_sum(g, recv, core):
    _, _, r, c = g.shape
    tr = _tile(r, 512)

    def body(core_ref, g_ref, r_ref, o_ref):
        o_ref[...] = (g_ref[...].astype(F32) + r_ref[...].astype(F32)).astype(o_ref.dtype)

    return pl.pallas_call(
        body, name="pair_sum",
        grid_spec=pltpu.PrefetchScalarGridSpec(
            num_scalar_prefetch=1, grid=(4, r // tr),
            in_specs=[pl.BlockSpec((None, None, tr, c), lambda k, i, core_ref: (k, core_ref[0], i, 0)),
                      pl.BlockSpec((None, tr, c), lambda k, i, core_ref: (k, i, 0))],
            out_specs=pl.BlockSpec((None, tr, c), lambda k, i, core_ref: (k, i, 0))),
        out_shape=jax.ShapeDtypeStruct((4, r, c), g.dtype),
        compiler_params=pltpu.CompilerParams(dimension_semantics=("parallel", "parallel"), vmem_limit_bytes=VMEM_LIMIT),
    )(core, g, recv)


def _sum_small(gathered):
    _, r, l = gathered.shape

    def body(g_ref, o_ref):
        acc = g_ref[0]
        for d in range(1, NDEV):
            acc = acc + g_ref[d]
        o_ref[...] = acc

    return pl.pallas_call(
        body, name="sum_small", in_specs=[pl.BlockSpec((NDEV, r, l), lambda: (0, 0, 0))],
        out_specs=pl.BlockSpec((r, l), lambda: (0, 0)), out_shape=jax.ShapeDtypeStruct((r, l), F32),
    )(gathered)


def _matmul(name, a, b, outs, grid, a_spec, b_spec, out_specs, n_red, nt, acc_shape,
            extra=(), extra_specs=(), epilogue=None, comm=None):
    n_extra, n_out = len(extra), len(outs)
    red_axes = tuple(range(len(grid) - n_red, len(grid)))
    red_sizes = tuple(grid[ax] for ax in red_axes)
    single = all(sz == 1 for sz in red_sizes)
    dims = (((1,), (1,)), ((), ())) if nt else (((1,), (0,)), ((), ()))

    def body(*refs):
        a_ref, b_ref = refs[0], refs[1]
        ex_refs = refs[2:2 + n_extra]
        o_refs = refs[2 + n_extra:2 + n_extra + n_out]
        acc_ref = refs[-1]

        def write(acc):
            vals = (acc,) if epilogue is None else epilogue(acc, *[r[...] for r in ex_refs])
            for o_ref, val in zip(o_refs, vals):
                o_ref[...] = val.astype(o_ref.dtype)

        if len(b_ref.shape) == 3:
            w = b_ref.shape[2]
            part = sum(lax.dot_general(a_ref[:, p * w:(p + 1) * w].astype(BF16), b_ref[p].astype(BF16), dims,
                                       preferred_element_type=F32) for p in range(b_ref.shape[0]))
        else:
            part = lax.dot_general(a_ref[...].astype(BF16), b_ref[...].astype(BF16), dims, preferred_element_type=F32)
        if single:
            write(part)
        else:
            ks = [pl.program_id(ax) for ax in red_axes]
            first = functools.reduce(jnp.logical_and, [k == 0 for k in ks])
            last = functools.reduce(jnp.logical_and, [k == sz - 1 for k, sz in zip(ks, red_sizes)])

            @pl.when(first)
            def _():
                acc_ref[...] = part

            @pl.when(jnp.logical_not(first))
            def _():
                acc_ref[...] += part

            @pl.when(last)
            def _():
                write(acc_ref[...])

    sem = ("parallel",) * (len(grid) - n_red) + ("arbitrary",) * n_red
    return _call(name, body, grid, [a_spec, b_spec, *extra_specs], list(out_specs), list(outs),
                 [pltpu.VMEM((8, LANES) if single else acc_shape, F32)], sem, (a, b, *extra), comm)


def _mm_cols(name, a, wg, bias=None, out_dtype=F32, comm=None, bf16_copy=False):
    m, k = a.shape
    _, _, ns = wg.shape
    tm, tk = _tile(m, 1024), _tile(k, 2048)
    grid = (m // tm, NDEV, k // tk)
    n_copies = 2 if bf16_copy else 1
    extra, extra_specs = (), ()
    epi = lambda acc: (acc,) * n_copies
    if bias is not None:
        extra, extra_specs = (bias,), (pl.BlockSpec((1, ns), lambda i, d, kk: (0, d)),)
        epi = lambda acc, bv: (acc + bv,) * n_copies
    outs = [jax.ShapeDtypeStruct((m, NDEV * ns), dt) for dt in (out_dtype, BF16)[:n_copies]]
    return _matmul(name, a, wg, outs, grid,
                   pl.BlockSpec((tm, tk), lambda i, d, kk: (i, kk)),
                   pl.BlockSpec((None, tk, ns), lambda i, d, kk: (d, kk, 0)),
                   [pl.BlockSpec((tm, ns), lambda i, d, kk: (i, d))] * n_copies, 1, False, (tm, ns),
                   extra, extra_specs, epi, comm)


def _mm_cols_t(name, a, wg, comm=None):
    m, _ = a.shape
    _, n, ns = wg.shape
    tm, tn = _tile(m, 1024), _tile(n, 1024)
    per = 4 if ns <= 1024 else 2
    grid = (m // tm, n // tn, NDEV // per)
    return _matmul(name, a, wg, [jax.ShapeDtypeStruct((m, n), F32)], grid,
                   pl.BlockSpec((tm, per * ns), lambda i, j, d: (i, d)),
                   pl.BlockSpec((per, tn, ns), lambda i, j, d: (d, j, 0)),
                   [pl.BlockSpec((tm, tn), lambda i, j, d: (i, j))], 1, True, (tm, tn), comm=comm)


def _mm_dw_cols(name, at, g, comm=None):
    m, t = at.shape
    ns = g.shape[1] // NDEV
    tm, tk = _tile(m, 1024), _tile(t, 2048)
    grid = (m // tm, NDEV, t // tk)
    return _matmul(name, at, g, [jax.ShapeDtypeStruct((NDEV, m, ns), BF16)], grid,
                   pl.BlockSpec((tm, tk), lambda i, d, kk: (i, kk)),
                   pl.BlockSpec((tk, ns), lambda i, d, kk: (kk, d)),
                   [pl.BlockSpec((None, tm, ns), lambda i, d, kk: (d, i, 0))], 1, False, (tm, ns), comm=comm)


def _mm_plain(name, a, b, nt, out_dtype, extra=(), extra_specs=(), epilogue=None, outs=None, out_specs=None, comm=None):
    m, k = a.shape
    n = b.shape[0] if nt else b.shape[1]
    tm, tn, tk = _tile(m, 1024), _tile(n, 1024), _tile(k, 2048)
    grid = (m // tm, n // tn, k // tk)
    b_spec = (pl.BlockSpec((tn, tk), lambda i, j, kk: (j, kk)) if nt
              else pl.BlockSpec((tk, tn), lambda i, j, kk: (kk, j)))
    if outs is None:
        outs = [jax.ShapeDtypeStruct((m, n), out_dtype)]
        out_specs = [pl.BlockSpec((tm, tn), lambda i, j, kk: (i, j))]
    return _matmul(name, a, b, outs, grid, pl.BlockSpec((tm, tk), lambda i, j, kk: (i, kk)), b_spec,
                   out_specs, 1, nt, (tm, tn), extra, extra_specs, epilogue, comm)


def _rms_rows(v):
    return lax.rsqrt(jnp.mean(v * v, axis=-1, keepdims=True) + EPS)


def _prenorm(x, g, comm=None):
    s, d = x.shape
    ts = _tile(s, 256)

    def body(x_ref, g_ref, h_ref, ht_ref):
        xv = x_ref[...]
        h = xv * _rms_rows(xv) * g_ref[...]
        h_ref[...] = h.astype(BF16)
        ht_ref[...] = h.T.astype(BF16)

    return _call("prenorm", body, (s // ts,),
                 [pl.BlockSpec((ts, d), lambda i: (i, 0)), pl.BlockSpec((1, d), lambda i: (0, 0))],
                 [pl.BlockSpec((ts, d), lambda i: (i, 0)), pl.BlockSpec((d, ts), lambda i: (0, i))],
                 [jax.ShapeDtypeStruct((s, d), BF16), jax.ShapeDtypeStruct((d, s), BF16)], [], ("parallel",), (x, g), comm)


def _ln_silu(u1, g, b):
    s, c = u1.shape
    ts = _tile(s, 256)

    def body(u_ref, g_ref, b_ref, o_ref, ot_ref):
        u = u_ref[...]
        mu = jnp.mean(u, axis=-1, keepdims=True)
        var = jnp.mean(jnp.square(u - mu), axis=-1, keepdims=True)
        u2 = (u - mu) * lax.rsqrt(var + EPS) * g_ref[...] + b_ref[...]
        u3 = u2 * _sigmoid(u2)
        o_ref[...] = u3.astype(BF16)
        ot_ref[...] = u3.T.astype(BF16)

    vec = pl.BlockSpec((1, c), lambda i: (0, 0))
    return _call("ln_silu", body, (s // ts,), [pl.BlockSpec((ts, c), lambda i: (i, 0)), vec, vec],
                 [pl.BlockSpec((ts, c), lambda i: (i, 0)), pl.BlockSpec((c, ts), lambda i: (0, i))],
                 [jax.ShapeDtypeStruct((s, c), BF16), jax.ShapeDtypeStruct((c, s), BF16)], [], ("parallel",), (u1, g, b))


def _ln_silu_bwd(du3, u1, g, b, comm=None):
    s, c = u1.shape
    ts = _tile(s, 256)

    def body(d_ref, u_ref, g_ref, b_ref, du1_ref, dg_ref, db_ref):
        @pl.when(pl.program_id(0) == 0)
        def _():
            dg_ref[...] = jnp.zeros_like(dg_ref)
            db_ref[...] = jnp.zeros_like(db_ref)

        u = u_ref[...]
        mu = jnp.mean(u, axis=-1, keepdims=True)
        var = jnp.mean(jnp.square(u - mu), axis=-1, keepdims=True)
        rstd = lax.rsqrt(var + EPS)
        uhat = (u - mu) * rstd
        u2 = uhat * g_ref[...] + b_ref[...]
        sg = _sigmoid(u2)
        du2 = d_ref[...] * (sg * (1.0 + u2 * (1.0 - sg)))
        dg_ref[...] += jnp.sum(du2 * uhat, axis=0, keepdims=True)
        db_ref[...] += jnp.sum(du2, axis=0, keepdims=True)
        duh = du2 * g_ref[...]
        du1_ref[...] = rstd * (duh - jnp.mean(duh, axis=-1, keepdims=True)
                               - uhat * jnp.mean(duh * uhat, axis=-1, keepdims=True))

    row = pl.BlockSpec((ts, c), lambda i: (i, 0))
    vec = pl.BlockSpec((1, c), lambda i: (0, 0))
    return _call("ln_silu_bwd", body, (s // ts,), [row, row, vec, vec], [row, vec, vec],
                 [jax.ShapeDtypeStruct((s, c), F32), jax.ShapeDtypeStruct((1, c), F32), jax.ShapeDtypeStruct((1, c), F32)],
                 [], ("arbitrary",), (du3, u1, g, b), comm)


def _merge(proj, o_sbp, o_cv, d, comm=None):
    s = proj.shape[0]
    w = d // 2
    ts = _tile(s, 256)

    def body(gs_ref, gc_ref, a_ref, b_ref, m_ref, mt_ref):
        mg = _sigmoid(gs_ref[...]) * a_ref[...] + _sigmoid(gc_ref[...]) * b_ref[...]
        m_ref[...] = mg.astype(BF16)
        mt_ref[...] = mg.T.astype(BF16)

    blk = pl.BlockSpec((ts, w), lambda i, j: (i, j))
    return _call("merge", body, (s // ts, 2),
                 [pl.BlockSpec((ts, w), lambda i, j: (i, 5 + j)), pl.BlockSpec((ts, w), lambda i, j: (i, 7 + j)), blk, blk],
                 [blk, pl.BlockSpec((w, ts), lambda i, j: (j, i))],
                 [jax.ShapeDtypeStruct((s, d), BF16), jax.ShapeDtypeStruct((d, s), BF16)], [],
                 ("parallel", "parallel"), (proj, proj, o_sbp, o_cv), comm)


def _merge_bwd(dmerged, proj, o_sbp, o_cv, d, comm=None):
    s = proj.shape[0]
    w = d // 2
    ts = _tile(s, 256)

    def body(dm_ref, gs_ref, gc_ref, a_ref, b_ref, da_ref, db_ref, dgs_ref, dgc_ref):
        dm = dm_ref[...]
        ss = _sigmoid(gs_ref[...])
        sc = _sigmoid(gc_ref[...])
        da_ref[...] = (dm * ss).astype(BF16)
        db_ref[...] = (dm * sc).astype(BF16)
        dgs_ref[...] = (dm * a_ref[...] * ss * (1.0 - ss)).astype(BF16)
        dgc_ref[...] = (dm * b_ref[...] * sc * (1.0 - sc)).astype(BF16)

    blk = pl.BlockSpec((ts, w), lambda i, j: (i, j))
    sds = jax.ShapeDtypeStruct((s, d), BF16)
    return _call("merge_bwd", body, (s // ts, 2),
                 [blk, pl.BlockSpec((ts, w), lambda i, j: (i, 5 + j)), pl.BlockSpec((ts, w), lambda i, j: (i, 7 + j)),
                  blk, blk],
                 [blk, blk, blk, blk], [sds, sds, sds, sds], [], ("parallel", "parallel"),
                 (dmerged, proj, proj, o_sbp, o_cv), comm)


def _postnorm_mix(x, y, g_post, g_pre, comm=None):
    s, d = x.shape
    ts = _tile(s, 256)

    def body(x_ref, y_ref, gp_ref, gn_ref, x1_ref, h_ref, ht_ref):
        yv = y_ref[...]
        x1 = x_ref[...] + yv * _rms_rows(yv) * gp_ref[...]
        x1_ref[...] = x1
        h = x1 * _rms_rows(x1) * gn_ref[...]
        h_ref[...] = h.astype(BF16)
        ht_ref[...] = h.T.astype(BF16)

    row = pl.BlockSpec((ts, d), lambda i: (i, 0))
    vec = pl.BlockSpec((1, d), lambda i: (0, 0))
    return _call("postnorm_mix", body, (s // ts,), [row, row, vec, vec],
                 [row, row, pl.BlockSpec((d, ts), lambda i: (0, i))],
                 [jax.ShapeDtypeStruct((s, d), F32), jax.ShapeDtypeStruct((s, d), BF16), jax.ShapeDtypeStruct((d, s), BF16)],
                 [], ("parallel",), (x, y, g_post, g_pre), comm)


def _rms_bwd(dout, vin, g):
    r = _rms_rows(vin)
    vhat = vin * r
    dyh = dout * g
    dvin = r * (dyh - vhat * jnp.mean(dyh * vhat, axis=-1, keepdims=True))
    return dvin, jnp.sum(dout * vhat, axis=0, keepdims=True)


def _loss_head(x1, f2, tgt, g):
    s, d = x1.shape
    ts = _tile(s, 256)

    def body(x1_ref, f_ref, t_ref, g_ref, dx2_ref, df2_ref, dg_ref, loss_ref):
        @pl.when(pl.program_id(0) == 0)
        def _():
            dg_ref[...] = jnp.zeros_like(dg_ref)
            loss_ref[...] = jnp.zeros_like(loss_ref)

        fv = f_ref[...]
        x2 = x1_ref[...] + fv * _rms_rows(fv) * g_ref[...]
        err = x2 - t_ref[...]
        loss_ref[...] += 0.5 * jnp.sum(jnp.mean(err * err, axis=-1, keepdims=True), axis=0, keepdims=True)
        dx2 = err * (1.0 / d)
        dx2_ref[...] = dx2
        df2, dg = _rms_bwd(dx2, fv, g_ref[...])
        df2_ref[...] = df2.astype(BF16)
        dg_ref[...] += dg

    row = pl.BlockSpec((ts, d), lambda i: (i, 0))
    vec = pl.BlockSpec((1, d), lambda i: (0, 0))
    return _call("loss_head", body, (s // ts,), [row, row, row, vec],
                 [row, row, vec, pl.BlockSpec((1, LANES), lambda i: (0, 0))],
                 [jax.ShapeDtypeStruct((s, d), F32), jax.ShapeDtypeStruct((s, d), BF16),
                  jax.ShapeDtypeStruct((1, d), F32), jax.ShapeDtypeStruct((1, LANES), F32)],
                 [], ("arbitrary",), (x1, f2, tgt, g))


def _midnorm_bwd(dx2, dh2, x1, y, g_post, g_pre, comm=None):
    s, d = x1.shape
    ts = _tile(s, 256)

    def body(dx2_ref, dh_ref, x1_ref, y_ref, gp_ref, gn_ref, dx1_ref, dy_ref, dgn_ref, dgp_ref):
        @pl.when(pl.program_id(0) == 0)
        def _():
            dgn_ref[...] = jnp.zeros_like(dgn_ref)
            dgp_ref[...] = jnp.zeros_like(dgp_ref)

        dxa, dgn = _rms_bwd(dh_ref[...], x1_ref[...], gn_ref[...])
        dx1 = dx2_ref[...] + dxa
        dx1_ref[...] = dx1
        dy, dgp = _rms_bwd(dx1, y_ref[...], gp_ref[...])
        dy_ref[...] = dy.astype(BF16)
        dgn_ref[...] += dgn
        dgp_ref[...] += dgp

    row = pl.BlockSpec((ts, d), lambda i: (i, 0))
    vec = pl.BlockSpec((1, d), lambda i: (0, 0))
    return _call("midnorm_bwd", body, (s // ts,), [row, row, row, row, vec, vec], [row, row, vec, vec],
                 [jax.ShapeDtypeStruct((s, d), F32), jax.ShapeDtypeStruct((s, d), BF16),
                  jax.ShapeDtypeStruct((1, d), F32), jax.ShapeDtypeStruct((1, d), F32)],
                 [], ("arbitrary",), (dx2, dh2, x1, y, g_post, g_pre), comm)


def _prenorm_bwd(dx1, dh, x, g, comm=None):
    s, d = x.shape
    ts = _tile(s, 256)

    def body(dx1_ref, dh_ref, x_ref, g_ref, dx_ref, dg_ref):
        @pl.when(pl.program_id(0) == 0)
        def _():
            dg_ref[...] = jnp.zeros_like(dg_ref)

        dxa, dg = _rms_bwd(dh_ref[...], x_ref[...], g_ref[...])
        dx_ref[...] = dx1_ref[...] + dxa
        dg_ref[...] += dg

    row = pl.BlockSpec((ts, d), lambda i: (i, 0))
    vec = pl.BlockSpec((1, d), lambda i: (0, 0))
    return _call("prenorm_bwd", body, (s // ts,), [row, row, row, vec], [row, vec],
                 [jax.ShapeDtypeStruct((s, d), F32), jax.ShapeDtypeStruct((1, d), F32)],
                 [], ("arbitrary",), (dx1, dh, x, g), comm)


def _colsum(a, comm=None):
    s, n = a.shape
    ts = _tile(s, 256)

    def body(a_ref, o_ref):
        @pl.when(pl.program_id(0) == 0)
        def _():
            o_ref[...] = jnp.zeros_like(o_ref)

        o_ref[...] += jnp.sum(a_ref[...].astype(F32), axis=0, keepdims=True)

    return _call("colsum", body, (s // ts,), [pl.BlockSpec((ts, n), lambda i: (i, 0))],
                 [pl.BlockSpec((1, n), lambda i: (0, 0))], [jax.ShapeDtypeStruct((1, n), F32)], [], ("arbitrary",), (a,),
                 comm)


def _shift_rows(win, off, t):
    n = win.shape[0]
    if off == 0:
        return win[:t]
    return pltpu.roll(win, n - off, axis=0)[:t]


def _conv_fwd(proj, w_pad, b_dw, c_total, comm=None):
    s = proj.shape[0]
    nct = c_total // LANES
    t = _tile(s, 256)

    def body(ga_ref, gb_ref, w_ref, b_ref, o_ref, u0_ref):
        u0_ref[pl.ds(0, CONV_PAD), :] = jnp.zeros((CONV_PAD, LANES), F32)
        u0_ref[pl.ds(CONV_PAD, s), :] = ga_ref[...] * _sigmoid(gb_ref[...])
        wv = w_ref[...]

        def chunk(r, carry):
            r0 = pl.multiple_of(r * t, t)
            win = u0_ref[pl.ds(r0, t + CONV_PAD), :]
            acc = jnp.broadcast_to(b_ref[...], (t, LANES))
            for j in range(CONV_WIDTH):
                acc = acc + wv[j:j + 1, :] * _shift_rows(win, j + CONV_PAD - (CONV_WIDTH - 1), t)
            o_ref[pl.ds(r0, t), :] = acc
            return carry

        lax.fori_loop(0, s // t, chunk, 0)

    return _call("conv_fwd", body, (nct,),
                 [pl.BlockSpec((s, LANES), lambda c: (0, 3 * nct + c)), pl.BlockSpec((s, LANES), lambda c: (0, 4 * nct + c)),
                  pl.BlockSpec((CONV_PAD, LANES), lambda c: (0, c)), pl.BlockSpec((1, LANES), lambda c: (0, c))],
                 [pl.BlockSpec((s, LANES), lambda c: (0, c))], [jax.ShapeDtypeStruct((s, c_total), F32)],
                 [pltpu.VMEM((s + CONV_PAD, LANES), F32)], ("parallel",), (proj, proj, w_pad, b_dw), comm)


def _conv_bwd(du1, proj, w_pad, c_total, comm=None):
    s = proj.shape[0]
    nct = c_total // LANES
    t = _tile(s, 256)

    def body(d_ref, ga_ref, gb_ref, w_ref, dga_ref, dgb_ref, dw_ref, db_ref, u0_ref, dp_ref):
        sg = _sigmoid(gb_ref[...])
        u0_ref[pl.ds(0, CONV_PAD), :] = jnp.zeros((CONV_PAD, LANES), F32)
        u0_ref[pl.ds(CONV_PAD, s), :] = ga_ref[...] * sg
        dp_ref[pl.ds(0, s), :] = d_ref[...]
        dp_ref[pl.ds(s, CONV_PAD), :] = jnp.zeros((CONV_PAD, LANES), F32)
        dw_ref[...] = jnp.zeros_like(dw_ref)
        db_ref[...] = jnp.sum(d_ref[...], axis=0, keepdims=True)
        wv = w_ref[...]

        def chunk(r, carry):
            r0 = pl.multiple_of(r * t, t)
            win = u0_ref[pl.ds(r0, t + CONV_PAD), :]
            dwin = dp_ref[pl.ds(r0, t + CONV_PAD), :]
            dcur = dwin[:t]
            du0 = jnp.zeros((t, LANES), F32)
            for j in range(CONV_WIDTH):
                du0 = du0 + wv[j:j + 1, :] * _shift_rows(dwin, CONV_WIDTH - 1 - j, t)
                sh = _shift_rows(win, j + CONV_PAD - (CONV_WIDTH - 1), t)
                dw_ref[j:j + 1, :] += jnp.sum(dcur * sh, axis=0, keepdims=True)
            gav = ga_ref[pl.ds(r0, t), :]
            sgv = _sigmoid(gb_ref[pl.ds(r0, t), :])
            dga_ref[pl.ds(r0, t), :] = (du0 * sgv).astype(BF16)
            dgb_ref[pl.ds(r0, t), :] = (du0 * gav * sgv * (1.0 - sgv)).astype(BF16)
            return carry

        lax.fori_loop(0, s // t, chunk, 0)

    col = pl.BlockSpec((s, LANES), lambda c: (0, c))
    return _call("conv_bwd", body, (nct,),
                 [col, pl.BlockSpec((s, LANES), lambda c: (0, 3 * nct + c)),
                  pl.BlockSpec((s, LANES), lambda c: (0, 4 * nct + c)), pl.BlockSpec((CONV_PAD, LANES), lambda c: (0, c))],
                 [col, col, pl.BlockSpec((CONV_PAD, LANES), lambda c: (0, c)), pl.BlockSpec((1, LANES), lambda c: (0, c))],
                 [jax.ShapeDtypeStruct((s, c_total), BF16), jax.ShapeDtypeStruct((s, c_total), BF16),
                  jax.ShapeDtypeStruct((CONV_PAD, c_total), F32), jax.ShapeDtypeStruct((1, c_total), F32)],
                 [pltpu.VMEM((s + CONV_PAD, LANES), F32), pltpu.VMEM((s + CONV_PAD, LANES), F32)],
                 ("parallel",), (du1, proj, proj, w_pad), comm)


TQ_PREF = 256
NU = 4
NU_BWD = 4
TK = 256


def _split_dot(v, tri):
    hi = v.astype(BF16)
    lo = (v - hi.astype(F32)).astype(BF16)
    return (jnp.dot(hi, tri, preferred_element_type=F32) + jnp.dot(lo, tri, preferred_element_type=F32))


def _causal_mask(i, j, tq):
    tpos = i * tq + lax.broadcasted_iota(jnp.int32, (tq, TK), 0)
    spos = j * TK + lax.broadcasted_iota(jnp.int32, (tq, TK), 1)
    return spos < tpos


def _log_terms(z, mask):
    sp = jnp.log(1.0 + jnp.exp(-jnp.abs(z)))
    return jnp.minimum(z, 0.0) - sp, jnp.where(mask, -jnp.maximum(z, 0.0) - sp, 0.0)


def _tri(after):
    r = lax.broadcasted_iota(jnp.int32, (TK, TK), 0)
    c = lax.broadcasted_iota(jnp.int32, (TK, TK), 1)
    return (r > c).astype(BF16) if after else (r < c).astype(BF16)


def _attn_fwd(proj, n_heads, comm=None):
    s = proj.shape[0]
    tq = _tile(s, TQ_PREF)
    scale = 1.0 / math.sqrt(HEAD_DIM)
    ratio = tq // TK

    def body(q_ref, k_ref, v_ref, o_ref, ot_ref, acc_ref, *clms):
        i = pl.program_id(1)
        heads = [slice(u * HEAD_DIM, (u + 1) * HEAD_DIM) for u in range(NU)]
        qs = [q_ref[:, hs].astype(BF16) for hs in heads]
        tri_after = _tri(True)
        acc_ref[...] = jnp.zeros_like(acc_ref)
        for cr in clms:
            cr[...] = jnp.zeros_like(cr)
        nkb = (i + 1) * ratio

        def step(jj, carry):
            j = nkb - 1 - jj
            rows = pl.ds(pl.multiple_of(j * TK, TK), TK)
            mask = _causal_mask(i, j, tq)
            zs = [lax.dot_general(qs[u], k_ref[rows, hs].astype(BF16), (((1,), (1,)), ((), ())),
                                  preferred_element_type=F32) * scale for u, hs in enumerate(heads)]
            lls = [_log_terms(z, mask) for z in zs]
            sufs = [clms[u][...] + _split_dot(lls[u][1], tri_after) for u in range(NU)]
            for u, hs in enumerate(heads):
                a = jnp.where(mask, jnp.exp(lls[u][0] + sufs[u]), 0.0)
                acc_ref[:, hs] += jnp.dot(a.astype(BF16), v_ref[rows, hs].astype(BF16), preferred_element_type=F32)
                clms[u][...] += jnp.sum(lls[u][1], axis=1, keepdims=True)
            return carry

        lax.fori_loop(0, nkb, step, 0)
        o = acc_ref[...]
        o_ref[...] = o
        ot_ref[...] = o.T.astype(BF16)

    w = NU * HEAD_DIM
    ng = n_heads // NU
    return _call("attn_fwd", body, (ng, s // tq),
                 [pl.BlockSpec((tq, w), lambda h, i: (i, h)),
                  pl.BlockSpec((s, w), lambda h, i: (0, ng + h)),
                  pl.BlockSpec((s, w), lambda h, i: (0, 2 * ng + h))],
                 [pl.BlockSpec((tq, w), lambda h, i: (i, h)), pl.BlockSpec((w, tq), lambda h, i: (h, i))],
                 [jax.ShapeDtypeStruct((s, n_heads * HEAD_DIM), F32), jax.ShapeDtypeStruct((n_heads * HEAD_DIM, s), BF16)],
                 [pltpu.VMEM((tq, w), F32), *[pltpu.VMEM((tq, 1), F32)] * NU],
                 ("parallel", "arbitrary"), (proj, proj, proj), comm)


def _attn_bwd(proj, do_sb, n_heads, comm=None):
    s = proj.shape[0]
    tq = _tile(s, TQ_PREF)
    scale = 1.0 / math.sqrt(HEAD_DIM)
    ratio = tq // TK
    n_kb = s // TK
    n_qb = s // tq
    nu = NU_BWD
    heads = [slice(u * HEAD_DIM, (u + 1) * HEAD_DIM) for u in range(nu)]
    nt_dims = (((1,), (1,)), ((), ()))

    def body(q_ref, k_ref, v_ref, do_ref, dq_ref, dk_ref, dv_ref, dka_ref, dva_ref, dl_ref, be_ref, dqa_ref, *c_refs):
        i = pl.program_id(1)

        @pl.when(i == 0)
        def _():
            dka_ref[...] = jnp.zeros_like(dka_ref)
            dva_ref[...] = jnp.zeros_like(dva_ref)

        qs = [q_ref[:, hs].astype(BF16) for hs in heads]
        dobs = [do_ref[:, hs].astype(BF16) for hs in heads]
        tri_after = _tri(True)
        tri_before = _tri(False)
        nkb = (i + 1) * ratio

        for cr in c_refs:
            cr[...] = jnp.zeros_like(cr)

        def sweep_a(jj, carry):
            j = nkb - 1 - jj
            rows = pl.ds(pl.multiple_of(j * TK, TK), TK)
            mask = _causal_mask(i, j, tq)
            zs = [lax.dot_general(qs[u], k_ref[rows, hs].astype(BF16), nt_dims, preferred_element_type=F32) * scale
                  for u, hs in enumerate(heads)]
            das = [lax.dot_general(dobs[u], v_ref[rows, hs].astype(BF16), nt_dims, preferred_element_type=F32)
                   for u, hs in enumerate(heads)]
            lls = [_log_terms(z, mask) for z in zs]
            sufs = [c_refs[u][...] + _split_dot(lls[u][1], tri_after) for u in range(nu)]
            for u, hs in enumerate(heads):
                a = jnp.where(mask, jnp.exp(lls[u][0] + sufs[u]), 0.0)
                dl_ref[u, j] = das[u] * a
                be_ref[u, j] = jnp.exp(lls[u][0])
                dva_ref[rows, hs] += jnp.dot(a.T.astype(BF16), dobs[u], preferred_element_type=F32)
                c_refs[u][...] += jnp.sum(lls[u][1], axis=1, keepdims=True)
            return carry

        lax.fori_loop(0, nkb, sweep_a, 0)

        for cr in c_refs:
            cr[...] = jnp.zeros_like(cr)
        dqa_ref[...] = jnp.zeros_like(dqa_ref)

        def sweep_b(j, carry):
            rows = pl.ds(pl.multiple_of(j * TK, TK), TK)
            mask = _causal_mask(i, j, tq)
            dls = [dl_ref[u, j] for u in range(nu)]
            ps = [c_refs[u][...] + _split_dot(dls[u], tri_before) for u in range(nu)]
            for u, hs in enumerate(heads):
                beta = be_ref[u, j]
                dz = jnp.where(mask, (dls[u] * (1.0 - beta) - beta * ps[u]) * scale, 0.0)
                dqa_ref[:, hs] += jnp.dot(dz.astype(BF16), k_ref[rows, hs].astype(BF16), preferred_element_type=F32)
                dka_ref[rows, hs] += jnp.dot(dz.T.astype(BF16), qs[u], preferred_element_type=F32)
                c_refs[u][...] += jnp.sum(dls[u], axis=1, keepdims=True)
            return carry

        lax.fori_loop(0, nkb, sweep_b, 0)
        dq_ref[...] = dqa_ref[...].astype(BF16)

        @pl.when(i == n_qb - 1)
        def _():
            dk_ref[...] = dka_ref[...].astype(BF16)
            dv_ref[...] = dva_ref[...].astype(BF16)

    w = nu * HEAD_DIM
    ng = n_heads // nu
    qblk = pl.BlockSpec((tq, w), lambda h, i: (i, h))
    full = pl.BlockSpec((s, w), lambda h, i: (0, h))
    sds = jax.ShapeDtypeStruct((s, n_heads * HEAD_DIM), BF16)
    return _call("attn_bwd", body, (ng, n_qb),
                 [qblk, pl.BlockSpec((s, w), lambda h, i: (0, ng + h)), pl.BlockSpec((s, w), lambda h, i: (0, 2 * ng + h)),
                  qblk],
                 [qblk, full, full], [sds, sds, sds],
                 [pltpu.VMEM((s, w), F32), pltpu.VMEM((s, w), F32),
                  pltpu.VMEM((nu, n_kb, tq, TK), F32), pltpu.VMEM((nu, n_kb, tq, TK), F32),
                  pltpu.VMEM((tq, w), F32), *[pltpu.VMEM((tq, 1), F32)] * nu],
                 ("parallel", "arbitrary"), (proj, proj, proj, do_sb), comm)


def _adamw(name, w, m, v, parts, part_specs, tr, comm=None):
    r, c = w.shape
    n_parts = len(parts)

    def body(*refs):
        w_ref, m_ref, v_ref = refs[:3]
        p_refs = refs[3:3 + n_parts]
        g_ref, d_ref, nm_ref, nv_ref = refs[3 + n_parts:]
        g = p_refs[0][...].astype(F32)
        for p in p_refs[1:]:
            g = g + p[...].astype(F32)
        nm = ADAM_B1 * m_ref[...] + (1.0 - ADAM_B1) * g
        nv = ADAM_B2 * v_ref[...] + (1.0 - ADAM_B2) * jnp.square(g)
        m_hat = nm / (1.0 - ADAM_B1 ** ADAM_STEP)
        v_hat = nv / (1.0 - ADAM_B2 ** ADAM_STEP)
        g_ref[...] = g
        d_ref[...] = -ADAM_LR * (m_hat / (jnp.sqrt(v_hat) + ADAM_EPS) + ADAM_WD * w_ref[...])
        nm_ref[...] = nm
        nv_ref[...] = nv

    blk = pl.BlockSpec((tr, c), lambda i: (i, 0))
    sds = jax.ShapeDtypeStruct((r, c), F32)
    return _call(name, body, (r // tr,), [blk, blk, blk, *part_specs], [blk] * 4, [sds] * 4, [], ("parallel",),
                 (w, m, v, *parts), comm)


def _adamw_big(name, w, m, v, recv, comm=None):
    r, c = w.shape
    tr = _tile(r, 128)
    order = (3, 0, 1, 2)
    specs = [pl.BlockSpec((None, tr, c), functools.partial(lambda i, slot: (slot, i, 0), slot=sl)) for sl in order]
    return _adamw(name, w, m, v, [recv] * 4, specs, tr, comm)


def _adamw_small(name, w, m, v, g):
    r, c = w.shape
    return _adamw(name, w, m, v, [g], [pl.BlockSpec((r, c), lambda i: (0, 0))], r)


def kernel(x, g_pre_mix, w_in, b_in, w_dw, b_dw, g_conv_ln, b_conv_ln, w_sb_out, w_conv_out, w_o, g_post_mix, g_pre_mlp, w_up, w_down, g_post_mlp, loss_target, m_g_pre_mix, m_w_in, m_b_in, m_w_dw, m_b_dw, m_g_conv_ln, m_b_conv_ln, m_w_sb_out, m_w_conv_out, m_w_o, m_g_post_mix, m_g_pre_mlp, m_w_up, m_w_down, m_g_post_mlp, v_g_pre_mix, v_w_in, v_b_in, v_w_dw, v_b_dw, v_g_conv_ln, v_b_conv_ln, v_w_sb_out, v_w_conv_out, v_w_o, v_g_post_mix, v_g_pre_mlp, v_w_up, v_w_down, v_g_post_mlp):
    xs, tgt = x[0], loss_target[0]
    s, d = xs.shape
    d_half = d // 2
    n_heads = d_half // HEAD_DIM
    d_ff = NDEV * w_up.shape[2]
    core = lax.axis_index("c").astype(jnp.int32).reshape(1)
    dev = 4 * lax.axis_index("x") + 2 * lax.axis_index("y") + lax.axis_index("c")

    w_dw_pad = jnp.pad(w_dw[0], ((0, CONV_PAD - CONV_WIDTH), (0, 0)))
    sh_in, sh_sb, sh_cv, sh_o, sh_up, sh_down = [w[0].astype(BF16) for w in (w_in, w_sb_out, w_conv_out, w_o, w_up, w_down)]

    ag_down = _Chunked(_ag_comm, [sh_down], sh_down.shape[0], 8)
    h, h_t, wg_in = _prenorm(xs, g_pre_mix, _ag_comm([sh_in]))
    ag_up = _Chunked(_ag_comm, [sh_up], sh_up.shape[0], 8)
    proj, proj_bf, wg_sb, wg_cv, wg_o, wg_dw, *part = _mm_cols(
        "proj", h, wg_in, bias=b_in, bf16_copy=True,
        comm=_join(_ag_comm([sh_sb, sh_cv, sh_o, w_dw_pad]), ag_up.take(1)))
    ag_up.done(part)
    o_sb, o_sb_t, wg_up = _attn_fwd(proj_bf, n_heads, ag_up.take(7))
    wf_dw = wg_dw.transpose(1, 0, 2).reshape(CONV_PAD, d_half)
    wf_o = wg_o.reshape(d, d)
    u1, *part = _conv_fwd(proj, wf_dw, b_dw, d_half, ag_down.take())
    ag_down.done(part)
    u3, u3_t = _ln_silu(u1, g_conv_ln, b_conv_ln)
    o_sbp = _mm_cols("sb_out", o_sb, wg_sb)[0]
    o_cv = _mm_cols("conv_out", u3, wg_cv)[0]
    merged, merged_t, *part = _merge(proj, o_sbp, o_cv, d, ag_down.take())
    ag_down.done(part)
    y, *part = _mm_plain("w_o", merged, wf_o, False, F32, comm=ag_down.take())
    ag_down.done(part)
    x1, h2, h2_t, *part = _postnorm_mix(xs, y, g_post_mix, g_pre_mlp, ag_down.take())
    ag_down.done(part)

    tm_up = _tile(s, 1024)
    ns_up = wg_up.shape[2]
    tk_up = _tile(d, 2048)

    def up_epilogue(acc):
        f = jnp.square(jnp.maximum(acc, 0.0))
        return acc, f, f.T

    a_act, f, f_t, wg_down = _matmul(
        "w_up", h2, wg_up,
        [jax.ShapeDtypeStruct((s, d_ff), BF16), jax.ShapeDtypeStruct((s, d_ff), BF16), jax.ShapeDtypeStruct((d_ff, s), BF16)],
        (s // tm_up, NDEV, d // tk_up),
        pl.BlockSpec((tm_up, tk_up), lambda i, dd, kk: (i, kk)),
        pl.BlockSpec((None, tk_up, ns_up), lambda i, dd, kk: (dd, kk, 0)),
        [pl.BlockSpec((tm_up, ns_up), lambda i, dd, kk: (i, dd)), pl.BlockSpec((tm_up, ns_up), lambda i, dd, kk: (i, dd)),
         pl.BlockSpec((ns_up, tm_up), lambda i, dd, kk: (dd, i))],
        1, False, (tm_up, ns_up), epilogue=up_epilogue, comm=ag_down.take(4))
    wf_down = wg_down.reshape(d_ff, d)
    f2 = _mm_plain("w_down", f, wf_down, False, F32)[0]
    dx2, df2, dg_post_mlp, loss_part = _loss_head(x1, f2, tgt, g_post_mlp)

    tm_b, tn_b = _tile(s, 1024), _tile(d_ff, 1024)
    da = _mm_plain("w_down_bwd", df2, wf_down, True, BF16,
                   extra=(a_act,), extra_specs=(pl.BlockSpec((tm_b, tn_b), lambda i, j, kk: (i, j)),),
                   epilogue=lambda acc, av: (acc * (2.0 * jnp.maximum(av.astype(F32), 0.0)),),
                   outs=[jax.ShapeDtypeStruct((s, d_ff), BF16)],
                   out_specs=[pl.BlockSpec((tm_b, tn_b), lambda i, j, kk: (i, j))])[0]
    gw_down = _mm_plain("w_down_grad", f_t, df2, False, BF16)[0]
    big_down = gw_down.reshape(4, 2, d_ff // NDEV, d)
    gw_up, sib_down = _mm_dw_cols("w_up_grad", h2_t, da, comm=_sibling_comm([big_down]))
    big_up = gw_up.reshape(4, 2, d, d_ff // NDEV)
    dh2, sib_up = _mm_cols_t("w_up_bwd", da, wg_up, comm=_sibling_comm([big_up]))
    rs_down = _Chunked(_chips_comm, [_pair_sum(big_down, sib_down, core)], d_ff // NDEV, 8)
    rs_up = _Chunked(_chips_comm, [_pair_sum(big_up, sib_up, core)], d, 8)
    dx1, dy, dg_pre_mlp, dg_post_mix, *part = _midnorm_bwd(dx2, dh2, x1, y, g_post_mix, g_pre_mlp, rs_down.take())
    rs_down.done(part)
    gw_o, *part = _mm_plain("w_o_grad", merged_t, dy, False, BF16, comm=rs_down.take())
    rs_down.done(part)
    dmerged, *part = _mm_plain("w_o_bwd", dy, wf_o, True, F32, comm=rs_down.take())
    rs_down.done(part)
    do_sbp, do_cv, dgate_sb, dgate_cv, *part = _merge_bwd(dmerged, proj, o_sbp, o_cv, d, rs_down.take())
    rs_down.done(part)
    gw_cv = _mm_dw_cols("conv_out_grad", u3_t, do_cv)[0]
    gw_sb = _mm_dw_cols("sb_out_grad", o_sb_t, do_sbp)[0]
    du3 = _mm_cols_t("conv_out_bwd", do_cv, wg_cv)[0]
    do_sb = _mm_cols_t("sb_out_bwd", do_sbp, wg_sb)[0]
    du1, dg_ln, db_ln = _ln_silu_bwd(du3, u1, g_conv_ln, b_conv_ln)
    big_mid = [gw_sb.reshape(4, 2, d_half, d // NDEV), gw_cv.reshape(4, 2, d_half, d // NDEV),
               gw_o.reshape(4, 2, d // NDEV, d)]
    dglu_a, dglu_b, dw_dw, db_dw, part, *sib_mid = _conv_bwd(du1, proj, wf_dw, d_half,
                                                             _join(rs_up.take(2), _sibling_comm(big_mid)))
    rs_up.done([part])
    sums_mid = [_pair_sum(g, r, core) for g, r in zip(big_mid, sib_mid)]
    dq, dk, dv, r_up, r_sb, r_cv, r_o = _attn_bwd(proj_bf, do_sb, n_heads, _join(rs_up.take(6), _chips_comm(sums_mid)))
    dproj = jnp.concatenate([dq, dk, dv, dglu_a, dglu_b, dgate_sb, dgate_cv], axis=1)
    gw_in, r_down = _mm_dw_cols("w_in_grad", h_t, dproj, comm=rs_down.take(4))
    big_in = gw_in.reshape(4, 2, d, gw_in.shape[2])
    db_in, sib_in = _colsum(dproj, _sibling_comm([big_in]))
    dh, r_in = _mm_cols_t("w_in_bwd", dproj, wg_in, comm=_chips_comm([_pair_sum(big_in, sib_in, core)]))
    grad_x, dg_pre_mix = _prenorm_bwd(dx1, dh, xs, g_pre_mix)

    small = [dg_pre_mix, db_in, dw_dw.reshape(1, -1), db_dw, dg_ln, db_ln, dg_post_mix, dg_pre_mlp, dg_post_mlp]
    sizes = [a.shape[1] for a in small]
    packed = jnp.concatenate(small, axis=1).reshape(-1, LANES)
    total = _sum_small(_all_gather_small(packed)).reshape(1, -1)
    offs = [0]
    for n in sizes:
        offs.append(offs[-1] + n)
    (g_g_pre_mix, g_b_in, g_w_dw_flat, g_b_dw, g_g_conv_ln, g_b_conv_ln, g_g_post_mix, g_g_pre_mlp,
     g_g_post_mlp) = [total[:, offs[k]:offs[k + 1]] for k in range(len(sizes))]
    ch = w_dw.shape[2]
    g_w_dw = lax.dynamic_slice_in_dim(g_w_dw_flat.reshape(CONV_PAD, d_half), dev * ch, ch, axis=1)[:CONV_WIDTH]

    loss = lax.psum(loss_part[0, 0], ("x", "y", "c"))

    res = {}
    res["w_up"] = _adamw_big("adamw_w_up", w_up[0], m_w_up[0], v_w_up[0], r_up)
    res["w_down"] = _adamw_big("adamw_w_down", w_down[0], m_w_down[0], v_w_down[0], r_down)
    res["g_pre_mix"] = _adamw_small("adamw_g_pre_mix", g_pre_mix, m_g_pre_mix, v_g_pre_mix, g_g_pre_mix)
    res["w_in"] = _adamw_big("adamw_w_in", w_in[0], m_w_in[0], v_w_in[0], r_in)
    res["b_in"] = _adamw_small("adamw_b_in", b_in, m_b_in, v_b_in, g_b_in)
    res["w_dw"] = _adamw_small("adamw_w_dw", w_dw[0], m_w_dw[0], v_w_dw[0], g_w_dw)
    res["b_dw"] = _adamw_small("adamw_b_dw", b_dw, m_b_dw, v_b_dw, g_b_dw)
    res["g_conv_ln"] = _adamw_small("adamw_g_conv_ln", g_conv_ln, m_g_conv_ln, v_g_conv_ln, g_g_conv_ln)
    res["b_conv_ln"] = _adamw_small("adamw_b_conv_ln", b_conv_ln, m_b_conv_ln, v_b_conv_ln, g_b_conv_ln)
    res["w_sb_out"] = _adamw_big("adamw_w_sb_out", w_sb_out[0], m_w_sb_out[0], v_w_sb_out[0], r_sb)
    res["w_conv_out"] = _adamw_big("adamw_w_conv_out", w_conv_out[0], m_w_conv_out[0], v_w_conv_out[0], r_cv)
    res["w_o"] = _adamw_big("adamw_w_o", w_o[0], m_w_o[0], v_w_o[0], r_o)
    res["g_post_mix"] = _adamw_small("adamw_g_post_mix", g_post_mix, m_g_post_mix, v_g_post_mix, g_g_post_mix)
    res["g_pre_mlp"] = _adamw_small("adamw_g_pre_mlp", g_pre_mlp, m_g_pre_mlp, v_g_pre_mlp, g_g_pre_mlp)
    res["g_post_mlp"] = _adamw_small("adamw_g_post_mlp", g_post_mlp, m_g_post_mlp, v_g_post_mlp, g_g_post_mlp)

    names = ["g_pre_mix", "w_in", "b_in", "w_dw", "b_dw", "g_conv_ln", "b_conv_ln", "w_sb_out", "w_conv_out", "w_o",
             "g_post_mix", "g_pre_mlp", "w_up", "w_down", "g_post_mlp"]
    three_d = {"w_in", "w_dw", "w_sb_out", "w_conv_out", "w_o", "w_up", "w_down"}

    def shaped(nm, arr):
        return arr[None] if nm in three_d else arr

    out = [loss, grad_x[None]]
    for k in range(4):
        out += [shaped(nm, res[nm][k]) for nm in names]
    return tuple(out)
```
